```python
import math
import jax
import jax.numpy as jnp
from jax import lax
import numpy as np

D_MODEL = 1024
BATCH = 8
SEQ = 2048
DEPTH = 2

HEAD_DIM = 64
A_HEADS = 6
A_KV_HEADS = 2
A_GROUP = A_HEADS // A_KV_HEADS
A_WIDTH = A_HEADS * HEAD_DIM
CMP_BLOCK = 32
CMP_STRIDE = 16
SEL_BLOCK = 64
N_SELECT = 16
WINDOW = 512
NSA_Q_BLOCK = 64
FORCE = 1e4
B_HEADS = 4
B_WIDTH = B_HEADS * HEAD_DIM
CONV_K = 4
MLSTM_CHUNK = 64
C_HEADS = 6
C_NOPE = 64
C_ROPE = 32
C_V = 64
C_WIDTH = C_HEADS * C_V
Q_RANK = 192
KV_RANK = 128
ROPE_THETA = 10000.0
ATTN_Q_BLOCK = 128
D_MIX = A_WIDTH + B_WIDTH + C_WIDTH
REL_BUCKETS = 32
REL_MAX_DIST = 128
N_GROUPS = 4
EXPERTS_PER_GROUP = 8
N_EXPERTS = N_GROUPS * EXPERTS_PER_GROUP
TOP_K_INNER = 2
D_EXPERT = 256
EPS = 1e-6
NEG = -1e30
IN_SPLITS = (A_WIDTH, 3 * 2 * A_KV_HEADS * HEAD_DIM, 3 * A_HEADS,
             2 * B_WIDTH, B_WIDTH, 2 * B_HEADS, B_WIDTH,
             Q_RANK, KV_RANK, C_ROPE)
D_IN = 384 + 768 + 18 + 512 + 256 + 8 + 256 + 192 + 128 + 32

kernel_name = 'hybrid_nsa_mlstm_mla_hmoe'


def split_points():
    pts, acc = [], 0
    for w in IN_SPLITS[:-1]:
        acc += w
        pts.append(acc)
    return pts


def rms_norm(x, g):
    xf = x.astype(jnp.float32)
    y = xf * lax.rsqrt(jnp.mean(xf * xf, axis=-1, keepdims=True) + EPS)
    return (y * g.astype(jnp.float32)).astype(x.dtype)


def masked_softmax(logits, mask):
    p = jax.nn.softmax(jnp.where(mask, logits, NEG), axis=-1)
    return jnp.where(mask, p, 0.0)


def t5_bucket(dist):
    n = jnp.maximum(dist, 0)
    exact = REL_BUCKETS // 2
    nf = jnp.maximum(n, 1).astype(jnp.float32)
    large = exact + (jnp.log(nf / exact) / math.log(REL_MAX_DIST / exact)
                     * (REL_BUCKETS - exact)).astype(jnp.int32)
    large = jnp.minimum(large, REL_BUCKETS - 1)
    return jnp.where(n < exact, n, large)


def rope(x, positions):
    half = x.shape[-1] // 2
    inv = 1.0 / (ROPE_THETA ** (jnp.arange(half, dtype=jnp.float32) * 2.0 / x.shape[-1]))
    ang = positions.astype(jnp.float32)[:, :, None, None] * inv
    cos, sin = jnp.cos(ang).astype(x.dtype), jnp.sin(ang).astype(x.dtype)
    x1, x2 = x[..., :half], x[..., half:]
    return jnp.concatenate([x1 * cos - x2 * sin, x2 * cos + x1 * sin], axis=-1)


def causal_dwconv(x, w):
    T = x.shape[1]
    xp = jnp.pad(x, ((0, 0), (CONV_K - 1, 0), (0, 0)))
    y = xp[:, 0:T] * w[0]
    for j in range(1, CONV_K):
        y = y + xp[:, j:j + T] * w[j]
    return y


def nsa_mixer(q, kv, gates, rel_bias, cmp_pe, cmp_w1, cmp_w2, q_norm_g, k_norm_g):
    bsz, T, _ = q.shape
    dt = q.dtype
    f32 = jnp.float32
    scale = HEAD_DIM ** -0.5
    n_cmp = (T - CMP_BLOCK) // CMP_STRIDE + 1
    n_sel = T // SEL_BLOCK
    k_top = min(N_SELECT, n_sel)
    qh = rms_norm(q.reshape(bsz, T, A_HEADS, HEAD_DIM), q_norm_g)
    qh = qh.reshape(bsz, T, A_KV_HEADS, A_GROUP, HEAD_DIM).transpose(0, 2, 3, 1, 4)
    kv = kv.reshape(bsz, T, 3, 2, A_KV_HEADS, HEAD_DIM)

    def to_heads(t):
        return t.transpose(0, 2, 1, 3)

    starts = jnp.arange(n_cmp) * CMP_STRIDE
    cmp_end = starts + (CMP_BLOCK - 1)
    tok = starts[:, None] + jnp.arange(CMP_BLOCK)[None, :]
    blk = kv[:, :, 0][:, tok]
    blk = blk + cmp_pe.transpose(1, 0, 2)[:, :, None, :]
    hid = jax.nn.silu(jnp.einsum('bnlshd,sldf->bnshf', blk, cmp_w1))
    comp = jnp.einsum('bnshf,sfd->bnshd', hid, cmp_w2)
    k_cmp = rms_norm(comp[:, :, 0], k_norm_g).transpose(0, 2, 1, 3)
    v_cmp = comp[:, :, 1].transpose(0, 2, 1, 3)
    k_slc = to_heads(rms_norm(kv[:, :, 1, 0], k_norm_g)).reshape(bsz, A_KV_HEADS, n_sel, SEL_BLOCK, HEAD_DIM)
    v_slc = to_heads(kv[:, :, 1, 1]).reshape(bsz, A_KV_HEADS, n_sel, SEL_BLOCK, HEAD_DIM)
    sel_id = jnp.arange(n_sel)
    overlap = ((starts[:, None] < (sel_id[None, :] + 1) * SEL_BLOCK)
               & (starts[:, None] + CMP_BLOCK > sel_id[None, :] * SEL_BLOCK)).astype(f32)
    pad = ((0, 0), (0, 0), (WINDOW, 0), (0, 0))
    k_win = jnp.pad(to_heads(rms_norm(kv[:, :, 2, 0], k_norm_g)), pad)
    v_win = jnp.pad(to_heads(kv[:, :, 2, 1]), pad)
    g = jax.nn.sigmoid(gates.astype(f32)).reshape(bsz, T, A_KV_HEADS, A_GROUP, 3).transpose(0, 2, 3, 1, 4)
    bias_tbl = rel_bias.astype(f32).reshape(REL_BUCKETS, A_KV_HEADS, A_GROUP)
    b_idx = jnp.arange(bsz)[:, None, None, None]
    h_idx = jnp.arange(A_KV_HEADS)[None, :, None, None]

    def head_bias(dist):
        return bias_tbl[t5_bucket(dist)].transpose(2, 3, 0, 1)

    def block(i):
        t0 = i * NSA_Q_BLOCK
        tq = t0 + jnp.arange(NSA_Q_BLOCK)
        qb = lax.dynamic_slice_in_dim(qh, t0, NSA_Q_BLOCK, axis=3)
        gb = lax.dynamic_slice_in_dim(g, t0, NSA_Q_BLOCK, axis=3).astype(dt)
        dist_c = tq[:, None] - cmp_end[None, :]
        lc = jnp.einsum('bhgqd,bhnd->bhgqn', qb, k_cmp).astype(f32) * scale + head_bias(dist_c)
        pc = masked_softmax(lc, dist_c >= 0)
        o_c = jnp.einsum('bhgqn,bhnd->bhgqd', pc.astype(dt), v_cmp)
        ps = jnp.einsum('bhgqn,ns->bhqs', pc, overlap)
        cur = tq[:, None] // SEL_BLOCK
        forced = (sel_id[None, :] == 0) | (sel_id[None, :] == cur) | (sel_id[None, :] == cur - 1)
        ps = jnp.where(forced, ps + FORCE, ps)
        ps = jnp.where(sel_id[None, :] <= cur, ps, -1.0)
        top_s, top_i = lax.top_k(ps, k_top)
        n_tok = k_top * SEL_BLOCK
        ks = k_slc[b_idx, h_idx, top_i].reshape(bsz, A_KV_HEADS, NSA_Q_BLOCK, n_tok, HEAD_DIM)
        vs = v_slc[b_idx, h_idx, top_i].reshape(bsz, A_KV_HEADS, NSA_Q_BLOCK, n_tok, HEAD_DIM)
        pos_s = (top_i[..., None] * SEL_BLOCK + jnp.arange(SEL_BLOCK)).reshape(bsz, A_KV_HEADS, NSA_Q_BLOCK, n_tok)
        dist_s = tq[:, None] - pos_s
        mask_s = (dist_s >= 0) & jnp.repeat(top_s >= 0, SEL_BLOCK, axis=-1)
        bias_s = bias_tbl[t5_bucket(dist_s), h_idx].transpose(0, 1, 4, 2, 3)
        ls = jnp.einsum('bhgqd,bhqnd->bhgqn', qb, ks).astype(f32) * scale + bias_s
        p_s = masked_softmax(ls, mask_s[:, :, None])
        o_s = jnp.einsum('bhgqn,bhqnd->bhgqd', p_s.astype(dt), vs)
        kw = lax.dynamic_slice_in_dim(k_win, t0, WINDOW + NSA_Q_BLOCK, axis=2)
        vw = lax.dynamic_slice_in_dim(v_win, t0, WINDOW + NSA_Q_BLOCK, axis=2)
        pos_w = t0 - WINDOW + jnp.arange(WINDOW + NSA_Q_BLOCK)
        dist_w = tq[:, None] - pos_w[None, :]
        mask_w = (dist_w >= 0) & (dist_w < WINDOW) & (pos_w[None, :] >= 0)
        lw = jnp.einsum('bhgqd,bhkd->bhgqk', qb, kw).astype(f32) * scale + head_bias(dist_w)
        p_w = masked_softmax(lw, mask_w)
        o_w = jnp.einsum('bhgqk,bhkd->bhgqd', p_w.astype(dt), vw)
        return gb[..., 0:1] * o_c + gb[..., 1:2] * o_s + gb[..., 2:3] * o_w

    out = lax.map(block, jnp.arange(T // NSA_Q_BLOCK))
    return out.transpose(1, 0, 4, 2, 3, 5).reshape(bsz, T, A_WIDTH)


def mlstm_chunkwise(q, k, v, i_pre, f_pre):
    bsz, H, T, dh = q.shape
    L = MLSTM_CHUNK
    nc = T // L
    log_f = jax.nn.log_sigmoid(f_pre)

    def chunks(t):
        return jnp.moveaxis(t.reshape(bsz, H, nc, L, *t.shape[3:]), 2, 0)

    tri = jnp.tril(jnp.ones((L, L), dtype=bool))

    def step(carry, inp):
        C, n, m = carry
        qc, kc, vc, ic, lfc = inp
        b = jnp.cumsum(lfc, axis=-1)
        D = jnp.where(tri, b[..., :, None] - b[..., None, :] + ic[..., None, :], -jnp.inf)
        m_inter = b + m[..., None]
        m_j = jnp.maximum(m_inter, D.max(-1))
        inter_scale = jnp.exp(m_inter - m_j)
        s = jnp.einsum('bhjd,bhsd->bhjs', qc, kc) * jnp.exp(D - m_j[..., None])
        num = jnp.einsum('bhjs,bhsd->bhjd', s, vc) + inter_scale[..., None] * jnp.einsum('bhed,bhjd->bhje', C, qc)
        den = s.sum(-1) + inter_scale * jnp.einsum('bhd,bhjd->bhj', n, qc)
        h = num / jnp.maximum(jnp.abs(den), jnp.exp(-m_j))[..., None]
        bL = b[..., -1]
        w_log = bL[..., None] - b + ic
        m_new = jnp.maximum(bL + m, w_log.max(-1))
        w = jnp.exp(w_log - m_new[..., None])
        decay = jnp.exp(bL + m - m_new)
        C_new = decay[..., None, None] * C + jnp.einsum('bhs,bhse,bhsd->bhed', w, vc, kc)
        n_new = decay[..., None] * n + jnp.einsum('bhs,bhsd->bhd', w, kc)
        return (C_new, n_new, m_new), h

    init = (jnp.zeros((bsz, H, dh, dh), jnp.float32), jnp.zeros((bsz, H, dh), jnp.float32),
            jnp.zeros((bsz, H), jnp.float32))
    _, hs = lax.scan(step, init, (chunks(q), chunks(k), chunks(v), chunks(i_pre), chunks(log_f)))
    return jnp.moveaxis(hs, 0, 2).reshape(bsz, H, T, dh)


def mlstm_mixer(qk, v, if_pre, o_pre, conv_w, i_bias, f_bias, out_norm_g):
    bsz, T, _ = qk.shape
    dt = qk.dtype
    f32 = jnp.float32
    qk = jax.nn.silu(causal_dwconv(qk, conv_w))

    def heads(t):
        return t.reshape(bsz, T, B_HEADS, HEAD_DIM).transpose(0, 2, 1, 3).astype(f32)

    q = heads(qk[..., :B_WIDTH])
    k = heads(qk[..., B_WIDTH:]) * (HEAD_DIM ** -0.5)
    vh = heads(v)
    i_pre = (if_pre[..., :B_HEADS].astype(f32) + i_bias.astype(f32)).transpose(0, 2, 1)
    f_pre = (if_pre[..., B_HEADS:].astype(f32) + f_bias.astype(f32)).transpose(0, 2, 1)
    h = mlstm_chunkwise(q, k, vh, i_pre, f_pre).transpose(0, 2, 1, 3)
    h = rms_norm(h, out_norm_g.reshape(B_HEADS, HEAD_DIM)).reshape(bsz, T, B_WIDTH)
    return (jax.nn.sigmoid(o_pre.astype(f32)) * h).astype(dt)


def mla_mixer(dq, dkv, kr, positions, q_a_norm, w_uq, kv_a_norm, w_ukv, qk_norm):
    bsz, T, _ = dq.shape
    dt = dq.dtype
    f32 = jnp.float32
    q = (rms_norm(dq, q_a_norm) @ w_uq).reshape(bsz, T, C_HEADS, C_NOPE + C_ROPE)
    kv = (rms_norm(dkv, kv_a_norm) @ w_ukv).reshape(bsz, T, C_HEADS, C_NOPE + C_V)
    q_nope = rms_norm(q[..., :C_NOPE], qk_norm[0, :C_NOPE]).transpose(0, 2, 1, 3)
    q_rope = rope(rms_norm(q[..., C_NOPE:], qk_norm[0, C_NOPE:]), positions).transpose(0, 2, 1, 3)
    k_nope = rms_norm(kv[..., :C_NOPE], qk_norm[1, :C_NOPE]).transpose(0, 2, 1, 3)
    v = kv[..., C_NOPE:].transpose(0, 2, 1, 3)
    k_rope = rope(rms_norm(kr, qk_norm[1, C_NOPE:])[:, :, None, :], positions)[:, :, 0]
    scale = (C_NOPE + C_ROPE) ** -0.5
    k_pos = jnp.arange(T)

    def block(i):
        t0 = i * ATTN_Q_BLOCK
        qn = lax.dynamic_slice_in_dim(q_nope, t0, ATTN_Q_BLOCK, axis=2)
        qr = lax.dynamic_slice_in_dim(q_rope, t0, ATTN_Q_BLOCK, axis=2)
        logits = (jnp.einsum('bhqd,bhkd->bhqk', qn, k_nope)
                  + jnp.einsum('bhqr,bkr->bhqk', qr, k_rope)).astype(f32) * scale
        mask = k_pos[None, :] <= (t0 + jnp.arange(ATTN_Q_BLOCK))[:, None]
        p = masked_softmax(logits, mask)
        return jnp.einsum('bhqk,bhkd->bhqd', p.astype(dt), v)

    out = lax.map(block, jnp.arange(T // ATTN_Q_BLOCK))
    return out.transpose(1, 0, 3, 2, 4).reshape(bsz, T, C_WIDTH)


def hier_moe(x, w_group, b_group, w_expert, b_expert, w_gate, w_up, w_down):
    bsz, T, D = x.shape
    dt = x.dtype
    f32 = jnp.float32
    xt = x.reshape(-1, D)
    gp = jax.nn.softmax((xt @ w_group).astype(f32) + b_group.astype(f32), axis=-1)
    g_top, g_idx = lax.top_k(gp, 1)
    el = ((xt @ w_expert).astype(f32) + b_expert.astype(f32)).reshape(-1, N_GROUPS, EXPERTS_PER_GROUP)
    el_sel = jnp.take_along_axis(el, g_idx[:, :, None], axis=1)[:, 0]
    e_top, e_idx = lax.top_k(jax.nn.softmax(el_sel, axis=-1), TOP_K_INNER)
    e_w = e_top / e_top.sum(-1, keepdims=True) * g_top
    expert_id = g_idx * EXPERTS_PER_GROUP + e_idx
    combine = (jax.nn.one_hot(expert_id, N_EXPERTS, dtype=f32) * e_w[..., None]).sum(1).astype(dt)
    h = jax.nn.silu(jnp.einsum('nd,edf->nef', xt, w_gate)) * jnp.einsum('nd,edf->nef', xt, w_up)
    y = jnp.einsum('nef,ne,efd->nd', h, combine, w_down)
    return y.reshape(bsz, T, D)


def setup_inputs(seed: int = 0) -> dict:
    key = jax.random.key(seed)
    ks = jax.random.split(key, 30)
    f32 = jnp.float32
    L = DEPTH

    def nrm(k, shape, scale):
        return jax.random.normal(k, shape, f32) * scale

    def gain(k, shape):
        return 1.0 + 0.1 * jax.random.normal(k, shape, f32)

    return {
        'x': nrm(ks[0], (BATCH, SEQ, D_MODEL), 1.0),
        'positions': (jax.random.randint(ks[1], (BATCH, 1), 0, 4096) + jnp.arange(SEQ)[None, :]).astype(jnp.int32),
        'rel_bias': nrm(ks[2], (REL_BUCKETS, A_HEADS), 0.5),
        'norm1_g': gain(ks[3], (L, D_MODEL)),
        'w_in': nrm(ks[4], (L, D_MODEL, D_IN), D_MODEL ** -0.5),
        'nsa_cmp_pe': nrm(ks[5], (L, 2, CMP_BLOCK, HEAD_DIM), 0.1),
        'nsa_cmp_w1': nrm(ks[6], (L, 2, CMP_BLOCK, HEAD_DIM, HEAD_DIM), (CMP_BLOCK * HEAD_DIM) ** -0.5),
        'nsa_cmp_w2': nrm(ks[7], (L, 2, HEAD_DIM, HEAD_DIM), HEAD_DIM ** -0.5),
        'nsa_q_norm': gain(ks[8], (L, HEAD_DIM)),
        'nsa_k_norm': gain(ks[9], (L, HEAD_DIM)),
        'nsa_out_norm': gain(ks[10], (L, A_WIDTH)),
        'mlstm_conv': nrm(ks[11], (L, CONV_K, 2 * B_WIDTH), 0.5),
        'mlstm_i_bias': nrm(ks[12], (L, B_HEADS), 0.1),
        'mlstm_f_bias': jnp.linspace(3.0, 6.0, B_HEADS, dtype=f32)[None, :] + nrm(ks[13], (L, B_HEADS), 0.1),
        'mlstm_out_norm': gain(ks[14], (L, B_WIDTH)),
        'mla_q_a_norm': gain(ks[15], (L, Q_RANK)),
        'mla_w_uq': nrm(ks[16], (L, Q_RANK, C_HEADS * (C_NOPE + C_ROPE)), Q_RANK ** -0.5),
        'mla_kv_a_norm': gain(ks[17], (L, KV_RANK)),
        'mla_w_ukv': nrm(ks[18], (L, KV_RANK, C_HEADS * (C_NOPE + C_V)), KV_RANK ** -0.5),
        'mla_qk_norm': gain(ks[19], (L, 2, C_NOPE + C_ROPE)),
        'mla_out_norm': gain(ks[20], (L, C_WIDTH)),
        'w_out': nrm(ks[21], (L, D_MIX, D_MODEL), D_MIX ** -0.5),
        'norm2_g': gain(ks[22], (L, D_MODEL)),
        'moe_w_group': nrm(ks[23], (L, D_MODEL, N_GROUPS), D_MODEL ** -0.5),
        'moe_b_group': nrm(ks[24], (L, N_GROUPS), 0.01),
        'moe_w_expert': nrm(ks[25], (L, D_MODEL, N_EXPERTS), D_MODEL ** -0.5),
        'moe_b_expert': nrm(ks[26], (L, N_EXPERTS), 0.01),
        'moe_w_gate': nrm(ks[27], (L, N_EXPERTS, D_MODEL, D_EXPERT), D_MODEL ** -0.5),
        'moe_w_up': nrm(ks[28], (L, N_EXPERTS, D_MODEL, D_EXPERT), D_MODEL ** -0.5),
        'moe_w_down': nrm(ks[29], (L, N_EXPERTS, D_EXPERT, D_MODEL), D_EXPERT ** -0.5),
    }


def reference(x, positions, rel_bias, norm1_g, w_in, nsa_cmp_pe, nsa_cmp_w1, nsa_cmp_w2, nsa_q_norm,
              nsa_k_norm, nsa_out_norm, mlstm_conv, mlstm_i_bias, mlstm_f_bias, mlstm_out_norm,
              mla_q_a_norm, mla_w_uq, mla_kv_a_norm, mla_w_ukv, mla_qk_norm, mla_out_norm, w_out,
              norm2_g, moe_w_group, moe_b_group, moe_w_expert, moe_b_expert, moe_w_gate, moe_w_up,
              moe_w_down):
    points = split_points()
    for l in range(DEPTH):
        h = rms_norm(x, norm1_g[l])
        proj = h @ w_in[l]
        a_q, a_kv, a_gate, b_qk, b_v, b_if, b_o, c_dq, c_dkv, c_kr = jnp.split(proj, points, axis=-1)
        y_a = nsa_mixer(a_q, a_kv, a_gate, rel_bias, nsa_cmp_pe[l], nsa_cmp_w1[l], nsa_cmp_w2[l],
                        nsa_q_norm[l], nsa_k_norm[l])
        y_b = mlstm_mixer(b_qk, b_v, b_if, b_o, mlstm_conv[l], mlstm_i_bias[l], mlstm_f_bias[l],
                          mlstm_out_norm[l])
        y_c = mla_mixer(c_dq, c_dkv, c_kr, positions, mla_q_a_norm[l], mla_w_uq[l], mla_kv_a_norm[l],
                        mla_w_ukv[l], mla_qk_norm[l])
        y = jnp.concatenate([rms_norm(y_a, nsa_out_norm[l]), y_b, rms_norm(y_c, mla_out_norm[l])], axis=-1)
        x = x + y @ w_out[l]
        x = x + hier_moe(rms_norm(x, norm2_g[l]), moe_w_group[l], moe_b_group[l], moe_w_expert[l],
                         moe_b_expert[l], moe_w_gate[l], moe_w_up[l], moe_w_down[l])
    return x
```

```python
import functools
import math

import numpy as np
import jax
import jax.numpy as jnp
from jax import lax
from jax.experimental import pallas as pl
from jax.experimental.pallas import tpu as pltpu

F32 = jnp.float32
BF16 = jnp.bfloat16

D_MODEL = 1024
HEAD_DIM = 64
LANES = 128
A_HEADS, A_KV_HEADS, A_GROUP = 6, 2, 3
CMP_BLOCK, CMP_STRIDE, SEL_BLOCK, N_SELECT, WINDOW = 32, 16, 64, 16, 512
FORCE = 1e4
B_HEADS, CONV_K, MLSTM_CHUNK = 4, 4, 64
C_HEADS, C_NOPE, C_ROPE, C_V = 6, 64, 32, 64
Q_RANK, KV_RANK = 192, 128
ROPE_THETA = 10000.0
REL_BUCKETS, REL_MAX_DIST = 32, 128
N_GROUPS, EXPERTS_PER_GROUP, N_EXPERTS, D_EXPERT = 4, 8, 32, 256
EPS = 1e-6
NEG = -1e30

AT = 256
ROW_TILE = 256
VMEM_LIMIT = 48 * 1024 * 1024

SEG_AQ, SEG_AKV, SEG_GATE, SEG_BQK, SEG_BV, SEG_BO = 0, 768, 2304, 2560, 3072, 3328
SEG_CDQ, SEG_CDKV, SEG_CKR, SEG_BIF, D_PROJ = 3584, 3840, 3968, 4096, 4224


def _params(*sem):
    return pltpu.CompilerParams(dimension_semantics=sem, vmem_limit_bytes=VMEM_LIMIT)


def _dot(a, b):
    return jnp.dot(a, b, preferred_element_type=F32)


def _dot_nt(a, b):
    return lax.dot_general(a, b, (((1,), (1,)), ((), ())), preferred_element_type=F32)


def _dot_tn(a, b):
    return lax.dot_general(a, b, (((0,), (0,)), ((), ())), preferred_element_type=F32)


def _split3(x):
    hi = x.astype(BF16)
    r1 = x - hi.astype(F32)
    mid = r1.astype(BF16)
    lo = (r1 - mid.astype(F32)).astype(BF16)
    return hi, mid, lo


def _dot_exact_rhs(x, m_bf16):
    hi, mid, lo = _split3(x)
    return _dot(hi, m_bf16) + _dot(mid, m_bf16) + _dot(lo, m_bf16)


def _dot_f32(x, w):
    xh, xm, xl = _split3(x)
    wh, wm, wl = _split3(w)
    return (_dot(xh, wh) + _dot(xm, wh) + _dot(xh, wm)
            + _dot(xl, wh) + _dot(xm, wm) + _dot(xh, wl))


def _bucket_np(dist):
    n = np.maximum(dist, 0)
    exact = REL_BUCKETS // 2
    nf = np.maximum(n, 1).astype(np.float64)
    large = exact + (np.log(nf / exact) / math.log(REL_MAX_DIST / exact) * (REL_BUCKETS - exact)).astype(np.int64)
    large = np.minimum(large, REL_BUCKETS - 1)
    return np.where(n < exact, n, large).astype(np.int32)


def _bucket_tables(T):
    r = np.arange(AT)[:, None]
    c = np.arange(AT)[None, :]
    diag = np.where(r >= c, _bucket_np(r - c), -1)
    prev = _bucket_np(AT + r - c)
    far = np.where(c > r, _bucket_np(2 * AT + r - c), -1)
    tiles = np.stack([diag, prev, far]).astype(np.int32)
    tq = np.arange(T)[:, None]
    n = np.arange(LANES)[None, :]
    dist_c = tq - (n * CMP_STRIDE + CMP_BLOCK - 1)
    cmp_tbl = np.where(dist_c >= 0, _bucket_np(dist_c), -1).astype(np.int32)
    return tiles, cmp_tbl


def _proj_colmap():
    cm = np.full((D_PROJ,), -1, np.int64)
    for hd in range(A_HEADS):
        cm[SEG_AQ + hd * LANES: SEG_AQ + hd * LANES + HEAD_DIM] = hd * HEAD_DIM + np.arange(HEAD_DIM)
    for s in range(12):
        cm[SEG_AKV + s * LANES: SEG_AKV + s * LANES + HEAD_DIM] = 384 + s * HEAD_DIM + np.arange(HEAD_DIM)
    for h in range(A_KV_HEADS):
        cm[SEG_GATE + h * LANES: SEG_GATE + h * LANES + 9] = 1152 + h * 9 + np.arange(9)
    cm[SEG_BQK: SEG_BQK + 512] = 1170 + np.arange(512)
    cm[SEG_BV: SEG_BV + 256] = 1682 + np.arange(256)
    cm[SEG_BIF: SEG_BIF + 8] = 1938 + np.arange(8)
    cm[SEG_BO: SEG_BO + 256] = 1946 + np.arange(256)
    cm[SEG_CDQ: SEG_CDQ + Q_RANK] = 2202 + np.arange(Q_RANK)
    cm[SEG_CDKV: SEG_CDKV + KV_RANK] = 2394 + np.arange(KV_RANK)
    cm[SEG_CKR + C_NOPE: SEG_CKR + C_NOPE + C_ROPE] = 2522 + np.arange(C_ROPE)
    return cm


def _take_cols(w, colmap):
    wz = jnp.concatenate([w, jnp.zeros((w.shape[0], 1), w.dtype)], axis=1)
    return wz[:, np.where(colmap < 0, w.shape[1], colmap)]


def _take_rows(w, rowmap):
    return _take_cols(w.T, rowmap).T


def _pad_lanes(v, n):
    v = v.reshape(1, -1)
    return jnp.pad(v, ((0, 0), (0, n - v.shape[1])))


def _bias_kernel(rb_ref, bt_ref, bc_ref, tiles_ref, cmp_ref):
    hd = pl.program_id(0)
    bt = bt_ref[...]
    bc = bc_ref[...]
    t = jnp.full(bt.shape, NEG, F32)
    c = jnp.full(bc.shape, NEG, F32)
    for b in range(REL_BUCKETS):
        val = rb_ref[b, hd]
        t = jnp.where(bt == b, val, t)
        c = jnp.where(bc == b, val, c)
    tiles_ref[0, 0] = t[0]
    tiles_ref[0, 1] = t[1]
    tiles_ref[0, 2] = jnp.full((AT, AT), rb_ref[REL_BUCKETS - 1, hd], F32)
    tiles_ref[0, 3] = t[2]
    cmp_ref[0] = c


def _bias_tables(rel_bias, T):
    bt, bc = _bucket_tables(T)
    return pl.pallas_call(
        _bias_kernel,
        grid=(A_HEADS,),
        in_specs=[pl.BlockSpec(memory_space=pltpu.SMEM),
                  pl.BlockSpec((3, AT, AT), lambda h: (0, 0, 0)),
                  pl.BlockSpec((T, LANES), lambda h: (0, 0))],
        out_specs=[pl.BlockSpec((1, 4, AT, AT), lambda h: (h, 0, 0, 0)),
                   pl.BlockSpec((1, T, LANES), lambda h: (h, 0, 0))],
        out_shape=[jax.ShapeDtypeStruct((A_HEADS, 4, AT, AT), F32),
                   jax.ShapeDtypeStruct((A_HEADS, T, LANES), F32)],
        compiler_params=_params("arbitrary"),
        name="bias_tables",
    )(rel_bias, jnp.asarray(bt), jnp.asarray(bc))


def _inproj_kernel(x_ref, g_ref, w_ref, o_ref):
    x = x_ref[...]
    ms = jnp.mean(x * x, axis=-1, keepdims=True)
    h = (x * lax.rsqrt(ms + EPS) * g_ref[...]).astype(BF16)
    for j in range(0, D_PROJ, 384):
        o_ref[:, j:j + 384] = _dot(h, w_ref[:, j:j + 384])


def _inproj(x2, g, w):
    N = x2.shape[0]
    tm = 256
    return pl.pallas_call(
        _inproj_kernel,
        grid=(N // tm,),
        in_specs=[pl.BlockSpec((tm, D_MODEL), lambda i: (i, 0)),
                  pl.BlockSpec((1, D_MODEL), lambda i: (0, 0)),
                  pl.BlockSpec((D_MODEL, D_PROJ), lambda i: (0, 0))],
        out_specs=pl.BlockSpec((tm, D_PROJ), lambda i: (i, 0)),
        out_shape=jax.ShapeDtypeStruct((N, D_PROJ), F32),
        compiler_params=_params("parallel"),
        name="inproj",
    )(x2, g, w)


def _rms_lanes(x, g, width):
    ms = jnp.sum(x * x, axis=-1, keepdims=True) * (1.0 / width)
    return x * lax.rsqrt(ms + EPS) * g


def _flash_step(qs, k_s, v_s, kc, bias_fn, mask, m_r, l_r, acc_r, groups):
    r0 = pl.multiple_of(kc * AT, AT)
    kblk = k_s[pl.ds(r0, AT), :]
    vblk = v_s[pl.ds(r0, AT), :]
    s = _dot_nt(qs, kblk)
    for g in range(groups):
        rows = slice(g * AT, (g + 1) * AT)
        sg = s[rows]
        if bias_fn is not None:
            sg = sg + bias_fn(g)
        if mask is not None:
            sg = jnp.where(mask, sg, NEG)
        mo = m_r[rows]
        mn = jnp.maximum(mo, jnp.max(sg, axis=-1, keepdims=True))
        p = jnp.exp(sg - mn)
        alpha = jnp.exp(mo - mn)
        l_r[rows] = alpha * l_r[rows] + jnp.sum(p, axis=-1, keepdims=True)
        acc_r[rows] = alpha * acc_r[rows] + _dot(p.astype(BF16), vblk)
        m_r[rows] = mn


def _nsa_kernel(q_ref, k0_ref, v0_ref, k1_ref, v1_ref, k2_ref, v2_ref, gate_ref, tiles_ref, biasc_ref,
                e3_ref, ov_ref, w1_ref, w2_ref, pe_ref, qg_ref, kg_ref, o_ref,
                kc_s, vc_s, k1_s, v1_s, k2_s, v2_s, selx_s, m_s, l_s, acc_s, mw_s, lw_s, accw_s):
    qi = pl.program_id(2)
    nq = pl.num_programs(2)
    kg = kg_ref[...]
    n_cmp_pad = kc_s.shape[0]

    @pl.when(qi == 0)
    def _prep():
        k1_s[...] = _rms_lanes(k1_ref[0], kg, HEAD_DIM).astype(BF16)
        v1_s[...] = v1_ref[0].astype(BF16)
        k2_s[...] = _rms_lanes(k2_ref[0], kg, HEAD_DIM).astype(BF16)
        v2_s[...] = v2_ref[0].astype(BF16)
        half = CMP_BLOCK // 2
        for s, (src, dst) in enumerate(((k0_ref, kc_s), (v0_ref, vc_s))):
            a = jnp.zeros((n_cmp_pad, HEAD_DIM), F32)
            bm = jnp.zeros((n_cmp_pad, HEAD_DIM), F32)
            for j in range(half):
                xj = src[0, pl.ds(j, n_cmp_pad, stride=CMP_STRIDE), :][:, :HEAD_DIM]
                a = a + _dot_f32(xj + pe_ref[s, j:j + 1, :], w1_ref[s, j])
                bm = bm + _dot_f32(xj + pe_ref[s, half + j:half + j + 1, :], w1_ref[s, half + j])
            pre = a + pltpu.roll(bm, n_cmp_pad - 1, axis=0)
            hid = pre * jax.nn.sigmoid(pre)
            comp = _dot_f32(hid, w2_ref[s])
            if s == 0:
                comp = _rms_lanes(comp, kg, HEAD_DIM)
            dst[...] = comp.astype(BF16)

    scale = HEAD_DIM ** -0.5
    q = q_ref[0]
    qg = qg_ref[...]
    qs = jnp.concatenate(
        [_rms_lanes(q[:, g * LANES:(g + 1) * LANES], qg, HEAD_DIM) * scale for g in range(A_GROUP)],
        axis=0).astype(BF16)

    biasc = biasc_ref[...].reshape(A_GROUP * AT, LANES)
    valid = biasc > 0.5 * NEG
    lcm = _dot_nt(qs, kc_s[...]) + biasc
    mc = jnp.max(lcm, axis=-1, keepdims=True)
    pc = jnp.where(valid, jnp.exp(lcm - mc), 0.0)
    lc = jnp.sum(pc, axis=-1, keepdims=True)
    pc = pc * jnp.where(lc > 0.0, 1.0 / lc, 0.0)
    o_c = _dot(pc.astype(BF16), vc_s[...])

    ps = _dot_exact_rhs(pc[0:AT] + pc[AT:2 * AT] + pc[2 * AT:3 * AT], ov_ref[...])
    n_sel = ps.shape[1]
    row = lax.broadcasted_iota(jnp.int32, (AT, n_sel), 0)
    sid = lax.broadcasted_iota(jnp.int32, (AT, n_sel), 1)
    cur = qi * (AT // SEL_BLOCK) + lax.shift_right_logical(row, 6)
    forced = (sid == 0) | (sid == cur) | (sid == cur - 1)
    psf = jnp.where(forced, ps + FORCE, ps)
    psf = jnp.where(sid <= cur, psf, -1.0)
    rank = jnp.zeros((AT, n_sel), F32)
    for sp in range(n_sel):
        col = psf[:, sp:sp + 1]
        beats = (col > psf) | ((col == psf) & (sid > sp))
        rank = rank + jnp.where(beats, 1.0, 0.0)
    sel = jnp.where((rank < float(min(N_SELECT, n_sel))) & (psf >= 0.0), 1.0, 0.0).astype(BF16)
    for c in range(selx_s.shape[0]):
        selx_s[c] = _dot(sel, e3_ref[c])

    m_s[...] = jnp.full(m_s.shape, NEG, F32)
    l_s[...] = jnp.zeros(l_s.shape, F32)
    acc_s[...] = jnp.zeros(acc_s.shape, F32)

    def sel_body(kc, carry):
        kind = jnp.minimum(qi - kc, 2)
        _flash_step(qs, k1_s, v1_s, kc, lambda g: tiles_ref[g, kind], selx_s[kc] > 0.5,
                    m_s, l_s, acc_s, A_GROUP)
        return carry

    lax.fori_loop(0, qi + 1, sel_body, 0)

    mw_s[...] = jnp.full(mw_s.shape, NEG, F32)
    lw_s[...] = jnp.zeros(lw_s.shape, F32)
    accw_s[...] = jnp.zeros(accw_s.shape, F32)
    for delta, kind in ((0, 0), (1, 1), (2, 3)):
        @pl.when(qi >= delta)
        def _win(delta=delta, kind=kind):
            _flash_step(qs, k2_s, v2_s, qi - delta, lambda g: tiles_ref[g, kind], None,
                        mw_s, lw_s, accw_s, A_GROUP)

    gt = jax.nn.sigmoid(gate_ref[0])
    for g in range(A_GROUP):
        rows = slice(g * AT, (g + 1) * AT)
        o = (gt[:, 3 * g:3 * g + 1] * o_c[rows]
             + gt[:, 3 * g + 1:3 * g + 2] * (acc_s[rows] / l_s[rows])
             + gt[:, 3 * g + 2:3 * g + 3] * (accw_s[rows] / lw_s[rows]))
        o_ref[0, :, g * LANES:(g + 1) * LANES] = o
    del nq


def _nsa(proj, tiles, biasc, cmp_pe, cmp_w1, cmp_w2, q_norm, k_norm):
    B, T, _ = proj.shape
    nq = T // AT
    n_cmp_pad = T // CMP_STRIDE
    n_sel = T // SEL_BLOCK
    starts = np.arange(n_cmp_pad) * CMP_STRIDE
    sid = np.arange(n_sel)
    overlap = ((starts[:, None] < (sid[None, :] + 1) * SEL_BLOCK)
               & (starts[:, None] + CMP_BLOCK > sid[None, :] * SEL_BLOCK)
               & (starts[:, None] + CMP_BLOCK <= T))
    e3 = (np.arange(T)[None, :] // SEL_BLOCK == sid[:, None]).reshape(n_sel, nq, AT).transpose(1, 0, 2)
    w2p = jnp.pad(cmp_w2, ((0, 0), (0, 0), (0, LANES - HEAD_DIM)))

    def slab(br, kv):
        base = SEG_AKV // LANES + (br * 2 + kv) * 2
        return pl.BlockSpec((1, T, LANES), lambda b, h, i: (b, 0, base + h))

    def full(shape):
        return pl.BlockSpec(shape, lambda b, h, i: (0,) * len(shape))

    rows = A_GROUP * AT
    return pl.pallas_call(
        _nsa_kernel,
        grid=(B, A_KV_HEADS, nq),
        in_specs=[pl.BlockSpec((1, AT, A_GROUP * LANES), lambda b, h, i: (b, i, h)),
                  slab(0, 0), slab(0, 1), slab(1, 0), slab(1, 1), slab(2, 0), slab(2, 1),
                  pl.BlockSpec((1, AT, LANES), lambda b, h, i: (b, i, SEG_GATE // LANES + h)),
                  pl.BlockSpec((A_GROUP, 4, AT, AT), lambda b, h, i: (h, 0, 0, 0)),
                  pl.BlockSpec((A_GROUP, AT, LANES), lambda b, h, i: (h, i, 0)),
                  full((nq, n_sel, AT)), full((n_cmp_pad, n_sel)),
                  full((2, CMP_BLOCK, HEAD_DIM, HEAD_DIM)), full((2, HEAD_DIM, LANES)),
                  full((2, CMP_BLOCK, HEAD_DIM)), full((1, LANES)), full((1, LANES))],
        out_specs=pl.BlockSpec((1, AT, A_GROUP * LANES), lambda b, h, i: (b, i, h)),
        out_shape=jax.ShapeDtypeStruct((B, T, A_HEADS * LANES), F32),
        scratch_shapes=[pltpu.VMEM((n_cmp_pad, LANES), BF16), pltpu.VMEM((n_cmp_pad, LANES), BF16),
                        pltpu.VMEM((T, LANES), BF16), pltpu.VMEM((T, LANES), BF16),
                        pltpu.VMEM((T, LANES), BF16), pltpu.VMEM((T, LANES), BF16),
                        pltpu.VMEM((nq, AT, AT), F32),
                        pltpu.VMEM((rows, 1), F32), pltpu.VMEM((rows, 1), F32), pltpu.VMEM((rows, LANES), F32),
                        pltpu.VMEM((rows, 1), F32), pltpu.VMEM((rows, 1), F32), pltpu.VMEM((rows, LANES), F32)],
        compiler_params=_params("parallel", "parallel", "arbitrary"),
        name="nsa",
    )(proj, proj, proj, proj, proj, proj, proj, proj, tiles, biasc,
      jnp.asarray(e3, BF16), jnp.asarray(overlap, BF16), cmp_w1, w2p, cmp_pe,
      _pad_lanes(q_norm, LANES), _pad_lanes(k_norm, LANES))


def _mlstm_kernel(qk_ref, v_ref, o_ref, gif_ref, cw_ref, gb_ref, gn_ref, tril_ref, y_ref,
                  xpad_s, c_s, n_s, m_s):
    T = qk_ref.shape[1]
    L = MLSTM_CHUNK
    xpad_s[0:8, :] = jnp.zeros((8, xpad_s.shape[1]), F32)
    xpad_s[8:8 + T, :] = qk_ref[0]
    c_s[...] = jnp.zeros(c_s.shape, F32)
    n_s[...] = jnp.zeros(n_s.shape, F32)
    m_s[...] = jnp.zeros(m_s.shape, F32)
    lane = lax.broadcasted_iota(jnp.int32, (1, LANES), 1)
    lo_half = lane < HEAD_DIM
    rowi = lax.broadcasted_iota(jnp.int32, (LANES, LANES), 0)
    coli = lax.broadcasted_iota(jnp.int32, (LANES, LANES), 1)
    blockdiag = (rowi < HEAD_DIM) == (coli < HEAD_DIM)
    tri = lax.broadcasted_iota(jnp.int32, (L, L), 0) >= lax.broadcasted_iota(jnp.int32, (L, L), 1)
    kscale = HEAD_DIM ** -0.5

    def chunk(c, carry):
        r0 = pl.multiple_of(c * L, L)
        win = xpad_s[pl.ds(r0, L + 8), :]
        conv = cw_ref[CONV_K - 1:CONV_K, :] * win[8:8 + L]
        for j in range(CONV_K - 1):
            conv = conv + cw_ref[j:j + 1, :] * pltpu.roll(win, CONV_K - 1 - j, axis=0)[8:8 + L]
        qk = conv * jax.nn.sigmoid(conv)
        gif = gif_ref[0, pl.ds(r0, L), :] + gb_ref[...]
        logf = jax.nn.log_sigmoid(gif)
        lh, lm, ll = _split3(logf)
        tril = tril_ref[...]
        bcum = _dot(tril, lh) + _dot(tril, lm) + _dot(tril, ll)
        gif_t = gif.T
        bcum_t = bcum.T
        for p in range(B_HEADS // 2):
            qp = qk[:, p * LANES:(p + 1) * LANES]
            kp = qk[:, 2 * LANES + p * LANES:2 * LANES + (p + 1) * LANES] * kscale
            vp = v_ref[0, pl.ds(r0, L), p * LANES:(p + 1) * LANES]
            c_old = c_s[p]
            n_old = n_s[p]
            cq = _dot_nt(qp, c_old)
            hval, wcol, dec, mnew = [], [], [], []
            for half in range(2):
                hh = 2 * p + half
                hmask = lo_half if half == 0 else jnp.logical_not(lo_half)
                b_col = bcum[:, B_HEADS + hh:B_HEADS + hh + 1]
                b_row = bcum_t[B_HEADS + hh:B_HEADS + hh + 1, :]
                i_col = gif[:, hh:hh + 1]
                i_row = gif_t[hh:hh + 1, :]
                m_prev = m_s[hh][0:1, 0:1]
                dmat = jnp.where(tri, b_col - b_row + i_row, -jnp.inf)
                m_inter = b_col + m_prev
                m_j = jnp.maximum(m_inter, jnp.max(dmat, axis=-1, keepdims=True))
                isc = jnp.exp(m_inter - m_j)
                qm = jnp.where(hmask, qp, 0.0)
                s = _dot_nt(qm, kp) * jnp.exp(dmat - m_j)
                qn = jnp.sum(qm * n_old, axis=-1, keepdims=True)
                den = jnp.sum(s, axis=-1, keepdims=True) + isc * qn
                num = _dot(s, vp) + isc * cq
                hval.append(num / jnp.maximum(jnp.abs(den), jnp.exp(-m_j)))
                b_last = b_col[L - 1:L, :]
                w_log = b_last - b_col + i_col
                m_new = jnp.maximum(b_last + m_prev, jnp.max(w_log, axis=0, keepdims=True))
                wcol.append(jnp.exp(w_log - m_new))
                dec.append(jnp.exp(b_last + m_prev - m_new))
                mnew.append(m_new)
            w_l = jnp.where(lo_half, wcol[0], wcol[1])
            c_upd = _dot_tn(vp * w_l, kp)
            c_s[p] = (jnp.where(rowi < HEAD_DIM, dec[0], dec[1]) * c_old
                      + jnp.where(blockdiag, c_upd, 0.0))
            n_s[p] = jnp.where(lo_half, dec[0], dec[1]) * n_old + jnp.sum(w_l * kp, axis=0, keepdims=True)
            for half in range(2):
                m_s[2 * p + half] = jnp.broadcast_to(mnew[half], m_s.shape[1:])
            h = jnp.where(lo_half, hval[0], hval[1])
            hsq = h * h
            ms0 = jnp.sum(jnp.where(lo_half, hsq, 0.0), axis=-1, keepdims=True) * (1.0 / HEAD_DIM)
            ms1 = jnp.sum(jnp.where(lo_half, 0.0, hsq), axis=-1, keepdims=True) * (1.0 / HEAD_DIM)
            hn = h * jnp.where(lo_half, lax.rsqrt(ms0 + EPS), lax.rsqrt(ms1 + EPS)) * gn_ref[:, p * LANES:(p + 1) * LANES]
            og = jax.nn.sigmoid(o_ref[0, pl.ds(r0, L), p * LANES:(p + 1) * LANES])
            y_ref[0, pl.ds(r0, L), p * LANES:(p + 1) * LANES] = og * hn
        return carry

    lax.fori_loop(0, T // L, chunk, 0)


def _mlstm(proj, conv_w, i_bias, f_bias, out_norm):
    B, T, _ = proj.shape
    W = B_HEADS * HEAD_DIM
    gb = _pad_lanes(jnp.concatenate([i_bias, f_bias]), LANES)
    tril = jnp.asarray(np.tril(np.ones((MLSTM_CHUNK, MLSTM_CHUNK))), BF16)

    def full(shape):
        return pl.BlockSpec(shape, lambda b: (0,) * len(shape))

    return pl.pallas_call(
        _mlstm_kernel,
        grid=(B,),
        in_specs=[pl.BlockSpec((1, T, 2 * W), lambda b: (b, 0, SEG_BQK // (2 * W))),
                  pl.BlockSpec((1, T, W), lambda b: (b, 0, SEG_BV // W)),
                  pl.BlockSpec((1, T, W), lambda b: (b, 0, SEG_BO // W)),
                  pl.BlockSpec((1, T, LANES), lambda b: (b, 0, SEG_BIF // LANES)),
                  full((CONV_K, 2 * W)), full((1, LANES)), full((1, W)),
                  full((MLSTM_CHUNK, MLSTM_CHUNK))],
        out_specs=pl.BlockSpec((1, T, W), lambda b: (b, 0, 0)),
        out_shape=jax.ShapeDtypeStruct((B, T, W), F32),
        scratch_shapes=[pltpu.VMEM((T + 8, 2 * W), F32),
                        pltpu.VMEM((B_HEADS // 2, LANES, LANES), F32),
                        pltpu.VMEM((B_HEADS // 2, 1, LANES), F32),
                        pltpu.VMEM((B_HEADS, 8, LANES), F32)],
        compiler_params=_params("parallel"),
        name="mlstm",
    )(proj, proj, proj, proj, conv_w, gb, out_norm.reshape(1, W), tril)


def _mla_prep_kernel(dq_ref, dkv_ref, kr_ref, pos_ref, wq_ref, wk_ref, wv_ref, qa_ref, kva_ref,
                     gq_ref, gk_ref, gkr_ref, invf_ref, sgn_ref, q_out, k_out, v_out):
    lane = lax.broadcasted_iota(jnp.int32, (1, LANES), 1)
    nope = lane < C_NOPE
    ropem = (lane >= C_NOPE) & (lane < C_NOPE + C_ROPE)
    first = lane < C_NOPE + C_ROPE // 2
    ang = pos_ref[...].astype(F32) * invf_ref[...]
    cs = jnp.cos(ang)
    sn = jnp.sin(ang) * sgn_ref[...]

    def rope(x):
        partner = jnp.where(first, pltpu.roll(x, LANES - C_ROPE // 2, axis=1), pltpu.roll(x, C_ROPE // 2, axis=1))
        return x * cs + partner * sn

    scale = (C_NOPE + C_ROPE) ** -0.5
    qn = _rms_lanes(dq_ref[...], qa_ref[...], Q_RANK).astype(BF16)
    q = _dot(qn, wq_ref[...])
    kvn = _rms_lanes(dkv_ref[...], kva_ref[...], KV_RANK).astype(BF16)
    kk = _dot(kvn, wk_ref[...])
    v_out[...] = _dot(kvn, wv_ref[...]).astype(BF16)
    kr = kr_ref[...]
    krn = kr * lax.rsqrt(jnp.sum(kr * kr, axis=-1, keepdims=True) * (1.0 / C_ROPE) + EPS) * gkr_ref[...]
    krr = rope(krn)
    for h in range(C_HEADS):
        sl = slice(h * LANES, (h + 1) * LANES)
        qh = q[:, sl]
        sq = qh * qh
        msn = jnp.sum(jnp.where(nope, sq, 0.0), axis=-1, keepdims=True) * (1.0 / C_NOPE)
        msr = jnp.sum(jnp.where(ropem, sq, 0.0), axis=-1, keepdims=True) * (1.0 / C_ROPE)
        qhn = qh * jnp.where(nope, lax.rsqrt(msn + EPS), lax.rsqrt(msr + EPS)) * gq_ref[...]
        q_out[:, sl] = (rope(qhn) * scale).astype(BF16)
        kh = kk[:, sl]
        msk = jnp.sum(kh * kh, axis=-1, keepdims=True) * (1.0 / C_NOPE)
        k_out[:, sl] = (kh * lax.rsqrt(msk + EPS) * gk_ref[...] + krr).astype(BF16)


def _mla_prep(proj2, pos, q_a_norm, w_uq, kv_a_norm, w_ukv, qk_norm):
    N = proj2.shape[0]
    tm = 512
    qd = C_NOPE + C_ROPE
    qmap = np.full((C_HEADS * LANES,), -1, np.int64)
    kmap = np.full((C_HEADS * LANES,), -1, np.int64)
    vmap = np.zeros((C_HEADS * C_V,), np.int64)
    for h in range(C_HEADS):
        qmap[h * LANES:h * LANES + qd] = h * qd + np.arange(qd)
        kmap[h * LANES:h * LANES + C_NOPE] = h * (C_NOPE + C_V) + np.arange(C_NOPE)
        vmap[h * C_V:(h + 1) * C_V] = h * (C_NOPE + C_V) + C_NOPE + np.arange(C_V)
    wq = jnp.pad(_take_cols(w_uq, qmap), ((0, 256 - Q_RANK), (0, 0))).astype(BF16)
    wk = _take_cols(w_ukv, kmap).astype(BF16)
    wv = _take_cols(w_ukv, vmap).astype(BF16)
    half = C_ROPE // 2
    inv = 1.0 / (ROPE_THETA ** (np.arange(half, dtype=np.float32) * 2.0 / C_ROPE))
    invf = np.zeros((1, LANES), np.float32)
    sgn = np.zeros((1, LANES), np.float32)
    invf[0, C_NOPE:C_NOPE + C_ROPE] = np.concatenate([inv, inv])
    sgn[0, C_NOPE:C_NOPE + half] = -1.0
    sgn[0, C_NOPE + half:C_NOPE + C_ROPE] = 1.0
    gkr = jnp.pad(qk_norm[1, C_NOPE:].reshape(1, -1), ((0, 0), (C_NOPE, LANES - C_NOPE - C_ROPE)))

    def full(shape):
        return pl.BlockSpec(shape, lambda i: (0,) * len(shape))

    return pl.pallas_call(
        _mla_prep_kernel,
        grid=(N // tm,),
        in_specs=[pl.BlockSpec((tm, 256), lambda i: (i, SEG_CDQ // 256)),
                  pl.BlockSpec((tm, LANES), lambda i: (i, SEG_CDKV // LANES)),
                  pl.BlockSpec((tm, LANES), lambda i: (i, SEG_CKR // LANES)),
                  pl.BlockSpec((tm, 1), lambda i: (i, 0)),
                  full((256, C_HEADS * LANES)), full((KV_RANK, C_HEADS * LANES)), full((KV_RANK, C_HEADS * C_V)),
                  full((1, 256)), full((1, LANES)), full((1, LANES)), full((1, LANES)), full((1, LANES)),
                  full((1, LANES)), full((1, LANES))],
        out_specs=[pl.BlockSpec((tm, C_HEADS * LANES), lambda i: (i, 0)),
                   pl.BlockSpec((tm, C_HEADS * LANES), lambda i: (i, 0)),
                   pl.BlockSpec((tm, C_HEADS * C_V), lambda i: (i, 0))],
        out_shape=[jax.ShapeDtypeStruct((N, C_HEADS * LANES), BF16),
                   jax.ShapeDtypeStruct((N, C_HEADS * LANES), BF16),
                   jax.ShapeDtypeStruct((N, C_HEADS * C_V), BF16)],
        compiler_params=_params("parallel"),
        name="mla_prep",
    )(proj2, proj2, proj2, pos, wq, wk, wv,
      _pad_lanes(q_a_norm, 256), kv_a_norm.reshape(1, -1), _pad_lanes(qk_norm[0], LANES),
      _pad_lanes(qk_norm[1, :C_NOPE], LANES), gkr, jnp.asarray(invf), jnp.asarray(sgn))


def _mla_attn_kernel(q_ref, k_ref, v_ref, o_ref, m_s, l_s, acc_s):
    qi = pl.program_id(2)
    r = lax.broadcasted_iota(jnp.int32, (AT, AT), 0)
    c = lax.broadcasted_iota(jnp.int32, (AT, AT), 1)
    causal = r >= c
    outs = []
    for hh in range(2):
        sl = slice(hh * LANES, (hh + 1) * LANES)
        qs = q_ref[0, :, sl]
        m_s[...] = jnp.full(m_s.shape, NEG, F32)
        l_s[...] = jnp.zeros(l_s.shape, F32)
        acc_s[...] = jnp.zeros(acc_s.shape, F32)
        k_h = k_ref.at[0, :, sl]
        v_h = v_ref.at[0]
        _flash_step(qs, k_h, v_h, qi, None, causal, m_s, l_s, acc_s, 1)

        def body(kc, carry):
            _flash_step(qs, k_h, v_h, kc, None, None, m_s, l_s, acc_s, 1)
            return carry

        lax.fori_loop(0, qi, body, 0)
        outs.append(acc_s[...] / l_s[...])
    lane = lax.broadcasted_iota(jnp.int32, (1, LANES), 1)
    o_ref[0] = jnp.where(lane < C_V, outs[0], outs[1])


def _mla_attn(q, k, v):
    B, T, _ = q.shape
    return pl.pallas_call(
        _mla_attn_kernel,
        grid=(B, C_HEADS // 2, T // AT),
        in_specs=[pl.BlockSpec((1, AT, 2 * LANES), lambda b, p, i: (b, i, p)),
                  pl.BlockSpec((1, T, 2 * LANES), lambda b, p, i: (b, 0, p)),
                  pl.BlockSpec((1, T, LANES), lambda b, p, i: (b, 0, p))],
        out_specs=pl.BlockSpec((1, AT, LANES), lambda b, p, i: (b, i, p)),
        out_shape=jax.ShapeDtypeStruct((B, T, C_HEADS * C_V), F32),
        scratch_shapes=[pltpu.VMEM((AT, 1), F32), pltpu.VMEM((AT, 1), F32), pltpu.VMEM((AT, LANES), F32)],
        compiler_params=_params("parallel", "parallel", "arbitrary"),
        name="mla_attn",
    )(q, k, v)


def _outproj_kernel(x_ref, ya_ref, yb_ref, yc_ref, wa_ref, wb_ref, wc_ref, ga_ref, gc_ref, g2_ref,
                    wr_ref, br_ref, x1_ref, h2_ref, ids_ref, wts_ref):
    ya = _rms_lanes(ya_ref[...], ga_ref[...], A_HEADS * HEAD_DIM).astype(BF16)
    yc = _rms_lanes(yc_ref[...], gc_ref[...], C_HEADS * C_V).astype(BF16)
    x1 = (x_ref[...] + _dot(ya, wa_ref[...]) + _dot(yb_ref[...].astype(BF16), wb_ref[...])
          + _dot(yc, wc_ref[...]))
    x1_ref[...] = x1
    h2 = _rms_lanes(x1, g2_ref[...], D_MODEL)
    h2_ref[...] = h2
    logits = _dot_f32(h2, wr_ref[...]) + br_ref[...]
    tm = logits.shape[0]
    lane = lax.broadcasted_iota(jnp.int32, (tm, LANES), 1)
    is_g = (lane >= N_EXPERTS) & (lane < N_EXPERTS + N_GROUPS)
    gl = jnp.where(is_g, logits, NEG)
    ge = jnp.where(is_g, jnp.exp(gl - jnp.max(gl, axis=-1, keepdims=True)), 0.0)
    gp = jnp.where(is_g, ge / jnp.sum(ge, axis=-1, keepdims=True), -1.0)
    g_top = jnp.max(gp, axis=-1, keepdims=True)
    g_idx = jnp.min(jnp.where(gp == g_top, lane, 2 * LANES), axis=-1, keepdims=True) - N_EXPERTS
    in_grp = (lane < N_EXPERTS) & (lax.shift_right_logical(lane, 3) == g_idx)
    el = jnp.where(in_grp, logits, NEG)
    ee = jnp.where(in_grp, jnp.exp(el - jnp.max(el, axis=-1, keepdims=True)), 0.0)
    ep = jnp.where(in_grp, ee / jnp.sum(ee, axis=-1, keepdims=True), -1.0)
    v1 = jnp.max(ep, axis=-1, keepdims=True)
    i1 = jnp.min(jnp.where(ep == v1, lane, 2 * LANES), axis=-1, keepdims=True)
    ep2 = jnp.where(lane == i1, -1.0, ep)
    v2 = jnp.max(ep2, axis=-1, keepdims=True)
    i2 = jnp.min(jnp.where(ep2 == v2, lane, 2 * LANES), axis=-1, keepdims=True)
    tot = v1 + v2
    ids_ref[...] = jnp.where(lane == 0, i1, jnp.where(lane == 1, i2, 0))
    wts_ref[...] = jnp.where(lane == 0, v1 / tot * g_top, jnp.where(lane == 1, v2 / tot * g_top, 0.0))


def _outproj(x2, ya, yb, yc, w_out, nsa_out_norm, mla_out_norm, norm2_g, w_group, b_group, w_expert, b_expert):
    N = x2.shape[0]
    tm = 256
    amap = np.full((A_HEADS * LANES,), -1, np.int64)
    for hd in range(A_HEADS):
        amap[hd * LANES:hd * LANES + HEAD_DIM] = hd * HEAD_DIM + np.arange(HEAD_DIM)
    wa = _take_rows(w_out[:384], amap).astype(BF16)
    wb = w_out[384:640].astype(BF16)
    wc = w_out[640:].astype(BF16)
    ga = _take_cols(nsa_out_norm.reshape(1, -1), amap)
    wr = jnp.pad(jnp.concatenate([w_expert, w_group], axis=1), ((0, 0), (0, LANES - N_EXPERTS - N_GROUPS)))
    br = _pad_lanes(jnp.concatenate([b_expert, b_group]), LANES)

    def full(shape):
        return pl.BlockSpec(shape, lambda i: (0,) * len(shape))

    def rows(w):
        return pl.BlockSpec((tm, w), lambda i: (i, 0))

    return pl.pallas_call(
        _outproj_kernel,
        grid=(N // tm,),
        in_specs=[rows(D_MODEL), rows(A_HEADS * LANES), rows(256), rows(384),
                  full((A_HEADS * LANES, D_MODEL)), full((256, D_MODEL)), full((384, D_MODEL)),
                  full((1, A_HEADS * LANES)), full((1, 384)), full((1, D_MODEL)),
                  full((D_MODEL, LANES)), full((1, LANES))],
        out_specs=[rows(D_MODEL), rows(D_MODEL), rows(LANES), rows(LANES)],
        out_shape=[jax.ShapeDtypeStruct((N, D_MODEL), F32), jax.ShapeDtypeStruct((N, D_MODEL), F32),
                   jax.ShapeDtypeStruct((N, LANES), jnp.int32), jax.ShapeDtypeStruct((N, LANES), F32)],
        compiler_params=_params("parallel"),
        name="outproj_router",
    )(x2, ya, yb, yc, wa, wb, wc, ga, mla_out_norm.reshape(1, -1), norm2_g.reshape(1, -1), wr, br)


def _rank_kernel(ids_ref, tril_ref, rank_ref, cnt_ref, carry_s):
    i = pl.program_id(0)

    @pl.when(i == 0)
    def _():
        carry_s[...] = jnp.zeros(carry_s.shape, F32)

    ids = ids_ref[...]
    tm = ids.shape[0]
    lane = lax.broadcasted_iota(jnp.int32, (tm, LANES), 1)
    i1 = ids[:, 0:1]
    i2 = ids[:, 1:2]
    oh = jnp.where((lane == i1) | (lane == i2), 1.0, 0.0)
    incl = _dot(tril_ref[...], oh.astype(BF16)) + carry_s[0:1, :]
    excl = incl - oh
    r1 = jnp.sum(jnp.where(lane == i1, excl, 0.0), axis=-1, keepdims=True)
    r2 = jnp.sum(jnp.where(lane == i2, excl, 0.0), axis=-1, keepdims=True)
    rank_ref[...] = jnp.where(lane == 0, r1, jnp.where(lane == 1, r2, 0.0)).astype(jnp.int32)
    tot = incl[tm - 1:tm, :]
    carry_s[...] = jnp.broadcast_to(tot, carry_s.shape)
    cnt_ref[...] = jnp.broadcast_to(tot, cnt_ref.shape).astype(jnp.int32)


def _moe_rank(ids):
    N = ids.shape[0]
    tm = 512
    tril = jnp.asarray(np.tril(np.ones((tm, tm))), BF16)
    return pl.pallas_call(
        _rank_kernel,
        grid=(N // tm,),
        in_specs=[pl.BlockSpec((tm, LANES), lambda i: (i, 0)), pl.BlockSpec((tm, tm), lambda i: (0, 0))],
        out_specs=[pl.BlockSpec((tm, LANES), lambda i: (i, 0)), pl.BlockSpec((8, LANES), lambda i: (0, 0))],
        out_shape=[jax.ShapeDtypeStruct((N, LANES), jnp.int32), jax.ShapeDtypeStruct((8, LANES), jnp.int32)],
        scratch_shapes=[pltpu.VMEM((8, LANES), F32)],
        compiler_params=_params("arbitrary"),
        name="moe_rank",
    )(ids, tril)


DISPATCH_TM = 1024


def _dispatch_kernel(off_ref, id1_ref, id2_ref, r1_ref, r2_ref, h_ref, zeros_ref, xs_ref, sem):
    del zeros_ref

    def row_copy(i, pos):
        return pltpu.make_async_copy(h_ref.at[pl.ds(i, 1), :], xs_ref.at[pl.ds(pos, 1), :], sem)

    def issue(i, carry):
        row_copy(i, off_ref[id1_ref[i]] + r1_ref[i]).start()
        row_copy(i, off_ref[id2_ref[i]] + r2_ref[i]).start()
        return carry

    lax.fori_loop(0, DISPATCH_TM, issue, 0)

    def drain(i, carry):
        row_copy(i, 0).wait()
        row_copy(i, 0).wait()
        return carry

    lax.fori_loop(0, DISPATCH_TM, drain, 0)


def _moe_dispatch(h2, off, id1, id2, r1, r2, n_rows):
    N = h2.shape[0]
    smem1d = pl.BlockSpec((DISPATCH_TM,), lambda i: (i,), memory_space=pltpu.SMEM)
    return pl.pallas_call(
        _dispatch_kernel,
        grid=(N // DISPATCH_TM,),
        in_specs=[pl.BlockSpec(memory_space=pltpu.SMEM), smem1d, smem1d, smem1d, smem1d,
                  pl.BlockSpec((DISPATCH_TM, D_MODEL), lambda i: (i, 0)),
                  pl.BlockSpec(memory_space=pl.ANY)],
        out_specs=pl.BlockSpec(memory_space=pl.ANY),
        out_shape=jax.ShapeDtypeStruct((n_rows, D_MODEL), F32),
        scratch_shapes=[pltpu.SemaphoreType.DMA(())],
        input_output_aliases={6: 0},
        compiler_params=_params("arbitrary"),
        name="moe_dispatch",
    )(off, id1, id2, r1, r2, h2, jnp.zeros((n_rows, D_MODEL), F32))


def _expert_kernel(te_ref, nu_ref, x_ref, wg_ref, wu_ref, wd_ref, y_ref):
    g = pl.program_id(0)

    @pl.when(g < nu_ref[0])
    def _():
        x = x_ref[...].astype(BF16)
        a = _dot(x, wg_ref[0].astype(BF16))
        u = _dot(x, wu_ref[0].astype(BF16))
        h = (a * jax.nn.sigmoid(a) * u).astype(BF16)
        y_ref[...] = _dot(h, wd_ref[0].astype(BF16))

    @pl.when(g >= nu_ref[0])
    def _():
        y_ref[...] = jnp.zeros(y_ref.shape, F32)


def _moe_experts(xs, tile_expert, n_used, w_gate, w_up, w_down):
    R = xs.shape[0]
    G = R // ROW_TILE

    def row_map(g, te, nu):
        return (jnp.minimum(g, nu[0] - 1), 0)

    def w_map(g, te, nu):
        return (te[jnp.minimum(g, nu[0] - 1)], 0, 0)

    return pl.pallas_call(
        _expert_kernel,
        grid_spec=pltpu.PrefetchScalarGridSpec(
            num_scalar_prefetch=2,
            grid=(G,),
            in_specs=[pl.BlockSpec((ROW_TILE, D_MODEL), row_map),
                      pl.BlockSpec((1, D_MODEL, D_EXPERT), w_map),
                      pl.BlockSpec((1, D_MODEL, D_EXPERT), w_map),
                      pl.BlockSpec((1, D_EXPERT, D_MODEL), w_map)],
            out_specs=pl.BlockSpec((ROW_TILE, D_MODEL), lambda g, te, nu: (g, 0))),
        out_shape=jax.ShapeDtypeStruct((R, D_MODEL), F32),
        compiler_params=_params("arbitrary"),
        name="moe_experts",
    )(tile_expert, n_used, xs, w_gate, w_up, w_down)


COMBINE_TM = 256


def _combine_kernel(off_ref, id1_ref, id2_ref, r1_ref, r2_ref, x1_ref, wts_ref, ys_ref, o_ref, b1_s, b2_s, sem):
    def row_copy(buf, i, pos):
        return pltpu.make_async_copy(ys_ref.at[pl.ds(pos, 1), :], buf.at[pl.ds(i, 1), :], sem)

    def issue(i, carry):
        row_copy(b1_s, i, off_ref[id1_ref[i]] + r1_ref[i]).start()
        row_copy(b2_s, i, off_ref[id2_ref[i]] + r2_ref[i]).start()
        return carry

    lax.fori_loop(0, COMBINE_TM, issue, 0)

    def drain(i, carry):
        row_copy(b1_s, i, 0).wait()
        row_copy(b2_s, i, 0).wait()
        return carry

    lax.fori_loop(0, COMBINE_TM, drain, 0)
    w = wts_ref[...]
    o_ref[...] = x1_ref[...] + w[:, 0:1] * b1_s[...] + w[:, 1:2] * b2_s[...]


def _moe_combine(x1, wts, ys, off, id1, id2, r1, r2):
    N = x1.shape[0]
    smem1d = pl.BlockSpec((COMBINE_TM,), lambda i: (i,), memory_space=pltpu.SMEM)
    return pl.pallas_call(
        _combine_kernel,
        grid=(N // COMBINE_TM,),
        in_specs=[pl.BlockSpec(memory_space=pltpu.SMEM), smem1d, smem1d, smem1d, smem1d,
                  pl.BlockSpec((COMBINE_TM, D_MODEL), lambda i: (i, 0)),
                  pl.BlockSpec((COMBINE_TM, LANES), lambda i: (i, 0)),
                  pl.BlockSpec(memory_space=pl.ANY)],
        out_specs=pl.BlockSpec((COMBINE_TM, D_MODEL), lambda i: (i, 0)),
        out_shape=jax.ShapeDtypeStruct((N, D_MODEL), F32),
        scratch_shapes=[pltpu.VMEM((COMBINE_TM, D_MODEL), F32), pltpu.VMEM((COMBINE_TM, D_MODEL), F32),
                        pltpu.SemaphoreType.DMA(())],
        compiler_params=_params("arbitrary"),
        name="moe_combine",
    )(off, id1, id2, r1, r2, x1, wts, ys)


def _moe(x1, h2, ids, wts, w_gate, w_up, w_down):
    N = x1.shape[0]
    n_rows = 2 * N + N_EXPERTS * ROW_TILE
    rank, cnt = _moe_rank(ids)
    counts = cnt[0, :N_EXPERTS]
    tiles_per = (counts + ROW_TILE - 1) // ROW_TILE
    tile_end = jnp.cumsum(tiles_per)
    off = ((tile_end - tiles_per) * ROW_TILE).astype(jnp.int32)
    n_used = tile_end[-1:].astype(jnp.int32)
    tile_expert = jnp.minimum(
        jnp.searchsorted(tile_end, jnp.arange(n_rows // ROW_TILE), side="right"), N_EXPERTS - 1).astype(jnp.int32)
    id1, id2, r1, r2 = ids[:, 0], ids[:, 1], rank[:, 0], rank[:, 1]
    xs = _moe_dispatch(h2, off, id1, id2, r1, r2, n_rows)
    ys = _moe_experts(xs, tile_expert, n_used, w_gate, w_up, w_down)
    return _moe_combine(x1, wts, ys, off, id1, id2, r1, r2)


def kernel(x, positions, rel_bias, norm1_g, w_in, nsa_cmp_pe, nsa_cmp_w1, nsa_cmp_w2, nsa_q_norm, nsa_k_norm, nsa_out_norm, mlstm_conv, mlstm_i_bias, mlstm_f_bias, mlstm_out_norm, mla_q_a_norm, mla_w_uq, mla_kv_a_norm, mla_w_ukv, mla_qk_norm, mla_out_norm, w_out, norm2_g, moe_w_group, moe_b_group, moe_w_expert, moe_b_expert, moe_w_gate, moe_w_up, moe_w_down):
    B, T, D = x.shape
    N = B * T
    depth = w_in.shape[0]
    colmap = _proj_colmap()
    tiles, biasc = _bias_tables(rel_bias, T)
    pos = positions.reshape(N, 1)
    x2 = x.reshape(N, D)
    for l in range(depth):
        w_proj = _take_cols(w_in[l], colmap).astype(BF16)
        proj2 = _inproj(x2, norm1_g[l].reshape(1, D), w_proj)
        proj = proj2.reshape(B, T, D_PROJ)
        ya = _nsa(proj, tiles, biasc, nsa_cmp_pe[l], nsa_cmp_w1[l], nsa_cmp_w2[l], nsa_q_norm[l], nsa_k_norm[l])
        yb = _mlstm(proj, mlstm_conv[l], mlstm_i_bias[l], mlstm_f_bias[l], mlstm_out_norm[l])
        q, k, v = _mla_prep(proj2, pos, mla_q_a_norm[l], mla_w_uq[l], mla_kv_a_norm[l], mla_w_ukv[l], mla_qk_norm[l])
        yc = _mla_attn(q.reshape(B, T, -1), k.reshape(B, T, -1), v.reshape(B, T, -1))
        x1, h2, ids, wts = _outproj(x2, ya.reshape(N, -1), yb.reshape(N, -1), yc.reshape(N, -1), w_out[l],
                                    nsa_out_norm[l], mla_out_norm[l], norm2_g[l], moe_w_group[l], moe_b_group[l],
                                    moe_w_expert[l], moe_b_expert[l])
        x2 = _moe(x1, h2, ids, wts, moe_w_gate[l], moe_w_up[l], moe_w_down[l])
    return x2.reshape(B, T, D)
```

```python
import functools
import math

import numpy as np
import jax
import jax.numpy as jnp
from jax import lax
from jax.experimental import pallas as pl
from jax.experimental.pallas import tpu as pltpu

F32 = jnp.float32
BF16 = jnp.bfloat16

D_MODEL = 1024
HEAD_DIM = 64
LANES = 128
A_HEADS, A_KV_HEADS, A_GROUP = 6, 2, 3
CMP_BLOCK, CMP_STRIDE, SEL_BLOCK, N_SELECT, WINDOW = 32, 16, 64, 16, 512
FORCE = 1e4
B_HEADS, CONV_K, MLSTM_CHUNK = 4, 4, 64
C_HEADS, C_NOPE, C_ROPE, C_V = 6, 64, 32, 64
Q_RANK, KV_RANK = 192, 128
ROPE_THETA = 10000.0
REL_BUCKETS, REL_MAX_DIST = 32, 128
N_GROUPS, EXPERTS_PER_GROUP, N_EXPERTS, D_EXPERT = 4, 8, 32, 256
EPS = 1e-6
NEG = -1e30

LOG2E = 1.4426950408889634

AT = 256
RB = 128
ROW_TILE = 256
VMEM_LIMIT = 48 * 1024 * 1024

SEG_AQ, SEG_AKV, SEG_GATE, SEG_BQK, SEG_BV, SEG_BO = 0, 768, 2304, 2560, 3072, 3328
SEG_CDQ, SEG_CDKV, SEG_CKR, SEG_BIF, D_PROJ = 3584, 3840, 3968, 4096, 4224


def _params(*sem):
    return pltpu.CompilerParams(dimension_semantics=sem, vmem_limit_bytes=VMEM_LIMIT)


def _dot(a, b):
    return jnp.dot(a, b, preferred_element_type=F32)


def _dot_nt(a, b):
    return lax.dot_general(a, b, (((1,), (1,)), ((), ())), preferred_element_type=F32)


def _dot_tn(a, b):
    return lax.dot_general(a, b, (((0,), (0,)), ((), ())), preferred_element_type=F32)


def _split3(x):
    hi = x.astype(BF16)
    r1 = x - hi.astype(F32)
    mid = r1.astype(BF16)
    lo = (r1 - mid.astype(F32)).astype(BF16)
    return hi, mid, lo


def _dot_exact_rhs(x, m_bf16):
    hi, mid, lo = _split3(x)
    return _dot(hi, m_bf16) + _dot(mid, m_bf16) + _dot(lo, m_bf16)


def _dot_f32(x, w):
    xh, xm, xl = _split3(x)
    wh, wm, wl = _split3(w)
    return (_dot(xh, wh) + _dot(xm, wh) + _dot(xh, wm)
            + _dot(xl, wh) + _dot(xm, wm) + _dot(xh, wl))


def _bucket_np(dist):
    n = np.maximum(dist, 0)
    exact = REL_BUCKETS // 2
    nf = np.maximum(n, 1).astype(np.float64)
    large = exact + (np.log(nf / exact) / math.log(REL_MAX_DIST / exact) * (REL_BUCKETS - exact)).astype(np.int64)
    large = np.minimum(large, REL_BUCKETS - 1)
    return np.where(n < exact, n, large).astype(np.int32)


def _bucket_tables(T):
    r = np.arange(AT)[:, None]
    c = np.arange(AT)[None, :]
    diag = np.where(r >= c, _bucket_np(r - c), -1)
    prev = _bucket_np(AT + r - c)
    far = np.where(c > r, _bucket_np(2 * AT + r - c), -1)
    tiles = np.stack([diag, prev, far]).astype(np.int32)
    tq = np.arange(T)[:, None]
    n = np.arange(LANES)[None, :]
    dist_c = tq - (n * CMP_STRIDE + CMP_BLOCK - 1)
    cmp_tbl = np.where(dist_c >= 0, _bucket_np(dist_c), -1).astype(np.int32)
    return tiles, cmp_tbl


def _proj_colmap():
    cm = np.full((D_PROJ,), -1, np.int64)
    for hd in range(A_HEADS):
        cm[SEG_AQ + hd * LANES: SEG_AQ + hd * LANES + HEAD_DIM] = hd * HEAD_DIM + np.arange(HEAD_DIM)
    for s in range(12):
        cm[SEG_AKV + s * LANES: SEG_AKV + s * LANES + HEAD_DIM] = 384 + s * HEAD_DIM + np.arange(HEAD_DIM)
    for h in range(A_KV_HEADS):
        cm[SEG_GATE + h * LANES: SEG_GATE + h * LANES + 9] = 1152 + h * 9 + np.arange(9)
    cm[SEG_BQK: SEG_BQK + 512] = 1170 + np.arange(512)
    cm[SEG_BV: SEG_BV + 256] = 1682 + np.arange(256)
    cm[SEG_BIF: SEG_BIF + 8] = 1938 + np.arange(8)
    cm[SEG_BO: SEG_BO + 256] = 1946 + np.arange(256)
    cm[SEG_CDQ: SEG_CDQ + Q_RANK] = 2202 + np.arange(Q_RANK)
    cm[SEG_CDKV: SEG_CDKV + KV_RANK] = 2394 + np.arange(KV_RANK)
    cm[SEG_CKR + C_NOPE: SEG_CKR + C_NOPE + C_ROPE] = 2522 + np.arange(C_ROPE)
    return cm


def _take_cols(w, colmap):
    wz = jnp.concatenate([w, jnp.zeros((w.shape[0], 1), w.dtype)], axis=1)
    return wz[:, np.where(colmap < 0, w.shape[1], colmap)]


def _take_rows(w, rowmap):
    return _take_cols(w.T, rowmap).T


def _pad_lanes(v, n):
    v = v.reshape(1, -1)
    return jnp.pad(v, ((0, 0), (0, n - v.shape[1])))


def _bias_kernel(rb_ref, bt_ref, bc_ref, tiles_ref, cmp_ref):
    hd = pl.program_id(0)
    bt = bt_ref[...]
    bc = bc_ref[...]
    t = jnp.full(bt.shape, NEG, F32)
    c = jnp.full(bc.shape, NEG, F32)
    far = rb_ref[REL_BUCKETS - 1, hd]
    for b in range(REL_BUCKETS):
        val = rb_ref[b, hd]
        t = jnp.where(bt == b, (val - far) * LOG2E, t)
        c = jnp.where(bc == b, val * LOG2E, c)
    tiles_ref[0, 0:3] = t
    tiles_ref[0, 3] = jnp.full((AT, AT), NEG, F32)
    cmp_ref[0] = c


def _bias_tables(rel_bias, T):
    bt, bc = _bucket_tables(T)
    return pl.pallas_call(
        _bias_kernel,
        grid=(A_HEADS,),
        in_specs=[pl.BlockSpec(memory_space=pltpu.SMEM),
                  pl.BlockSpec((3, AT, AT), lambda h: (0, 0, 0)),
                  pl.BlockSpec((T, LANES), lambda h: (0, 0))],
        out_specs=[pl.BlockSpec((1, 4, AT, AT), lambda h: (h, 0, 0, 0)),
                   pl.BlockSpec((1, T, LANES), lambda h: (h, 0, 0))],
        out_shape=[jax.ShapeDtypeStruct((A_HEADS, 4, AT, AT), F32),
                   jax.ShapeDtypeStruct((A_HEADS, T, LANES), F32)],
        compiler_params=_params("arbitrary"),
        name="bias_tables",
    )(rel_bias, jnp.asarray(bt), jnp.asarray(bc))


def _inproj_kernel(x_ref, g_ref, w_ref, o_ref):
    x = x_ref[...]
    ms = jnp.mean(x * x, axis=-1, keepdims=True)
    h = (x * lax.rsqrt(ms + EPS) * g_ref[...]).astype(BF16)
    for j in range(0, D_PROJ, 384):
        o_ref[:, j:j + 384] = _dot(h, w_ref[:, j:j + 384])


def _inproj(x2, g, w):
    N = x2.shape[0]
    tm = 256
    return pl.pallas_call(
        _inproj_kernel,
        grid=(N // tm,),
        in_specs=[pl.BlockSpec((tm, D_MODEL), lambda i: (i, 0)),
                  pl.BlockSpec((1, D_MODEL), lambda i: (0, 0)),
                  pl.BlockSpec((D_MODEL, D_PROJ), lambda i: (0, 0))],
        out_specs=pl.BlockSpec((tm, D_PROJ), lambda i: (i, 0)),
        out_shape=jax.ShapeDtypeStruct((N, D_PROJ), F32),
        compiler_params=_params("parallel"),
        name="inproj",
    )(x2, g, w)


def _rms_lanes(x, g, width):
    ms = jnp.sum(x * x, axis=-1, keepdims=True) * (1.0 / width)
    return x * lax.rsqrt(ms + EPS) * g


def _qk_halves(q_r, k_blk):
    half_rows = q_r.shape[0] // 2
    return [_dot_nt(q_r[h0:h0 + half_rows, :], k_blk) for h0 in range(0, q_r.shape[0], half_rows)]


def _flash_chunk(q_r, k_blk, v_blk, bias_fn, m_r, acc_r):
    _softmax_pv(_qk_halves(q_r, k_blk), v_blk, bias_fn, m_r, acc_r)


def _softmax_pv(s_halves, v_blk, bias_fn, m_r, acc_r):
    half_rows = m_r.shape[0] // 2
    for hi_, h0 in enumerate(range(0, m_r.shape[0], half_rows)):
        s_all = s_halves[hi_]
        ps, alphas = [], []
        for r in range(half_rows // RB):
            rows = slice(h0 + r * RB, h0 + (r + 1) * RB)
            s = s_all[r * RB:(r + 1) * RB]
            if bias_fn is not None:
                s = s + bias_fn((h0 + r * RB) // RB)
            lo, hi = s[:, :LANES], s[:, LANES:]
            mo = m_r[rows, :]
            mn = jnp.maximum(mo, jnp.max(jnp.maximum(lo, hi), axis=-1, keepdims=True))
            ps.append(jnp.exp2(jnp.concatenate([lo - mn, hi - mn], axis=1)).astype(BF16))
            alphas.append(jnp.exp2(mo - mn))
            m_r[rows, :] = mn
        pv = _dot(jnp.concatenate(ps, axis=0), v_blk[hi_] if isinstance(v_blk, (list, tuple)) else v_blk)
        for r in range(half_rows // RB):
            rows = slice(h0 + r * RB, h0 + (r + 1) * RB)
            acc_r[rows, :] = alphas[r] * acc_r[rows, :] + pv[r * RB:(r + 1) * RB]


def _nsa_kernel(q_ref, k0_ref, v0_ref, k1_ref, v1_ref, k2_ref, v2_ref, gate_ref, tiles_ref, biasc_ref,
                ovt_ref, place_ref, w1_ref, w2_ref, pe_ref, qg_ref, kg_ref, o_ref,
                kc_s, vc_s, k1_s, v1_s, k2_s, v2_s, qx_s, s_s, m_s, acc_s, mw_s, accw_s):
    qi = pl.program_id(2)
    kg = kg_ref[...]
    n_cmp_pad = kc_s.shape[0]
    per_g = AT // RB

    @pl.when(qi == 0)
    def _prep():
        T = k1_s.shape[0]
        lane_t = lax.broadcasted_iota(jnp.int32, (T, LANES), 1)
        row_t = lax.broadcasted_iota(jnp.int32, (T, LANES), 0)
        block_onehot = lax.shift_right_logical(row_t, 6) == lane_t - HEAD_DIM
        ones_lane = lane_t == HEAD_DIM
        k1_s[...] = jnp.where(block_onehot, 1.0, _rms_lanes(k1_ref[0], kg, HEAD_DIM)).astype(BF16)
        v1_s[...] = jnp.where(ones_lane, 1.0, v1_ref[0]).astype(BF16)
        k2_s[...] = _rms_lanes(k2_ref[0], kg, HEAD_DIM).astype(BF16)
        v2_s[...] = jnp.where(ones_lane, 1.0, v2_ref[0]).astype(BF16)
        half = CMP_BLOCK // 2
        for s, (src, dst) in enumerate(((k0_ref, kc_s), (v0_ref, vc_s))):
            a = jnp.zeros((n_cmp_pad, HEAD_DIM), F32)
            bm = jnp.zeros((n_cmp_pad, HEAD_DIM), F32)
            for j in range(half):
                xj = src[0, pl.ds(j, n_cmp_pad, stride=CMP_STRIDE), :][:, :HEAD_DIM]
                a = a + _dot_f32(xj + pe_ref[s, j:j + 1, :], w1_ref[s, j])
                bm = bm + _dot_f32(xj + pe_ref[s, half + j:half + j + 1, :], w1_ref[s, half + j])
            pre = a + pltpu.roll(bm, n_cmp_pad - 1, axis=0)
            hid = pre * jax.nn.sigmoid(pre)
            comp = _dot_f32(hid, w2_ref[s])
            if s == 0:
                comp = _rms_lanes(comp, kg, HEAD_DIM)
            dst[...] = comp.astype(BF16)

    qscale = HEAD_DIM ** -0.5 * LOG2E
    q = q_ref[0]
    qg = qg_ref[...]
    qs = jnp.concatenate(
        [_rms_lanes(q[:, g * LANES:(g + 1) * LANES], qg, HEAD_DIM) * qscale for g in range(A_GROUP)],
        axis=0).astype(BF16)

    biasc = biasc_ref[...].reshape(A_GROUP * AT, LANES)
    valid = biasc > 0.5 * NEG
    lcm = _dot_nt(qs, kc_s[...]) + biasc
    mc = jnp.max(lcm, axis=-1, keepdims=True)
    pc = jnp.where(valid, jnp.exp2(lcm - mc), 0.0)
    lc = jnp.sum(pc, axis=-1, keepdims=True)
    pc = pc * jnp.where(lc > 0.0, 1.0 / lc, 0.0)
    o_c = _dot(pc.astype(BF16), vc_s[...])

    ovt = ovt_ref[...]
    p_hi, p_mid, p_lo = _split3(pc[0:AT] + pc[AT:2 * AT] + pc[2 * AT:3 * AT])
    ps = _dot_nt(ovt, p_hi) + _dot_nt(ovt, p_mid) + _dot_nt(ovt, p_lo)
    n_sel = ps.shape[0]
    sid = lax.broadcasted_iota(jnp.int32, (n_sel, AT), 0)
    qcol = lax.broadcasted_iota(jnp.int32, (n_sel, AT), 1)
    cur = qi * (AT // SEL_BLOCK) + lax.shift_right_logical(qcol, 6)
    forced = (sid == 0) | (sid == cur) | (sid == cur - 1)
    psf = jnp.where(forced, ps + FORCE, ps)
    psf = jnp.where(sid <= cur, psf, -1.0)
    rank = jnp.zeros((n_sel, AT), F32)
    for sp in range(n_sel):
        other = psf[sp:sp + 1, :]
        beats = (other > psf) | ((other == psf) & (sid > sp))
        rank = rank + jnp.where(beats, 1.0, 0.0)
    keep = (rank < float(min(N_SELECT, n_sel))) & (psf >= 0.0)
    selneg = _dot_tn(jnp.where(keep, 0.0, NEG).astype(BF16), place_ref[...]).astype(BF16)
    lane_q = lax.broadcasted_iota(jnp.int32, (AT, LANES), 1)
    for g in range(A_GROUP):
        qx_s[g * AT:(g + 1) * AT, :] = jnp.where(lane_q < HEAD_DIM, qs[g * AT:(g + 1) * AT], selneg)

    def near_bias(delta, kind):
        kd = jnp.where(qi >= delta, kind, 3)
        return lambda r: tiles_ref[r // per_g, kd, pl.ds((r % per_g) * RB, RB), :]

    def key_rows(kc):
        return pl.ds(pl.multiple_of(kc * AT, AT), AT)

    m_s[...] = jnp.full(m_s.shape, NEG, F32)
    acc_s[...] = jnp.zeros(acc_s.shape, F32)
    mw_s[...] = jnp.full(mw_s.shape, NEG, F32)
    accw_s[...] = jnp.zeros(accw_s.shape, F32)
    for delta, kind in ((0, 0), (1, 1), (2, 2)):
        rows_k = key_rows(jnp.maximum(qi - delta, 0))
        if delta < 2:
            _flash_chunk(qx_s, k1_s[rows_k, :], v1_s[rows_k, :], near_bias(delta, kind), m_s, acc_s)
        _flash_chunk(qx_s, k2_s[rows_k, :], v2_s[rows_k, :], near_bias(delta, kind), mw_s, accw_s)

    @pl.when(qi >= 2)
    def _first_scores():
        for h, s in enumerate(_qk_halves(qx_s, k1_s[key_rows(0), :])):
            s_s[h] = s

    def far_body(kc, carry):
        s_next = _qk_halves(qx_s, k1_s[key_rows(kc + 1), :])
        _softmax_pv([s_s[0], s_s[1]], v1_s[key_rows(kc), :], None, m_s, acc_s)
        for h, s in enumerate(s_next):
            s_s[h] = s
        return carry

    lax.fori_loop(0, jnp.maximum(qi - 1, 0), far_body, 0)

    gt = jax.nn.sigmoid(gate_ref[0])
    for g in range(A_GROUP):
        rows = slice(g * AT, (g + 1) * AT)
        a_s = acc_s[rows, :]
        a_w = accw_s[rows, :]
        o = (gt[:, 3 * g:3 * g + 1] * o_c[rows]
             + gt[:, 3 * g + 1:3 * g + 2] * (a_s / a_s[:, HEAD_DIM:HEAD_DIM + 1])
             + gt[:, 3 * g + 2:3 * g + 3] * (a_w / a_w[:, HEAD_DIM:HEAD_DIM + 1]))
        o_ref[0, :, g * LANES:(g + 1) * LANES] = jnp.where(lane_q < HEAD_DIM, o, 0.0)


def _nsa(proj, tiles, biasc, cmp_pe, cmp_w1, cmp_w2, q_norm, k_norm):
    B, T, _ = proj.shape
    nq = T // AT
    n_cmp_pad = T // CMP_STRIDE
    n_sel = T // SEL_BLOCK
    assert n_sel <= LANES - HEAD_DIM - 1 and n_cmp_pad == LANES
    starts = np.arange(n_cmp_pad) * CMP_STRIDE
    sid = np.arange(n_sel)
    overlap = ((starts[:, None] < (sid[None, :] + 1) * SEL_BLOCK)
               & (starts[:, None] + CMP_BLOCK > sid[None, :] * SEL_BLOCK)
               & (starts[:, None] + CMP_BLOCK <= T))
    place = np.zeros((n_sel, LANES), np.float32)
    place[sid, HEAD_DIM + sid] = 1.0
    w2p = jnp.pad(cmp_w2, ((0, 0), (0, 0), (0, LANES - HEAD_DIM)))

    def slab(br, kv):
        base = SEG_AKV // LANES + (br * 2 + kv) * 2
        return pl.BlockSpec((1, T, LANES), lambda b, h, i: (b, 0, base + h))

    def full(shape):
        return pl.BlockSpec(shape, lambda b, h, i: (0,) * len(shape))

    rows = A_GROUP * AT
    return pl.pallas_call(
        _nsa_kernel,
        grid=(B, A_KV_HEADS, nq),
        in_specs=[pl.BlockSpec((1, AT, A_GROUP * LANES), lambda b, h, i: (b, i, h)),
                  slab(0, 0), slab(0, 1), slab(1, 0), slab(1, 1), slab(2, 0), slab(2, 1),
                  pl.BlockSpec((1, AT, LANES), lambda b, h, i: (b, i, SEG_GATE // LANES + h)),
                  pl.BlockSpec((A_GROUP, 4, AT, AT), lambda b, h, i: (h, 0, 0, 0)),
                  pl.BlockSpec((A_GROUP, AT, LANES), lambda b, h, i: (h, i, 0)),
                  full((n_sel, n_cmp_pad)), full((n_sel, LANES)),
                  full((2, CMP_BLOCK, HEAD_DIM, HEAD_DIM)), full((2, HEAD_DIM, LANES)),
                  full((2, CMP_BLOCK, HEAD_DIM)), full((1, LANES)), full((1, LANES))],
        out_specs=pl.BlockSpec((1, AT, A_GROUP * LANES), lambda b, h, i: (b, i, h)),
        out_shape=jax.ShapeDtypeStruct((B, T, A_HEADS * LANES), F32),
        scratch_shapes=[pltpu.VMEM((n_cmp_pad, LANES), BF16), pltpu.VMEM((n_cmp_pad, LANES), BF16),
                        pltpu.VMEM((T, LANES), BF16), pltpu.VMEM((T, LANES), BF16),
                        pltpu.VMEM((T, LANES), BF16), pltpu.VMEM((T, LANES), BF16),
                        pltpu.VMEM((rows, LANES), BF16), pltpu.VMEM((2, rows // 2, AT), F32),
                        pltpu.VMEM((rows, LANES), F32), pltpu.VMEM((rows, LANES), F32),
                        pltpu.VMEM((rows, LANES), F32), pltpu.VMEM((rows, LANES), F32)],
        compiler_params=_params("parallel", "parallel", "arbitrary"),
        name="nsa",
    )(proj, proj, proj, proj, proj, proj, proj, proj, tiles, biasc,
      jnp.asarray(overlap.T, BF16), jnp.asarray(place, BF16), cmp_w1, w2p, cmp_pe,
      _pad_lanes(q_norm, LANES), _pad_lanes(k_norm, LANES))


def _mlstm_kernel(qk_ref, v_ref, o_ref, gif_ref, cw_ref, gb_ref, gn_ref, tril_ref, y_ref,
                  xpad_s, c_s, n_s, m_s):
    T = qk_ref.shape[1]
    L = MLSTM_CHUNK
    xpad_s[0:8, :] = jnp.zeros((8, xpad_s.shape[1]), F32)
    xpad_s[8:8 + T, :] = qk_ref[0]
    c_s[...] = jnp.zeros(c_s.shape, F32)
    n_s[...] = jnp.zeros(n_s.shape, F32)
    m_s[...] = jnp.zeros(m_s.shape, F32)
    lane = lax.broadcasted_iota(jnp.int32, (1, LANES), 1)
    lo_half = lane < HEAD_DIM
    rowi = lax.broadcasted_iota(jnp.int32, (LANES, LANES), 0)
    coli = lax.broadcasted_iota(jnp.int32, (LANES, LANES), 1)
    blockdiag = (rowi < HEAD_DIM) == (coli < HEAD_DIM)
    tri = lax.broadcasted_iota(jnp.int32, (L, L), 0) >= lax.broadcasted_iota(jnp.int32, (L, L), 1)
    kscale = HEAD_DIM ** -0.5

    def chunk(c, carry):
        r0 = pl.multiple_of(c * L, L)
        win = xpad_s[pl.ds(r0, L + 8), :]
        conv = cw_ref[CONV_K - 1:CONV_K, :] * win[8:8 + L]
        for j in range(CONV_K - 1):
            conv = conv + cw_ref[j:j + 1, :] * pltpu.roll(win, CONV_K - 1 - j, axis=0)[8:8 + L]
        qk = conv * jax.nn.sigmoid(conv)
        gif = gif_ref[0, pl.ds(r0, L), :] + gb_ref[...]
        logf = jax.nn.log_sigmoid(gif)
        lh, lm, ll = _split3(logf)
        tril = tril_ref[...]
        bcum = _dot(tril, lh) + _dot(tril, lm) + _dot(tril, ll)
        gif_t = gif.T
        bcum_t = bcum.T
        for p in range(B_HEADS // 2):
            qp = qk[:, p * LANES:(p + 1) * LANES]
            kp = qk[:, 2 * LANES + p * LANES:2 * LANES + (p + 1) * LANES] * kscale
            vp = v_ref[0, pl.ds(r0, L), p * LANES:(p + 1) * LANES]
            c_old = c_s[p]
            n_old = n_s[p]
            cq = _dot_nt(qp, c_old)
            hval, wcol, dec, mnew = [], [], [], []
            for half in range(2):
                hh = 2 * p + half
                hmask = lo_half if half == 0 else jnp.logical_not(lo_half)
                b_col = bcum[:, B_HEADS + hh:B_HEADS + hh + 1]
                b_row = bcum_t[B_HEADS + hh:B_HEADS + hh + 1, :]
                i_col = gif[:, hh:hh + 1]
                i_row = gif_t[hh:hh + 1, :]
                m_prev = m_s[hh][0:1, 0:1]
                dmat = jnp.where(tri, b_col - b_row + i_row, -jnp.inf)
                m_inter = b_col + m_prev
                m_j = jnp.maximum(m_inter, jnp.max(dmat, axis=-1, keepdims=True))
                isc = jnp.exp(m_inter - m_j)
                qm = jnp.where(hmask, qp, 0.0)
                s = _dot_nt(qm, kp) * jnp.exp(dmat - m_j)
                qn = jnp.sum(qm * n_old, axis=-1, keepdims=True)
                den = jnp.sum(s, axis=-1, keepdims=True) + isc * qn
                num = _dot(s, vp) + isc * cq
                hval.append(num / jnp.maximum(jnp.abs(den), jnp.exp(-m_j)))
                b_last = b_col[L - 1:L, :]
                w_log = b_last - b_col + i_col
                m_new = jnp.maximum(b_last + m_prev, jnp.max(w_log, axis=0, keepdims=True))
                wcol.append(jnp.exp(w_log - m_new))
                dec.append(jnp.exp(b_last + m_prev - m_new))
                mnew.append(m_new)
            w_l = jnp.where(lo_half, wcol[0], wcol[1])
            c_upd = _dot_tn(vp * w_l, kp)
            c_s[p] = (jnp.where(rowi < HEAD_DIM, dec[0], dec[1]) * c_old
                      + jnp.where(blockdiag, c_upd, 0.0))
            n_s[p] = jnp.where(lo_half, dec[0], dec[1]) * n_old + jnp.sum(w_l * kp, axis=0, keepdims=True)
            for half in range(2):
                m_s[2 * p + half] = jnp.broadcast_to(mnew[half], m_s.shape[1:])
            h = jnp.where(lo_half, hval[0], hval[1])
            hsq = h * h
            ms0 = jnp.sum(jnp.where(lo_half, hsq, 0.0), axis=-1, keepdims=True) * (1.0 / HEAD_DIM)
            ms1 = jnp.sum(jnp.where(lo_half, 0.0, hsq), axis=-1, keepdims=True) * (1.0 / HEAD_DIM)
            hn = h * jnp.where(lo_half, lax.rsqrt(ms0 + EPS), lax.rsqrt(ms1 + EPS)) * gn_ref[:, p * LANES:(p + 1) * LANES]
            og = jax.nn.sigmoid(o_ref[0, pl.ds(r0, L), p * LANES:(p + 1) * LANES])
            y_ref[0, pl.ds(r0, L), p * LANES:(p + 1) * LANES] = og * hn
        return carry

    lax.fori_loop(0, T // L, chunk, 0)


def _mlstm(proj, conv_w, i_bias, f_bias, out_norm):
    B, T, _ = proj.shape
    W = B_HEADS * HEAD_DIM
    gb = _pad_lanes(jnp.concatenate([i_bias, f_bias]), LANES)
    tril = jnp.asarray(np.tril(np.ones((MLSTM_CHUNK, MLSTM_CHUNK))), BF16)

    def full(shape):
        return pl.BlockSpec(shape, lambda b: (0,) * len(shape))

    return pl.pallas_call(
        _mlstm_kernel,
        grid=(B,),
        in_specs=[pl.BlockSpec((1, T, 2 * W), lambda b: (b, 0, SEG_BQK // (2 * W))),
                  pl.BlockSpec((1, T, W), lambda b: (b, 0, SEG_BV // W)),
                  pl.BlockSpec((1, T, W), lambda b: (b, 0, SEG_BO // W)),
                  pl.BlockSpec((1, T, LANES), lambda b: (b, 0, SEG_BIF // LANES)),
                  full((CONV_K, 2 * W)), full((1, LANES)), full((1, W)),
                  full((MLSTM_CHUNK, MLSTM_CHUNK))],
        out_specs=pl.BlockSpec((1, T, W), lambda b: (b, 0, 0)),
        out_shape=jax.ShapeDtypeStruct((B, T, W), F32),
        scratch_shapes=[pltpu.VMEM((T + 8, 2 * W), F32),
                        pltpu.VMEM((B_HEADS // 2, LANES, LANES), F32),
                        pltpu.VMEM((B_HEADS // 2, 1, LANES), F32),
                        pltpu.VMEM((B_HEADS, 8, LANES), F32)],
        compiler_params=_params("parallel"),
        name="mlstm",
    )(proj, proj, proj, proj, conv_w, gb, out_norm.reshape(1, W), tril)


def _mla_prep_kernel(dq_ref, dkv_ref, kr_ref, pos_ref, wq_ref, wk_ref, wv_ref, qa_ref, kva_ref,
                     gq_ref, gk_ref, gkr_ref, invf_ref, sgn_ref, q_out, k_out, v_out):
    lane = lax.broadcasted_iota(jnp.int32, (1, LANES), 1)
    nope = lane < C_NOPE
    ropem = (lane >= C_NOPE) & (lane < C_NOPE + C_ROPE)
    first = lane < C_NOPE + C_ROPE // 2
    ang = pos_ref[...].astype(F32) * invf_ref[...]
    cs = jnp.cos(ang)
    sn = jnp.sin(ang) * sgn_ref[...]

    def rope(x):
        partner = jnp.where(first, pltpu.roll(x, LANES - C_ROPE // 2, axis=1), pltpu.roll(x, C_ROPE // 2, axis=1))
        return x * cs + partner * sn

    scale = (C_NOPE + C_ROPE) ** -0.5 * LOG2E
    qn = _rms_lanes(dq_ref[...], qa_ref[...], Q_RANK).astype(BF16)
    q = _dot(qn, wq_ref[...])
    kvn = _rms_lanes(dkv_ref[...], kva_ref[...], KV_RANK).astype(BF16)
    kk = _dot(kvn, wk_ref[...])
    vv = _dot(kvn, wv_ref[...])
    lane_v = lax.broadcasted_iota(jnp.int32, vv.shape, 1)
    v_out[...] = jnp.where((lane_v & (LANES - 1)) == C_V, 1.0, vv).astype(BF16)
    kr = kr_ref[...]
    krn = kr * lax.rsqrt(jnp.sum(kr * kr, axis=-1, keepdims=True) * (1.0 / C_ROPE) + EPS) * gkr_ref[...]
    krr = rope(krn)
    for h in range(C_HEADS):
        sl = slice(h * LANES, (h + 1) * LANES)
        qh = q[:, sl]
        sq = qh * qh
        msn = jnp.sum(jnp.where(nope, sq, 0.0), axis=-1, keepdims=True) * (1.0 / C_NOPE)
        msr = jnp.sum(jnp.where(ropem, sq, 0.0), axis=-1, keepdims=True) * (1.0 / C_ROPE)
        qhn = qh * jnp.where(nope, lax.rsqrt(msn + EPS), lax.rsqrt(msr + EPS)) * gq_ref[...]
        q_out[:, sl] = (rope(qhn) * scale).astype(BF16)
        kh = kk[:, sl]
        msk = jnp.sum(kh * kh, axis=-1, keepdims=True) * (1.0 / C_NOPE)
        k_out[:, sl] = (kh * lax.rsqrt(msk + EPS) * gk_ref[...] + krr).astype(BF16)


def _mla_prep(proj2, pos, q_a_norm, w_uq, kv_a_norm, w_ukv, qk_norm):
    N = proj2.shape[0]
    tm = 512
    qd = C_NOPE + C_ROPE
    qmap = np.full((C_HEADS * LANES,), -1, np.int64)
    kmap = np.full((C_HEADS * LANES,), -1, np.int64)
    vmap = np.full((C_HEADS * LANES,), -1, np.int64)
    for h in range(C_HEADS):
        qmap[h * LANES:h * LANES + qd] = h * qd + np.arange(qd)
        kmap[h * LANES:h * LANES + C_NOPE] = h * (C_NOPE + C_V) + np.arange(C_NOPE)
        vmap[h * LANES:h * LANES + C_V] = h * (C_NOPE + C_V) + C_NOPE + np.arange(C_V)
    wq = jnp.pad(_take_cols(w_uq, qmap), ((0, 256 - Q_RANK), (0, 0))).astype(BF16)
    wk = _take_cols(w_ukv, kmap).astype(BF16)
    wv = _take_cols(w_ukv, vmap).astype(BF16)
    half = C_ROPE // 2
    inv = 1.0 / (ROPE_THETA ** (np.arange(half, dtype=np.float32) * 2.0 / C_ROPE))
    invf = np.zeros((1, LANES), np.float32)
    sgn = np.zeros((1, LANES), np.float32)
    invf[0, C_NOPE:C_NOPE + C_ROPE] = np.concatenate([inv, inv])
    sgn[0, C_NOPE:C_NOPE + half] = -1.0
    sgn[0, C_NOPE + half:C_NOPE + C_ROPE] = 1.0
    gkr = jnp.pad(qk_norm[1, C_NOPE:].reshape(1, -1), ((0, 0), (C_NOPE, LANES - C_NOPE - C_ROPE)))

    def full(shape):
        return pl.BlockSpec(shape, lambda i: (0,) * len(shape))

    return pl.pallas_call(
        _mla_prep_kernel,
        grid=(N // tm,),
        in_specs=[pl.BlockSpec((tm, 256), lambda i: (i, SEG_CDQ // 256)),
                  pl.BlockSpec((tm, LANES), lambda i: (i, SEG_CDKV // LANES)),
                  pl.BlockSpec((tm, LANES), lambda i: (i, SEG_CKR // LANES)),
                  pl.BlockSpec((tm, 1), lambda i: (i, 0)),
                  full((256, C_HEADS * LANES)), full((KV_RANK, C_HEADS * LANES)), full((KV_RANK, C_HEADS * LANES)),
                  full((1, 256)), full((1, LANES)), full((1, LANES)), full((1, LANES)), full((1, LANES)),
                  full((1, LANES)), full((1, LANES))],
        out_specs=[pl.BlockSpec((tm, C_HEADS * LANES), lambda i: (i, 0)),
                   pl.BlockSpec((tm, C_HEADS * LANES), lambda i: (i, 0)),
                   pl.BlockSpec((tm, C_HEADS * LANES), lambda i: (i, 0))],
        out_shape=[jax.ShapeDtypeStruct((N, C_HEADS * LANES), BF16),
                   jax.ShapeDtypeStruct((N, C_HEADS * LANES), BF16),
                   jax.ShapeDtypeStruct((N, C_HEADS * LANES), BF16)],
        compiler_params=_params("parallel"),
        name="mla_prep",
    )(proj2, proj2, proj2, pos, wq, wk, wv,
      _pad_lanes(q_a_norm, 256), kv_a_norm.reshape(1, -1), _pad_lanes(qk_norm[0], LANES),
      _pad_lanes(qk_norm[1, :C_NOPE], LANES), gkr, jnp.asarray(invf), jnp.asarray(sgn))


def _mla_attn_kernel(q_ref, k_ref, v_ref, causal_ref, o_ref, s_s, m_s, acc_s):
    qi = pl.program_id(2)
    m_s[...] = jnp.full(m_s.shape, NEG, F32)
    acc_s[...] = jnp.zeros(acc_s.shape, F32)
    heads = [slice(hh * LANES, (hh + 1) * LANES) for hh in range(2)]
    per_head = AT // RB

    def key_rows(kc):
        return pl.ds(pl.multiple_of(kc * AT, AT), AT)

    def scores(kc):
        return [_dot_nt(q_ref[0, :, sl], k_ref[0, key_rows(kc), sl]) for sl in heads]

    def values(kc):
        return [v_ref[0, key_rows(kc), sl] for sl in heads]

    _softmax_pv(scores(qi), values(qi), lambda r: causal_ref[pl.ds((r % per_head) * RB, RB), :], m_s, acc_s)

    @pl.when(qi >= 1)
    def _first_scores():
        for h, s in enumerate(scores(0)):
            s_s[h] = s

    def body(kc, carry):
        s_next = scores(kc + 1)
        _softmax_pv([s_s[0], s_s[1]], values(kc), None, m_s, acc_s)
        for h, s in enumerate(s_next):
            s_s[h] = s
        return carry

    lax.fori_loop(0, qi, body, 0)
    lane = lax.broadcasted_iota(jnp.int32, (AT, LANES), 1)
    outs = []
    for hh in range(2):
        a = acc_s[hh * AT:(hh + 1) * AT, :]
        outs.append(jnp.where(lane < C_V, a / a[:, C_V:C_V + 1], 0.0))
    o_ref[0] = outs[0] + pltpu.roll(outs[1], C_V, axis=1)


def _mla_attn(q, k, v):
    B, T, _ = q.shape
    causal = np.where(np.arange(AT)[:, None] >= np.arange(AT)[None, :], 0.0, NEG).astype(np.float32)
    pair = pl.BlockSpec((1, T, 2 * LANES), lambda b, p, i: (b, 0, p))
    return pl.pallas_call(
        _mla_attn_kernel,
        grid=(B, C_HEADS // 2, T // AT),
        in_specs=[pl.BlockSpec((1, AT, 2 * LANES), lambda b, p, i: (b, i, p)), pair, pair,
                  pl.BlockSpec((AT, AT), lambda b, p, i: (0, 0))],
        out_specs=pl.BlockSpec((1, AT, LANES), lambda b, p, i: (b, i, p)),
        out_shape=jax.ShapeDtypeStruct((B, T, C_HEADS * C_V), F32),
        scratch_shapes=[pltpu.VMEM((2, AT, AT), F32), pltpu.VMEM((2 * AT, LANES), F32),
                        pltpu.VMEM((2 * AT, LANES), F32)],
        compiler_params=_params("parallel", "parallel", "arbitrary"),
        name="mla_attn",
    )(q, k, v, jnp.asarray(causal))


def _outproj_kernel(x_ref, ya_ref, yb_ref, yc_ref, wa_ref, wb_ref, wc_ref, ga_ref, gc_ref, g2_ref,
                    wr_ref, br_ref, x1_ref, h2_ref, ids_ref, wts_ref):
    ya = _rms_lanes(ya_ref[...], ga_ref[...], A_HEADS * HEAD_DIM).astype(BF16)
    yc = _rms_lanes(yc_ref[...], gc_ref[...], C_HEADS * C_V).astype(BF16)
    x1 = (x_ref[...] + _dot(ya, wa_ref[...]) + _dot(yb_ref[...].astype(BF16), wb_ref[...])
          + _dot(yc, wc_ref[...]))
    x1_ref[...] = x1
    h2 = _rms_lanes(x1, g2_ref[...], D_MODEL)
    h2_ref[...] = h2
    logits = _dot_f32(h2, wr_ref[...]) + br_ref[...]
    tm = logits.shape[0]
    lane = lax.broadcasted_iota(jnp.int32, (tm, LANES), 1)
    is_g = (lane >= N_EXPERTS) & (lane < N_EXPERTS + N_GROUPS)
    gl = jnp.where(is_g, logits, NEG)
    ge = jnp.where(is_g, jnp.exp(gl - jnp.max(gl, axis=-1, keepdims=True)), 0.0)
    gp = jnp.where(is_g, ge / jnp.sum(ge, axis=-1, keepdims=True), -1.0)
    g_top = jnp.max(gp, axis=-1, keepdims=True)
    g_idx = jnp.min(jnp.where(gp == g_top, lane, 2 * LANES), axis=-1, keepdims=True) - N_EXPERTS
    in_grp = (lane < N_EXPERTS) & (lax.shift_right_logical(lane, 3) == g_idx)
    el = jnp.where(in_grp, logits, NEG)
    ee = jnp.where(in_grp, jnp.exp(el - jnp.max(el, axis=-1, keepdims=True)), 0.0)
    ep = jnp.where(in_grp, ee / jnp.sum(ee, axis=-1, keepdims=True), -1.0)
    v1 = jnp.max(ep, axis=-1, keepdims=True)
    i1 = jnp.min(jnp.where(ep == v1, lane, 2 * LANES), axis=-1, keepdims=True)
    ep2 = jnp.where(lane == i1, -1.0, ep)
    v2 = jnp.max(ep2, axis=-1, keepdims=True)
    i2 = jnp.min(jnp.where(ep2 == v2, lane, 2 * LANES), axis=-1, keepdims=True)
    tot = v1 + v2
    ids_ref[...] = jnp.where(lane == 0, i1, jnp.where(lane == 1, i2, 0))
    wts_ref[...] = jnp.where(lane == 0, v1 / tot * g_top, jnp.where(lane == 1, v2 / tot * g_top, 0.0))


def _outproj(x2, ya, yb, yc, w_out, nsa_out_norm, mla_out_norm, norm2_g, w_group, b_group, w_expert, b_expert):
    N = x2.shape[0]
    tm = 256
    amap = np.full((A_HEADS * LANES,), -1, np.int64)
    for hd in range(A_HEADS):
        amap[hd * LANES:hd * LANES + HEAD_DIM] = hd * HEAD_DIM + np.arange(HEAD_DIM)
    wa = _take_rows(w_out[:384], amap).astype(BF16)
    wb = w_out[384:640].astype(BF16)
    wc = w_out[640:].astype(BF16)
    ga = _take_cols(nsa_out_norm.reshape(1, -1), amap)
    wr = jnp.pad(jnp.concatenate([w_expert, w_group], axis=1), ((0, 0), (0, LANES - N_EXPERTS - N_GROUPS)))
    br = _pad_lanes(jnp.concatenate([b_expert, b_group]), LANES)

    def full(shape):
        return pl.BlockSpec(shape, lambda i: (0,) * len(shape))

    def rows(w):
        return pl.BlockSpec((tm, w), lambda i: (i, 0))

    return pl.pallas_call(
        _outproj_kernel,
        grid=(N // tm,),
        in_specs=[rows(D_MODEL), rows(A_HEADS * LANES), rows(256), rows(384),
                  full((A_HEADS * LANES, D_MODEL)), full((256, D_MODEL)), full((384, D_MODEL)),
                  full((1, A_HEADS * LANES)), full((1, 384)), full((1, D_MODEL)),
                  full((D_MODEL, LANES)), full((1, LANES))],
        out_specs=[rows(D_MODEL), rows(D_MODEL), rows(LANES), rows(LANES)],
        out_shape=[jax.ShapeDtypeStruct((N, D_MODEL), F32), jax.ShapeDtypeStruct((N, D_MODEL), F32),
                   jax.ShapeDtypeStruct((N, LANES), jnp.int32), jax.ShapeDtypeStruct((N, LANES), F32)],
        compiler_params=_params("parallel"),
        name="outproj_router",
    )(x2, ya, yb, yc, wa, wb, wc, ga, mla_out_norm.reshape(1, -1), norm2_g.reshape(1, -1), wr, br)


def _rank_kernel(ids_ref, tril_ref, rank_ref, cnt_ref, carry_s):
    i = pl.program_id(0)

    @pl.when(i == 0)
    def _():
        carry_s[...] = jnp.zeros(carry_s.shape, F32)

    ids = ids_ref[...]
    tm = ids.shape[0]
    lane = lax.broadcasted_iota(jnp.int32, (tm, LANES), 1)
    i1 = ids[:, 0:1]
    i2 = ids[:, 1:2]
    oh = jnp.where((lane == i1) | (lane == i2), 1.0, 0.0)
    incl = _dot(tril_ref[...], oh.astype(BF16)) + carry_s[0:1, :]
    excl = incl - oh
    r1 = jnp.sum(jnp.where(lane == i1, excl, 0.0), axis=-1, keepdims=True)
    r2 = jnp.sum(jnp.where(lane == i2, excl, 0.0), axis=-1, keepdims=True)
    rank_ref[...] = jnp.where(lane == 0, r1, jnp.where(lane == 1, r2, 0.0)).astype(jnp.int32)
    tot = incl[tm - 1:tm, :]
    carry_s[...] = jnp.broadcast_to(tot, carry_s.shape)
    cnt_ref[...] = jnp.broadcast_to(tot, cnt_ref.shape).astype(jnp.int32)


def _moe_rank(ids):
    N = ids.shape[0]
    tm = 512
    tril = jnp.asarray(np.tril(np.ones((tm, tm))), BF16)
    return pl.pallas_call(
        _rank_kernel,
        grid=(N // tm,),
        in_specs=[pl.BlockSpec((tm, LANES), lambda i: (i, 0)), pl.BlockSpec((tm, tm), lambda i: (0, 0))],
        out_specs=[pl.BlockSpec((tm, LANES), lambda i: (i, 0)), pl.BlockSpec((8, LANES), lambda i: (0, 0))],
        out_shape=[jax.ShapeDtypeStruct((N, LANES), jnp.int32), jax.ShapeDtypeStruct((8, LANES), jnp.int32)],
        scratch_shapes=[pltpu.VMEM((8, LANES), F32)],
        compiler_params=_params("arbitrary"),
        name="moe_rank",
    )(ids, tril)


DISPATCH_TM = 1024


def _dispatch_kernel(off_ref, id1_ref, id2_ref, r1_ref, r2_ref, h_ref, zeros_ref, xs_ref, sem):
    del zeros_ref

    def row_copy(i, pos):
        return pltpu.make_async_copy(h_ref.at[pl.ds(i, 1), :], xs_ref.at[pl.ds(pos, 1), :], sem)

    def issue(i, carry):
        row_copy(i, off_ref[id1_ref[i]] + r1_ref[i]).start()
        row_copy(i, off_ref[id2_ref[i]] + r2_ref[i]).start()
        return carry

    lax.fori_loop(0, DISPATCH_TM, issue, 0)

    def drain(i, carry):
        row_copy(i, 0).wait()
        row_copy(i, 0).wait()
        return carry

    lax.fori_loop(0, DISPATCH_TM, drain, 0)


def _moe_dispatch(h2, off, id1, id2, r1, r2, n_rows):
    N = h2.shape[0]
    smem1d = pl.BlockSpec((DISPATCH_TM,), lambda i: (i,), memory_space=pltpu.SMEM)
    return pl.pallas_call(
        _dispatch_kernel,
        grid=(N // DISPATCH_TM,),
        in_specs=[pl.BlockSpec(memory_space=pltpu.SMEM), smem1d, smem1d, smem1d, smem1d,
                  pl.BlockSpec((DISPATCH_TM, D_MODEL), lambda i: (i, 0)),
                  pl.BlockSpec(memory_space=pl.ANY)],
        out_specs=pl.BlockSpec(memory_space=pl.ANY),
        out_shape=jax.ShapeDtypeStruct((n_rows, D_MODEL), F32),
        scratch_shapes=[pltpu.SemaphoreType.DMA(())],
        input_output_aliases={6: 0},
        compiler_params=_params("arbitrary"),
        name="moe_dispatch",
    )(off, id1, id2, r1, r2, h2, jnp.zeros((n_rows, D_MODEL), F32))


def _expert_kernel(te_ref, nu_ref, x_ref, wg_ref, wu_ref, wd_ref, y_ref):
    g = pl.program_id(0)

    @pl.when(g < nu_ref[0])
    def _():
        x = x_ref[...].astype(BF16)
        a = _dot(x, wg_ref[0].astype(BF16))
        u = _dot(x, wu_ref[0].astype(BF16))
        h = (a * jax.nn.sigmoid(a) * u).astype(BF16)
        y_ref[...] = _dot(h, wd_ref[0].astype(BF16))

    @pl.when(g >= nu_ref[0])
    def _():
        y_ref[...] = jnp.zeros(y_ref.shape, F32)


def _moe_experts(xs, tile_expert, n_used, w_gate, w_up, w_down):
    R = xs.shape[0]
    G = R // ROW_TILE

    def row_map(g, te, nu):
        return (jnp.minimum(g, nu[0] - 1), 0)

    def w_map(g, te, nu):
        return (te[jnp.minimum(g, nu[0] - 1)], 0, 0)

    return pl.pallas_call(
        _expert_kernel,
        grid_spec=pltpu.PrefetchScalarGridSpec(
            num_scalar_prefetch=2,
            grid=(G,),
            in_specs=[pl.BlockSpec((ROW_TILE, D_MODEL), row_map),
                      pl.BlockSpec((1, D_MODEL, D_EXPERT), w_map),
                      pl.BlockSpec((1, D_MODEL, D_EXPERT), w_map),
                      pl.BlockSpec((1, D_EXPERT, D_MODEL), w_map)],
            out_specs=pl.BlockSpec((ROW_TILE, D_MODEL), lambda g, te, nu: (g, 0))),
        out_shape=jax.ShapeDtypeStruct((R, D_MODEL), F32),
        compiler_params=_params("arbitrary"),
        name="moe_experts",
    )(tile_expert, n_used, xs, w_gate, w_up, w_down)


COMBINE_TM = 256


def _combine_kernel(off_ref, id1_ref, id2_ref, r1_ref, r2_ref, x1_ref, wts_ref, ys_ref, o_ref, b1_s, b2_s, sem):
    def row_copy(buf, i, pos):
        return pltpu.make_async_copy(ys_ref.at[pl.ds(pos, 1), :], buf.at[pl.ds(i, 1), :], sem)

    def issue(i, carry):
        row_copy(b1_s, i, off_ref[id1_ref[i]] + r1_ref[i]).start()
        row_copy(b2_s, i, off_ref[id2_ref[i]] + r2_ref[i]).start()
        return carry

    lax.fori_loop(0, COMBINE_TM, issue, 0)

    def drain(i, carry):
        row_copy(b1_s, i, 0).wait()
        row_copy(b2_s, i, 0).wait()
        return carry

    lax.fori_loop(0, COMBINE_TM, drain, 0)
    w = wts_ref[...]
    o_ref[...] = x1_ref[...] + w[:, 0:1] * b1_s[...] + w[:, 1:2] * b2_s[...]


def _moe_combine(x1, wts, ys, off, id1, id2, r1, r2):
    N = x1.shape[0]
    smem1d = pl.BlockSpec((COMBINE_TM,), lambda i: (i,), memory_space=pltpu.SMEM)
    return pl.pallas_call(
        _combine_kernel,
        grid=(N // COMBINE_TM,),
        in_specs=[pl.BlockSpec(memory_space=pltpu.SMEM), smem1d, smem1d, smem1d, smem1d,
                  pl.BlockSpec((COMBINE_TM, D_MODEL), lambda i: (i, 0)),
                  pl.BlockSpec((COMBINE_TM, LANES), lambda i: (i, 0)),
                  pl.BlockSpec(memory_space=pl.ANY)],
        out_specs=pl.BlockSpec((COMBINE_TM, D_MODEL), lambda i: (i, 0)),
        out_shape=jax.ShapeDtypeStruct((N, D_MODEL), F32),
        scratch_shapes=[pltpu.VMEM((COMBINE_TM, D_MODEL), F32), pltpu.VMEM((COMBINE_TM, D_MODEL), F32),
                        pltpu.SemaphoreType.DMA(())],
        compiler_params=_params("arbitrary"),
        name="moe_combine",
    )(off, id1, id2, r1, r2, x1, wts, ys)


def _moe(x1, h2, ids, wts, w_gate, w_up, w_down):
    N = x1.shape[0]
    n_rows = 2 * N + N_EXPERTS * ROW_TILE
    rank, cnt = _moe_rank(ids)
    counts = cnt[0, :N_EXPERTS]
    tiles_per = (counts + ROW_TILE - 1) // ROW_TILE
    tile_end = jnp.cumsum(tiles_per)
    off = ((tile_end - tiles_per) * ROW_TILE).astype(jnp.int32)
    n_used = tile_end[-1:].astype(jnp.int32)
    tile_ids = jnp.arange(n_rows // ROW_TILE, dtype=jnp.int32)
    tile_expert = jnp.minimum(jnp.sum(tile_end[None, :] <= tile_ids[:, None], axis=1), N_EXPERTS - 1).astype(jnp.int32)
    id1, id2, r1, r2 = ids[:, 0], ids[:, 1], rank[:, 0], rank[:, 1]
    xs = _moe_dispatch(h2, off, id1, id2, r1, r2, n_rows)
    ys = _moe_experts(xs, tile_expert, n_used, w_gate, w_up, w_down)
    return _moe_combine(x1, wts, ys, off, id1, id2, r1, r2)


def kernel(x, positions, rel_bias, norm1_g, w_in, nsa_cmp_pe, nsa_cmp_w1, nsa_cmp_w2, nsa_q_norm, nsa_k_norm, nsa_out_norm, mlstm_conv, mlstm_i_bias, mlstm_f_bias, mlstm_out_norm, mla_q_a_norm, mla_w_uq, mla_kv_a_norm, mla_w_ukv, mla_qk_norm, mla_out_norm, w_out, norm2_g, moe_w_group, moe_b_group, moe_w_expert, moe_b_expert, moe_w_gate, moe_w_up, moe_w_down):
    B, T, D = x.shape
    N = B * T
    depth = w_in.shape[0]
    colmap = _proj_colmap()
    tiles, biasc = _bias_tables(rel_bias, T)
    pos = positions.reshape(N, 1)
    x2 = x.reshape(N, D)
    for l in range(depth):
        w_proj = _take_cols(w_in[l], colmap).astype(BF16)
        proj2 = _inproj(x2, norm1_g[l].reshape(1, D), w_proj)
        proj = proj2.reshape(B, T, D_PROJ)
        ya = _nsa(proj, tiles, biasc, nsa_cmp_pe[l], nsa_cmp_w1[l], nsa_cmp_w2[l], nsa_q_norm[l], nsa_k_norm[l])
        yb = _mlstm(proj, mlstm_conv[l], mlstm_i_bias[l], mlstm_f_bias[l], mlstm_out_norm[l])
        q, k, v = _mla_prep(proj2, pos, mla_q_a_norm[l], mla_w_uq[l], mla_kv_a_norm[l], mla_w_ukv[l], mla_qk_norm[l])
        yc = _mla_attn(q.reshape(B, T, -1), k.reshape(B, T, -1), v.reshape(B, T, -1))
        x1, h2, ids, wts = _outproj(x2, ya.reshape(N, -1), yb.reshape(N, -1), yc.reshape(N, -1), w_out[l],
                                    nsa_out_norm[l], mla_out_norm[l], norm2_g[l], moe_w_group[l], moe_b_group[l],
                                    moe_w_expert[l], moe_b_expert[l])
        x2 = _moe(x1, h2, ids, wts, moe_w_gate[l], moe_w_up[l], moe_w_down[l])
    return x2.reshape(B, T, D)
```

```python
import functools
import math

import numpy as np
import jax
import jax.numpy as jnp
from jax import lax
from jax.experimental import pallas as pl
from jax.experimental.pallas import tpu as pltpu

F32 = jnp.float32
BF16 = jnp.bfloat16

D_MODEL = 1024
HEAD_DIM = 64
LANES = 128
A_HEADS, A_KV_HEADS, A_GROUP = 6, 2, 3
CMP_BLOCK, CMP_STRIDE, SEL_BLOCK, N_SELECT, WINDOW = 32, 16, 64, 16, 512
FORCE = 1e4
B_HEADS, CONV_K, MLSTM_CHUNK = 4, 4, 64
C_HEADS, C_NOPE, C_ROPE, C_V = 6, 64, 32, 64
Q_RANK, KV_RANK = 192, 128
ROPE_THETA = 10000.0
REL_BUCKETS, REL_MAX_DIST = 32, 128
N_GROUPS, EXPERTS_PER_GROUP, N_EXPERTS, D_EXPERT = 4, 8, 32, 256
EPS = 1e-6
NEG = -1e30

LOG2E = 1.4426950408889634

AT = 256
RB = 128
ROW_TILE = 256
VMEM_LIMIT = 48 * 1024 * 1024

SEG_AQ, SEG_AKV, SEG_GATE, SEG_BQK, SEG_BV, SEG_BO = 0, 768, 2304, 2560, 3072, 3328
SEG_CDQ, SEG_CDKV, SEG_CKR, SEG_BIF, D_PROJ = 3584, 3840, 3968, 4096, 4224


def _params(*sem):
    return pltpu.CompilerParams(dimension_semantics=sem, vmem_limit_bytes=VMEM_LIMIT)


def _dot(a, b):
    return jnp.dot(a, b, preferred_element_type=F32)


def _dot_nt(a, b):
    return lax.dot_general(a, b, (((1,), (1,)), ((), ())), preferred_element_type=F32)


def _dot_tn(a, b):
    return lax.dot_general(a, b, (((0,), (0,)), ((), ())), preferred_element_type=F32)


def _split3(x):
    hi = x.astype(BF16)
    r1 = x - hi.astype(F32)
    mid = r1.astype(BF16)
    lo = (r1 - mid.astype(F32)).astype(BF16)
    return hi, mid, lo


def _dot_exact_rhs(x, m_bf16):
    hi, mid, lo = _split3(x)
    return _dot(hi, m_bf16) + _dot(mid, m_bf16) + _dot(lo, m_bf16)


def _dot_f32(x, w):
    xh, xm, xl = _split3(x)
    wh, wm, wl = _split3(w)
    return (_dot(xh, wh) + _dot(xm, wh) + _dot(xh, wm)
            + _dot(xl, wh) + _dot(xm, wm) + _dot(xh, wl))


def _bucket_np(dist):
    n = np.maximum(dist, 0)
    exact = REL_BUCKETS // 2
    nf = np.maximum(n, 1).astype(np.float64)
    large = exact + (np.log(nf / exact) / math.log(REL_MAX_DIST / exact) * (REL_BUCKETS - exact)).astype(np.int64)
    large = np.minimum(large, REL_BUCKETS - 1)
    return np.where(n < exact, n, large).astype(np.int32)


def _bucket_tables(T):
    r = np.arange(AT)[:, None]
    c = np.arange(AT)[None, :]
    diag = np.where(r >= c, _bucket_np(r - c), -1)
    prev = _bucket_np(AT + r - c)
    far = np.where(c > r, _bucket_np(2 * AT + r - c), -1)
    tiles = np.stack([diag, prev, far]).astype(np.int32)
    tq = np.arange(T)[:, None]
    n = np.arange(LANES)[None, :]
    dist_c = tq - (n * CMP_STRIDE + CMP_BLOCK - 1)
    cmp_tbl = np.where(dist_c >= 0, _bucket_np(dist_c), -1).astype(np.int32)
    return tiles, cmp_tbl


def _proj_colmap():
    cm = np.full((D_PROJ,), -1, np.int64)
    for hd in range(A_HEADS):
        cm[SEG_AQ + hd * LANES: SEG_AQ + hd * LANES + HEAD_DIM] = hd * HEAD_DIM + np.arange(HEAD_DIM)
    for s in range(12):
        cm[SEG_AKV + s * LANES: SEG_AKV + s * LANES + HEAD_DIM] = 384 + s * HEAD_DIM + np.arange(HEAD_DIM)
    for h in range(A_KV_HEADS):
        cm[SEG_GATE + h * LANES: SEG_GATE + h * LANES + 9] = 1152 + h * 9 + np.arange(9)
    cm[SEG_BQK: SEG_BQK + 512] = 1170 + np.arange(512)
    cm[SEG_BV: SEG_BV + 256] = 1682 + np.arange(256)
    cm[SEG_BIF: SEG_BIF + 8] = 1938 + np.arange(8)
    cm[SEG_BO: SEG_BO + 256] = 1946 + np.arange(256)
    cm[SEG_CDQ: SEG_CDQ + Q_RANK] = 2202 + np.arange(Q_RANK)
    cm[SEG_CDKV: SEG_CDKV + KV_RANK] = 2394 + np.arange(KV_RANK)
    cm[SEG_CKR + C_NOPE: SEG_CKR + C_NOPE + C_ROPE] = 2522 + np.arange(C_ROPE)
    return cm


def _take_cols(w, colmap):
    wz = jnp.concatenate([w, jnp.zeros((w.shape[0], 1), w.dtype)], axis=1)
    return wz[:, np.where(colmap < 0, w.shape[1], colmap)]


def _take_rows(w, rowmap):
    return _take_cols(w.T, rowmap).T


def _pad_lanes(v, n):
    v = v.reshape(1, -1)
    return jnp.pad(v, ((0, 0), (0, n - v.shape[1])))


def _bias_kernel(rb_ref, bt_ref, bc_ref, tiles_ref, cmp_ref):
    hd = pl.program_id(0)
    bt = bt_ref[...]
    bc = bc_ref[...]
    t = jnp.full(bt.shape, NEG, F32)
    c = jnp.full(bc.shape, NEG, F32)
    far = rb_ref[REL_BUCKETS - 1, hd]
    for b in range(REL_BUCKETS):
        val = rb_ref[b, hd]
        t = jnp.where(bt == b, (val - far) * LOG2E, t)
        c = jnp.where(bc == b, val * LOG2E, c)
    tiles_ref[0, 0:3] = t
    tiles_ref[0, 3] = jnp.full((AT, AT), NEG, F32)
    cmp_ref[0] = c


def _bias_tables(rel_bias, T):
    bt, bc = _bucket_tables(T)
    return pl.pallas_call(
        _bias_kernel,
        grid=(A_HEADS,),
        in_specs=[pl.BlockSpec(memory_space=pltpu.SMEM),
                  pl.BlockSpec((3, AT, AT), lambda h: (0, 0, 0)),
                  pl.BlockSpec((T, LANES), lambda h: (0, 0))],
        out_specs=[pl.BlockSpec((1, 4, AT, AT), lambda h: (h, 0, 0, 0)),
                   pl.BlockSpec((1, T, LANES), lambda h: (h, 0, 0))],
        out_shape=[jax.ShapeDtypeStruct((A_HEADS, 4, AT, AT), F32),
                   jax.ShapeDtypeStruct((A_HEADS, T, LANES), F32)],
        compiler_params=_params("arbitrary"),
        name="bias_tables",
    )(rel_bias, jnp.asarray(bt), jnp.asarray(bc))


def _inproj_kernel(x_ref, g_ref, w_ref, o_ref):
    x = x_ref[...]
    ms = jnp.mean(x * x, axis=-1, keepdims=True)
    h = (x * lax.rsqrt(ms + EPS) * g_ref[...]).astype(BF16)
    for j in range(0, D_PROJ, 384):
        o_ref[:, j:j + 384] = _dot(h, w_ref[:, j:j + 384])


def _inproj(x2, g, w):
    N = x2.shape[0]
    tm = 256
    return pl.pallas_call(
        _inproj_kernel,
        grid=(N // tm,),
        in_specs=[pl.BlockSpec((tm, D_MODEL), lambda i: (i, 0)),
                  pl.BlockSpec((1, D_MODEL), lambda i: (0, 0)),
                  pl.BlockSpec((D_MODEL, D_PROJ), lambda i: (0, 0))],
        out_specs=pl.BlockSpec((tm, D_PROJ), lambda i: (i, 0)),
        out_shape=jax.ShapeDtypeStruct((N, D_PROJ), F32),
        compiler_params=_params("parallel"),
        name="inproj",
    )(x2, g, w)


def _rms_lanes(x, g, width):
    ms = jnp.sum(x * x, axis=-1, keepdims=True) * (1.0 / width)
    return x * lax.rsqrt(ms + EPS) * g


def _qk_halves(q_r, k_blk):
    half_rows = q_r.shape[0] // 2
    return [_dot_nt(q_r[h0:h0 + half_rows, :], k_blk) for h0 in range(0, q_r.shape[0], half_rows)]


def _flash_chunk(q_r, k_blk, v_blk, bias_fn, m_r, acc_r):
    _softmax_pv(_qk_halves(q_r, k_blk), v_blk, bias_fn, m_r, acc_r)


def _softmax_pv(s_halves, v_blk, bias_fn, m_r, acc_r):
    half_rows = m_r.shape[0] // 2
    for hi_, h0 in enumerate(range(0, m_r.shape[0], half_rows)):
        s_all = s_halves[hi_]
        ps, alphas = [], []
        for r in range(half_rows // RB):
            rows = slice(h0 + r * RB, h0 + (r + 1) * RB)
            s = s_all[r * RB:(r + 1) * RB]
            if bias_fn is not None:
                s = s + bias_fn((h0 + r * RB) // RB)
            lo, hi = s[:, :LANES], s[:, LANES:]
            mo = m_r[rows, :]
            mn = jnp.maximum(mo, jnp.max(jnp.maximum(lo, hi), axis=-1, keepdims=True))
            ps.append(jnp.exp2(jnp.concatenate([lo - mn, hi - mn], axis=1)).astype(BF16))
            alphas.append(jnp.exp2(mo - mn))
            m_r[rows, :] = mn
        pv = _dot(jnp.concatenate(ps, axis=0), v_blk[hi_] if isinstance(v_blk, (list, tuple)) else v_blk)
        for r in range(half_rows // RB):
            rows = slice(h0 + r * RB, h0 + (r + 1) * RB)
            acc_r[rows, :] = alphas[r] * acc_r[rows, :] + pv[r * RB:(r + 1) * RB]


def _nsa_kernel(q_ref, k0_ref, v0_ref, k1_ref, v1_ref, k2_ref, v2_ref, gate_ref, tiles_ref, biasc_ref,
                ovt_ref, place_ref, w1_ref, w2_ref, pe_ref, qg_ref, kg_ref, o_ref,
                kc_s, vc_s, k1_s, v1_s, k2_s, v2_s, qx_s, s_s, m_s, acc_s, mw_s, accw_s):
    qi = pl.program_id(2)
    kg = kg_ref[...]
    n_cmp_pad = kc_s.shape[0]
    per_g = AT // RB

    @pl.when(qi == 0)
    def _prep():
        T = k1_s.shape[0]
        lane_t = lax.broadcasted_iota(jnp.int32, (T, LANES), 1)
        row_t = lax.broadcasted_iota(jnp.int32, (T, LANES), 0)
        block_onehot = lax.shift_right_logical(row_t, 6) == lane_t - HEAD_DIM
        ones_lane = lane_t == HEAD_DIM
        k1_s[...] = jnp.where(block_onehot, 1.0, _rms_lanes(k1_ref[0], kg, HEAD_DIM)).astype(BF16)
        v1_s[...] = jnp.where(ones_lane, 1.0, v1_ref[0]).astype(BF16)
        k2_s[...] = _rms_lanes(k2_ref[0], kg, HEAD_DIM).astype(BF16)
        v2_s[...] = jnp.where(ones_lane, 1.0, v2_ref[0]).astype(BF16)
        half = CMP_BLOCK // 2
        for s, (src, dst) in enumerate(((k0_ref, kc_s), (v0_ref, vc_s))):
            a = jnp.zeros((n_cmp_pad, HEAD_DIM), F32)
            bm = jnp.zeros((n_cmp_pad, HEAD_DIM), F32)
            for j in range(half):
                xj = src[0, pl.ds(j, n_cmp_pad, stride=CMP_STRIDE), :][:, :HEAD_DIM]
                a = a + _dot_f32(xj + pe_ref[s, j:j + 1, :], w1_ref[s, j])
                bm = bm + _dot_f32(xj + pe_ref[s, half + j:half + j + 1, :], w1_ref[s, half + j])
            pre = a + pltpu.roll(bm, n_cmp_pad - 1, axis=0)
            hid = pre * jax.nn.sigmoid(pre)
            comp = _dot_f32(hid, w2_ref[s])
            if s == 0:
                comp = _rms_lanes(comp, kg, HEAD_DIM)
            dst[...] = comp.astype(BF16)

    qscale = HEAD_DIM ** -0.5 * LOG2E
    q = q_ref[0]
    qg = qg_ref[...]
    qs = jnp.concatenate(
        [_rms_lanes(q[:, g * LANES:(g + 1) * LANES], qg, HEAD_DIM) * qscale for g in range(A_GROUP)],
        axis=0).astype(BF16)

    biasc = biasc_ref[...].reshape(A_GROUP * AT, LANES)
    valid = biasc > 0.5 * NEG
    lcm = _dot_nt(qs, kc_s[...]) + biasc
    mc = jnp.max(lcm, axis=-1, keepdims=True)
    pc = jnp.where(valid, jnp.exp2(lcm - mc), 0.0)
    lc = jnp.sum(pc, axis=-1, keepdims=True)
    pc = pc * jnp.where(lc > 0.0, 1.0 / lc, 0.0)
    o_c = _dot(pc.astype(BF16), vc_s[...])

    ovt = ovt_ref[...]
    p_hi, p_mid, p_lo = _split3(pc[0:AT] + pc[AT:2 * AT] + pc[2 * AT:3 * AT])
    ps = _dot_nt(ovt, p_hi) + _dot_nt(ovt, p_mid) + _dot_nt(ovt, p_lo)
    n_sel = ps.shape[0]
    sid = lax.broadcasted_iota(jnp.int32, (n_sel, AT), 0)
    qcol = lax.broadcasted_iota(jnp.int32, (n_sel, AT), 1)
    cur = qi * (AT // SEL_BLOCK) + lax.shift_right_logical(qcol, 6)
    forced = (sid == 0) | (sid == cur) | (sid == cur - 1)
    psf = jnp.where(forced, ps + FORCE, ps)
    psf = jnp.where(sid <= cur, psf, -1.0)
    rank = jnp.zeros((n_sel, AT), F32)
    for sp in range(n_sel):
        other = psf[sp:sp + 1, :]
        beats = (other > psf) | ((other == psf) & (sid > sp))
        rank = rank + jnp.where(beats, 1.0, 0.0)
    keep = (rank < float(min(N_SELECT, n_sel))) & (psf >= 0.0)
    selneg = _dot_tn(jnp.where(keep, 0.0, NEG).astype(BF16), place_ref[...]).astype(BF16)
    lane_q = lax.broadcasted_iota(jnp.int32, (AT, LANES), 1)
    for g in range(A_GROUP):
        qx_s[g * AT:(g + 1) * AT, :] = jnp.where(lane_q < HEAD_DIM, qs[g * AT:(g + 1) * AT], selneg)

    def near_bias(delta, kind):
        kd = jnp.where(qi >= delta, kind, 3)
        return lambda r: tiles_ref[r // per_g, kd, pl.ds((r % per_g) * RB, RB), :]

    def key_rows(kc):
        return pl.ds(pl.multiple_of(kc * AT, AT), AT)

    m_s[...] = jnp.full(m_s.shape, NEG, F32)
    acc_s[...] = jnp.zeros(acc_s.shape, F32)
    mw_s[...] = jnp.full(mw_s.shape, NEG, F32)
    accw_s[...] = jnp.zeros(accw_s.shape, F32)
    for delta, kind in ((0, 0), (1, 1), (2, 2)):
        rows_k = key_rows(jnp.maximum(qi - delta, 0))
        if delta < 2:
            _flash_chunk(qx_s, k1_s[rows_k, :], v1_s[rows_k, :], near_bias(delta, kind), m_s, acc_s)
        _flash_chunk(qx_s, k2_s[rows_k, :], v2_s[rows_k, :], near_bias(delta, kind), mw_s, accw_s)

    @pl.when(qi >= 2)
    def _first_scores():
        for h, s in enumerate(_qk_halves(qx_s, k1_s[key_rows(0), :])):
            s_s[h] = s

    def far_body(kc, carry):
        s_next = _qk_halves(qx_s, k1_s[key_rows(kc + 1), :])
        _softmax_pv([s_s[0], s_s[1]], v1_s[key_rows(kc), :], None, m_s, acc_s)
        for h, s in enumerate(s_next):
            s_s[h] = s
        return carry

    lax.fori_loop(0, jnp.maximum(qi - 1, 0), far_body, 0)

    gt = jax.nn.sigmoid(gate_ref[0])
    for g in range(A_GROUP):
        rows = slice(g * AT, (g + 1) * AT)
        a_s = acc_s[rows, :]
        a_w = accw_s[rows, :]
        o = (gt[:, 3 * g:3 * g + 1] * o_c[rows]
             + gt[:, 3 * g + 1:3 * g + 2] * (a_s / a_s[:, HEAD_DIM:HEAD_DIM + 1])
             + gt[:, 3 * g + 2:3 * g + 3] * (a_w / a_w[:, HEAD_DIM:HEAD_DIM + 1]))
        o_ref[0, :, g * LANES:(g + 1) * LANES] = jnp.where(lane_q < HEAD_DIM, o, 0.0)


def _nsa(proj, tiles, biasc, cmp_pe, cmp_w1, cmp_w2, q_norm, k_norm):
    B, T, _ = proj.shape
    nq = T // AT
    n_cmp_pad = T // CMP_STRIDE
    n_sel = T // SEL_BLOCK
    assert n_sel <= LANES - HEAD_DIM - 1 and n_cmp_pad == LANES
    starts = np.arange(n_cmp_pad) * CMP_STRIDE
    sid = np.arange(n_sel)
    overlap = ((starts[:, None] < (sid[None, :] + 1) * SEL_BLOCK)
               & (starts[:, None] + CMP_BLOCK > sid[None, :] * SEL_BLOCK)
               & (starts[:, None] + CMP_BLOCK <= T))
    place = np.zeros((n_sel, LANES), np.float32)
    place[sid, HEAD_DIM + sid] = 1.0
    w2p = jnp.pad(cmp_w2, ((0, 0), (0, 0), (0, LANES - HEAD_DIM)))

    def slab(br, kv):
        base = SEG_AKV // LANES + (br * 2 + kv) * 2
        return pl.BlockSpec((1, T, LANES), lambda b, h, i: (b, 0, base + h))

    def full(shape):
        return pl.BlockSpec(shape, lambda b, h, i: (0,) * len(shape))

    rows = A_GROUP * AT
    return pl.pallas_call(
        _nsa_kernel,
        grid=(B, A_KV_HEADS, nq),
        in_specs=[pl.BlockSpec((1, AT, A_GROUP * LANES), lambda b, h, i: (b, i, h)),
                  slab(0, 0), slab(0, 1), slab(1, 0), slab(1, 1), slab(2, 0), slab(2, 1),
                  pl.BlockSpec((1, AT, LANES), lambda b, h, i: (b, i, SEG_GATE // LANES + h)),
                  pl.BlockSpec((A_GROUP, 4, AT, AT), lambda b, h, i: (h, 0, 0, 0)),
                  pl.BlockSpec((A_GROUP, AT, LANES), lambda b, h, i: (h, i, 0)),
                  full((n_sel, n_cmp_pad)), full((n_sel, LANES)),
                  full((2, CMP_BLOCK, HEAD_DIM, HEAD_DIM)), full((2, HEAD_DIM, LANES)),
                  full((2, CMP_BLOCK, HEAD_DIM)), full((1, LANES)), full((1, LANES))],
        out_specs=pl.BlockSpec((1, AT, A_GROUP * LANES), lambda b, h, i: (b, i, h)),
        out_shape=jax.ShapeDtypeStruct((B, T, A_HEADS * LANES), F32),
        scratch_shapes=[pltpu.VMEM((n_cmp_pad, LANES), BF16), pltpu.VMEM((n_cmp_pad, LANES), BF16),
                        pltpu.VMEM((T, LANES), BF16), pltpu.VMEM((T, LANES), BF16),
                        pltpu.VMEM((T, LANES), BF16), pltpu.VMEM((T, LANES), BF16),
                        pltpu.VMEM((rows, LANES), BF16), pltpu.VMEM((2, rows // 2, AT), F32),
                        pltpu.VMEM((rows, LANES), F32), pltpu.VMEM((rows, LANES), F32),
                        pltpu.VMEM((rows, LANES), F32), pltpu.VMEM((rows, LANES), F32)],
        compiler_params=_params("parallel", "parallel", "arbitrary"),
        name="nsa",
    )(proj, proj, proj, proj, proj, proj, proj, proj, tiles, biasc,
      jnp.asarray(overlap.T, BF16), jnp.asarray(place, BF16), cmp_w1, w2p, cmp_pe,
      _pad_lanes(q_norm, LANES), _pad_lanes(k_norm, LANES))


def _mlstm_kernel(qk_ref, v_ref, o_ref, gif_ref, cw_ref, gb_ref, gn_ref, tril_ref, y_ref,
                  xpad_s, c_s, n_s, m_s):
    nb, tb = qk_ref.shape[0], qk_ref.shape[1]
    L = MLSTM_CHUNK
    n_pairs = B_HEADS // 2

    @pl.when(pl.program_id(1) == 0)
    def _init():
        xpad_s[:, tb:tb + 8, :] = jnp.zeros((nb, 8, xpad_s.shape[2]), F32)
        c_s[...] = jnp.zeros(c_s.shape, F32)
        n_s[...] = jnp.zeros(n_s.shape, F32)
        m_s[...] = jnp.zeros(m_s.shape, F32)

    for bb in range(nb):
        xpad_s[bb, 0:8, :] = xpad_s[bb, tb:tb + 8, :]
        xpad_s[bb, 8:8 + tb, :] = qk_ref[bb]
    lane = lax.broadcasted_iota(jnp.int32, (1, LANES), 1)
    lo_half = lane < HEAD_DIM
    rowi = lax.broadcasted_iota(jnp.int32, (LANES, LANES), 0)
    coli = lax.broadcasted_iota(jnp.int32, (LANES, LANES), 1)
    blockdiag = (rowi < HEAD_DIM) == (coli < HEAD_DIM)
    tri = lax.broadcasted_iota(jnp.int32, (L, L), 0) >= lax.broadcasted_iota(jnp.int32, (L, L), 1)
    kscale = HEAD_DIM ** -0.5

    def chunk(c, carry):
        for bb in range(nb):
            chunk_one(bb, c)
        return carry

    def chunk_one(bb, c):
        r0 = pl.multiple_of(c * L, L)
        win = xpad_s[bb, pl.ds(r0, L + 8), :]
        conv = cw_ref[CONV_K - 1:CONV_K, :] * win[8:8 + L]
        for j in range(CONV_K - 1):
            conv = conv + cw_ref[j:j + 1, :] * pltpu.roll(win, CONV_K - 1 - j, axis=0)[8:8 + L]
        qk = conv * jax.nn.sigmoid(conv)
        gif = gif_ref[bb, pl.ds(r0, L), :] + gb_ref[...]
        logf = jax.nn.log_sigmoid(gif)
        lh, lm, ll = _split3(logf)
        tril = tril_ref[...]
        bcum = _dot(tril, lh) + _dot(tril, lm) + _dot(tril, ll)
        gif_t = gif.T
        bcum_t = bcum.T
        for p in range(n_pairs):
            qp = qk[:, p * LANES:(p + 1) * LANES]
            kp = qk[:, 2 * LANES + p * LANES:2 * LANES + (p + 1) * LANES] * kscale
            vp = v_ref[bb, pl.ds(r0, L), p * LANES:(p + 1) * LANES]
            c_old = c_s[bb * n_pairs + p]
            n_old = n_s[bb * n_pairs + p]
            cq = _dot_nt(qp, c_old)
            hval, wcol, dec, mnew = [], [], [], []
            for half in range(2):
                hh = 2 * p + half
                hmask = lo_half if half == 0 else jnp.logical_not(lo_half)
                b_col = bcum[:, B_HEADS + hh:B_HEADS + hh + 1]
                b_row = bcum_t[B_HEADS + hh:B_HEADS + hh + 1, :]
                i_col = gif[:, hh:hh + 1]
                i_row = gif_t[hh:hh + 1, :]
                m_prev = m_s[bb * B_HEADS + hh][0:1, 0:1]
                dmat = jnp.where(tri, b_col - b_row + i_row, -jnp.inf)
                m_inter = b_col + m_prev
                m_j = jnp.maximum(m_inter, jnp.max(dmat, axis=-1, keepdims=True))
                isc = jnp.exp(m_inter - m_j)
                qm = jnp.where(hmask, qp, 0.0)
                s = _dot_nt(qm, kp) * jnp.exp(dmat - m_j)
                qn = jnp.sum(qm * n_old, axis=-1, keepdims=True)
                den = jnp.sum(s, axis=-1, keepdims=True) + isc * qn
                num = _dot(s, vp) + isc * cq
                hval.append(num / jnp.maximum(jnp.abs(den), jnp.exp(-m_j)))
                b_last = b_col[L - 1:L, :]
                w_log = b_last - b_col + i_col
                m_new = jnp.maximum(b_last + m_prev, jnp.max(w_log, axis=0, keepdims=True))
                wcol.append(jnp.exp(w_log - m_new))
                dec.append(jnp.exp(b_last + m_prev - m_new))
                mnew.append(m_new)
            w_l = jnp.where(lo_half, wcol[0], wcol[1])
            c_upd = _dot_tn(vp * w_l, kp)
            c_s[bb * n_pairs + p] = (jnp.where(rowi < HEAD_DIM, dec[0], dec[1]) * c_old
                                     + jnp.where(blockdiag, c_upd, 0.0))
            n_s[bb * n_pairs + p] = (jnp.where(lo_half, dec[0], dec[1]) * n_old
                                     + jnp.sum(w_l * kp, axis=0, keepdims=True))
            for half in range(2):
                m_s[bb * B_HEADS + 2 * p + half] = jnp.broadcast_to(mnew[half], m_s.shape[1:])
            h = jnp.where(lo_half, hval[0], hval[1])
            hsq = h * h
            ms0 = jnp.sum(jnp.where(lo_half, hsq, 0.0), axis=-1, keepdims=True) * (1.0 / HEAD_DIM)
            ms1 = jnp.sum(jnp.where(lo_half, 0.0, hsq), axis=-1, keepdims=True) * (1.0 / HEAD_DIM)
            hn = h * jnp.where(lo_half, lax.rsqrt(ms0 + EPS), lax.rsqrt(ms1 + EPS)) * gn_ref[:, p * LANES:(p + 1) * LANES]
            og = jax.nn.sigmoid(o_ref[bb, pl.ds(r0, L), p * LANES:(p + 1) * LANES])
            y_ref[bb, pl.ds(r0, L), p * LANES:(p + 1) * LANES] = og * hn

    lax.fori_loop(0, tb // L, chunk, 0)


MLSTM_NB = 2
MLSTM_TB = 512


def _mlstm(proj, conv_w, i_bias, f_bias, out_norm):
    B, T, _ = proj.shape
    W = B_HEADS * HEAD_DIM
    nb, tb = MLSTM_NB, MLSTM_TB
    gb = _pad_lanes(jnp.concatenate([i_bias, f_bias]), LANES)
    tril = jnp.asarray(np.tril(np.ones((MLSTM_CHUNK, MLSTM_CHUNK))), BF16)

    def full(shape):
        return pl.BlockSpec(shape, lambda b, t: (0,) * len(shape))

    return pl.pallas_call(
        _mlstm_kernel,
        grid=(B // nb, T // tb),
        in_specs=[pl.BlockSpec((nb, tb, 2 * W), lambda b, t: (b, t, SEG_BQK // (2 * W))),
                  pl.BlockSpec((nb, tb, W), lambda b, t: (b, t, SEG_BV // W)),
                  pl.BlockSpec((nb, tb, W), lambda b, t: (b, t, SEG_BO // W)),
                  pl.BlockSpec((nb, tb, LANES), lambda b, t: (b, t, SEG_BIF // LANES)),
                  full((CONV_K, 2 * W)), full((1, LANES)), full((1, W)),
                  full((MLSTM_CHUNK, MLSTM_CHUNK))],
        out_specs=pl.BlockSpec((nb, tb, W), lambda b, t: (b, t, 0)),
        out_shape=jax.ShapeDtypeStruct((B, T, W), F32),
        scratch_shapes=[pltpu.VMEM((nb, tb + 8, 2 * W), F32),
                        pltpu.VMEM((nb * B_HEADS // 2, LANES, LANES), F32),
                        pltpu.VMEM((nb * B_HEADS // 2, 1, LANES), F32),
                        pltpu.VMEM((nb * B_HEADS, 8, LANES), F32)],
        compiler_params=_params("parallel", "arbitrary"),
        name="mlstm",
    )(proj, proj, proj, proj, conv_w, gb, out_norm.reshape(1, W), tril)


def _mla_prep_kernel(dq_ref, dkv_ref, kr_ref, pos_ref, wq_ref, wk_ref, wv_ref, qa_ref, kva_ref,
                     gq_ref, gk_ref, gkr_ref, invf_ref, sgn_ref, q_out, k_out, v_out):
    lane = lax.broadcasted_iota(jnp.int32, (1, LANES), 1)
    nope = lane < C_NOPE
    ropem = (lane >= C_NOPE) & (lane < C_NOPE + C_ROPE)
    first = lane < C_NOPE + C_ROPE // 2
    ang = pos_ref[...].astype(F32) * invf_ref[...]
    cs = jnp.cos(ang)
    sn = jnp.sin(ang) * sgn_ref[...]

    def rope(x):
        partner = jnp.where(first, pltpu.roll(x, LANES - C_ROPE // 2, axis=1), pltpu.roll(x, C_ROPE // 2, axis=1))
        return x * cs + partner * sn

    scale = (C_NOPE + C_ROPE) ** -0.5 * LOG2E
    qn = _rms_lanes(dq_ref[...], qa_ref[...], Q_RANK).astype(BF16)
    q = _dot(qn, wq_ref[...])
    kvn = _rms_lanes(dkv_ref[...], kva_ref[...], KV_RANK).astype(BF16)
    kk = _dot(kvn, wk_ref[...])
    vv = _dot(kvn, wv_ref[...])
    lane_v = lax.broadcasted_iota(jnp.int32, vv.shape, 1)
    v_out[...] = jnp.where((lane_v & (LANES - 1)) == C_V, 1.0, vv).astype(BF16)
    kr = kr_ref[...]
    krn = kr * lax.rsqrt(jnp.sum(kr * kr, axis=-1, keepdims=True) * (1.0 / C_ROPE) + EPS) * gkr_ref[...]
    krr = rope(krn)
    for h in range(C_HEADS):
        sl = slice(h * LANES, (h + 1) * LANES)
        qh = q[:, sl]
        sq = qh * qh
        msn = jnp.sum(jnp.where(nope, sq, 0.0), axis=-1, keepdims=True) * (1.0 / C_NOPE)
        msr = jnp.sum(jnp.where(ropem, sq, 0.0), axis=-1, keepdims=True) * (1.0 / C_ROPE)
        qhn = qh * jnp.where(nope, lax.rsqrt(msn + EPS), lax.rsqrt(msr + EPS)) * gq_ref[...]
        q_out[:, sl] = (rope(qhn) * scale).astype(BF16)
        kh = kk[:, sl]
        msk = jnp.sum(kh * kh, axis=-1, keepdims=True) * (1.0 / C_NOPE)
        k_out[:, sl] = (kh * lax.rsqrt(msk + EPS) * gk_ref[...] + krr).astype(BF16)


def _mla_prep(proj2, pos, q_a_norm, w_uq, kv_a_norm, w_ukv, qk_norm):
    N = proj2.shape[0]
    tm = 512
    qd = C_NOPE + C_ROPE
    qmap = np.full((C_HEADS * LANES,), -1, np.int64)
    kmap = np.full((C_HEADS * LANES,), -1, np.int64)
    vmap = np.full((C_HEADS * LANES,), -1, np.int64)
    for h in range(C_HEADS):
        qmap[h * LANES:h * LANES + qd] = h * qd + np.arange(qd)
        kmap[h * LANES:h * LANES + C_NOPE] = h * (C_NOPE + C_V) + np.arange(C_NOPE)
        vmap[h * LANES:h * LANES + C_V] = h * (C_NOPE + C_V) + C_NOPE + np.arange(C_V)
    wq = jnp.pad(_take_cols(w_uq, qmap), ((0, 256 - Q_RANK), (0, 0))).astype(BF16)
    wk = _take_cols(w_ukv, kmap).astype(BF16)
    wv = _take_cols(w_ukv, vmap).astype(BF16)
    half = C_ROPE // 2
    inv = 1.0 / (ROPE_THETA ** (np.arange(half, dtype=np.float32) * 2.0 / C_ROPE))
    invf = np.zeros((1, LANES), np.float32)
    sgn = np.zeros((1, LANES), np.float32)
    invf[0, C_NOPE:C_NOPE + C_ROPE] = np.concatenate([inv, inv])
    sgn[0, C_NOPE:C_NOPE + half] = -1.0
    sgn[0, C_NOPE + half:C_NOPE + C_ROPE] = 1.0
    gkr = jnp.pad(qk_norm[1, C_NOPE:].reshape(1, -1), ((0, 0), (C_NOPE, LANES - C_NOPE - C_ROPE)))

    def full(shape):
        return pl.BlockSpec(shape, lambda i: (0,) * len(shape))

    return pl.pallas_call(
        _mla_prep_kernel,
        grid=(N // tm,),
        in_specs=[pl.BlockSpec((tm, 256), lambda i: (i, SEG_CDQ // 256)),
                  pl.BlockSpec((tm, LANES), lambda i: (i, SEG_CDKV // LANES)),
                  pl.BlockSpec((tm, LANES), lambda i: (i, SEG_CKR // LANES)),
                  pl.BlockSpec((tm, 1), lambda i: (i, 0)),
                  full((256, C_HEADS * LANES)), full((KV_RANK, C_HEADS * LANES)), full((KV_RANK, C_HEADS * LANES)),
                  full((1, 256)), full((1, LANES)), full((1, LANES)), full((1, LANES)), full((1, LANES)),
                  full((1, LANES)), full((1, LANES))],
        out_specs=[pl.BlockSpec((tm, C_HEADS * LANES), lambda i: (i, 0)),
                   pl.BlockSpec((tm, C_HEADS * LANES), lambda i: (i, 0)),
                   pl.BlockSpec((tm, C_HEADS * LANES), lambda i: (i, 0))],
        out_shape=[jax.ShapeDtypeStruct((N, C_HEADS * LANES), BF16),
                   jax.ShapeDtypeStruct((N, C_HEADS * LANES), BF16),
                   jax.ShapeDtypeStruct((N, C_HEADS * LANES), BF16)],
        compiler_params=_params("parallel"),
        name="mla_prep",
    )(proj2, proj2, proj2, pos, wq, wk, wv,
      _pad_lanes(q_a_norm, 256), kv_a_norm.reshape(1, -1), _pad_lanes(qk_norm[0], LANES),
      _pad_lanes(qk_norm[1, :C_NOPE], LANES), gkr, jnp.asarray(invf), jnp.asarray(sgn))


def _mla_attn_kernel(q_ref, k_ref, v_ref, causal_ref, o_ref, s_s, m_s, acc_s):
    qi = pl.program_id(2)
    m_s[...] = jnp.full(m_s.shape, NEG, F32)
    acc_s[...] = jnp.zeros(acc_s.shape, F32)
    heads = [slice(hh * LANES, (hh + 1) * LANES) for hh in range(2)]
    per_head = AT // RB

    def key_rows(kc):
        return pl.ds(pl.multiple_of(kc * AT, AT), AT)

    def scores(kc):
        return [_dot_nt(q_ref[0, :, sl], k_ref[0, key_rows(kc), sl]) for sl in heads]

    def values(kc):
        return [v_ref[0, key_rows(kc), sl] for sl in heads]

    _softmax_pv(scores(qi), values(qi), lambda r: causal_ref[pl.ds((r % per_head) * RB, RB), :], m_s, acc_s)

    @pl.when(qi >= 1)
    def _first_scores():
        for h, s in enumerate(scores(0)):
            s_s[h] = s

    def body(kc, carry):
        s_next = scores(kc + 1)
        _softmax_pv([s_s[0], s_s[1]], values(kc), None, m_s, acc_s)
        for h, s in enumerate(s_next):
            s_s[h] = s
        return carry

    lax.fori_loop(0, qi, body, 0)
    lane = lax.broadcasted_iota(jnp.int32, (AT, LANES), 1)
    outs = []
    for hh in range(2):
        a = acc_s[hh * AT:(hh + 1) * AT, :]
        outs.append(jnp.where(lane < C_V, a / a[:, C_V:C_V + 1], 0.0))
    o_ref[0] = outs[0] + pltpu.roll(outs[1], C_V, axis=1)


def _mla_attn(q, k, v):
    B, T, _ = q.shape
    causal = np.where(np.arange(AT)[:, None] >= np.arange(AT)[None, :], 0.0, NEG).astype(np.float32)
    pair = pl.BlockSpec((1, T, 2 * LANES), lambda b, p, i: (b, 0, p))
    return pl.pallas_call(
        _mla_attn_kernel,
        grid=(B, C_HEADS // 2, T // AT),
        in_specs=[pl.BlockSpec((1, AT, 2 * LANES), lambda b, p, i: (b, i, p)), pair, pair,
                  pl.BlockSpec((AT, AT), lambda b, p, i: (0, 0))],
        out_specs=pl.BlockSpec((1, AT, LANES), lambda b, p, i: (b, i, p)),
        out_shape=jax.ShapeDtypeStruct((B, T, C_HEADS * C_V), F32),
        scratch_shapes=[pltpu.VMEM((2, AT, AT), F32), pltpu.VMEM((2 * AT, LANES), F32),
                        pltpu.VMEM((2 * AT, LANES), F32)],
        compiler_params=_params("parallel", "parallel", "arbitrary"),
        name="mla_attn",
    )(q, k, v, jnp.asarray(causal))


def _outproj_kernel(x_ref, ya_ref, yb_ref, yc_ref, wa_ref, wb_ref, wc_ref, ga_ref, gc_ref, g2_ref,
                    wr_ref, br_ref, x1_ref, h2_ref, ids_ref, wts_ref):
    ya = _rms_lanes(ya_ref[...], ga_ref[...], A_HEADS * HEAD_DIM).astype(BF16)
    yc = _rms_lanes(yc_ref[...], gc_ref[...], C_HEADS * C_V).astype(BF16)
    x1 = (x_ref[...] + _dot(ya, wa_ref[...]) + _dot(yb_ref[...].astype(BF16), wb_ref[...])
          + _dot(yc, wc_ref[...]))
    x1_ref[...] = x1
    h2 = _rms_lanes(x1, g2_ref[...], D_MODEL)
    h2_ref[...] = h2
    logits = _dot_f32(h2, wr_ref[...]) + br_ref[...]
    tm = logits.shape[0]
    lane = lax.broadcasted_iota(jnp.int32, (tm, LANES), 1)
    is_g = (lane >= N_EXPERTS) & (lane < N_EXPERTS + N_GROUPS)
    gl = jnp.where(is_g, logits, NEG)
    ge = jnp.where(is_g, jnp.exp(gl - jnp.max(gl, axis=-1, keepdims=True)), 0.0)
    gp = jnp.where(is_g, ge / jnp.sum(ge, axis=-1, keepdims=True), -1.0)
    g_top = jnp.max(gp, axis=-1, keepdims=True)
    g_idx = jnp.min(jnp.where(gp == g_top, lane, 2 * LANES), axis=-1, keepdims=True) - N_EXPERTS
    in_grp = (lane < N_EXPERTS) & (lax.shift_right_logical(lane, 3) == g_idx)
    el = jnp.where(in_grp, logits, NEG)
    ee = jnp.where(in_grp, jnp.exp(el - jnp.max(el, axis=-1, keepdims=True)), 0.0)
    ep = jnp.where(in_grp, ee / jnp.sum(ee, axis=-1, keepdims=True), -1.0)
    v1 = jnp.max(ep, axis=-1, keepdims=True)
    i1 = jnp.min(jnp.where(ep == v1, lane, 2 * LANES), axis=-1, keepdims=True)
    ep2 = jnp.where(lane == i1, -1.0, ep)
    v2 = jnp.max(ep2, axis=-1, keepdims=True)
    i2 = jnp.min(jnp.where(ep2 == v2, lane, 2 * LANES), axis=-1, keepdims=True)
    tot = v1 + v2
    ids_ref[...] = jnp.where(lane == 0, i1, jnp.where(lane == 1, i2, 0))
    wts_ref[...] = jnp.where(lane == 0, v1 / tot * g_top, jnp.where(lane == 1, v2 / tot * g_top, 0.0))


def _outproj(x2, ya, yb, yc, w_out, nsa_out_norm, mla_out_norm, norm2_g, w_group, b_group, w_expert, b_expert):
    N = x2.shape[0]
    tm = 256
    amap = np.full((A_HEADS * LANES,), -1, np.int64)
    for hd in range(A_HEADS):
        amap[hd * LANES:hd * LANES + HEAD_DIM] = hd * HEAD_DIM + np.arange(HEAD_DIM)
    wa = _take_rows(w_out[:384], amap).astype(BF16)
    wb = w_out[384:640].astype(BF16)
    wc = w_out[640:].astype(BF16)
    ga = _take_cols(nsa_out_norm.reshape(1, -1), amap)
    wr = jnp.pad(jnp.concatenate([w_expert, w_group], axis=1), ((0, 0), (0, LANES - N_EXPERTS - N_GROUPS)))
    br = _pad_lanes(jnp.concatenate([b_expert, b_group]), LANES)

    def full(shape):
        return pl.BlockSpec(shape, lambda i: (0,) * len(shape))

    def rows(w):
        return pl.BlockSpec((tm, w), lambda i: (i, 0))

    return pl.pallas_call(
        _outproj_kernel,
        grid=(N // tm,),
        in_specs=[rows(D_MODEL), rows(A_HEADS * LANES), rows(256), rows(384),
                  full((A_HEADS * LANES, D_MODEL)), full((256, D_MODEL)), full((384, D_MODEL)),
                  full((1, A_HEADS * LANES)), full((1, 384)), full((1, D_MODEL)),
                  full((D_MODEL, LANES)), full((1, LANES))],
        out_specs=[rows(D_MODEL), rows(D_MODEL), rows(LANES), rows(LANES)],
        out_shape=[jax.ShapeDtypeStruct((N, D_MODEL), F32), jax.ShapeDtypeStruct((N, D_MODEL), F32),
                   jax.ShapeDtypeStruct((N, LANES), jnp.int32), jax.ShapeDtypeStruct((N, LANES), F32)],
        compiler_params=_params("parallel"),
        name="outproj_router",
    )(x2, ya, yb, yc, wa, wb, wc, ga, mla_out_norm.reshape(1, -1), norm2_g.reshape(1, -1), wr, br)


def _rank_kernel(ids_ref, tril_ref, rank_ref, cnt_ref, carry_s):
    i = pl.program_id(0)

    @pl.when(i == 0)
    def _():
        carry_s[...] = jnp.zeros(carry_s.shape, F32)

    ids = ids_ref[...]
    tm = ids.shape[0]
    lane = lax.broadcasted_iota(jnp.int32, (tm, LANES), 1)
    i1 = ids[:, 0:1]
    i2 = ids[:, 1:2]
    oh = jnp.where((lane == i1) | (lane == i2), 1.0, 0.0)
    incl = _dot(tril_ref[...], oh.astype(BF16)) + carry_s[0:1, :]
    excl = incl - oh
    r1 = jnp.sum(jnp.where(lane == i1, excl, 0.0), axis=-1, keepdims=True)
    r2 = jnp.sum(jnp.where(lane == i2, excl, 0.0), axis=-1, keepdims=True)
    rank_ref[...] = jnp.where(lane == 0, r1, jnp.where(lane == 1, r2, 0.0)).astype(jnp.int32)
    tot = incl[tm - 1:tm, :]
    carry_s[...] = jnp.broadcast_to(tot, carry_s.shape)
    cnt_ref[...] = jnp.broadcast_to(tot, cnt_ref.shape).astype(jnp.int32)


def _moe_rank(ids):
    N = ids.shape[0]
    tm = 512
    tril = jnp.asarray(np.tril(np.ones((tm, tm))), BF16)
    return pl.pallas_call(
        _rank_kernel,
        grid=(N // tm,),
        in_specs=[pl.BlockSpec((tm, LANES), lambda i: (i, 0)), pl.BlockSpec((tm, tm), lambda i: (0, 0))],
        out_specs=[pl.BlockSpec((tm, LANES), lambda i: (i, 0)), pl.BlockSpec((8, LANES), lambda i: (0, 0))],
        out_shape=[jax.ShapeDtypeStruct((N, LANES), jnp.int32), jax.ShapeDtypeStruct((8, LANES), jnp.int32)],
        scratch_shapes=[pltpu.VMEM((8, LANES), F32)],
        compiler_params=_params("arbitrary"),
        name="moe_rank",
    )(ids, tril)


DISPATCH_TM = 1024


DMA_UNROLL = 8


def _dispatch_kernel(p1_ref, p2_ref, h_ref, zeros_ref, xs_ref, sem):
    del zeros_ref

    def row_copy(i, pos):
        return pltpu.make_async_copy(h_ref.at[pl.ds(i, 1), :], xs_ref.at[pl.ds(pos, 1), :], sem)

    def issue(j, carry):
        for u in range(DMA_UNROLL):
            i = j * DMA_UNROLL + u
            row_copy(i, p1_ref[i]).start(priority=0)
            row_copy(i, p2_ref[i]).start(priority=1)
        return carry

    lax.fori_loop(0, DISPATCH_TM // DMA_UNROLL, issue, 0)
    whole = pltpu.make_async_copy(h_ref, xs_ref.at[pl.ds(0, DISPATCH_TM), :], sem)
    whole.wait()
    whole.wait()


def _moe_dispatch(h2, pos1, pos2, n_rows):
    N = h2.shape[0]
    smem1d = pl.BlockSpec((DISPATCH_TM,), lambda i: (i,), memory_space=pltpu.SMEM)
    return pl.pallas_call(
        _dispatch_kernel,
        grid=(N // DISPATCH_TM,),
        in_specs=[smem1d, smem1d,
                  pl.BlockSpec((DISPATCH_TM, D_MODEL), lambda i: (i, 0)),
                  pl.BlockSpec(memory_space=pl.ANY)],
        out_specs=pl.BlockSpec(memory_space=pl.ANY),
        out_shape=jax.ShapeDtypeStruct((n_rows, D_MODEL), F32),
        scratch_shapes=[pltpu.SemaphoreType.DMA(())],
        input_output_aliases={3: 0},
        compiler_params=_params("arbitrary"),
        name="moe_dispatch",
    )(pos1, pos2, h2, jnp.zeros((n_rows, D_MODEL), F32))


def _expert_kernel(te_ref, nu_ref, x_ref, wg_ref, wu_ref, wd_ref, y_ref):
    g = pl.program_id(0)

    @pl.when(g < nu_ref[0])
    def _():
        x = x_ref[...].astype(BF16)
        a = _dot(x, wg_ref[0, 0].astype(BF16))
        u = _dot(x, wu_ref[0, 0].astype(BF16))
        h = (a * jax.nn.sigmoid(a) * u).astype(BF16)
        y_ref[...] = _dot(h, wd_ref[0, 0].astype(BF16))

    @pl.when(g >= nu_ref[0])
    def _():
        y_ref[...] = jnp.zeros(y_ref.shape, F32)


def _moe_experts(xs, tile_expert, n_used, layer, w_gate, w_up, w_down):
    R = xs.shape[0]
    G = R // ROW_TILE

    def row_map(g, te, nu):
        return (jnp.minimum(g, nu[0] - 1), 0)

    def w_map(g, te, nu):
        return (layer, te[jnp.minimum(g, nu[0] - 1)], 0, 0)

    return pl.pallas_call(
        _expert_kernel,
        grid_spec=pltpu.PrefetchScalarGridSpec(
            num_scalar_prefetch=2,
            grid=(G,),
            in_specs=[pl.BlockSpec((ROW_TILE, D_MODEL), row_map),
                      pl.BlockSpec((1, 1, D_MODEL, D_EXPERT), w_map),
                      pl.BlockSpec((1, 1, D_MODEL, D_EXPERT), w_map),
                      pl.BlockSpec((1, 1, D_EXPERT, D_MODEL), w_map)],
            out_specs=pl.BlockSpec((ROW_TILE, D_MODEL), lambda g, te, nu: (g, 0))),
        out_shape=jax.ShapeDtypeStruct((R, D_MODEL), F32),
        compiler_params=_params("arbitrary"),
        name="moe_experts",
    )(tile_expert, n_used, xs, w_gate, w_up, w_down)


COMBINE_TM = 256


def _combine_kernel(p1_ref, p2_ref, x1_ref, wts_ref, ys_ref, o_ref, b1_s, b2_s, sem):
    def row_copy(buf, i, pos):
        return pltpu.make_async_copy(ys_ref.at[pl.ds(pos, 1), :], buf.at[pl.ds(i, 1), :], sem)

    def issue(j, carry):
        for u in range(DMA_UNROLL):
            i = j * DMA_UNROLL + u
            row_copy(b1_s, i, p1_ref[i]).start(priority=0)
            row_copy(b2_s, i, p2_ref[i]).start(priority=1)
        return carry

    lax.fori_loop(0, COMBINE_TM // DMA_UNROLL, issue, 0)
    pltpu.make_async_copy(ys_ref.at[pl.ds(0, COMBINE_TM), :], b1_s, sem).wait()
    pltpu.make_async_copy(ys_ref.at[pl.ds(0, COMBINE_TM), :], b2_s, sem).wait()
    w = wts_ref[...]
    o_ref[...] = x1_ref[...] + w[:, 0:1] * b1_s[...] + w[:, 1:2] * b2_s[...]


def _moe_combine(x1, wts, ys, pos1, pos2):
    N = x1.shape[0]
    smem1d = pl.BlockSpec((COMBINE_TM,), lambda i: (i,), memory_space=pltpu.SMEM)
    return pl.pallas_call(
        _combine_kernel,
        grid=(N // COMBINE_TM,),
        in_specs=[smem1d, smem1d,
                  pl.BlockSpec((COMBINE_TM, D_MODEL), lambda i: (i, 0)),
                  pl.BlockSpec((COMBINE_TM, LANES), lambda i: (i, 0)),
                  pl.BlockSpec(memory_space=pl.ANY)],
        out_specs=pl.BlockSpec((COMBINE_TM, D_MODEL), lambda i: (i, 0)),
        out_shape=jax.ShapeDtypeStruct((N, D_MODEL), F32),
        scratch_shapes=[pltpu.VMEM((COMBINE_TM, D_MODEL), F32), pltpu.VMEM((COMBINE_TM, D_MODEL), F32),
                        pltpu.SemaphoreType.DMA(())],
        compiler_params=_params("arbitrary"),
        name="moe_combine",
    )(pos1, pos2, x1, wts, ys)


def _moe(x1, h2, ids, wts, layer, w_gate, w_up, w_down):
    N = x1.shape[0]
    n_rows = 2 * N + N_EXPERTS * ROW_TILE
    rank, cnt = _moe_rank(ids)
    counts = cnt[0, :N_EXPERTS]
    tiles_per = (counts + ROW_TILE - 1) // ROW_TILE
    tile_end = jnp.cumsum(tiles_per)
    off = ((tile_end - tiles_per) * ROW_TILE).astype(jnp.int32)
    n_used = tile_end[-1:].astype(jnp.int32)
    tile_ids = jnp.arange(n_rows // ROW_TILE, dtype=jnp.int32)
    tile_expert = jnp.minimum(jnp.sum(tile_end[None, :] <= tile_ids[:, None], axis=1), N_EXPERTS - 1).astype(jnp.int32)
    pos = off[ids[:, :2]] + rank[:, :2]
    pos1, pos2 = pos[:, 0], pos[:, 1]
    xs = _moe_dispatch(h2, pos1, pos2, n_rows)
    ys = _moe_experts(xs, tile_expert, n_used, layer, w_gate, w_up, w_down)
    return _moe_combine(x1, wts, ys, pos1, pos2)


def kernel(x, positions, rel_bias, norm1_g, w_in, nsa_cmp_pe, nsa_cmp_w1, nsa_cmp_w2, nsa_q_norm, nsa_k_norm, nsa_out_norm, mlstm_conv, mlstm_i_bias, mlstm_f_bias, mlstm_out_norm, mla_q_a_norm, mla_w_uq, mla_kv_a_norm, mla_w_ukv, mla_qk_norm, mla_out_norm, w_out, norm2_g, moe_w_group, moe_b_group, moe_w_expert, moe_b_expert, moe_w_gate, moe_w_up, moe_w_down):
    B, T, D = x.shape
    N = B * T
    depth = w_in.shape[0]
    colmap = _proj_colmap()
    tiles, biasc = _bias_tables(rel_bias, T)
    pos = positions.reshape(N, 1)
    x2 = x.reshape(N, D)
    for l in range(depth):
        w_proj = _take_cols(w_in[l], colmap).astype(BF16)
        proj2 = _inproj(x2, norm1_g[l].reshape(1, D), w_proj)
        proj = proj2.reshape(B, T, D_PROJ)
        ya = _nsa(proj, tiles, biasc, nsa_cmp_pe[l], nsa_cmp_w1[l], nsa_cmp_w2[l], nsa_q_norm[l], nsa_k_norm[l])
        yb = _mlstm(proj, mlstm_conv[l], mlstm_i_bias[l], mlstm_f_bias[l], mlstm_out_norm[l])
        q, k, v = _mla_prep(proj2, pos, mla_q_a_norm[l], mla_w_uq[l], mla_kv_a_norm[l], mla_w_ukv[l], mla_qk_norm[l])
        yc = _mla_attn(q.reshape(B, T, -1), k.reshape(B, T, -1), v.reshape(B, T, -1))
        x1, h2, ids, wts = _outproj(x2, ya.reshape(N, -1), yb.reshape(N, -1), yc.reshape(N, -1), w_out[l],
                                    nsa_out_norm[l], mla_out_norm[l], norm2_g[l], moe_w_group[l], moe_b_group[l],
                                    moe_w_expert[l], moe_b_expert[l])
        x2 = _moe(x1, h2, ids, wts, l, moe_w_gate, moe_w_up, moe_w_down)
    return x2.reshape(B, T, D)
```

```python
import functools
import math

import numpy as np
import jax
import jax.numpy as jnp
from jax import lax
from jax.experimental import pallas as pl
from jax.experimental.pallas import tpu as pltpu

F32 = jnp.float32
BF16 = jnp.bfloat16

D_MODEL = 1024
HEAD_DIM = 64
LANES = 128
A_HEADS, A_KV_HEADS, A_GROUP = 6, 2, 3
CMP_BLOCK, CMP_STRIDE, SEL_BLOCK, N_SELECT, WINDOW = 32, 16, 64, 16, 512
FORCE = 1e4
B_HEADS, CONV_K, MLSTM_CHUNK = 4, 4, 64
C_HEADS, C_NOPE, C_ROPE, C_V = 6, 64, 32, 64
Q_RANK, KV_RANK = 192, 128
ROPE_THETA = 10000.0
REL_BUCKETS, REL_MAX_DIST = 32, 128
N_GROUPS, EXPERTS_PER_GROUP, N_EXPERTS, D_EXPERT = 4, 8, 32, 256
EPS = 1e-6
NEG = -1e30

LOG2E = 1.4426950408889634

AT = 256
RB = 128
ROW_TILE = 256
VMEM_LIMIT = 48 * 1024 * 1024

SEG_AQ, SEG_AKV, SEG_GATE, SEG_BQK, SEG_BV, SEG_BO = 0, 768, 2304, 2560, 3072, 3328
SEG_CDQ, SEG_CDKV, SEG_CKR, SEG_BIF, D_PROJ = 3584, 3840, 3968, 4096, 4224


def _params(*sem):
    return pltpu.CompilerParams(dimension_semantics=sem, vmem_limit_bytes=VMEM_LIMIT)


def _dot(a, b):
    return jnp.dot(a, b, preferred_element_type=F32)


def _dot_nt(a, b):
    return lax.dot_general(a, b, (((1,), (1,)), ((), ())), preferred_element_type=F32)


def _dot_tn(a, b):
    return lax.dot_general(a, b, (((0,), (0,)), ((), ())), preferred_element_type=F32)


def _split3(x):
    hi = x.astype(BF16)
    r1 = x - hi.astype(F32)
    mid = r1.astype(BF16)
    lo = (r1 - mid.astype(F32)).astype(BF16)
    return hi, mid, lo


def _dot_exact_rhs(x, m_bf16):
    hi, mid, lo = _split3(x)
    return _dot(hi, m_bf16) + _dot(mid, m_bf16) + _dot(lo, m_bf16)


def _dot_f32(x, w):
    xh = x.astype(BF16)
    xm = (x - xh.astype(F32)).astype(BF16)
    wh = w.astype(BF16)
    wm = (w - wh.astype(F32)).astype(BF16)
    return _dot(xh, wh) + _dot(xm, wh) + _dot(xh, wm)


def _bucket_np(dist):
    n = np.maximum(dist, 0)
    exact = REL_BUCKETS // 2
    nf = np.maximum(n, 1).astype(np.float64)
    large = exact + (np.log(nf / exact) / math.log(REL_MAX_DIST / exact) * (REL_BUCKETS - exact)).astype(np.int64)
    large = np.minimum(large, REL_BUCKETS - 1)
    return np.where(n < exact, n, large).astype(np.int32)


def _bucket_tables(T):
    r = np.arange(AT)[:, None]
    c = np.arange(AT)[None, :]
    diag = np.where(r >= c, _bucket_np(r - c), -1)
    prev = _bucket_np(AT + r - c)
    far = np.where(c > r, _bucket_np(2 * AT + r - c), -1)
    tiles = np.stack([diag, prev, far]).astype(np.int32)
    tq = np.arange(T)[:, None]
    n = np.arange(LANES)[None, :]
    dist_c = tq - (n * CMP_STRIDE + CMP_BLOCK - 1)
    cmp_tbl = np.where(dist_c >= 0, _bucket_np(dist_c), -1).astype(np.int32)
    return tiles, cmp_tbl


def _proj_colmap():
    cm = np.full((D_PROJ,), -1, np.int64)
    for hd in range(A_HEADS):
        cm[SEG_AQ + hd * LANES: SEG_AQ + hd * LANES + HEAD_DIM] = hd * HEAD_DIM + np.arange(HEAD_DIM)
    for s in range(12):
        cm[SEG_AKV + s * LANES: SEG_AKV + s * LANES + HEAD_DIM] = 384 + s * HEAD_DIM + np.arange(HEAD_DIM)
    for h in range(A_KV_HEADS):
        cm[SEG_GATE + h * LANES: SEG_GATE + h * LANES + 9] = 1152 + h * 9 + np.arange(9)
    cm[SEG_BQK: SEG_BQK + 512] = 1170 + np.arange(512)
    cm[SEG_BV: SEG_BV + 256] = 1682 + np.arange(256)
    cm[SEG_BIF: SEG_BIF + 8] = 1938 + np.arange(8)
    cm[SEG_BO: SEG_BO + 256] = 1946 + np.arange(256)
    cm[SEG_CDQ: SEG_CDQ + Q_RANK] = 2202 + np.arange(Q_RANK)
    cm[SEG_CDKV: SEG_CDKV + KV_RANK] = 2394 + np.arange(KV_RANK)
    cm[SEG_CKR + C_NOPE: SEG_CKR + C_NOPE + C_ROPE] = 2522 + np.arange(C_ROPE)
    return cm


def _take_cols(w, colmap):
    wz = jnp.concatenate([w, jnp.zeros((w.shape[0], 1), w.dtype)], axis=1)
    return wz[:, np.where(colmap < 0, w.shape[1], colmap)]


def _take_rows(w, rowmap):
    return _take_cols(w.T, rowmap).T


def _pad_lanes(v, n):
    v = v.reshape(1, -1)
    return jnp.pad(v, ((0, 0), (0, n - v.shape[1])))


def _bias_kernel(rb_ref, bt_ref, bc_ref, tiles_ref, cmp_ref):
    hd = pl.program_id(0)
    bt = bt_ref[...]
    bc = bc_ref[...]
    t = jnp.full(bt.shape, NEG, F32)
    c = jnp.full(bc.shape, NEG, F32)
    far = rb_ref[REL_BUCKETS - 1, hd]
    for b in range(REL_BUCKETS):
        val = rb_ref[b, hd]
        t = jnp.where(bt == b, (val - far) * LOG2E, t)
        c = jnp.where(bc == b, val * LOG2E, c)
    tiles_ref[0, 0:3] = t
    tiles_ref[0, 3] = jnp.full((AT, AT), NEG, F32)
    cmp_ref[0] = c


def _bias_tables(rel_bias, T):
    bt, bc = _bucket_tables(T)
    return pl.pallas_call(
        _bias_kernel,
        grid=(A_HEADS,),
        in_specs=[pl.BlockSpec(memory_space=pltpu.SMEM),
                  pl.BlockSpec((3, AT, AT), lambda h: (0, 0, 0)),
                  pl.BlockSpec((T, LANES), lambda h: (0, 0))],
        out_specs=[pl.BlockSpec((1, 4, AT, AT), lambda h: (h, 0, 0, 0)),
                   pl.BlockSpec((1, T, LANES), lambda h: (h, 0, 0))],
        out_shape=[jax.ShapeDtypeStruct((A_HEADS, 4, AT, AT), F32),
                   jax.ShapeDtypeStruct((A_HEADS, T, LANES), F32)],
        compiler_params=_params("arbitrary"),
        name="bias_tables",
    )(rel_bias, jnp.asarray(bt), jnp.asarray(bc))


def _inproj_kernel(x_ref, g_ref, w_ref, o_ref):
    x = x_ref[...]
    ms = jnp.mean(x * x, axis=-1, keepdims=True)
    h = (x * lax.rsqrt(ms + EPS) * g_ref[...]).astype(BF16)
    for j in range(0, D_PROJ, 384):
        o_ref[:, j:j + 384] = _dot(h, w_ref[:, j:j + 384])


def _inproj(x2, g, w):
    N = x2.shape[0]
    tm = 512
    return pl.pallas_call(
        _inproj_kernel,
        grid=(N // tm,),
        in_specs=[pl.BlockSpec((tm, D_MODEL), lambda i: (i, 0)),
                  pl.BlockSpec((1, D_MODEL), lambda i: (0, 0)),
                  pl.BlockSpec((D_MODEL, D_PROJ), lambda i: (0, 0))],
        out_specs=pl.BlockSpec((tm, D_PROJ), lambda i: (i, 0)),
        out_shape=jax.ShapeDtypeStruct((N, D_PROJ), F32),
        compiler_params=_params("parallel"),
        name="inproj",
    )(x2, g, w)


def _rms_lanes(x, g, width):
    ms = jnp.sum(x * x, axis=-1, keepdims=True) * (1.0 / width)
    return x * lax.rsqrt(ms + EPS) * g


def _qk_halves(q_r, k_blk):
    half_rows = q_r.shape[0] // 2
    return [_dot_nt(q_r[h0:h0 + half_rows, :], k_blk) for h0 in range(0, q_r.shape[0], half_rows)]


def _flash_chunk(q_r, k_blk, v_blk, bias_fn, m_r, acc_r):
    _softmax_pv(_qk_halves(q_r, k_blk), v_blk, bias_fn, m_r, acc_r)


def _softmax_pv(s_halves, v_blk, bias_fn, m_r, acc_r):
    half_rows = m_r.shape[0] // 2
    for hi_, h0 in enumerate(range(0, m_r.shape[0], half_rows)):
        s_all = s_halves[hi_]
        ps, alphas = [], []
        for r in range(half_rows // RB):
            rows = slice(h0 + r * RB, h0 + (r + 1) * RB)
            s = s_all[r * RB:(r + 1) * RB]
            if bias_fn is not None:
                s = s + bias_fn((h0 + r * RB) // RB)
            lo, hi = s[:, :LANES], s[:, LANES:]
            mo = m_r[rows, :]
            mn = jnp.maximum(mo, jnp.max(jnp.maximum(lo, hi), axis=-1, keepdims=True))
            ps.append(jnp.exp2(jnp.concatenate([lo - mn, hi - mn], axis=1)).astype(BF16))
            alphas.append(jnp.exp2(mo - mn))
            m_r[rows, :] = mn
        pv = _dot(jnp.concatenate(ps, axis=0), v_blk[hi_] if isinstance(v_blk, (list, tuple)) else v_blk)
        for r in range(half_rows // RB):
            rows = slice(h0 + r * RB, h0 + (r + 1) * RB)
            acc_r[rows, :] = alphas[r] * acc_r[rows, :] + pv[r * RB:(r + 1) * RB]


def _nsa_kernel(q_ref, k0_ref, v0_ref, k1_ref, v1_ref, k2_ref, v2_ref, gate_ref, tiles_ref, biasc_ref,
                ovt_ref, place_ref, w1_ref, w2_ref, pe_ref, qg_ref, kg_ref, o_ref,
                kc_s, vc_s, k1_s, v1_s, k2_s, v2_s, qx_s, s_s, m_s, acc_s, mw_s, accw_s):
    qi = pl.program_id(2)
    kg = kg_ref[...]
    n_cmp_pad = kc_s.shape[0]
    per_g = AT // RB

    @pl.when(qi == 0)
    def _prep():
        T = k1_s.shape[0]
        lane_t = lax.broadcasted_iota(jnp.int32, (T, LANES), 1)
        row_t = lax.broadcasted_iota(jnp.int32, (T, LANES), 0)
        block_onehot = lax.shift_right_logical(row_t, 6) == lane_t - HEAD_DIM
        ones_lane = lane_t == HEAD_DIM
        k1_s[...] = jnp.where(block_onehot, 1.0, _rms_lanes(k1_ref[0], kg, HEAD_DIM)).astype(BF16)
        v1_s[...] = jnp.where(ones_lane, 1.0, v1_ref[0]).astype(BF16)
        k2_s[...] = _rms_lanes(k2_ref[0], kg, HEAD_DIM).astype(BF16)
        v2_s[...] = jnp.where(ones_lane, 1.0, v2_ref[0]).astype(BF16)
        half = CMP_BLOCK // 2
        for s, (src, dst) in enumerate(((k0_ref, kc_s), (v0_ref, vc_s))):
            a = jnp.zeros((n_cmp_pad, HEAD_DIM), F32)
            bm = jnp.zeros((n_cmp_pad, HEAD_DIM), F32)
            for j in range(half):
                xj = src[0, pl.ds(j, n_cmp_pad, stride=CMP_STRIDE), :][:, :HEAD_DIM]
                a = a + _dot_f32(xj + pe_ref[s, j:j + 1, :], w1_ref[s, j])
                bm = bm + _dot_f32(xj + pe_ref[s, half + j:half + j + 1, :], w1_ref[s, half + j])
            pre = a + pltpu.roll(bm, n_cmp_pad - 1, axis=0)
            hid = pre * jax.nn.sigmoid(pre)
            comp = _dot_f32(hid, w2_ref[s])
            if s == 0:
                comp = _rms_lanes(comp, kg, HEAD_DIM)
            dst[...] = comp.astype(BF16)

    qscale = HEAD_DIM ** -0.5 * LOG2E
    q = q_ref[0]
    qg = qg_ref[...]
    qs = jnp.concatenate(
        [_rms_lanes(q[:, g * LANES:(g + 1) * LANES], qg, HEAD_DIM) * qscale for g in range(A_GROUP)],
        axis=0).astype(BF16)

    biasc = biasc_ref[...].reshape(A_GROUP * AT, LANES)
    valid = biasc > 0.5 * NEG
    lcm = _dot_nt(qs, kc_s[...]) + biasc
    mc = jnp.max(lcm, axis=-1, keepdims=True)
    pc = jnp.where(valid, jnp.exp2(lcm - mc), 0.0)
    lc = jnp.sum(pc, axis=-1, keepdims=True)
    pc = pc * jnp.where(lc > 0.0, 1.0 / lc, 0.0)
    o_c = _dot(pc.astype(BF16), vc_s[...])

    ovt = ovt_ref[...]
    p_hi, p_mid, p_lo = _split3(pc[0:AT] + pc[AT:2 * AT] + pc[2 * AT:3 * AT])
    ps = _dot_nt(ovt, p_hi) + _dot_nt(ovt, p_mid) + _dot_nt(ovt, p_lo)
    n_sel = ps.shape[0]
    sid = lax.broadcasted_iota(jnp.int32, (n_sel, AT), 0)
    qcol = lax.broadcasted_iota(jnp.int32, (n_sel, AT), 1)
    cur = qi * (AT // SEL_BLOCK) + lax.shift_right_logical(qcol, 6)
    forced = (sid == 0) | (sid == cur) | (sid == cur - 1)
    psf = jnp.where(forced, ps + FORCE, ps)
    psf = jnp.where(sid <= cur, psf, -1.0)
    rank = jnp.zeros((n_sel, AT), F32)
    for sp in range(n_sel):
        other = psf[sp:sp + 1, :]
        beats = (other > psf) | ((other == psf) & (sid > sp))
        rank = rank + jnp.where(beats, 1.0, 0.0)
    keep = (rank < float(min(N_SELECT, n_sel))) & (psf >= 0.0)
    selneg = _dot_tn(jnp.where(keep, 0.0, NEG).astype(BF16), place_ref[...]).astype(BF16)
    lane_q = lax.broadcasted_iota(jnp.int32, (AT, LANES), 1)
    for g in range(A_GROUP):
        qx_s[g * AT:(g + 1) * AT, :] = jnp.where(lane_q < HEAD_DIM, qs[g * AT:(g + 1) * AT], selneg)

    def near_bias(delta, kind):
        kd = jnp.where(qi >= delta, kind, 3)
        return lambda r: tiles_ref[r // per_g, kd, pl.ds((r % per_g) * RB, RB), :]

    def key_rows(kc):
        return pl.ds(pl.multiple_of(kc * AT, AT), AT)

    m_s[...] = jnp.full(m_s.shape, NEG, F32)
    acc_s[...] = jnp.zeros(acc_s.shape, F32)
    mw_s[...] = jnp.full(mw_s.shape, NEG, F32)
    accw_s[...] = jnp.zeros(accw_s.shape, F32)
    for delta, kind in ((0, 0), (1, 1), (2, 2)):
        rows_k = key_rows(jnp.maximum(qi - delta, 0))
        if delta < 2:
            _flash_chunk(qx_s, k1_s[rows_k, :], v1_s[rows_k, :], near_bias(delta, kind), m_s, acc_s)
        _flash_chunk(qx_s, k2_s[rows_k, :], v2_s[rows_k, :], near_bias(delta, kind), mw_s, accw_s)

    for h, s in enumerate(_qk_halves(qx_s, k1_s[key_rows(0), :])):
        s_s[h] = s

    def far_body(kc, carry):
        s_next = _qk_halves(qx_s, k1_s[key_rows(kc + 1), :])
        _softmax_pv([s_s[0], s_s[1]], v1_s[key_rows(kc), :], None, m_s, acc_s)
        for h, s in enumerate(s_next):
            s_s[h] = s
        return carry

    lax.fori_loop(0, jnp.maximum(qi - 1, 0), far_body, 0)

    gt = jax.nn.sigmoid(gate_ref[0])
    for g in range(A_GROUP):
        rows = slice(g * AT, (g + 1) * AT)
        a_s = acc_s[rows, :]
        a_w = accw_s[rows, :]
        o = (gt[:, 3 * g:3 * g + 1] * o_c[rows]
             + gt[:, 3 * g + 1:3 * g + 2] * (a_s / a_s[:, HEAD_DIM:HEAD_DIM + 1])
             + gt[:, 3 * g + 2:3 * g + 3] * (a_w / a_w[:, HEAD_DIM:HEAD_DIM + 1]))
        o_ref[0, :, g * LANES:(g + 1) * LANES] = jnp.where(lane_q < HEAD_DIM, o, 0.0)


def _nsa(proj, tiles, biasc, cmp_pe, cmp_w1, cmp_w2, q_norm, k_norm):
    B, T, _ = proj.shape
    nq = T // AT
    n_cmp_pad = T // CMP_STRIDE
    n_sel = T // SEL_BLOCK
    assert n_sel <= LANES - HEAD_DIM - 1 and n_cmp_pad == LANES
    starts = np.arange(n_cmp_pad) * CMP_STRIDE
    sid = np.arange(n_sel)
    overlap = ((starts[:, None] < (sid[None, :] + 1) * SEL_BLOCK)
               & (starts[:, None] + CMP_BLOCK > sid[None, :] * SEL_BLOCK)
               & (starts[:, None] + CMP_BLOCK <= T))
    place = np.zeros((n_sel, LANES), np.float32)
    place[sid, HEAD_DIM + sid] = 1.0
    w2p = jnp.pad(cmp_w2, ((0, 0), (0, 0), (0, LANES - HEAD_DIM)))

    def slab(br, kv):
        base = SEG_AKV // LANES + (br * 2 + kv) * 2
        return pl.BlockSpec((1, T, LANES), lambda b, h, i: (b, 0, base + h))

    def full(shape):
        return pl.BlockSpec(shape, lambda b, h, i: (0,) * len(shape))

    rows = A_GROUP * AT
    return pl.pallas_call(
        _nsa_kernel,
        grid=(B, A_KV_HEADS, nq),
        in_specs=[pl.BlockSpec((1, AT, A_GROUP * LANES), lambda b, h, i: (b, i, h)),
                  slab(0, 0), slab(0, 1), slab(1, 0), slab(1, 1), slab(2, 0), slab(2, 1),
                  pl.BlockSpec((1, AT, LANES), lambda b, h, i: (b, i, SEG_GATE // LANES + h)),
                  pl.BlockSpec((A_GROUP, 4, AT, AT), lambda b, h, i: (h, 0, 0, 0)),
                  pl.BlockSpec((A_GROUP, AT, LANES), lambda b, h, i: (h, i, 0)),
                  full((n_sel, n_cmp_pad)), full((n_sel, LANES)),
                  full((2, CMP_BLOCK, HEAD_DIM, HEAD_DIM)), full((2, HEAD_DIM, LANES)),
                  full((2, CMP_BLOCK, HEAD_DIM)), full((1, LANES)), full((1, LANES))],
        out_specs=pl.BlockSpec((1, AT, A_GROUP * LANES), lambda b, h, i: (b, i, h)),
        out_shape=jax.ShapeDtypeStruct((B, T, A_HEADS * LANES), F32),
        scratch_shapes=[pltpu.VMEM((n_cmp_pad, LANES), BF16), pltpu.VMEM((n_cmp_pad, LANES), BF16),
                        pltpu.VMEM((T, LANES), BF16), pltpu.VMEM((T, LANES), BF16),
                        pltpu.VMEM((T, LANES), BF16), pltpu.VMEM((T, LANES), BF16),
                        pltpu.VMEM((rows, LANES), BF16), pltpu.VMEM((2, rows // 2, AT), F32),
                        pltpu.VMEM((rows, LANES), F32), pltpu.VMEM((rows, LANES), F32),
                        pltpu.VMEM((rows, LANES), F32), pltpu.VMEM((rows, LANES), F32)],
        compiler_params=_params("parallel", "parallel", "arbitrary"),
        name="nsa",
    )(proj, proj, proj, proj, proj, proj, proj, proj, tiles, biasc,
      jnp.asarray(overlap.T, BF16), jnp.asarray(place, BF16), cmp_w1, w2p, cmp_pe,
      _pad_lanes(q_norm, LANES), _pad_lanes(k_norm, LANES))


def _mlstm_kernel(qk_ref, v_ref, o_ref, gif_ref, cw_ref, gb_ref, gn_ref, tril_ref, y_ref,
                  xpad_s, c_s, n_s, m_s):
    nb, tb = qk_ref.shape[0], qk_ref.shape[1]
    L = MLSTM_CHUNK
    n_pairs = B_HEADS // 2

    @pl.when(pl.program_id(1) == 0)
    def _init():
        xpad_s[:, tb:tb + 8, :] = jnp.zeros((nb, 8, xpad_s.shape[2]), F32)
        c_s[...] = jnp.zeros(c_s.shape, F32)
        n_s[...] = jnp.zeros(n_s.shape, F32)
        m_s[...] = jnp.zeros(m_s.shape, F32)

    for bb in range(nb):
        xpad_s[bb, 0:8, :] = xpad_s[bb, tb:tb + 8, :]
        xpad_s[bb, 8:8 + tb, :] = qk_ref[bb]
    lane = lax.broadcasted_iota(jnp.int32, (1, LANES), 1)
    lo_half = lane < HEAD_DIM
    rowi = lax.broadcasted_iota(jnp.int32, (LANES, LANES), 0)
    coli = lax.broadcasted_iota(jnp.int32, (LANES, LANES), 1)
    blockdiag = (rowi < HEAD_DIM) == (coli < HEAD_DIM)
    tri = lax.broadcasted_iota(jnp.int32, (L, L), 0) >= lax.broadcasted_iota(jnp.int32, (L, L), 1)
    kscale = HEAD_DIM ** -0.5

    def chunk(c, carry):
        for bb in range(nb):
            chunk_one(bb, c)
        return carry

    def chunk_one(bb, c):
        r0 = pl.multiple_of(c * L, L)
        win = xpad_s[bb, pl.ds(r0, L + 8), :]
        conv = cw_ref[CONV_K - 1:CONV_K, :] * win[8:8 + L]
        for j in range(CONV_K - 1):
            conv = conv + cw_ref[j:j + 1, :] * pltpu.roll(win, CONV_K - 1 - j, axis=0)[8:8 + L]
        qk = conv * jax.nn.sigmoid(conv)
        gif = gif_ref[bb, pl.ds(r0, L), :] + gb_ref[...]
        logf = jax.nn.log_sigmoid(gif)
        lh, lm, ll = _split3(logf)
        tril = tril_ref[...]
        bcum = _dot(tril, lh) + _dot(tril, lm) + _dot(tril, ll)
        gif_t = gif.T
        bcum_t = bcum.T
        for p in range(n_pairs):
            qp = qk[:, p * LANES:(p + 1) * LANES]
            kp = qk[:, 2 * LANES + p * LANES:2 * LANES + (p + 1) * LANES] * kscale
            vp = v_ref[bb, pl.ds(r0, L), p * LANES:(p + 1) * LANES]
            c_old = c_s[bb * n_pairs + p]
            n_old = n_s[bb * n_pairs + p]
            cq = _dot_nt(qp, c_old)
            hval, wcol, dec, mnew = [], [], [], []
            for half in range(2):
                hh = 2 * p + half
                hmask = lo_half if half == 0 else jnp.logical_not(lo_half)
                b_col = bcum[:, B_HEADS + hh:B_HEADS + hh + 1]
                b_row = bcum_t[B_HEADS + hh:B_HEADS + hh + 1, :]
                i_col = gif[:, hh:hh + 1]
                i_row = gif_t[hh:hh + 1, :]
                m_prev = m_s[bb * B_HEADS + hh][0:1, 0:1]
                dmat = jnp.where(tri, b_col - b_row + i_row, -jnp.inf)
                m_inter = b_col + m_prev
                m_j = jnp.maximum(m_inter, jnp.max(dmat, axis=-1, keepdims=True))
                isc = jnp.exp(m_inter - m_j)
                qm = jnp.where(hmask, qp, 0.0)
                s = _dot_nt(qm, kp) * jnp.exp(dmat - m_j)
                qn = jnp.sum(qm * n_old, axis=-1, keepdims=True)
                den = jnp.sum(s, axis=-1, keepdims=True) + isc * qn
                num = _dot(s, vp) + isc * cq
                hval.append(num / jnp.maximum(jnp.abs(den), jnp.exp(-m_j)))
                b_last = b_col[L - 1:L, :]
                w_log = b_last - b_col + i_col
                m_new = jnp.maximum(b_last + m_prev, jnp.max(w_log, axis=0, keepdims=True))
                wcol.append(jnp.exp(w_log - m_new))
                dec.append(jnp.exp(b_last + m_prev - m_new))
                mnew.append(m_new)
            w_l = jnp.where(lo_half, wcol[0], wcol[1])
            c_upd = _dot_tn(vp * w_l, kp)
            c_s[bb * n_pairs + p] = (jnp.where(rowi < HEAD_DIM, dec[0], dec[1]) * c_old
                                     + jnp.where(blockdiag, c_upd, 0.0))
            n_s[bb * n_pairs + p] = (jnp.where(lo_half, dec[0], dec[1]) * n_old
                                     + jnp.sum(w_l * kp, axis=0, keepdims=True))
            for half in range(2):
                m_s[bb * B_HEADS + 2 * p + half] = jnp.broadcast_to(mnew[half], m_s.shape[1:])
            h = jnp.where(lo_half, hval[0], hval[1])
            hsq = h * h
            ms0 = jnp.sum(jnp.where(lo_half, hsq, 0.0), axis=-1, keepdims=True) * (1.0 / HEAD_DIM)
            ms1 = jnp.sum(jnp.where(lo_half, 0.0, hsq), axis=-1, keepdims=True) * (1.0 / HEAD_DIM)
            hn = h * jnp.where(lo_half, lax.rsqrt(ms0 + EPS), lax.rsqrt(ms1 + EPS)) * gn_ref[:, p * LANES:(p + 1) * LANES]
            og = jax.nn.sigmoid(o_ref[bb, pl.ds(r0, L), p * LANES:(p + 1) * LANES])
            y_ref[bb, pl.ds(r0, L), p * LANES:(p + 1) * LANES] = og * hn

    lax.fori_loop(0, tb // L, chunk, 0)


MLSTM_NB = 2
MLSTM_TB = 512


def _mlstm(proj, conv_w, i_bias, f_bias, out_norm):
    B, T, _ = proj.shape
    W = B_HEADS * HEAD_DIM
    nb, tb = MLSTM_NB, MLSTM_TB
    gb = _pad_lanes(jnp.concatenate([i_bias, f_bias]), LANES)
    tril = jnp.asarray(np.tril(np.ones((MLSTM_CHUNK, MLSTM_CHUNK))), BF16)

    def full(shape):
        return pl.BlockSpec(shape, lambda b, t: (0,) * len(shape))

    return pl.pallas_call(
        _mlstm_kernel,
        grid=(B // nb, T // tb),
        in_specs=[pl.BlockSpec((nb, tb, 2 * W), lambda b, t: (b, t, SEG_BQK // (2 * W))),
                  pl.BlockSpec((nb, tb, W), lambda b, t: (b, t, SEG_BV // W)),
                  pl.BlockSpec((nb, tb, W), lambda b, t: (b, t, SEG_BO // W)),
                  pl.BlockSpec((nb, tb, LANES), lambda b, t: (b, t, SEG_BIF // LANES)),
                  full((CONV_K, 2 * W)), full((1, LANES)), full((1, W)),
                  full((MLSTM_CHUNK, MLSTM_CHUNK))],
        out_specs=pl.BlockSpec((nb, tb, W), lambda b, t: (b, t, 0)),
        out_shape=jax.ShapeDtypeStruct((B, T, W), F32),
        scratch_shapes=[pltpu.VMEM((nb, tb + 8, 2 * W), F32),
                        pltpu.VMEM((nb * B_HEADS // 2, LANES, LANES), F32),
                        pltpu.VMEM((nb * B_HEADS // 2, 1, LANES), F32),
                        pltpu.VMEM((nb * B_HEADS, 8, LANES), F32)],
        compiler_params=_params("parallel", "arbitrary"),
        name="mlstm",
    )(proj, proj, proj, proj, conv_w, gb, out_norm.reshape(1, W), tril)


def _rope_kernel(pos_ref, invf_ref, sgn_ref, cs_ref, sn_ref):
    ang = pos_ref[...].astype(F32) * invf_ref[...]
    cs_ref[...] = jnp.cos(ang)
    sn_ref[...] = jnp.sin(ang) * sgn_ref[...]


def _rope_tables(pos):
    N = pos.shape[0]
    tm = 1024
    half = C_ROPE // 2
    inv = 1.0 / (ROPE_THETA ** (np.arange(half, dtype=np.float32) * 2.0 / C_ROPE))
    invf = np.zeros((1, LANES), np.float32)
    sgn = np.zeros((1, LANES), np.float32)
    invf[0, C_NOPE:C_NOPE + C_ROPE] = np.concatenate([inv, inv])
    sgn[0, C_NOPE:C_NOPE + half] = -1.0
    sgn[0, C_NOPE + half:C_NOPE + C_ROPE] = 1.0
    row = pl.BlockSpec((1, LANES), lambda i: (0, 0))
    tile = pl.BlockSpec((tm, LANES), lambda i: (i, 0))
    return pl.pallas_call(
        _rope_kernel,
        grid=(N // tm,),
        in_specs=[pl.BlockSpec((tm, 1), lambda i: (i, 0)), row, row],
        out_specs=[tile, tile],
        out_shape=[jax.ShapeDtypeStruct((N, LANES), F32), jax.ShapeDtypeStruct((N, LANES), F32)],
        compiler_params=_params("parallel"),
        name="rope_tables",
    )(pos, jnp.asarray(invf), jnp.asarray(sgn))


def _mla_prep_kernel(dq_ref, dkv_ref, kr_ref, cs_ref, sn_ref, wq_ref, wk_ref, wv_ref, qa_ref, kva_ref,
                     gq_ref, gk_ref, gkr_ref, q_out, k_out, v_out):
    lane = lax.broadcasted_iota(jnp.int32, (1, LANES), 1)
    nope = lane < C_NOPE
    ropem = (lane >= C_NOPE) & (lane < C_NOPE + C_ROPE)
    first = lane < C_NOPE + C_ROPE // 2
    cs = cs_ref[...]
    sn = sn_ref[...]

    def rope(x):
        partner = jnp.where(first, pltpu.roll(x, LANES - C_ROPE // 2, axis=1), pltpu.roll(x, C_ROPE // 2, axis=1))
        return x * cs + partner * sn

    scale = (C_NOPE + C_ROPE) ** -0.5 * LOG2E
    qn = _rms_lanes(dq_ref[...], qa_ref[...], Q_RANK).astype(BF16)
    q = _dot(qn, wq_ref[...])
    kvn = _rms_lanes(dkv_ref[...], kva_ref[...], KV_RANK).astype(BF16)
    kk = _dot(kvn, wk_ref[...])
    vv = _dot(kvn, wv_ref[...])
    lane_v = lax.broadcasted_iota(jnp.int32, vv.shape, 1)
    v_out[...] = jnp.where((lane_v & (LANES - 1)) == C_V, 1.0, vv).astype(BF16)
    kr = kr_ref[...]
    krn = kr * lax.rsqrt(jnp.sum(kr * kr, axis=-1, keepdims=True) * (1.0 / C_ROPE) + EPS) * gkr_ref[...]
    krr = rope(krn)
    for h in range(C_HEADS):
        sl = slice(h * LANES, (h + 1) * LANES)
        qh = q[:, sl]
        sq = qh * qh
        msn = jnp.sum(jnp.where(nope, sq, 0.0), axis=-1, keepdims=True) * (1.0 / C_NOPE)
        msr = jnp.sum(jnp.where(ropem, sq, 0.0), axis=-1, keepdims=True) * (1.0 / C_ROPE)
        qhn = qh * jnp.where(nope, lax.rsqrt(msn + EPS), lax.rsqrt(msr + EPS)) * gq_ref[...]
        q_out[:, sl] = (rope(qhn) * scale).astype(BF16)
        kh = kk[:, sl]
        msk = jnp.sum(kh * kh, axis=-1, keepdims=True) * (1.0 / C_NOPE)
        k_out[:, sl] = (kh * lax.rsqrt(msk + EPS) * gk_ref[...] + krr).astype(BF16)


def _mla_prep(proj2, rope_cs, rope_sn, q_a_norm, w_uq, kv_a_norm, w_ukv, qk_norm):
    N = proj2.shape[0]
    tm = 512
    qd = C_NOPE + C_ROPE
    qmap = np.full((C_HEADS * LANES,), -1, np.int64)
    kmap = np.full((C_HEADS * LANES,), -1, np.int64)
    vmap = np.full((C_HEADS * LANES,), -1, np.int64)
    for h in range(C_HEADS):
        qmap[h * LANES:h * LANES + qd] = h * qd + np.arange(qd)
        kmap[h * LANES:h * LANES + C_NOPE] = h * (C_NOPE + C_V) + np.arange(C_NOPE)
        vmap[h * LANES:h * LANES + C_V] = h * (C_NOPE + C_V) + C_NOPE + np.arange(C_V)
    wq = jnp.pad(_take_cols(w_uq, qmap), ((0, 256 - Q_RANK), (0, 0))).astype(BF16)
    wk = _take_cols(w_ukv, kmap).astype(BF16)
    wv = _take_cols(w_ukv, vmap).astype(BF16)
    gkr = jnp.pad(qk_norm[1, C_NOPE:].reshape(1, -1), ((0, 0), (C_NOPE, LANES - C_NOPE - C_ROPE)))

    def full(shape):
        return pl.BlockSpec(shape, lambda i: (0,) * len(shape))

    return pl.pallas_call(
        _mla_prep_kernel,
        grid=(N // tm,),
        in_specs=[pl.BlockSpec((tm, 256), lambda i: (i, SEG_CDQ // 256)),
                  pl.BlockSpec((tm, LANES), lambda i: (i, SEG_CDKV // LANES)),
                  pl.BlockSpec((tm, LANES), lambda i: (i, SEG_CKR // LANES)),
                  pl.BlockSpec((tm, LANES), lambda i: (i, 0)), pl.BlockSpec((tm, LANES), lambda i: (i, 0)),
                  full((256, C_HEADS * LANES)), full((KV_RANK, C_HEADS * LANES)), full((KV_RANK, C_HEADS * LANES)),
                  full((1, 256)), full((1, LANES)), full((1, LANES)), full((1, LANES)), full((1, LANES))],
        out_specs=[pl.BlockSpec((tm, C_HEADS * LANES), lambda i: (i, 0)),
                   pl.BlockSpec((tm, C_HEADS * LANES), lambda i: (i, 0)),
                   pl.BlockSpec((tm, C_HEADS * LANES), lambda i: (i, 0))],
        out_shape=[jax.ShapeDtypeStruct((N, C_HEADS * LANES), BF16),
                   jax.ShapeDtypeStruct((N, C_HEADS * LANES), BF16),
                   jax.ShapeDtypeStruct((N, C_HEADS * LANES), BF16)],
        compiler_params=_params("parallel"),
        name="mla_prep",
    )(proj2, proj2, proj2, rope_cs, rope_sn, wq, wk, wv,
      _pad_lanes(q_a_norm, 256), kv_a_norm.reshape(1, -1), _pad_lanes(qk_norm[0], LANES),
      _pad_lanes(qk_norm[1, :C_NOPE], LANES), gkr)


def _mla_attn_kernel(q_ref, k_ref, v_ref, causal_ref, o_ref, s_s, m_s, acc_s):
    qi = pl.program_id(2)
    m_s[...] = jnp.full(m_s.shape, NEG, F32)
    acc_s[...] = jnp.zeros(acc_s.shape, F32)
    heads = [slice(hh * LANES, (hh + 1) * LANES) for hh in range(2)]
    per_head = AT // RB

    def key_rows(kc):
        return pl.ds(pl.multiple_of(kc * AT, AT), AT)

    def scores(kc):
        return [_dot_nt(q_ref[0, :, sl], k_ref[0, key_rows(kc), sl]) for sl in heads]

    def values(kc):
        return [v_ref[0, key_rows(kc), sl] for sl in heads]

    _softmax_pv(scores(qi), values(qi), lambda r: causal_ref[pl.ds((r % per_head) * RB, RB), :], m_s, acc_s)

    for h, s in enumerate(scores(0)):
        s_s[h] = s

    def body(kc, carry):
        s_next = scores(kc + 1)
        _softmax_pv([s_s[0], s_s[1]], values(kc), None, m_s, acc_s)
        for h, s in enumerate(s_next):
            s_s[h] = s
        return carry

    lax.fori_loop(0, qi, body, 0)
    lane = lax.broadcasted_iota(jnp.int32, (AT, LANES), 1)
    outs = []
    for hh in range(2):
        a = acc_s[hh * AT:(hh + 1) * AT, :]
        outs.append(jnp.where(lane < C_V, a / a[:, C_V:C_V + 1], 0.0))
    o_ref[0] = outs[0] + pltpu.roll(outs[1], C_V, axis=1)


def _mla_attn(q, k, v):
    B, T, _ = q.shape
    causal = np.where(np.arange(AT)[:, None] >= np.arange(AT)[None, :], 0.0, NEG).astype(np.float32)
    pair = pl.BlockSpec((1, T, 2 * LANES), lambda b, p, i: (b, 0, p))
    return pl.pallas_call(
        _mla_attn_kernel,
        grid=(B, C_HEADS // 2, T // AT),
        in_specs=[pl.BlockSpec((1, AT, 2 * LANES), lambda b, p, i: (b, i, p)), pair, pair,
                  pl.BlockSpec((AT, AT), lambda b, p, i: (0, 0))],
        out_specs=pl.BlockSpec((1, AT, LANES), lambda b, p, i: (b, i, p)),
        out_shape=jax.ShapeDtypeStruct((B, T, C_HEADS * C_V), F32),
        scratch_shapes=[pltpu.VMEM((2, AT, AT), F32), pltpu.VMEM((2 * AT, LANES), F32),
                        pltpu.VMEM((2 * AT, LANES), F32)],
        compiler_params=_params("parallel", "parallel", "arbitrary"),
        name="mla_attn",
    )(q, k, v, jnp.asarray(causal))


def _outproj_kernel(x_ref, ya_ref, yb_ref, yc_ref, wa_ref, wb_ref, wc_ref, ga_ref, gc_ref, g2_ref,
                    wr_ref, br_ref, x1_ref, h2_ref, ids_ref, wts_ref, cnt_ref):
    ya = _rms_lanes(ya_ref[...], ga_ref[...], A_HEADS * HEAD_DIM).astype(BF16)
    yc = _rms_lanes(yc_ref[...], gc_ref[...], C_HEADS * C_V).astype(BF16)
    x1 = (x_ref[...] + _dot(ya, wa_ref[...]) + _dot(yb_ref[...].astype(BF16), wb_ref[...])
          + _dot(yc, wc_ref[...]))
    x1_ref[...] = x1
    h2 = _rms_lanes(x1, g2_ref[...], D_MODEL)
    h2_ref[...] = h2
    h2_hi = h2.astype(BF16)
    h2_mid = (h2 - h2_hi.astype(F32)).astype(BF16)
    logits = (_dot(h2_hi, wr_ref[0]) + _dot(h2_mid, wr_ref[0]) + _dot(h2_hi, wr_ref[1])) + br_ref[...]
    tm = logits.shape[0]
    lane = lax.broadcasted_iota(jnp.int32, (tm, LANES), 1)
    is_g = (lane >= N_EXPERTS) & (lane < N_EXPERTS + N_GROUPS)
    gl = jnp.where(is_g, logits, NEG)
    ge = jnp.where(is_g, jnp.exp(gl - jnp.max(gl, axis=-1, keepdims=True)), 0.0)
    gp = jnp.where(is_g, ge / jnp.sum(ge, axis=-1, keepdims=True), -1.0)
    g_top = jnp.max(gp, axis=-1, keepdims=True)
    g_idx = jnp.min(jnp.where(gp == g_top, lane, 2 * LANES), axis=-1, keepdims=True) - N_EXPERTS
    in_grp = (lane < N_EXPERTS) & (lax.shift_right_logical(lane, 3) == g_idx)
    el = jnp.where(in_grp, logits, NEG)
    ee = jnp.where(in_grp, jnp.exp(el - jnp.max(el, axis=-1, keepdims=True)), 0.0)
    ep = jnp.where(in_grp, ee / jnp.sum(ee, axis=-1, keepdims=True), -1.0)
    v1 = jnp.max(ep, axis=-1, keepdims=True)
    i1 = jnp.min(jnp.where(ep == v1, lane, 2 * LANES), axis=-1, keepdims=True)
    ep2 = jnp.where(lane == i1, -1.0, ep)
    v2 = jnp.max(ep2, axis=-1, keepdims=True)
    i2 = jnp.min(jnp.where(ep2 == v2, lane, 2 * LANES), axis=-1, keepdims=True)
    tot = v1 + v2
    ids_ref[...] = jnp.where(lane == 0, i1, jnp.where(lane == 1, i2, 0))
    wts_ref[...] = jnp.where(lane == 0, v1 / tot * g_top, jnp.where(lane == 1, v2 / tot * g_top, 0.0))
    per_expert = jnp.sum(jnp.where((lane == i1) | (lane == i2), 1.0, 0.0), axis=0, keepdims=True)
    cnt_ref[...] = jnp.broadcast_to(per_expert, cnt_ref.shape)


def _outproj(x2, ya, yb, yc, w_out, nsa_out_norm, mla_out_norm, norm2_g, w_group, b_group, w_expert, b_expert):
    N = x2.shape[0]
    tm = 256
    amap = np.full((A_HEADS * LANES,), -1, np.int64)
    for hd in range(A_HEADS):
        amap[hd * LANES:hd * LANES + HEAD_DIM] = hd * HEAD_DIM + np.arange(HEAD_DIM)
    wa = _take_rows(w_out[:384], amap).astype(BF16)
    wb = w_out[384:640].astype(BF16)
    wc = w_out[640:].astype(BF16)
    ga = _take_cols(nsa_out_norm.reshape(1, -1), amap)
    wr = jnp.pad(jnp.concatenate([w_expert, w_group], axis=1), ((0, 0), (0, LANES - N_EXPERTS - N_GROUPS)))
    wr_hi = wr.astype(BF16)
    wr = jnp.stack([wr_hi, (wr - wr_hi.astype(F32)).astype(BF16)])
    br = _pad_lanes(jnp.concatenate([b_expert, b_group]), LANES)

    def full(shape):
        return pl.BlockSpec(shape, lambda i: (0,) * len(shape))

    def rows(w):
        return pl.BlockSpec((tm, w), lambda i: (i, 0))

    return pl.pallas_call(
        _outproj_kernel,
        grid=(N // tm,),
        in_specs=[rows(D_MODEL), rows(A_HEADS * LANES), rows(256), rows(384),
                  full((A_HEADS * LANES, D_MODEL)), full((256, D_MODEL)), full((384, D_MODEL)),
                  full((1, A_HEADS * LANES)), full((1, 384)), full((1, D_MODEL)),
                  full((2, D_MODEL, LANES)), full((1, LANES))],
        out_specs=[rows(D_MODEL), rows(D_MODEL), rows(LANES), rows(LANES),
                   pl.BlockSpec((8, LANES), lambda i: (i, 0))],
        out_shape=[jax.ShapeDtypeStruct((N, D_MODEL), F32), jax.ShapeDtypeStruct((N, D_MODEL), F32),
                   jax.ShapeDtypeStruct((N, LANES), jnp.int32), jax.ShapeDtypeStruct((N, LANES), F32),
                   jax.ShapeDtypeStruct((N // tm * 8, LANES), F32)],
        compiler_params=_params("parallel"),
        name="outproj_router",
    )(x2, ya, yb, yc, wa, wb, wc, ga, mla_out_norm.reshape(1, -1), norm2_g.reshape(1, -1), wr, br)


def _rank_kernel(ids_ref, off_ref, tril_ref, pos_ref, carry_s):
    i = pl.program_id(0)

    @pl.when(i == 0)
    def _():
        carry_s[...] = jnp.broadcast_to(off_ref[...], carry_s.shape)

    ids = ids_ref[...]
    tm = ids.shape[0]
    lane = lax.broadcasted_iota(jnp.int32, (tm, LANES), 1)
    i1 = ids[:, 0:1]
    i2 = ids[:, 1:2]
    oh = jnp.where((lane == i1) | (lane == i2), 1.0, 0.0)
    incl = _dot(tril_ref[...], oh.astype(BF16)) + carry_s[0:1, :]
    excl = incl - oh
    p1 = jnp.sum(jnp.where(lane == i1, excl, 0.0), axis=-1, keepdims=True)
    p2 = jnp.sum(jnp.where(lane == i2, excl, 0.0), axis=-1, keepdims=True)
    pos_ref[...] = jnp.where(lane == 0, p1, jnp.where(lane == 1, p2, 0.0)).astype(jnp.int32)
    carry_s[...] = jnp.broadcast_to(incl[tm - 1:tm, :], carry_s.shape)


def _moe_positions(ids, off_row):
    N = ids.shape[0]
    tm = 512
    tril = jnp.asarray(np.tril(np.ones((tm, tm))), BF16)
    return pl.pallas_call(
        _rank_kernel,
        grid=(N // tm,),
        in_specs=[pl.BlockSpec((tm, LANES), lambda i: (i, 0)), pl.BlockSpec((1, LANES), lambda i: (0, 0)),
                  pl.BlockSpec((tm, tm), lambda i: (0, 0))],
        out_specs=pl.BlockSpec((tm, LANES), lambda i: (i, 0)),
        out_shape=jax.ShapeDtypeStruct((N, LANES), jnp.int32),
        scratch_shapes=[pltpu.VMEM((8, LANES), F32)],
        compiler_params=_params("arbitrary"),
        name="moe_positions",
    )(ids, off_row, tril)


DISPATCH_TM = 1024


DMA_UNROLL = 8


def _dispatch_kernel(off_ref, tp_ref, nu_ref, p1_ref, p2_ref, h_ref, xs_ref, zero_s, sem, zsem):
    n_tiles = xs_ref.shape[0] // ROW_TILE

    @pl.when(pl.program_id(0) == 0)
    def _zero_fill():
        zero_s[...] = jnp.zeros(zero_s.shape, F32)

        def tile_copy(row0):
            return pltpu.make_async_copy(
                zero_s, xs_ref.at[pl.ds(pl.multiple_of(row0, ROW_TILE), ROW_TILE), :], zsem)

        for e in range(N_EXPERTS):
            @pl.when(tp_ref[e] > 0)
            def _start(e=e):
                tile_copy(off_ref[e] + (tp_ref[e] - 1) * ROW_TILE).start()

        def tail_start(g, carry):
            tile_copy(g * ROW_TILE).start()
            return carry

        lax.fori_loop(nu_ref[0], n_tiles, tail_start, 0)
        for e in range(N_EXPERTS):
            @pl.when(tp_ref[e] > 0)
            def _wait():
                tile_copy(0).wait()

        def tail_wait(g, carry):
            tile_copy(0).wait()
            return carry

        lax.fori_loop(nu_ref[0], n_tiles, tail_wait, 0)

    def row_copy(i, pos):
        return pltpu.make_async_copy(h_ref.at[pl.ds(i, 1), :], xs_ref.at[pl.ds(pos, 1), :], sem)

    def issue(j, carry):
        for u in range(DMA_UNROLL):
            i = j * DMA_UNROLL + u
            row_copy(i, p1_ref[i]).start(priority=0)
            row_copy(i, p2_ref[i]).start(priority=1)
        return carry

    lax.fori_loop(0, DISPATCH_TM // DMA_UNROLL, issue, 0)
    whole = pltpu.make_async_copy(h_ref, xs_ref.at[pl.ds(0, DISPATCH_TM), :], sem)
    whole.wait()
    whole.wait()


def _moe_dispatch(h2, off, tiles_per, n_used, pos1, pos2, n_rows):
    N = h2.shape[0]
    smem1d = pl.BlockSpec((DISPATCH_TM,), lambda i: (i,), memory_space=pltpu.SMEM)
    smem = pl.BlockSpec(memory_space=pltpu.SMEM)
    return pl.pallas_call(
        _dispatch_kernel,
        grid=(N // DISPATCH_TM,),
        in_specs=[smem, smem, smem, smem1d, smem1d,
                  pl.BlockSpec((DISPATCH_TM, D_MODEL), lambda i: (i, 0))],
        out_specs=pl.BlockSpec(memory_space=pl.ANY),
        out_shape=jax.ShapeDtypeStruct((n_rows, D_MODEL), F32),
        scratch_shapes=[pltpu.VMEM((ROW_TILE, D_MODEL), F32), pltpu.SemaphoreType.DMA(()),
                        pltpu.SemaphoreType.DMA(())],
        compiler_params=_params("arbitrary"),
        name="moe_dispatch",
    )(off, tiles_per, n_used, pos1, pos2, h2)


def _expert_kernel(te_ref, nu_ref, x_ref, wg_ref, wu_ref, wd_ref, y_ref):
    g = pl.program_id(0)

    @pl.when(g < nu_ref[0])
    def _():
        x = x_ref[...].astype(BF16)
        a = _dot(x, wg_ref[0, 0].astype(BF16))
        u = _dot(x, wu_ref[0, 0].astype(BF16))
        h = (a * jax.nn.sigmoid(a) * u).astype(BF16)
        y_ref[...] = _dot(h, wd_ref[0, 0].astype(BF16))

    @pl.when(g >= nu_ref[0])
    def _():
        y_ref[...] = jnp.zeros(y_ref.shape, F32)


def _moe_experts(xs, tile_expert, n_used, layer, w_gate, w_up, w_down):
    R = xs.shape[0]
    G = R // ROW_TILE

    def row_map(g, te, nu):
        return (jnp.minimum(g, nu[0] - 1), 0)

    def w_map(g, te, nu):
        return (layer, te[jnp.minimum(g, nu[0] - 1)], 0, 0)

    return pl.pallas_call(
        _expert_kernel,
        grid_spec=pltpu.PrefetchScalarGridSpec(
            num_scalar_prefetch=2,
            grid=(G,),
            in_specs=[pl.BlockSpec((ROW_TILE, D_MODEL), row_map),
                      pl.BlockSpec((1, 1, D_MODEL, D_EXPERT), w_map),
                      pl.BlockSpec((1, 1, D_MODEL, D_EXPERT), w_map),
                      pl.BlockSpec((1, 1, D_EXPERT, D_MODEL), w_map)],
            out_specs=pl.BlockSpec((ROW_TILE, D_MODEL), lambda g, te, nu: (g, 0))),
        out_shape=jax.ShapeDtypeStruct((R, D_MODEL), F32),
        compiler_params=_params("arbitrary"),
        name="moe_experts",
    )(tile_expert, n_used, xs, w_gate, w_up, w_down)


COMBINE_TM = 256


def _combine_kernel(p1_ref, p2_ref, x1_ref, wts_ref, ys_ref, o_ref, b1_s, b2_s, sem):
    def row_copy(buf, i, pos):
        return pltpu.make_async_copy(ys_ref.at[pl.ds(pos, 1), :], buf.at[pl.ds(i, 1), :], sem)

    def issue(j, carry):
        for u in range(DMA_UNROLL):
            i = j * DMA_UNROLL + u
            row_copy(b1_s, i, p1_ref[i]).start(priority=0)
            row_copy(b2_s, i, p2_ref[i]).start(priority=1)
        return carry

    lax.fori_loop(0, COMBINE_TM // DMA_UNROLL, issue, 0)
    pltpu.make_async_copy(ys_ref.at[pl.ds(0, COMBINE_TM), :], b1_s, sem).wait()
    pltpu.make_async_copy(ys_ref.at[pl.ds(0, COMBINE_TM), :], b2_s, sem).wait()
    w = wts_ref[...]
    o_ref[...] = x1_ref[...] + w[:, 0:1] * b1_s[...] + w[:, 1:2] * b2_s[...]


def _moe_combine(x1, wts, ys, pos1, pos2):
    N = x1.shape[0]
    smem1d = pl.BlockSpec((COMBINE_TM,), lambda i: (i,), memory_space=pltpu.SMEM)
    return pl.pallas_call(
        _combine_kernel,
        grid=(N // COMBINE_TM,),
        in_specs=[smem1d, smem1d,
                  pl.BlockSpec((COMBINE_TM, D_MODEL), lambda i: (i, 0)),
                  pl.BlockSpec((COMBINE_TM, LANES), lambda i: (i, 0)),
                  pl.BlockSpec(memory_space=pl.ANY)],
        out_specs=pl.BlockSpec((COMBINE_TM, D_MODEL), lambda i: (i, 0)),
        out_shape=jax.ShapeDtypeStruct((N, D_MODEL), F32),
        scratch_shapes=[pltpu.VMEM((COMBINE_TM, D_MODEL), F32), pltpu.VMEM((COMBINE_TM, D_MODEL), F32),
                        pltpu.SemaphoreType.DMA(())],
        compiler_params=_params("arbitrary"),
        name="moe_combine",
    )(pos1, pos2, x1, wts, ys)


def _moe(x1, h2, ids, wts, tile_counts, layer, w_gate, w_up, w_down):
    N = x1.shape[0]
    n_rows = 2 * N + N_EXPERTS * ROW_TILE
    counts = jnp.sum(tile_counts[::8, :N_EXPERTS], axis=0).astype(jnp.int32)
    tiles_per = (counts + ROW_TILE - 1) // ROW_TILE
    tile_end = jnp.cumsum(tiles_per)
    off = ((tile_end - tiles_per) * ROW_TILE).astype(jnp.int32)
    n_used = tile_end[-1:].astype(jnp.int32)
    tile_ids = jnp.arange(n_rows // ROW_TILE, dtype=jnp.int32)
    tile_expert = jnp.minimum(jnp.sum(tile_end[None, :] <= tile_ids[:, None], axis=1), N_EXPERTS - 1).astype(jnp.int32)
    pos = _moe_positions(ids, _pad_lanes(off.astype(F32), LANES))
    pos1, pos2 = pos[:, 0], pos[:, 1]
    xs = _moe_dispatch(h2, off, tiles_per.astype(jnp.int32), n_used, pos1, pos2, n_rows)
    ys = _moe_experts(xs, tile_expert, n_used, layer, w_gate, w_up, w_down)
    return _moe_combine(x1, wts, ys, pos1, pos2)


def kernel(x, positions, rel_bias, norm1_g, w_in, nsa_cmp_pe, nsa_cmp_w1, nsa_cmp_w2, nsa_q_norm, nsa_k_norm, nsa_out_norm, mlstm_conv, mlstm_i_bias, mlstm_f_bias, mlstm_out_norm, mla_q_a_norm, mla_w_uq, mla_kv_a_norm, mla_w_ukv, mla_qk_norm, mla_out_norm, w_out, norm2_g, moe_w_group, moe_b_group, moe_w_expert, moe_b_expert, moe_w_gate, moe_w_up, moe_w_down):
    B, T, D = x.shape
    N = B * T
    depth = w_in.shape[0]
    colmap = _proj_colmap()
    tiles, biasc = _bias_tables(rel_bias, T)
    rope_cs, rope_sn = _rope_tables(positions.reshape(N, 1))
    x2 = x.reshape(N, D)
    for l in range(depth):
        w_proj = _take_cols(w_in[l], colmap).astype(BF16)
        proj2 = _inproj(x2, norm1_g[l].reshape(1, D), w_proj)
        proj = proj2.reshape(B, T, D_PROJ)
        ya = _nsa(proj, tiles, biasc, nsa_cmp_pe[l], nsa_cmp_w1[l], nsa_cmp_w2[l], nsa_q_norm[l], nsa_k_norm[l])
        yb = _mlstm(proj, mlstm_conv[l], mlstm_i_bias[l], mlstm_f_bias[l], mlstm_out_norm[l])
        q, k, v = _mla_prep(proj2, rope_cs, rope_sn, mla_q_a_norm[l], mla_w_uq[l], mla_kv_a_norm[l], mla_w_ukv[l], mla_qk_norm[l])
        yc = _mla_attn(q.reshape(B, T, -1), k.reshape(B, T, -1), v.reshape(B, T, -1))
        x1, h2, ids, wts, tile_counts = _outproj(x2, ya.reshape(N, -1), yb.reshape(N, -1), yc.reshape(N, -1), w_out[l],
                                    nsa_out_norm[l], mla_out_norm[l], norm2_g[l], moe_w_group[l], moe_b_group[l],
                                    moe_w_expert[l], moe_b_expert[l])
        x2 = _moe(x1, h2, ids, wts, tile_counts, l, moe_w_gate, moe_w_up, moe_w_down)
    return x2.reshape(B, T, D)
```

```python
import functools
import math

import numpy as np
import jax
import jax.numpy as jnp
from jax import lax
from jax.experimental import pallas as pl
from jax.experimental.pallas import tpu as pltpu

F32 = jnp.float32
BF16 = jnp.bfloat16

D_MODEL = 1024
HEAD_DIM = 64
LANES = 128
A_HEADS, A_KV_HEADS, A_GROUP = 6, 2, 3
CMP_BLOCK, CMP_STRIDE, SEL_BLOCK, N_SELECT, WINDOW = 32, 16, 64, 16, 512
FORCE = 1e4
B_HEADS, CONV_K, MLSTM_CHUNK = 4, 4, 64
C_HEADS, C_NOPE, C_ROPE, C_V = 6, 64, 32, 64
Q_RANK, KV_RANK = 192, 128
ROPE_THETA = 10000.0
REL_BUCKETS, REL_MAX_DIST = 32, 128
N_GROUPS, EXPERTS_PER_GROUP, N_EXPERTS, D_EXPERT = 4, 8, 32, 256
EPS = 1e-6
NEG = -1e30

LOG2E = 1.4426950408889634

AT = 256
RB = 128
ROW_TILE = 256
VMEM_LIMIT = 48 * 1024 * 1024

SEG_AQ, SEG_AKV, SEG_GATE, SEG_BQK, SEG_BV, SEG_BO = 0, 768, 2304, 2560, 3072, 3328
SEG_CDQ, SEG_CDKV, SEG_CKR, SEG_BIF, D_PROJ = 3584, 3840, 3968, 4096, 4224


def _params(*sem):
    return pltpu.CompilerParams(dimension_semantics=sem, vmem_limit_bytes=VMEM_LIMIT)


def _dot(a, b):
    return jnp.dot(a, b, preferred_element_type=F32)


def _dot_nt(a, b):
    return lax.dot_general(a, b, (((1,), (1,)), ((), ())), preferred_element_type=F32)


def _dot_tn(a, b):
    return lax.dot_general(a, b, (((0,), (0,)), ((), ())), preferred_element_type=F32)


def _split3(x):
    hi = x.astype(BF16)
    r1 = x - hi.astype(F32)
    mid = r1.astype(BF16)
    lo = (r1 - mid.astype(F32)).astype(BF16)
    return hi, mid, lo


def _dot_exact_rhs(x, m_bf16):
    hi, mid, lo = _split3(x)
    return _dot(hi, m_bf16) + _dot(mid, m_bf16) + _dot(lo, m_bf16)


def _dot_f32(x, w):
    xh = x.astype(BF16)
    xm = (x - xh.astype(F32)).astype(BF16)
    wh = w.astype(BF16)
    wm = (w - wh.astype(F32)).astype(BF16)
    return _dot(xh, wh) + _dot(xm, wh) + _dot(xh, wm)


def _bucket_np(dist):
    n = np.maximum(dist, 0)
    exact = REL_BUCKETS // 2
    nf = np.maximum(n, 1).astype(np.float64)
    large = exact + (np.log(nf / exact) / math.log(REL_MAX_DIST / exact) * (REL_BUCKETS - exact)).astype(np.int64)
    large = np.minimum(large, REL_BUCKETS - 1)
    return np.where(n < exact, n, large).astype(np.int32)


def _bucket_tables(T):
    r = np.arange(AT)[:, None]
    c = np.arange(AT)[None, :]
    diag = np.where(r >= c, _bucket_np(r - c), -1)
    prev = _bucket_np(AT + r - c)
    far = np.where(c > r, _bucket_np(2 * AT + r - c), -1)
    tiles = np.stack([diag, prev, far]).astype(np.int32)
    tq = np.arange(T)[:, None]
    n = np.arange(LANES)[None, :]
    dist_c = tq - (n * CMP_STRIDE + CMP_BLOCK - 1)
    cmp_tbl = np.where(dist_c >= 0, _bucket_np(dist_c), -1).astype(np.int32)
    return tiles, cmp_tbl


def _proj_colmap():
    cm = np.full((D_PROJ,), -1, np.int64)
    for hd in range(A_HEADS):
        cm[SEG_AQ + hd * LANES: SEG_AQ + hd * LANES + HEAD_DIM] = hd * HEAD_DIM + np.arange(HEAD_DIM)
    for s in range(12):
        cm[SEG_AKV + s * LANES: SEG_AKV + s * LANES + HEAD_DIM] = 384 + s * HEAD_DIM + np.arange(HEAD_DIM)
    for h in range(A_KV_HEADS):
        cm[SEG_GATE + h * LANES: SEG_GATE + h * LANES + 9] = 1152 + h * 9 + np.arange(9)
    cm[SEG_BQK: SEG_BQK + 512] = 1170 + np.arange(512)
    cm[SEG_BV: SEG_BV + 256] = 1682 + np.arange(256)
    cm[SEG_BIF: SEG_BIF + 8] = 1938 + np.arange(8)
    cm[SEG_BO: SEG_BO + 256] = 1946 + np.arange(256)
    cm[SEG_CDQ: SEG_CDQ + Q_RANK] = 2202 + np.arange(Q_RANK)
    cm[SEG_CDKV: SEG_CDKV + KV_RANK] = 2394 + np.arange(KV_RANK)
    cm[SEG_CKR + C_NOPE: SEG_CKR + C_NOPE + C_ROPE] = 2522 + np.arange(C_ROPE)
    return cm


def _take_cols(w, colmap):
    wz = jnp.concatenate([w, jnp.zeros((w.shape[0], 1), w.dtype)], axis=1)
    return wz[:, np.where(colmap < 0, w.shape[1], colmap)]


def _take_rows(w, rowmap):
    return _take_cols(w.T, rowmap).T


def _pad_lanes(v, n):
    v = v.reshape(1, -1)
    return jnp.pad(v, ((0, 0), (0, n - v.shape[1])))


def _bias_kernel(rb_ref, bt_ref, bc_ref, tiles_ref, cmp_ref):
    hd = pl.program_id(0)
    bt = bt_ref[...]
    bc = bc_ref[...]
    t = jnp.full(bt.shape, NEG, F32)
    c = jnp.full(bc.shape, NEG, F32)
    far = rb_ref[REL_BUCKETS - 1, hd]
    for b in range(REL_BUCKETS):
        val = rb_ref[b, hd]
        t = jnp.where(bt == b, (val - far) * LOG2E, t)
        c = jnp.where(bc == b, val * LOG2E, c)
    tiles_ref[0, 0:3] = t
    tiles_ref[0, 3] = jnp.full((AT, AT), NEG, F32)
    cmp_ref[0] = c


def _bias_tables(rel_bias, T):
    bt, bc = _bucket_tables(T)
    return pl.pallas_call(
        _bias_kernel,
        grid=(A_HEADS,),
        in_specs=[pl.BlockSpec(memory_space=pltpu.SMEM),
                  pl.BlockSpec((3, AT, AT), lambda h: (0, 0, 0)),
                  pl.BlockSpec((T, LANES), lambda h: (0, 0))],
        out_specs=[pl.BlockSpec((1, 4, AT, AT), lambda h: (h, 0, 0, 0)),
                   pl.BlockSpec((1, T, LANES), lambda h: (h, 0, 0))],
        out_shape=[jax.ShapeDtypeStruct((A_HEADS, 4, AT, AT), F32),
                   jax.ShapeDtypeStruct((A_HEADS, T, LANES), F32)],
        compiler_params=_params("arbitrary"),
        name="bias_tables",
    )(rel_bias, jnp.asarray(bt), jnp.asarray(bc))


def _inproj_kernel(x_ref, g_ref, w_ref, o_ref):
    x = x_ref[...]
    ms = jnp.mean(x * x, axis=-1, keepdims=True)
    h = (x * lax.rsqrt(ms + EPS) * g_ref[...]).astype(BF16)
    for j in range(0, D_PROJ, 384):
        o_ref[:, j:j + 384] = _dot(h, w_ref[:, j:j + 384])


def _inproj(x2, g, w):
    N = x2.shape[0]
    tm = 512
    return pl.pallas_call(
        _inproj_kernel,
        grid=(N // tm,),
        in_specs=[pl.BlockSpec((tm, D_MODEL), lambda i: (i, 0)),
                  pl.BlockSpec((1, D_MODEL), lambda i: (0, 0)),
                  pl.BlockSpec((D_MODEL, D_PROJ), lambda i: (0, 0))],
        out_specs=pl.BlockSpec((tm, D_PROJ), lambda i: (i, 0)),
        out_shape=jax.ShapeDtypeStruct((N, D_PROJ), F32),
        compiler_params=_params("parallel"),
        name="inproj",
    )(x2, g, w)


def _rms_lanes(x, g, width):
    ms = jnp.sum(x * x, axis=-1, keepdims=True) * (1.0 / width)
    return x * lax.rsqrt(ms + EPS) * g


def _qk_halves(q_r, k_blk):
    half_rows = q_r.shape[0] // 2
    return [_dot_nt(q_r[h0:h0 + half_rows, :], k_blk) for h0 in range(0, q_r.shape[0], half_rows)]


def _flash_chunk(q_r, k_blk, v_blk, bias_fn, m_r, acc_r):
    _softmax_pv(_qk_halves(q_r, k_blk), v_blk, bias_fn, m_r, acc_r)


def _softmax_pv(s_halves, v_blk, bias_fn, m_r, acc_r):
    half_rows = m_r.shape[0] // 2
    for hi_, h0 in enumerate(range(0, m_r.shape[0], half_rows)):
        s_all = s_halves[hi_]
        ps, alphas = [], []
        for r in range(half_rows // RB):
            rows = slice(h0 + r * RB, h0 + (r + 1) * RB)
            s = s_all[r * RB:(r + 1) * RB]
            if bias_fn is not None:
                s = s + bias_fn((h0 + r * RB) // RB)
            lo, hi = s[:, :LANES], s[:, LANES:]
            mo = m_r[rows, :]
            mn = jnp.maximum(mo, jnp.max(jnp.maximum(lo, hi), axis=-1, keepdims=True))
            ps.append(jnp.exp2(jnp.concatenate([lo - mn, hi - mn], axis=1)).astype(BF16))
            alphas.append(jnp.exp2(mo - mn))
            m_r[rows, :] = mn
        pv = _dot(jnp.concatenate(ps, axis=0), v_blk[hi_] if isinstance(v_blk, (list, tuple)) else v_blk)
        for r in range(half_rows // RB):
            rows = slice(h0 + r * RB, h0 + (r + 1) * RB)
            acc_r[rows, :] = alphas[r] * acc_r[rows, :] + pv[r * RB:(r + 1) * RB]


def _nsa_kernel(q_ref, k0_ref, v0_ref, k1_ref, v1_ref, k2_ref, v2_ref, gate_ref, tiles_ref, biasc_ref,
                ovt_ref, place_ref, w1_ref, w2_ref, pe_ref, qg_ref, kg_ref, o_ref,
                kc_s, vc_s, k1_s, v1_s, k2_s, v2_s, qx_s, s_s, m_s, acc_s, mw_s, accw_s):
    qi = pl.program_id(2)
    kg = kg_ref[...]
    n_cmp_pad = kc_s.shape[0]
    per_g = AT // RB

    @pl.when(qi == 0)
    def _prep():
        T = k1_s.shape[0]
        lane_t = lax.broadcasted_iota(jnp.int32, (T, LANES), 1)
        row_t = lax.broadcasted_iota(jnp.int32, (T, LANES), 0)
        block_onehot = lax.shift_right_logical(row_t, 6) == lane_t - HEAD_DIM
        ones_lane = lane_t == HEAD_DIM
        k1_s[...] = jnp.where(block_onehot, 1.0, _rms_lanes(k1_ref[0], kg, HEAD_DIM)).astype(BF16)
        v1_s[...] = jnp.where(ones_lane, 1.0, v1_ref[0]).astype(BF16)
        k2_s[...] = _rms_lanes(k2_ref[0], kg, HEAD_DIM).astype(BF16)
        v2_s[...] = jnp.where(ones_lane, 1.0, v2_ref[0]).astype(BF16)
        half = CMP_BLOCK // 2
        for s, (src, dst) in enumerate(((k0_ref, kc_s), (v0_ref, vc_s))):
            a = jnp.zeros((n_cmp_pad, HEAD_DIM), F32)
            bm = jnp.zeros((n_cmp_pad, HEAD_DIM), F32)
            for j in range(half):
                xj = src[0, pl.ds(j, n_cmp_pad, stride=CMP_STRIDE), :][:, :HEAD_DIM]
                a = a + _dot_f32(xj + pe_ref[s, j:j + 1, :], w1_ref[s, j])
                bm = bm + _dot_f32(xj + pe_ref[s, half + j:half + j + 1, :], w1_ref[s, half + j])
            pre = a + pltpu.roll(bm, n_cmp_pad - 1, axis=0)
            hid = pre * jax.nn.sigmoid(pre)
            comp = _dot_f32(hid, w2_ref[s])
            if s == 0:
                comp = _rms_lanes(comp, kg, HEAD_DIM)
            dst[...] = comp.astype(BF16)

    qscale = HEAD_DIM ** -0.5 * LOG2E
    q = q_ref[0]
    qg = qg_ref[...]
    qs = jnp.concatenate(
        [_rms_lanes(q[:, g * LANES:(g + 1) * LANES], qg, HEAD_DIM) * qscale for g in range(A_GROUP)],
        axis=0).astype(BF16)

    biasc = biasc_ref[...].reshape(A_GROUP * AT, LANES)
    valid = biasc > 0.5 * NEG
    lcm = _dot_nt(qs, kc_s[...]) + biasc
    mc = jnp.max(lcm, axis=-1, keepdims=True)
    pc = jnp.where(valid, jnp.exp2(lcm - mc), 0.0)
    lc = jnp.sum(pc, axis=-1, keepdims=True)
    pc = pc * jnp.where(lc > 0.0, 1.0 / lc, 0.0)
    o_c = _dot(pc.astype(BF16), vc_s[...])

    ovt = ovt_ref[...]
    p_hi, p_mid, p_lo = _split3(pc[0:AT] + pc[AT:2 * AT] + pc[2 * AT:3 * AT])
    ps = _dot_nt(ovt, p_hi) + _dot_nt(ovt, p_mid) + _dot_nt(ovt, p_lo)
    n_sel = ps.shape[0]
    sid = lax.broadcasted_iota(jnp.int32, (n_sel, AT), 0)
    qcol = lax.broadcasted_iota(jnp.int32, (n_sel, AT), 1)
    cur = qi * (AT // SEL_BLOCK) + lax.shift_right_logical(qcol, 6)
    forced = (sid == 0) | (sid == cur) | (sid == cur - 1)
    psf = jnp.where(forced, ps + FORCE, ps)
    psf = jnp.where(sid <= cur, psf, -1.0)
    rank = jnp.zeros((n_sel, AT), F32)
    for sp in range(n_sel):
        other = psf[sp:sp + 1, :]
        beats = (other > psf) | ((other == psf) & (sid > sp))
        rank = rank + jnp.where(beats, 1.0, 0.0)
    keep = (rank < float(min(N_SELECT, n_sel))) & (psf >= 0.0)
    selneg = _dot_tn(jnp.where(keep, 0.0, NEG).astype(BF16), place_ref[...]).astype(BF16)
    lane_q = lax.broadcasted_iota(jnp.int32, (AT, LANES), 1)
    for g in range(A_GROUP):
        qx_s[g * AT:(g + 1) * AT, :] = jnp.where(lane_q < HEAD_DIM, qs[g * AT:(g + 1) * AT], selneg)

    def near_bias(delta, kind):
        kd = jnp.where(qi >= delta, kind, 3)
        return lambda r: tiles_ref[r // per_g, kd, pl.ds((r % per_g) * RB, RB), :]

    def key_rows(kc):
        return pl.ds(pl.multiple_of(kc * AT, AT), AT)

    m_s[...] = jnp.full(m_s.shape, NEG, F32)
    acc_s[...] = jnp.zeros(acc_s.shape, F32)
    mw_s[...] = jnp.full(mw_s.shape, NEG, F32)
    accw_s[...] = jnp.zeros(accw_s.shape, F32)
    for delta, kind in ((0, 0), (1, 1), (2, 2)):
        rows_k = key_rows(jnp.maximum(qi - delta, 0))
        if delta < 2:
            _flash_chunk(qx_s, k1_s[rows_k, :], v1_s[rows_k, :], near_bias(delta, kind), m_s, acc_s)
        _flash_chunk(qx_s, k2_s[rows_k, :], v2_s[rows_k, :], near_bias(delta, kind), mw_s, accw_s)

    for h, s in enumerate(_qk_halves(qx_s, k1_s[key_rows(0), :])):
        s_s[h] = s

    def far_body(kc, carry):
        s_next = _qk_halves(qx_s, k1_s[key_rows(kc + 1), :])
        _softmax_pv([s_s[0], s_s[1]], v1_s[key_rows(kc), :], None, m_s, acc_s)
        for h, s in enumerate(s_next):
            s_s[h] = s
        return carry

    lax.fori_loop(0, jnp.maximum(qi - 1, 0), far_body, 0)

    gt = jax.nn.sigmoid(gate_ref[0])
    for g in range(A_GROUP):
        rows = slice(g * AT, (g + 1) * AT)
        a_s = acc_s[rows, :]
        a_w = accw_s[rows, :]
        o = (gt[:, 3 * g:3 * g + 1] * o_c[rows]
             + gt[:, 3 * g + 1:3 * g + 2] * (a_s / a_s[:, HEAD_DIM:HEAD_DIM + 1])
             + gt[:, 3 * g + 2:3 * g + 3] * (a_w / a_w[:, HEAD_DIM:HEAD_DIM + 1]))
        o_ref[0, :, g * LANES:(g + 1) * LANES] = jnp.where(lane_q < HEAD_DIM, o, 0.0)


def _nsa(proj, tiles, biasc, cmp_pe, cmp_w1, cmp_w2, q_norm, k_norm):
    B, T, _ = proj.shape
    nq = T // AT
    n_cmp_pad = T // CMP_STRIDE
    n_sel = T // SEL_BLOCK
    assert n_sel <= LANES - HEAD_DIM - 1 and n_cmp_pad == LANES
    starts = np.arange(n_cmp_pad) * CMP_STRIDE
    sid = np.arange(n_sel)
    overlap = ((starts[:, None] < (sid[None, :] + 1) * SEL_BLOCK)
               & (starts[:, None] + CMP_BLOCK > sid[None, :] * SEL_BLOCK)
               & (starts[:, None] + CMP_BLOCK <= T))
    place = np.zeros((n_sel, LANES), np.float32)
    place[sid, HEAD_DIM + sid] = 1.0
    w2p = jnp.pad(cmp_w2, ((0, 0), (0, 0), (0, LANES - HEAD_DIM)))

    def slab(br, kv):
        base = SEG_AKV // LANES + (br * 2 + kv) * 2
        return pl.BlockSpec((1, T, LANES), lambda b, h, i: (b, 0, base + h))

    def full(shape):
        return pl.BlockSpec(shape, lambda b, h, i: (0,) * len(shape))

    rows = A_GROUP * AT
    return pl.pallas_call(
        _nsa_kernel,
        grid=(B, A_KV_HEADS, nq),
        in_specs=[pl.BlockSpec((1, AT, A_GROUP * LANES), lambda b, h, i: (b, i, h)),
                  slab(0, 0), slab(0, 1), slab(1, 0), slab(1, 1), slab(2, 0), slab(2, 1),
                  pl.BlockSpec((1, AT, LANES), lambda b, h, i: (b, i, SEG_GATE // LANES + h)),
                  pl.BlockSpec((A_GROUP, 4, AT, AT), lambda b, h, i: (h, 0, 0, 0)),
                  pl.BlockSpec((A_GROUP, AT, LANES), lambda b, h, i: (h, i, 0)),
                  full((n_sel, n_cmp_pad)), full((n_sel, LANES)),
                  full((2, CMP_BLOCK, HEAD_DIM, HEAD_DIM)), full((2, HEAD_DIM, LANES)),
                  full((2, CMP_BLOCK, HEAD_DIM)), full((1, LANES)), full((1, LANES))],
        out_specs=pl.BlockSpec((1, AT, A_GROUP * LANES), lambda b, h, i: (b, i, h)),
        out_shape=jax.ShapeDtypeStruct((B, T, A_HEADS * LANES), F32),
        scratch_shapes=[pltpu.VMEM((n_cmp_pad, LANES), BF16), pltpu.VMEM((n_cmp_pad, LANES), BF16),
                        pltpu.VMEM((T, LANES), BF16), pltpu.VMEM((T, LANES), BF16),
                        pltpu.VMEM((T, LANES), BF16), pltpu.VMEM((T, LANES), BF16),
                        pltpu.VMEM((rows, LANES), BF16), pltpu.VMEM((2, rows // 2, AT), F32),
                        pltpu.VMEM((rows, LANES), F32), pltpu.VMEM((rows, LANES), F32),
                        pltpu.VMEM((rows, LANES), F32), pltpu.VMEM((rows, LANES), F32)],
        compiler_params=_params("parallel", "parallel", "arbitrary"),
        name="nsa",
    )(proj, proj, proj, proj, proj, proj, proj, proj, tiles, biasc,
      jnp.asarray(overlap.T, BF16), jnp.asarray(place, BF16), cmp_w1, w2p, cmp_pe,
      _pad_lanes(q_norm, LANES), _pad_lanes(k_norm, LANES))


def _mlstm_kernel(qk_ref, v_ref, o_ref, gif_ref, cw_ref, gb_ref, gn_ref, tril_ref, y_ref,
                  xpad_s, c_s, n_s, m_s):
    nb, tb = qk_ref.shape[0], qk_ref.shape[1]
    L = MLSTM_CHUNK
    n_pairs = B_HEADS // 2

    @pl.when(pl.program_id(1) == 0)
    def _init():
        xpad_s[:, tb:tb + 8, :] = jnp.zeros((nb, 8, xpad_s.shape[2]), F32)
        c_s[...] = jnp.zeros(c_s.shape, F32)
        n_s[...] = jnp.zeros(n_s.shape, F32)
        m_s[...] = jnp.zeros(m_s.shape, F32)

    for bb in range(nb):
        xpad_s[bb, 0:8, :] = xpad_s[bb, tb:tb + 8, :]
        xpad_s[bb, 8:8 + tb, :] = qk_ref[bb]
    lane = lax.broadcasted_iota(jnp.int32, (1, LANES), 1)
    lo_half = lane < HEAD_DIM
    rowi = lax.broadcasted_iota(jnp.int32, (LANES, LANES), 0)
    coli = lax.broadcasted_iota(jnp.int32, (LANES, LANES), 1)
    blockdiag = (rowi < HEAD_DIM) == (coli < HEAD_DIM)
    tri = lax.broadcasted_iota(jnp.int32, (L, L), 0) >= lax.broadcasted_iota(jnp.int32, (L, L), 1)
    kscale = HEAD_DIM ** -0.5

    def chunk(c, carry):
        states = [([c_s[bb * n_pairs + p] for p in range(n_pairs)],
                   [n_s[bb * n_pairs + p] for p in range(n_pairs)],
                   [m_s[bb * B_HEADS + hh][0:1, 0:1] for hh in range(B_HEADS)]) for bb in range(nb)]
        pending = []
        for bb in range(nb):
            pending.extend(chunk_one(bb, c, states[bb]))
        for ref, idx, val in pending:
            ref[idx] = val
        return carry

    def chunk_one(bb, c, state):
        c_olds, n_olds, m_prevs = state
        stores = []
        r0 = pl.multiple_of(c * L, L)
        win = xpad_s[bb, pl.ds(r0, L + 8), :]
        conv = cw_ref[CONV_K - 1:CONV_K, :] * win[8:8 + L]
        for j in range(CONV_K - 1):
            conv = conv + cw_ref[j:j + 1, :] * pltpu.roll(win, CONV_K - 1 - j, axis=0)[8:8 + L]
        qk = conv * jax.nn.sigmoid(conv)
        gif = gif_ref[bb, pl.ds(r0, L), :] + gb_ref[...]
        logf = jax.nn.log_sigmoid(gif)
        lh, lm, ll = _split3(logf)
        tril = tril_ref[...]
        bcum = _dot(tril, lh) + _dot(tril, lm) + _dot(tril, ll)
        gif_t = gif.T
        bcum_t = bcum.T
        for p in range(n_pairs):
            qp = qk[:, p * LANES:(p + 1) * LANES]
            kp = qk[:, 2 * LANES + p * LANES:2 * LANES + (p + 1) * LANES] * kscale
            vp = v_ref[bb, pl.ds(r0, L), p * LANES:(p + 1) * LANES]
            c_old = c_olds[p]
            n_old = n_olds[p]
            cq = _dot_nt(qp, c_old)
            hval, wcol, dec, mnew = [], [], [], []
            for half in range(2):
                hh = 2 * p + half
                hmask = lo_half if half == 0 else jnp.logical_not(lo_half)
                b_col = bcum[:, B_HEADS + hh:B_HEADS + hh + 1]
                b_row = bcum_t[B_HEADS + hh:B_HEADS + hh + 1, :]
                i_col = gif[:, hh:hh + 1]
                i_row = gif_t[hh:hh + 1, :]
                m_prev = m_prevs[hh]
                dmat = jnp.where(tri, b_col - b_row + i_row, -jnp.inf)
                m_inter = b_col + m_prev
                m_j = jnp.maximum(m_inter, jnp.max(dmat, axis=-1, keepdims=True))
                isc = jnp.exp(m_inter - m_j)
                qm = jnp.where(hmask, qp, 0.0)
                s = _dot_nt(qm, kp) * jnp.exp(dmat - m_j)
                qn = jnp.sum(qm * n_old, axis=-1, keepdims=True)
                den = jnp.sum(s, axis=-1, keepdims=True) + isc * qn
                num = _dot(s, vp) + isc * cq
                hval.append(num / jnp.maximum(jnp.abs(den), jnp.exp(-m_j)))
                b_last = b_col[L - 1:L, :]
                w_log = b_last - b_col + i_col
                m_new = jnp.maximum(b_last + m_prev, jnp.max(w_log, axis=0, keepdims=True))
                wcol.append(jnp.exp(w_log - m_new))
                dec.append(jnp.exp(b_last + m_prev - m_new))
                mnew.append(m_new)
            w_l = jnp.where(lo_half, wcol[0], wcol[1])
            c_upd = _dot_tn(vp * w_l, kp)
            stores.append((c_s, bb * n_pairs + p, jnp.where(rowi < HEAD_DIM, dec[0], dec[1]) * c_old
                           + jnp.where(blockdiag, c_upd, 0.0)))
            stores.append((n_s, bb * n_pairs + p, jnp.where(lo_half, dec[0], dec[1]) * n_old
                           + jnp.sum(w_l * kp, axis=0, keepdims=True)))
            for half in range(2):
                stores.append((m_s, bb * B_HEADS + 2 * p + half, jnp.broadcast_to(mnew[half], m_s.shape[1:])))
            h = jnp.where(lo_half, hval[0], hval[1])
            hsq = h * h
            ms0 = jnp.sum(jnp.where(lo_half, hsq, 0.0), axis=-1, keepdims=True) * (1.0 / HEAD_DIM)
            ms1 = jnp.sum(jnp.where(lo_half, 0.0, hsq), axis=-1, keepdims=True) * (1.0 / HEAD_DIM)
            hn = h * jnp.where(lo_half, lax.rsqrt(ms0 + EPS), lax.rsqrt(ms1 + EPS)) * gn_ref[:, p * LANES:(p + 1) * LANES]
            og = jax.nn.sigmoid(o_ref[bb, pl.ds(r0, L), p * LANES:(p + 1) * LANES])
            stores.append((y_ref, (bb, pl.ds(r0, L), slice(p * LANES, (p + 1) * LANES)), og * hn))
        return stores

    lax.fori_loop(0, tb // L, chunk, 0)


MLSTM_NB = 2
MLSTM_TB = 512


def _mlstm(proj, conv_w, i_bias, f_bias, out_norm):
    B, T, _ = proj.shape
    W = B_HEADS * HEAD_DIM
    nb, tb = MLSTM_NB, MLSTM_TB
    gb = _pad_lanes(jnp.concatenate([i_bias, f_bias]), LANES)
    tril = jnp.asarray(np.tril(np.ones((MLSTM_CHUNK, MLSTM_CHUNK))), BF16)

    def full(shape):
        return pl.BlockSpec(shape, lambda b, t: (0,) * len(shape))

    return pl.pallas_call(
        _mlstm_kernel,
        grid=(B // nb, T // tb),
        in_specs=[pl.BlockSpec((nb, tb, 2 * W), lambda b, t: (b, t, SEG_BQK // (2 * W))),
                  pl.BlockSpec((nb, tb, W), lambda b, t: (b, t, SEG_BV // W)),
                  pl.BlockSpec((nb, tb, W), lambda b, t: (b, t, SEG_BO // W)),
                  pl.BlockSpec((nb, tb, LANES), lambda b, t: (b, t, SEG_BIF // LANES)),
                  full((CONV_K, 2 * W)), full((1, LANES)), full((1, W)),
                  full((MLSTM_CHUNK, MLSTM_CHUNK))],
        out_specs=pl.BlockSpec((nb, tb, W), lambda b, t: (b, t, 0)),
        out_shape=jax.ShapeDtypeStruct((B, T, W), F32),
        scratch_shapes=[pltpu.VMEM((nb, tb + 8, 2 * W), F32),
                        pltpu.VMEM((nb * B_HEADS // 2, LANES, LANES), F32),
                        pltpu.VMEM((nb * B_HEADS // 2, 1, LANES), F32),
                        pltpu.VMEM((nb * B_HEADS, 8, LANES), F32)],
        compiler_params=_params("parallel", "arbitrary"),
        name="mlstm",
    )(proj, proj, proj, proj, conv_w, gb, out_norm.reshape(1, W), tril)


def _rope_kernel(pos_ref, invf_ref, sgn_ref, cs_ref, sn_ref):
    ang = pos_ref[...].astype(F32) * invf_ref[...]
    cs_ref[...] = jnp.cos(ang)
    sn_ref[...] = jnp.sin(ang) * sgn_ref[...]


def _rope_tables(pos):
    N = pos.shape[0]
    tm = 1024
    half = C_ROPE // 2
    inv = 1.0 / (ROPE_THETA ** (np.arange(half, dtype=np.float32) * 2.0 / C_ROPE))
    invf = np.zeros((1, LANES), np.float32)
    sgn = np.zeros((1, LANES), np.float32)
    invf[0, C_NOPE:C_NOPE + C_ROPE] = np.concatenate([inv, inv])
    sgn[0, C_NOPE:C_NOPE + half] = -1.0
    sgn[0, C_NOPE + half:C_NOPE + C_ROPE] = 1.0
    row = pl.BlockSpec((1, LANES), lambda i: (0, 0))
    tile = pl.BlockSpec((tm, LANES), lambda i: (i, 0))
    return pl.pallas_call(
        _rope_kernel,
        grid=(N // tm,),
        in_specs=[pl.BlockSpec((tm, 1), lambda i: (i, 0)), row, row],
        out_specs=[tile, tile],
        out_shape=[jax.ShapeDtypeStruct((N, LANES), F32), jax.ShapeDtypeStruct((N, LANES), F32)],
        compiler_params=_params("parallel"),
        name="rope_tables",
    )(pos, jnp.asarray(invf), jnp.asarray(sgn))


def _mla_prep_kernel(dq_ref, dkv_ref, kr_ref, cs_ref, sn_ref, wq_ref, wk_ref, wv_ref, qa_ref, kva_ref,
                     gq_ref, gk_ref, gkr_ref, q_out, k_out, v_out):
    lane = lax.broadcasted_iota(jnp.int32, (1, LANES), 1)
    nope = lane < C_NOPE
    ropem = (lane >= C_NOPE) & (lane < C_NOPE + C_ROPE)
    first = lane < C_NOPE + C_ROPE // 2
    cs = cs_ref[...]
    sn = sn_ref[...]

    def rope(x):
        partner = jnp.where(first, pltpu.roll(x, LANES - C_ROPE // 2, axis=1), pltpu.roll(x, C_ROPE // 2, axis=1))
        return x * cs + partner * sn

    scale = (C_NOPE + C_ROPE) ** -0.5 * LOG2E
    qn = _rms_lanes(dq_ref[...], qa_ref[...], Q_RANK).astype(BF16)
    q = _dot(qn, wq_ref[...])
    kvn = _rms_lanes(dkv_ref[...], kva_ref[...], KV_RANK).astype(BF16)
    kk = _dot(kvn, wk_ref[...])
    vv = _dot(kvn, wv_ref[...])
    lane_v = lax.broadcasted_iota(jnp.int32, vv.shape, 1)
    v_out[...] = jnp.where((lane_v & (LANES - 1)) == C_V, 1.0, vv).astype(BF16)
    kr = kr_ref[...]
    krn = kr * lax.rsqrt(jnp.sum(kr * kr, axis=-1, keepdims=True) * (1.0 / C_ROPE) + EPS) * gkr_ref[...]
    krr = rope(krn)
    for h in range(C_HEADS):
        sl = slice(h * LANES, (h + 1) * LANES)
        qh = q[:, sl]
        sq = qh * qh
        msn = jnp.sum(jnp.where(nope, sq, 0.0), axis=-1, keepdims=True) * (1.0 / C_NOPE)
        msr = jnp.sum(jnp.where(ropem, sq, 0.0), axis=-1, keepdims=True) * (1.0 / C_ROPE)
        qhn = qh * jnp.where(nope, lax.rsqrt(msn + EPS), lax.rsqrt(msr + EPS)) * gq_ref[...]
        q_out[:, sl] = (rope(qhn) * scale).astype(BF16)
        kh = kk[:, sl]
        msk = jnp.sum(kh * kh, axis=-1, keepdims=True) * (1.0 / C_NOPE)
        k_out[:, sl] = (kh * lax.rsqrt(msk + EPS) * gk_ref[...] + krr).astype(BF16)


def _mla_prep(proj2, rope_cs, rope_sn, q_a_norm, w_uq, kv_a_norm, w_ukv, qk_norm):
    N = proj2.shape[0]
    tm = 512
    qd = C_NOPE + C_ROPE
    qmap = np.full((C_HEADS * LANES,), -1, np.int64)
    kmap = np.full((C_HEADS * LANES,), -1, np.int64)
    vmap = np.full((C_HEADS * LANES,), -1, np.int64)
    for h in range(C_HEADS):
        qmap[h * LANES:h * LANES + qd] = h * qd + np.arange(qd)
        kmap[h * LANES:h * LANES + C_NOPE] = h * (C_NOPE + C_V) + np.arange(C_NOPE)
        vmap[h * LANES:h * LANES + C_V] = h * (C_NOPE + C_V) + C_NOPE + np.arange(C_V)
    wq = jnp.pad(_take_cols(w_uq, qmap), ((0, 256 - Q_RANK), (0, 0))).astype(BF16)
    wk = _take_cols(w_ukv, kmap).astype(BF16)
    wv = _take_cols(w_ukv, vmap).astype(BF16)
    gkr = jnp.pad(qk_norm[1, C_NOPE:].reshape(1, -1), ((0, 0), (C_NOPE, LANES - C_NOPE - C_ROPE)))

    def full(shape):
        return pl.BlockSpec(shape, lambda i: (0,) * len(shape))

    return pl.pallas_call(
        _mla_prep_kernel,
        grid=(N // tm,),
        in_specs=[pl.BlockSpec((tm, 256), lambda i: (i, SEG_CDQ // 256)),
                  pl.BlockSpec((tm, LANES), lambda i: (i, SEG_CDKV // LANES)),
                  pl.BlockSpec((tm, LANES), lambda i: (i, SEG_CKR // LANES)),
                  pl.BlockSpec((tm, LANES), lambda i: (i, 0)), pl.BlockSpec((tm, LANES), lambda i: (i, 0)),
                  full((256, C_HEADS * LANES)), full((KV_RANK, C_HEADS * LANES)), full((KV_RANK, C_HEADS * LANES)),
                  full((1, 256)), full((1, LANES)), full((1, LANES)), full((1, LANES)), full((1, LANES))],
        out_specs=[pl.BlockSpec((tm, C_HEADS * LANES), lambda i: (i, 0)),
                   pl.BlockSpec((tm, C_HEADS * LANES), lambda i: (i, 0)),
                   pl.BlockSpec((tm, C_HEADS * LANES), lambda i: (i, 0))],
        out_shape=[jax.ShapeDtypeStruct((N, C_HEADS * LANES), BF16),
                   jax.ShapeDtypeStruct((N, C_HEADS * LANES), BF16),
                   jax.ShapeDtypeStruct((N, C_HEADS * LANES), BF16)],
        compiler_params=_params("parallel"),
        name="mla_prep",
    )(proj2, proj2, proj2, rope_cs, rope_sn, wq, wk, wv,
      _pad_lanes(q_a_norm, 256), kv_a_norm.reshape(1, -1), _pad_lanes(qk_norm[0], LANES),
      _pad_lanes(qk_norm[1, :C_NOPE], LANES), gkr)


def _mla_attn_kernel(q_ref, k_ref, v_ref, causal_ref, o_ref, s_s, m_s, acc_s):
    qi = pl.program_id(2)
    m_s[...] = jnp.full(m_s.shape, NEG, F32)
    acc_s[...] = jnp.zeros(acc_s.shape, F32)
    heads = [slice(hh * LANES, (hh + 1) * LANES) for hh in range(2)]
    per_head = AT // RB

    def key_rows(kc):
        return pl.ds(pl.multiple_of(kc * AT, AT), AT)

    def scores(kc):
        return [_dot_nt(q_ref[0, :, sl], k_ref[0, key_rows(kc), sl]) for sl in heads]

    def values(kc):
        return [v_ref[0, key_rows(kc), sl] for sl in heads]

    _softmax_pv(scores(qi), values(qi), lambda r: causal_ref[pl.ds((r % per_head) * RB, RB), :], m_s, acc_s)

    for h, s in enumerate(scores(0)):
        s_s[h] = s

    def body(kc, carry):
        s_next = scores(kc + 1)
        _softmax_pv([s_s[0], s_s[1]], values(kc), None, m_s, acc_s)
        for h, s in enumerate(s_next):
            s_s[h] = s
        return carry

    lax.fori_loop(0, qi, body, 0)
    lane = lax.broadcasted_iota(jnp.int32, (AT, LANES), 1)
    outs = []
    for hh in range(2):
        a = acc_s[hh * AT:(hh + 1) * AT, :]
        outs.append(jnp.where(lane < C_V, a / a[:, C_V:C_V + 1], 0.0))
    o_ref[0] = outs[0] + pltpu.roll(outs[1], C_V, axis=1)


def _mla_attn(q, k, v):
    B, T, _ = q.shape
    causal = np.where(np.arange(AT)[:, None] >= np.arange(AT)[None, :], 0.0, NEG).astype(np.float32)
    pair = pl.BlockSpec((1, T, 2 * LANES), lambda b, p, i: (b, 0, p))
    return pl.pallas_call(
        _mla_attn_kernel,
        grid=(B, C_HEADS // 2, T // AT),
        in_specs=[pl.BlockSpec((1, AT, 2 * LANES), lambda b, p, i: (b, i, p)), pair, pair,
                  pl.BlockSpec((AT, AT), lambda b, p, i: (0, 0))],
        out_specs=pl.BlockSpec((1, AT, LANES), lambda b, p, i: (b, i, p)),
        out_shape=jax.ShapeDtypeStruct((B, T, C_HEADS * C_V), F32),
        scratch_shapes=[pltpu.VMEM((2, AT, AT), F32), pltpu.VMEM((2 * AT, LANES), F32),
                        pltpu.VMEM((2 * AT, LANES), F32)],
        compiler_params=_params("parallel", "parallel", "arbitrary"),
        name="mla_attn",
    )(q, k, v, jnp.asarray(causal))


def _outproj_kernel(x_ref, ya_ref, yb_ref, yc_ref, wa_ref, wb_ref, wc_ref, ga_ref, gc_ref, g2_ref,
                    wr_ref, br_ref, x1_ref, h2_ref, ids_ref, wts_ref, cnt_ref):
    ya = _rms_lanes(ya_ref[...], ga_ref[...], A_HEADS * HEAD_DIM).astype(BF16)
    yc = _rms_lanes(yc_ref[...], gc_ref[...], C_HEADS * C_V).astype(BF16)
    x1 = (x_ref[...] + _dot(ya, wa_ref[...]) + _dot(yb_ref[...].astype(BF16), wb_ref[...])
          + _dot(yc, wc_ref[...]))
    x1_ref[...] = x1
    h2 = _rms_lanes(x1, g2_ref[...], D_MODEL)
    h2_ref[...] = h2
    h2_hi = h2.astype(BF16)
    h2_mid = (h2 - h2_hi.astype(F32)).astype(BF16)
    logits = (_dot(h2_hi, wr_ref[0]) + _dot(h2_mid, wr_ref[0]) + _dot(h2_hi, wr_ref[1])) + br_ref[...]
    tm = logits.shape[0]
    lane = lax.broadcasted_iota(jnp.int32, (tm, LANES), 1)
    is_g = (lane >= N_EXPERTS) & (lane < N_EXPERTS + N_GROUPS)
    gl = jnp.where(is_g, logits, NEG)
    ge = jnp.where(is_g, jnp.exp(gl - jnp.max(gl, axis=-1, keepdims=True)), 0.0)
    gp = jnp.where(is_g, ge / jnp.sum(ge, axis=-1, keepdims=True), -1.0)
    g_top = jnp.max(gp, axis=-1, keepdims=True)
    g_idx = jnp.min(jnp.where(gp == g_top, lane, 2 * LANES), axis=-1, keepdims=True) - N_EXPERTS
    in_grp = (lane < N_EXPERTS) & (lax.shift_right_logical(lane, 3) == g_idx)
    el = jnp.where(in_grp, logits, NEG)
    ee = jnp.where(in_grp, jnp.exp(el - jnp.max(el, axis=-1, keepdims=True)), 0.0)
    ep = jnp.where(in_grp, ee / jnp.sum(ee, axis=-1, keepdims=True), -1.0)
    v1 = jnp.max(ep, axis=-1, keepdims=True)
    i1 = jnp.min(jnp.where(ep == v1, lane, 2 * LANES), axis=-1, keepdims=True)
    ep2 = jnp.where(lane == i1, -1.0, ep)
    v2 = jnp.max(ep2, axis=-1, keepdims=True)
    i2 = jnp.min(jnp.where(ep2 == v2, lane, 2 * LANES), axis=-1, keepdims=True)
    tot = v1 + v2
    ids_ref[...] = jnp.where(lane == 0, i1, jnp.where(lane == 1, i2, 0))
    wts_ref[...] = jnp.where(lane == 0, v1 / tot * g_top, jnp.where(lane == 1, v2 / tot * g_top, 0.0))
    per_expert = jnp.sum(jnp.where((lane == i1) | (lane == i2), 1.0, 0.0), axis=0, keepdims=True)
    cnt_ref[...] = jnp.broadcast_to(per_expert, cnt_ref.shape)


def _outproj(x2, ya, yb, yc, w_out, nsa_out_norm, mla_out_norm, norm2_g, w_group, b_group, w_expert, b_expert):
    N = x2.shape[0]
    tm = 256
    amap = np.full((A_HEADS * LANES,), -1, np.int64)
    for hd in range(A_HEADS):
        amap[hd * LANES:hd * LANES + HEAD_DIM] = hd * HEAD_DIM + np.arange(HEAD_DIM)
    wa = _take_rows(w_out[:384], amap).astype(BF16)
    wb = w_out[384:640].astype(BF16)
    wc = w_out[640:].astype(BF16)
    ga = _take_cols(nsa_out_norm.reshape(1, -1), amap)
    wr = jnp.pad(jnp.concatenate([w_expert, w_group], axis=1), ((0, 0), (0, LANES - N_EXPERTS - N_GROUPS)))
    wr_hi = wr.astype(BF16)
    wr = jnp.stack([wr_hi, (wr - wr_hi.astype(F32)).astype(BF16)])
    br = _pad_lanes(jnp.concatenate([b_expert, b_group]), LANES)

    def full(shape):
        return pl.BlockSpec(shape, lambda i: (0,) * len(shape))

    def rows(w):
        return pl.BlockSpec((tm, w), lambda i: (i, 0))

    return pl.pallas_call(
        _outproj_kernel,
        grid=(N // tm,),
        in_specs=[rows(D_MODEL), rows(A_HEADS * LANES), rows(256), rows(384),
                  full((A_HEADS * LANES, D_MODEL)), full((256, D_MODEL)), full((384, D_MODEL)),
                  full((1, A_HEADS * LANES)), full((1, 384)), full((1, D_MODEL)),
                  full((2, D_MODEL, LANES)), full((1, LANES))],
        out_specs=[rows(D_MODEL), rows(D_MODEL), rows(LANES), rows(LANES),
                   pl.BlockSpec((8, LANES), lambda i: (i, 0))],
        out_shape=[jax.ShapeDtypeStruct((N, D_MODEL), F32), jax.ShapeDtypeStruct((N, D_MODEL), F32),
                   jax.ShapeDtypeStruct((N, LANES), jnp.int32), jax.ShapeDtypeStruct((N, LANES), F32),
                   jax.ShapeDtypeStruct((N // tm * 8, LANES), F32)],
        compiler_params=_params("parallel"),
        name="outproj_router",
    )(x2, ya, yb, yc, wa, wb, wc, ga, mla_out_norm.reshape(1, -1), norm2_g.reshape(1, -1), wr, br)


def _rank_kernel(ids_ref, off_ref, tril_ref, pos_ref, carry_s):
    i = pl.program_id(0)

    @pl.when(i == 0)
    def _():
        carry_s[...] = jnp.broadcast_to(off_ref[...], carry_s.shape)

    ids = ids_ref[...]
    tm = ids.shape[0]
    lane = lax.broadcasted_iota(jnp.int32, (tm, LANES), 1)
    i1 = ids[:, 0:1]
    i2 = ids[:, 1:2]
    oh = jnp.where((lane == i1) | (lane == i2), 1.0, 0.0)
    incl = _dot(tril_ref[...], oh.astype(BF16)) + carry_s[0:1, :]
    excl = incl - oh
    p1 = jnp.sum(jnp.where(lane == i1, excl, 0.0), axis=-1, keepdims=True)
    p2 = jnp.sum(jnp.where(lane == i2, excl, 0.0), axis=-1, keepdims=True)
    pos_ref[...] = jnp.where(lane == 0, p1, jnp.where(lane == 1, p2, 0.0)).astype(jnp.int32)
    carry_s[...] = jnp.broadcast_to(incl[tm - 1:tm, :], carry_s.shape)


def _moe_positions(ids, off_row):
    N = ids.shape[0]
    tm = 512
    tril = jnp.asarray(np.tril(np.ones((tm, tm))), BF16)
    return pl.pallas_call(
        _rank_kernel,
        grid=(N // tm,),
        in_specs=[pl.BlockSpec((tm, LANES), lambda i: (i, 0)), pl.BlockSpec((1, LANES), lambda i: (0, 0)),
                  pl.BlockSpec((tm, tm), lambda i: (0, 0))],
        out_specs=pl.BlockSpec((tm, LANES), lambda i: (i, 0)),
        out_shape=jax.ShapeDtypeStruct((N, LANES), jnp.int32),
        scratch_shapes=[pltpu.VMEM((8, LANES), F32)],
        compiler_params=_params("arbitrary"),
        name="moe_positions",
    )(ids, off_row, tril)


DISPATCH_TM = 1024


DMA_UNROLL = 8


def _src_kernel(p1_ref, p2_ref, src_ref):
    step = pl.program_id(0)

    @pl.when(step == 0)
    def _init():
        def zero(j, carry):
            for u in range(DMA_UNROLL):
                src_ref[j * DMA_UNROLL + u] = 0
            return carry

        lax.fori_loop(0, src_ref.shape[0] // DMA_UNROLL, zero, 0)

    def put(j, carry):
        for u in range(DMA_UNROLL):
            i = j * DMA_UNROLL + u
            tok = step * DISPATCH_TM + i
            src_ref[p1_ref[i]] = tok
            src_ref[p2_ref[i]] = tok
        return carry

    lax.fori_loop(0, DISPATCH_TM // DMA_UNROLL, put, 0)


def _moe_sources(pos1, pos2, n_rows):
    N = pos1.shape[0]
    smem1d = pl.BlockSpec((DISPATCH_TM,), lambda i: (i,), memory_space=pltpu.SMEM)
    return pl.pallas_call(
        _src_kernel,
        grid=(N // DISPATCH_TM,),
        in_specs=[smem1d, smem1d],
        out_specs=pl.BlockSpec(memory_space=pltpu.SMEM),
        out_shape=jax.ShapeDtypeStruct((n_rows,), jnp.int32),
        compiler_params=_params("arbitrary"),
        name="moe_sources",
    )(pos1, pos2)


def _expert_kernel(te_ref, nu_ref, src_ref, nxt_ref, h_ref, wg_ref, wu_ref, wd_ref, y_ref, x_s, sems):
    g = pl.program_id(0)
    n_used = nu_ref[0]

    def gather(idx_ref, slot):
        def issue(j, carry):
            for u in range(DMA_UNROLL):
                i = j * DMA_UNROLL + u
                pltpu.make_async_copy(h_ref.at[pl.ds(idx_ref[i], 1), :], x_s.at[slot, pl.ds(i, 1), :],
                                      sems.at[slot]).start(priority=u % 2)
            return carry

        lax.fori_loop(0, ROW_TILE // DMA_UNROLL, issue, 0)

    @pl.when(g == 0)
    def _first():
        gather(src_ref, 0)

    for parity in range(2):
        @pl.when((g + 1 < n_used) & ((g + 1) % 2 == parity))
        def _ahead(parity=parity):
            gather(nxt_ref, parity)

    @pl.when(g < n_used)
    def _compute():
        slot = g % 2
        pltpu.make_async_copy(h_ref.at[pl.ds(0, ROW_TILE), :], x_s.at[slot], sems.at[slot]).wait()
        x = x_s[slot].astype(BF16)
        a = _dot(x, wg_ref[0, 0].astype(BF16))
        u = _dot(x, wu_ref[0, 0].astype(BF16))
        h = (a * jax.nn.sigmoid(a) * u).astype(BF16)
        y_ref[...] = _dot(h, wd_ref[0, 0].astype(BF16))

    @pl.when(g >= n_used)
    def _unused():
        y_ref[...] = jnp.zeros(y_ref.shape, F32)


def _moe_experts(h2, src, tile_expert, n_used, layer, w_gate, w_up, w_down):
    R = src.shape[0]
    G = R // ROW_TILE

    def w_map(g, te, nu):
        return (layer, te[jnp.minimum(g, nu[0] - 1)], 0, 0)

    return pl.pallas_call(
        _expert_kernel,
        grid_spec=pltpu.PrefetchScalarGridSpec(
            num_scalar_prefetch=2,
            grid=(G,),
            in_specs=[pl.BlockSpec((ROW_TILE,), lambda g, te, nu: (jnp.minimum(g, nu[0] - 1),),
                                   memory_space=pltpu.SMEM),
                      pl.BlockSpec((ROW_TILE,), lambda g, te, nu: (jnp.minimum(g + 1, nu[0] - 1),),
                                   memory_space=pltpu.SMEM),
                      pl.BlockSpec(memory_space=pl.ANY),
                      pl.BlockSpec((1, 1, D_MODEL, D_EXPERT), w_map),
                      pl.BlockSpec((1, 1, D_MODEL, D_EXPERT), w_map),
                      pl.BlockSpec((1, 1, D_EXPERT, D_MODEL), w_map)],
            out_specs=pl.BlockSpec((ROW_TILE, D_MODEL), lambda g, te, nu: (g, 0)),
            scratch_shapes=[pltpu.VMEM((2, ROW_TILE, D_MODEL), F32), pltpu.SemaphoreType.DMA((2,))]),
        out_shape=jax.ShapeDtypeStruct((R, D_MODEL), F32),
        compiler_params=_params("arbitrary"),
        name="moe_experts",
    )(tile_expert, n_used, src, src, h2, w_gate, w_up, w_down)


COMBINE_TM = 256


def _combine_kernel(p1_ref, p2_ref, x1_ref, wts_ref, ys_ref, o_ref, b1_s, b2_s, sem):
    def row_copy(buf, i, pos):
        return pltpu.make_async_copy(ys_ref.at[pl.ds(pos, 1), :], buf.at[pl.ds(i, 1), :], sem)

    def issue(j, carry):
        for u in range(DMA_UNROLL):
            i = j * DMA_UNROLL + u
            row_copy(b1_s, i, p1_ref[i]).start(priority=0)
            row_copy(b2_s, i, p2_ref[i]).start(priority=1)
        return carry

    lax.fori_loop(0, COMBINE_TM // DMA_UNROLL, issue, 0)
    pltpu.make_async_copy(ys_ref.at[pl.ds(0, COMBINE_TM), :], b1_s, sem).wait()
    pltpu.make_async_copy(ys_ref.at[pl.ds(0, COMBINE_TM), :], b2_s, sem).wait()
    w = wts_ref[...]
    o_ref[...] = x1_ref[...] + w[:, 0:1] * b1_s[...] + w[:, 1:2] * b2_s[...]


def _moe_combine(x1, wts, ys, pos1, pos2):
    N = x1.shape[0]
    smem1d = pl.BlockSpec((COMBINE_TM,), lambda i: (i,), memory_space=pltpu.SMEM)
    return pl.pallas_call(
        _combine_kernel,
        grid=(N // COMBINE_TM,),
        in_specs=[smem1d, smem1d,
                  pl.BlockSpec((COMBINE_TM, D_MODEL), lambda i: (i, 0)),
                  pl.BlockSpec((COMBINE_TM, LANES), lambda i: (i, 0)),
                  pl.BlockSpec(memory_space=pl.ANY)],
        out_specs=pl.BlockSpec((COMBINE_TM, D_MODEL), lambda i: (i, 0)),
        out_shape=jax.ShapeDtypeStruct((N, D_MODEL), F32),
        scratch_shapes=[pltpu.VMEM((COMBINE_TM, D_MODEL), F32), pltpu.VMEM((COMBINE_TM, D_MODEL), F32),
                        pltpu.SemaphoreType.DMA(())],
        compiler_params=_params("arbitrary"),
        name="moe_combine",
    )(pos1, pos2, x1, wts, ys)


def _moe(x1, h2, ids, wts, tile_counts, layer, w_gate, w_up, w_down):
    N = x1.shape[0]
    n_rows = 2 * N + N_EXPERTS * ROW_TILE
    counts = jnp.sum(tile_counts[::8, :N_EXPERTS], axis=0).astype(jnp.int32)
    tiles_per = (counts + ROW_TILE - 1) // ROW_TILE
    tile_end = jnp.cumsum(tiles_per)
    off = ((tile_end - tiles_per) * ROW_TILE).astype(jnp.int32)
    n_used = tile_end[-1:].astype(jnp.int32)
    tile_ids = jnp.arange(n_rows // ROW_TILE, dtype=jnp.int32)
    tile_expert = jnp.minimum(jnp.sum(tile_end[None, :] <= tile_ids[:, None], axis=1), N_EXPERTS - 1).astype(jnp.int32)
    pos = _moe_positions(ids, _pad_lanes(off.astype(F32), LANES))
    pos1, pos2 = pos[:, 0], pos[:, 1]
    src = _moe_sources(pos1, pos2, n_rows)
    ys = _moe_experts(h2, src, tile_expert, n_used, layer, w_gate, w_up, w_down)
    return _moe_combine(x1, wts, ys, pos1, pos2)


def kernel(x, positions, rel_bias, norm1_g, w_in, nsa_cmp_pe, nsa_cmp_w1, nsa_cmp_w2, nsa_q_norm, nsa_k_norm, nsa_out_norm, mlstm_conv, mlstm_i_bias, mlstm_f_bias, mlstm_out_norm, mla_q_a_norm, mla_w_uq, mla_kv_a_norm, mla_w_ukv, mla_qk_norm, mla_out_norm, w_out, norm2_g, moe_w_group, moe_b_group, moe_w_expert, moe_b_expert, moe_w_gate, moe_w_up, moe_w_down):
    B, T, D = x.shape
    N = B * T
    depth = w_in.shape[0]
    colmap = _proj_colmap()
    tiles, biasc = _bias_tables(rel_bias, T)
    rope_cs, rope_sn = _rope_tables(positions.reshape(N, 1))
    x2 = x.reshape(N, D)
    for l in range(depth):
        w_proj = _take_cols(w_in[l], colmap).astype(BF16)
        proj2 = _inproj(x2, norm1_g[l].reshape(1, D), w_proj)
        proj = proj2.reshape(B, T, D_PROJ)
        ya = _nsa(proj, tiles, biasc, nsa_cmp_pe[l], nsa_cmp_w1[l], nsa_cmp_w2[l], nsa_q_norm[l], nsa_k_norm[l])
        yb = _mlstm(proj, mlstm_conv[l], mlstm_i_bias[l], mlstm_f_bias[l], mlstm_out_norm[l])
        q, k, v = _mla_prep(proj2, rope_cs, rope_sn, mla_q_a_norm[l], mla_w_uq[l], mla_kv_a_norm[l], mla_w_ukv[l], mla_qk_norm[l])
        yc = _mla_attn(q.reshape(B, T, -1), k.reshape(B, T, -1), v.reshape(B, T, -1))
        x1, h2, ids, wts, tile_counts = _outproj(x2, ya.reshape(N, -1), yb.reshape(N, -1), yc.reshape(N, -1), w_out[l],
                                    nsa_out_norm[l], mla_out_norm[l], norm2_g[l], moe_w_group[l], moe_b_group[l],
                                    moe_w_expert[l], moe_b_expert[l])
        x2 = _moe(x1, h2, ids, wts, tile_counts, l, moe_w_gate, moe_w_up, moe_w_down)
    return x2.reshape(B, T, D)
```

```python
import functools
import math

import numpy as np
import jax
import jax.numpy as jnp
from jax import lax
from jax.experimental import pallas as pl
from jax.experimental.pallas import tpu as pltpu

F32 = jnp.float32
BF16 = jnp.bfloat16

D_MODEL = 1024
HEAD_DIM = 64
LANES = 128
A_HEADS, A_KV_HEADS, A_GROUP = 6, 2, 3
CMP_BLOCK, CMP_STRIDE, SEL_BLOCK, N_SELECT, WINDOW = 32, 16, 64, 16, 512
FORCE = 1e4
B_HEADS, CONV_K, MLSTM_CHUNK = 4, 4, 64
C_HEADS, C_NOPE, C_ROPE, C_V = 6, 64, 32, 64
Q_RANK, KV_RANK = 192, 128
ROPE_THETA = 10000.0
REL_BUCKETS, REL_MAX_DIST = 32, 128
N_GROUPS, EXPERTS_PER_GROUP, N_EXPERTS, D_EXPERT = 4, 8, 32, 256
EPS = 1e-6
NEG = -1e30

LOG2E = 1.4426950408889634

AT = 256
RB = 128
ROW_TILE = 256
VMEM_LIMIT = 48 * 1024 * 1024

SEG_AQ, SEG_AKV, SEG_GATE, SEG_BQK, SEG_BV, SEG_BO = 0, 768, 2304, 2560, 3072, 3328
SEG_CDQ, SEG_CDKV, SEG_CKR, SEG_BIF, D_PROJ = 3584, 3840, 3968, 4096, 4608


def _params(*sem):
    return pltpu.CompilerParams(dimension_semantics=sem, vmem_limit_bytes=VMEM_LIMIT)


def _dot(a, b):
    return jnp.dot(a, b, preferred_element_type=F32)


def _dot_nt(a, b):
    return lax.dot_general(a, b, (((1,), (1,)), ((), ())), preferred_element_type=F32)


def _dot_tn(a, b):
    return lax.dot_general(a, b, (((0,), (0,)), ((), ())), preferred_element_type=F32)


def _split3(x):
    hi = x.astype(BF16)
    r1 = x - hi.astype(F32)
    mid = r1.astype(BF16)
    lo = (r1 - mid.astype(F32)).astype(BF16)
    return hi, mid, lo


def _dot_exact_rhs(x, m_bf16):
    hi, mid, lo = _split3(x)
    return _dot(hi, m_bf16) + _dot(mid, m_bf16) + _dot(lo, m_bf16)


def _dot_f32(x, w):
    xh = x.astype(BF16)
    xm = (x - xh.astype(F32)).astype(BF16)
    wh = w.astype(BF16)
    wm = (w - wh.astype(F32)).astype(BF16)
    return _dot(xh, wh) + _dot(xm, wh) + _dot(xh, wm)


def _bucket_np(dist):
    n = np.maximum(dist, 0)
    exact = REL_BUCKETS // 2
    nf = np.maximum(n, 1).astype(np.float64)
    large = exact + (np.log(nf / exact) / math.log(REL_MAX_DIST / exact) * (REL_BUCKETS - exact)).astype(np.int64)
    large = np.minimum(large, REL_BUCKETS - 1)
    return np.where(n < exact, n, large).astype(np.int32)


def _bucket_tables(T):
    r = np.arange(AT)[:, None]
    c = np.arange(AT)[None, :]
    diag = np.where(r >= c, _bucket_np(r - c), -1)
    prev = _bucket_np(AT + r - c)
    far = np.where(c > r, _bucket_np(2 * AT + r - c), -1)
    tiles = np.stack([diag, prev, far]).astype(np.int32)
    tq = np.arange(T)[:, None]
    n = np.arange(LANES)[None, :]
    dist_c = tq - (n * CMP_STRIDE + CMP_BLOCK - 1)
    cmp_tbl = np.where(dist_c >= 0, _bucket_np(dist_c), -1).astype(np.int32)
    return tiles, cmp_tbl


def _proj_colmap():
    cm = np.full((D_PROJ,), -1, np.int64)
    for hd in range(A_HEADS):
        cm[SEG_AQ + hd * LANES: SEG_AQ + hd * LANES + HEAD_DIM] = hd * HEAD_DIM + np.arange(HEAD_DIM)
    for s in range(12):
        cm[SEG_AKV + s * LANES: SEG_AKV + s * LANES + HEAD_DIM] = 384 + s * HEAD_DIM + np.arange(HEAD_DIM)
    for h in range(A_KV_HEADS):
        cm[SEG_GATE + h * LANES: SEG_GATE + h * LANES + 9] = 1152 + h * 9 + np.arange(9)
    cm[SEG_BQK: SEG_BQK + 512] = 1170 + np.arange(512)
    cm[SEG_BV: SEG_BV + 256] = 1682 + np.arange(256)
    cm[SEG_BIF: SEG_BIF + 2 * B_HEADS * HEAD_DIM] = 1938 + np.arange(2 * B_HEADS * HEAD_DIM) // HEAD_DIM
    cm[SEG_BO: SEG_BO + 256] = 1946 + np.arange(256)
    cm[SEG_CDQ: SEG_CDQ + Q_RANK] = 2202 + np.arange(Q_RANK)
    cm[SEG_CDKV: SEG_CDKV + KV_RANK] = 2394 + np.arange(KV_RANK)
    cm[SEG_CKR + C_NOPE: SEG_CKR + C_NOPE + C_ROPE] = 2522 + np.arange(C_ROPE)
    return cm


def _take_cols(w, colmap):
    wz = jnp.concatenate([w, jnp.zeros((w.shape[0], 1), w.dtype)], axis=1)
    return wz[:, np.where(colmap < 0, w.shape[1], colmap)]


def _take_rows(w, rowmap):
    return _take_cols(w.T, rowmap).T


def _pad_lanes(v, n):
    v = v.reshape(1, -1)
    return jnp.pad(v, ((0, 0), (0, n - v.shape[1])))


def _bias_kernel(rb_ref, bt_ref, bc_ref, tiles_ref, cmp_ref):
    hd = pl.program_id(0)
    bt = bt_ref[...]
    bc = bc_ref[...]
    t = jnp.full(bt.shape, NEG, F32)
    c = jnp.full(bc.shape, NEG, F32)
    far = rb_ref[REL_BUCKETS - 1, hd]
    for b in range(REL_BUCKETS):
        val = rb_ref[b, hd]
        t = jnp.where(bt == b, (val - far) * LOG2E, t)
        c = jnp.where(bc == b, val * LOG2E, c)
    tiles_ref[0, 0:3] = t
    tiles_ref[0, 3] = jnp.full((AT, AT), NEG, F32)
    cmp_ref[0] = c


def _bias_tables(rel_bias, T):
    bt, bc = _bucket_tables(T)
    return pl.pallas_call(
        _bias_kernel,
        grid=(A_HEADS,),
        in_specs=[pl.BlockSpec(memory_space=pltpu.SMEM),
                  pl.BlockSpec((3, AT, AT), lambda h: (0, 0, 0)),
                  pl.BlockSpec((T, LANES), lambda h: (0, 0))],
        out_specs=[pl.BlockSpec((1, 4, AT, AT), lambda h: (h, 0, 0, 0)),
                   pl.BlockSpec((1, T, LANES), lambda h: (h, 0, 0))],
        out_shape=[jax.ShapeDtypeStruct((A_HEADS, 4, AT, AT), F32),
                   jax.ShapeDtypeStruct((A_HEADS, T, LANES), F32)],
        compiler_params=_params("arbitrary"),
        name="bias_tables",
    )(rel_bias, jnp.asarray(bt), jnp.asarray(bc))


def _inproj_kernel(x_ref, g_ref, w_ref, o_ref):
    x = x_ref[...]
    ms = jnp.mean(x * x, axis=-1, keepdims=True)
    h = (x * lax.rsqrt(ms + EPS) * g_ref[...]).astype(BF16)
    for j in range(0, D_PROJ, 384):
        o_ref[:, j:j + 384] = _dot(h, w_ref[:, j:j + 384])


def _inproj(x2, g, w):
    N = x2.shape[0]
    tm = 512
    return pl.pallas_call(
        _inproj_kernel,
        grid=(N // tm,),
        in_specs=[pl.BlockSpec((tm, D_MODEL), lambda i: (i, 0)),
                  pl.BlockSpec((1, D_MODEL), lambda i: (0, 0)),
                  pl.BlockSpec((D_MODEL, D_PROJ), lambda i: (0, 0))],
        out_specs=pl.BlockSpec((tm, D_PROJ), lambda i: (i, 0)),
        out_shape=jax.ShapeDtypeStruct((N, D_PROJ), F32),
        compiler_params=_params("parallel"),
        name="inproj",
    )(x2, g, w)


def _rms_lanes(x, g, width):
    ms = jnp.sum(x * x, axis=-1, keepdims=True) * (1.0 / width)
    return x * lax.rsqrt(ms + EPS) * g


def _qk_halves(q_r, k_blk):
    half_rows = q_r.shape[0] // 2
    return [_dot_nt(q_r[h0:h0 + half_rows, :], k_blk) for h0 in range(0, q_r.shape[0], half_rows)]


def _flash_chunk(q_r, k_blk, v_blk, bias_fn, m_r, acc_r):
    _softmax_pv(_qk_halves(q_r, k_blk), v_blk, bias_fn, m_r, acc_r)


def _softmax_pv(s_halves, v_blk, bias_fn, m_r, acc_r):
    half_rows = m_r.shape[0] // 2
    for hi_, h0 in enumerate(range(0, m_r.shape[0], half_rows)):
        s_all = s_halves[hi_]
        ps, alphas = [], []
        for r in range(half_rows // RB):
            rows = slice(h0 + r * RB, h0 + (r + 1) * RB)
            s = s_all[r * RB:(r + 1) * RB]
            if bias_fn is not None:
                s = s + bias_fn((h0 + r * RB) // RB)
            lo, hi = s[:, :LANES], s[:, LANES:]
            mo = m_r[rows, :]
            mn = jnp.maximum(mo, jnp.max(jnp.maximum(lo, hi), axis=-1, keepdims=True))
            ps.append(jnp.exp2(jnp.concatenate([lo - mn, hi - mn], axis=1)).astype(BF16))
            alphas.append(jnp.exp2(mo - mn))
            m_r[rows, :] = mn
        pv = _dot(jnp.concatenate(ps, axis=0), v_blk[hi_] if isinstance(v_blk, (list, tuple)) else v_blk)
        for r in range(half_rows // RB):
            rows = slice(h0 + r * RB, h0 + (r + 1) * RB)
            acc_r[rows, :] = alphas[r] * acc_r[rows, :] + pv[r * RB:(r + 1) * RB]


def _nsa_kernel(q_ref, k0_ref, v0_ref, k1_ref, v1_ref, k2_ref, v2_ref, gate_ref, tiles_ref, biasc_ref,
                ovt_ref, place_ref, w1_ref, w2_ref, pe_ref, qg_ref, kg_ref, o_ref,
                kc_s, vc_s, k1_s, v1_s, k2_s, v2_s, qx_s, s_s, m_s, acc_s, mw_s, accw_s):
    qi = pl.program_id(2)
    kg = kg_ref[...]
    n_cmp_pad = kc_s.shape[0]
    per_g = AT // RB

    @pl.when(qi == 0)
    def _prep():
        T = k1_s.shape[0]
        lane_t = lax.broadcasted_iota(jnp.int32, (T, LANES), 1)
        row_t = lax.broadcasted_iota(jnp.int32, (T, LANES), 0)
        block_onehot = lax.shift_right_logical(row_t, 6) == lane_t - HEAD_DIM
        ones_lane = lane_t == HEAD_DIM
        k1_s[...] = jnp.where(block_onehot, 1.0, _rms_lanes(k1_ref[0], kg, HEAD_DIM)).astype(BF16)
        v1_s[...] = jnp.where(ones_lane, 1.0, v1_ref[0]).astype(BF16)
        k2_s[...] = _rms_lanes(k2_ref[0], kg, HEAD_DIM).astype(BF16)
        v2_s[...] = jnp.where(ones_lane, 1.0, v2_ref[0]).astype(BF16)
        half = CMP_BLOCK // 2
        for s, (src, dst) in enumerate(((k0_ref, kc_s), (v0_ref, vc_s))):
            a = jnp.zeros((n_cmp_pad, HEAD_DIM), F32)
            bm = jnp.zeros((n_cmp_pad, HEAD_DIM), F32)
            for j in range(half):
                xj = src[0, pl.ds(j, n_cmp_pad, stride=CMP_STRIDE), :][:, :HEAD_DIM]
                a = a + _dot_f32(xj + pe_ref[s, j:j + 1, :], w1_ref[s, j])
                bm = bm + _dot_f32(xj + pe_ref[s, half + j:half + j + 1, :], w1_ref[s, half + j])
            pre = a + pltpu.roll(bm, n_cmp_pad - 1, axis=0)
            hid = pre * jax.nn.sigmoid(pre)
            comp = _dot_f32(hid, w2_ref[s])
            if s == 0:
                comp = _rms_lanes(comp, kg, HEAD_DIM)
            dst[...] = comp.astype(BF16)

    qscale = HEAD_DIM ** -0.5 * LOG2E
    q = q_ref[0]
    qg = qg_ref[...]
    qs = jnp.concatenate(
        [_rms_lanes(q[:, g * LANES:(g + 1) * LANES], qg, HEAD_DIM) * qscale for g in range(A_GROUP)],
        axis=0).astype(BF16)

    biasc = biasc_ref[...].reshape(A_GROUP * AT, LANES)
    valid = biasc > 0.5 * NEG
    lcm = _dot_nt(qs, kc_s[...]) + biasc
    mc = jnp.max(lcm, axis=-1, keepdims=True)
    pc = jnp.where(valid, jnp.exp2(lcm - mc), 0.0)
    lc = jnp.sum(pc, axis=-1, keepdims=True)
    pc = pc * jnp.where(lc > 0.0, 1.0 / lc, 0.0)
    o_c = _dot(pc.astype(BF16), vc_s[...])

    ovt = ovt_ref[...]
    p_hi, p_mid, p_lo = _split3(pc[0:AT] + pc[AT:2 * AT] + pc[2 * AT:3 * AT])
    ps3 = _dot_nt(ovt, jnp.concatenate([p_hi, p_mid, p_lo], axis=0))
    ps = ps3[:, :AT] + ps3[:, AT:2 * AT] + ps3[:, 2 * AT:]
    n_sel = ps.shape[0]
    sid = lax.broadcasted_iota(jnp.int32, (n_sel, AT), 0)
    qcol = lax.broadcasted_iota(jnp.int32, (n_sel, AT), 1)
    cur = qi * (AT // SEL_BLOCK) + lax.shift_right_logical(qcol, 6)
    forced = (sid == 0) | (sid == cur) | (sid == cur - 1)
    psf = jnp.where(forced, ps + FORCE, ps)
    psf = jnp.where(sid <= cur, psf, -1.0)
    rank = jnp.zeros((n_sel, AT), F32)
    for sp in range(n_sel):
        other = psf[sp:sp + 1, :]
        beats = (other > psf) | ((other == psf) & (sid > sp))
        rank = rank + jnp.where(beats, 1.0, 0.0)
    keep = (rank < float(min(N_SELECT, n_sel))) & (psf >= 0.0)
    selneg = _dot_tn(jnp.where(keep, 0.0, NEG).astype(BF16), place_ref[...]).astype(BF16)
    lane_q = lax.broadcasted_iota(jnp.int32, (AT, LANES), 1)
    for g in range(A_GROUP):
        qx_s[g * AT:(g + 1) * AT, :] = jnp.where(lane_q < HEAD_DIM, qs[g * AT:(g + 1) * AT], selneg)

    def near_bias(delta, kind):
        kd = jnp.where(qi >= delta, kind, 3)
        return lambda r: tiles_ref[r // per_g, kd, pl.ds((r % per_g) * RB, RB), :]

    def key_rows(kc):
        return pl.ds(pl.multiple_of(kc * AT, AT), AT)

    m_s[...] = jnp.full(m_s.shape, NEG, F32)
    acc_s[...] = jnp.zeros(acc_s.shape, F32)
    mw_s[...] = jnp.full(mw_s.shape, NEG, F32)
    accw_s[...] = jnp.zeros(accw_s.shape, F32)
    for delta, kind in ((0, 0), (1, 1), (2, 2)):
        rows_k = key_rows(jnp.maximum(qi - delta, 0))
        if delta < 2:
            _flash_chunk(qx_s, k1_s[rows_k, :], v1_s[rows_k, :], near_bias(delta, kind), m_s, acc_s)
        _flash_chunk(qx_s, k2_s[rows_k, :], v2_s[rows_k, :], near_bias(delta, kind), mw_s, accw_s)

    for h, s in enumerate(_qk_halves(qx_s, k1_s[key_rows(0), :])):
        s_s[h] = s

    def far_body(kc, carry):
        s_next = _qk_halves(qx_s, k1_s[key_rows(kc + 1), :])
        _softmax_pv([s_s[0], s_s[1]], v1_s[key_rows(kc), :], None, m_s, acc_s)
        for h, s in enumerate(s_next):
            s_s[h] = s
        return carry

    lax.fori_loop(0, jnp.maximum(qi - 1, 0), far_body, 0)

    gt = jax.nn.sigmoid(gate_ref[0])
    for g in range(A_GROUP):
        rows = slice(g * AT, (g + 1) * AT)
        a_s = acc_s[rows, :]
        a_w = accw_s[rows, :]
        o = (gt[:, 3 * g:3 * g + 1] * o_c[rows]
             + gt[:, 3 * g + 1:3 * g + 2] * (a_s / a_s[:, HEAD_DIM:HEAD_DIM + 1])
             + gt[:, 3 * g + 2:3 * g + 3] * (a_w / a_w[:, HEAD_DIM:HEAD_DIM + 1]))
        o_ref[0, :, g * LANES:(g + 1) * LANES] = jnp.where(lane_q < HEAD_DIM, o, 0.0)


def _nsa(proj, tiles, biasc, cmp_pe, cmp_w1, cmp_w2, q_norm, k_norm):
    B, T, _ = proj.shape
    nq = T // AT
    n_cmp_pad = T // CMP_STRIDE
    n_sel = T // SEL_BLOCK
    assert n_sel <= LANES - HEAD_DIM - 1 and n_cmp_pad == LANES
    starts = np.arange(n_cmp_pad) * CMP_STRIDE
    sid = np.arange(n_sel)
    overlap = ((starts[:, None] < (sid[None, :] + 1) * SEL_BLOCK)
               & (starts[:, None] + CMP_BLOCK > sid[None, :] * SEL_BLOCK)
               & (starts[:, None] + CMP_BLOCK <= T))
    place = np.zeros((n_sel, LANES), np.float32)
    place[sid, HEAD_DIM + sid] = 1.0
    w2p = jnp.pad(cmp_w2, ((0, 0), (0, 0), (0, LANES - HEAD_DIM)))

    def slab(br, kv):
        base = SEG_AKV // LANES + (br * 2 + kv) * 2
        return pl.BlockSpec((1, T, LANES), lambda b, h, i: (b, 0, base + h))

    def full(shape):
        return pl.BlockSpec(shape, lambda b, h, i: (0,) * len(shape))

    rows = A_GROUP * AT
    return pl.pallas_call(
        _nsa_kernel,
        grid=(B, A_KV_HEADS, nq),
        in_specs=[pl.BlockSpec((1, AT, A_GROUP * LANES), lambda b, h, i: (b, i, h)),
                  slab(0, 0), slab(0, 1), slab(1, 0), slab(1, 1), slab(2, 0), slab(2, 1),
                  pl.BlockSpec((1, AT, LANES), lambda b, h, i: (b, i, SEG_GATE // LANES + h)),
                  pl.BlockSpec((A_GROUP, 4, AT, AT), lambda b, h, i: (h, 0, 0, 0)),
                  pl.BlockSpec((A_GROUP, AT, LANES), lambda b, h, i: (h, i, 0)),
                  full((n_sel, n_cmp_pad)), full((n_sel, LANES)),
                  full((2, CMP_BLOCK, HEAD_DIM, HEAD_DIM)), full((2, HEAD_DIM, LANES)),
                  full((2, CMP_BLOCK, HEAD_DIM)), full((1, LANES)), full((1, LANES))],
        out_specs=pl.BlockSpec((1, AT, A_GROUP * LANES), lambda b, h, i: (b, i, h)),
        out_shape=jax.ShapeDtypeStruct((B, T, A_HEADS * LANES), F32),
        scratch_shapes=[pltpu.VMEM((n_cmp_pad, LANES), BF16), pltpu.VMEM((n_cmp_pad, LANES), BF16),
                        pltpu.VMEM((T, LANES), BF16), pltpu.VMEM((T, LANES), BF16),
                        pltpu.VMEM((T, LANES), BF16), pltpu.VMEM((T, LANES), BF16),
                        pltpu.VMEM((rows, LANES), BF16), pltpu.VMEM((2, rows // 2, AT), F32),
                        pltpu.VMEM((rows, LANES), F32), pltpu.VMEM((rows, LANES), F32),
                        pltpu.VMEM((rows, LANES), F32), pltpu.VMEM((rows, LANES), F32)],
        compiler_params=_params("parallel", "parallel", "arbitrary"),
        name="nsa",
    )(proj, proj, proj, proj, proj, proj, proj, proj, tiles, biasc,
      jnp.asarray(overlap.T, BF16), jnp.asarray(place, BF16), cmp_w1, w2p, cmp_pe,
      _pad_lanes(q_norm, LANES), _pad_lanes(k_norm, LANES))


MLSTM_NB = 2
MLSTM_TB = 512


def _mlstm4_kernel(qk_ref, v_ref, o_ref, gi_ref, gf_ref, cw_ref, bi_ref, bf_ref, gn_ref, tril_ref, y_ref,
                   xpad_s, c_s, n_s, m_s):
    nb, tb = qk_ref.shape[0], qk_ref.shape[1]
    L = MLSTM_CHUNK
    W = B_HEADS * HEAD_DIM

    @pl.when(pl.program_id(1) == 0)
    def _init():
        xpad_s[:, tb:tb + 8, :] = jnp.zeros((nb, 8, xpad_s.shape[2]), F32)
        c_s[...] = jnp.zeros(c_s.shape, F32)
        n_s[...] = jnp.zeros(n_s.shape, F32)
        m_s[...] = jnp.zeros(m_s.shape, F32)

    for bb in range(nb):
        xpad_s[bb, 0:8, :] = xpad_s[bb, tb:tb + 8, :]
        xpad_s[bb, 8:8 + tb, :] = qk_ref[bb]
    lane_head = lax.shift_right_logical(lax.broadcasted_iota(jnp.int32, (1, W), 1), 6)
    seg = [lane_head == a for a in range(B_HEADS)]
    row_l = lax.broadcasted_iota(jnp.int32, (L, W), 0)
    lane_l = lax.broadcasted_iota(jnp.int32, (L, W), 1) & (HEAD_DIM - 1)
    tri = lane_l <= row_l
    diag = lane_l == row_l
    blockdiag = (lax.shift_right_logical(lax.broadcasted_iota(jnp.int32, (W, W), 0), 6)
                 == lax.shift_right_logical(lax.broadcasted_iota(jnp.int32, (W, W), 1), 6))
    kscale = HEAD_DIM ** -0.5

    def seg_reduce(x, op, fill):
        out = None
        for a in range(B_HEADS):
            r = op(jnp.where(seg[a], x, fill)[:, (a // 2) * LANES:(a // 2 + 1) * LANES], axis=-1, keepdims=True)
            out = r if out is None else jnp.where(seg[a], r, out)
        return out

    def chunk_one(bb, c):
        r0 = pl.multiple_of(c * L, L)
        win = xpad_s[bb, pl.ds(r0, L + 8), :]
        conv = cw_ref[CONV_K - 1:CONV_K, :] * win[8:8 + L]
        for j in range(CONV_K - 1):
            conv = conv + cw_ref[j:j + 1, :] * pltpu.roll(win, CONV_K - 1 - j, axis=0)[8:8 + L]
        qk = conv * jax.nn.sigmoid(conv)
        q4 = qk[:, :W]
        k4 = qk[:, W:] * kscale
        v4 = v_ref[bb, pl.ds(r0, L), :]
        i4 = gi_ref[bb, pl.ds(r0, L), :] + bi_ref[...]
        logf = jax.nn.log_sigmoid(gf_ref[bb, pl.ds(r0, L), :] + bf_ref[...])
        lh, lm, ll = _split3(logf)
        cum = _dot(tril_ref[...], jnp.concatenate([lh, lm, ll], axis=1))
        b4 = cum[:, :W] + cum[:, W:2 * W] + cum[:, 2 * W:]
        c_old, n_old, m_prev = c_s[bb], n_s[bb], m_s[bb][0:1, :]
        r4 = jnp.sum(jnp.where(diag, i4 - b4, 0.0), axis=0, keepdims=True)
        dmat = jnp.where(tri, b4 + r4, -jnp.inf)
        m_inter = b4 + m_prev
        m_j = jnp.maximum(m_inter, seg_reduce(dmat, jnp.max, -jnp.inf))
        isc = jnp.exp(m_inter - m_j)
        k_heads = jnp.concatenate([jnp.where(seg[a], k4, 0.0) for a in range(B_HEADS)], axis=0)
        v_heads = jnp.concatenate([jnp.where(seg[a], v4, 0.0) for a in range(B_HEADS)], axis=0)
        s = _dot_nt(q4, k_heads) * jnp.exp(dmat - m_j)
        den = seg_reduce(s, jnp.sum, 0.0) + isc * seg_reduce(q4 * n_old, jnp.sum, 0.0)
        num = _dot(s, v_heads) + isc * _dot_nt(q4, c_old)
        h = num / jnp.maximum(jnp.abs(den), jnp.exp(-m_j))
        b_last = b4[L - 1:L, :]
        w_log = b_last - b4 + i4
        m_new = jnp.maximum(b_last + m_prev, jnp.max(w_log, axis=0, keepdims=True))
        w4 = jnp.exp(w_log - m_new)
        dec = jnp.exp(b_last + m_prev - m_new)
        c_s[bb] = dec * c_old + jnp.where(blockdiag, _dot_tn(v4 * w4, k4), 0.0)
        n_s[bb] = dec * n_old + jnp.sum(w4 * k4, axis=0, keepdims=True)
        m_s[bb] = jnp.broadcast_to(m_new, m_s.shape[1:])
        ms = seg_reduce(h * h, jnp.sum, 0.0) * (1.0 / HEAD_DIM)
        hn = h * lax.rsqrt(ms + EPS) * gn_ref[...]
        y_ref[bb, pl.ds(r0, L), :] = jax.nn.sigmoid(o_ref[bb, pl.ds(r0, L), :]) * hn

    def chunk(c, carry):
        for bb in range(nb):
            chunk_one(bb, c)
        return carry

    lax.fori_loop(0, tb // L, chunk, 0)


def _mlstm4(proj, conv_w, i_bias, f_bias, out_norm):
    B, T, _ = proj.shape
    W = B_HEADS * HEAD_DIM
    nb, tb = MLSTM_NB, MLSTM_TB
    tril = jnp.asarray(np.tril(np.ones((MLSTM_CHUNK, MLSTM_CHUNK))), BF16)

    def full(shape):
        return pl.BlockSpec(shape, lambda b, t: (0,) * len(shape))

    def seg(width, off):
        return pl.BlockSpec((nb, tb, width), lambda b, t: (b, t, off // width))

    return pl.pallas_call(
        _mlstm4_kernel,
        grid=(B // nb, T // tb),
        in_specs=[seg(2 * W, SEG_BQK), seg(W, SEG_BV), seg(W, SEG_BO), seg(W, SEG_BIF), seg(W, SEG_BIF + W),
                  full((CONV_K, 2 * W)), full((1, W)), full((1, W)), full((1, W)),
                  full((MLSTM_CHUNK, MLSTM_CHUNK))],
        out_specs=pl.BlockSpec((nb, tb, W), lambda b, t: (b, t, 0)),
        out_shape=jax.ShapeDtypeStruct((B, T, W), F32),
        scratch_shapes=[pltpu.VMEM((nb, tb + 8, 2 * W), F32),
                        pltpu.VMEM((nb, W, W), F32),
                        pltpu.VMEM((nb, 1, W), F32),
                        pltpu.VMEM((nb, 8, W), F32)],
        compiler_params=_params("parallel", "arbitrary"),
        name="mlstm",
    )(proj, proj, proj, proj, proj, conv_w, jnp.repeat(i_bias, HEAD_DIM).reshape(1, W),
      jnp.repeat(f_bias, HEAD_DIM).reshape(1, W), out_norm.reshape(1, W), tril)


def _rope_kernel(pos_ref, invf_ref, sgn_ref, cs_ref, sn_ref):
    ang = pos_ref[...].astype(F32) * invf_ref[...]
    cs_ref[...] = jnp.cos(ang)
    sn_ref[...] = jnp.sin(ang) * sgn_ref[...]


def _rope_tables(pos):
    N = pos.shape[0]
    tm = 1024
    half = C_ROPE // 2
    inv = 1.0 / (ROPE_THETA ** (np.arange(half, dtype=np.float32) * 2.0 / C_ROPE))
    invf = np.zeros((1, LANES), np.float32)
    sgn = np.zeros((1, LANES), np.float32)
    invf[0, C_NOPE:C_NOPE + C_ROPE] = np.concatenate([inv, inv])
    sgn[0, C_NOPE:C_NOPE + half] = -1.0
    sgn[0, C_NOPE + half:C_NOPE + C_ROPE] = 1.0
    row = pl.BlockSpec((1, LANES), lambda i: (0, 0))
    tile = pl.BlockSpec((tm, LANES), lambda i: (i, 0))
    return pl.pallas_call(
        _rope_kernel,
        grid=(N // tm,),
        in_specs=[pl.BlockSpec((tm, 1), lambda i: (i, 0)), row, row],
        out_specs=[tile, tile],
        out_shape=[jax.ShapeDtypeStruct((N, LANES), F32), jax.ShapeDtypeStruct((N, LANES), F32)],
        compiler_params=_params("parallel"),
        name="rope_tables",
    )(pos, jnp.asarray(invf), jnp.asarray(sgn))


def _mla_prep_kernel(dq_ref, dkv_ref, kr_ref, cs_ref, sn_ref, wq_ref, wk_ref, wv_ref, qa_ref, kva_ref,
                     gq_ref, gk_ref, gkr_ref, q_out, k_out, v_out):
    lane = lax.broadcasted_iota(jnp.int32, (1, LANES), 1)
    nope = lane < C_NOPE
    ropem = (lane >= C_NOPE) & (lane < C_NOPE + C_ROPE)
    first = lane < C_NOPE + C_ROPE // 2
    cs = cs_ref[...]
    sn = sn_ref[...]

    def rope(x):
        partner = jnp.where(first, pltpu.roll(x, LANES - C_ROPE // 2, axis=1), pltpu.roll(x, C_ROPE // 2, axis=1))
        return x * cs + partner * sn

    scale = (C_NOPE + C_ROPE) ** -0.5 * LOG2E
    qn = _rms_lanes(dq_ref[...], qa_ref[...], Q_RANK).astype(BF16)
    q = _dot(qn, wq_ref[...])
    kvn = _rms_lanes(dkv_ref[...], kva_ref[...], KV_RANK).astype(BF16)
    kk = _dot(kvn, wk_ref[...])
    vv = _dot(kvn, wv_ref[...])
    lane_v = lax.broadcasted_iota(jnp.int32, vv.shape, 1)
    v_out[...] = jnp.where((lane_v & (LANES - 1)) == C_V, 1.0, vv).astype(BF16)
    kr = kr_ref[...]
    krn = kr * lax.rsqrt(jnp.sum(kr * kr, axis=-1, keepdims=True) * (1.0 / C_ROPE) + EPS) * gkr_ref[...]
    krr = rope(krn)
    for h in range(C_HEADS):
        sl = slice(h * LANES, (h + 1) * LANES)
        qh = q[:, sl]
        sq = qh * qh
        msn = jnp.sum(jnp.where(nope, sq, 0.0), axis=-1, keepdims=True) * (1.0 / C_NOPE)
        msr = jnp.sum(jnp.where(ropem, sq, 0.0), axis=-1, keepdims=True) * (1.0 / C_ROPE)
        qhn = qh * jnp.where(nope, lax.rsqrt(msn + EPS), lax.rsqrt(msr + EPS)) * gq_ref[...]
        q_out[:, sl] = (rope(qhn) * scale).astype(BF16)
        kh = kk[:, sl]
        msk = jnp.sum(kh * kh, axis=-1, keepdims=True) * (1.0 / C_NOPE)
        k_out[:, sl] = (kh * lax.rsqrt(msk + EPS) * gk_ref[...] + krr).astype(BF16)


def _mla_prep(proj2, rope_cs, rope_sn, q_a_norm, w_uq, kv_a_norm, w_ukv, qk_norm):
    N = proj2.shape[0]
    tm = 512
    qd = C_NOPE + C_ROPE
    qmap = np.full((C_HEADS * LANES,), -1, np.int64)
    kmap = np.full((C_HEADS * LANES,), -1, np.int64)
    vmap = np.full((C_HEADS * LANES,), -1, np.int64)
    for h in range(C_HEADS):
        qmap[h * LANES:h * LANES + qd] = h * qd + np.arange(qd)
        kmap[h * LANES:h * LANES + C_NOPE] = h * (C_NOPE + C_V) + np.arange(C_NOPE)
        vmap[h * LANES:h * LANES + C_V] = h * (C_NOPE + C_V) + C_NOPE + np.arange(C_V)
    wq = jnp.pad(_take_cols(w_uq, qmap), ((0, 256 - Q_RANK), (0, 0))).astype(BF16)
    wk = _take_cols(w_ukv, kmap).astype(BF16)
    wv = _take_cols(w_ukv, vmap).astype(BF16)
    gkr = jnp.pad(qk_norm[1, C_NOPE:].reshape(1, -1), ((0, 0), (C_NOPE, LANES - C_NOPE - C_ROPE)))

    def full(shape):
        return pl.BlockSpec(shape, lambda i: (0,) * len(shape))

    return pl.pallas_call(
        _mla_prep_kernel,
        grid=(N // tm,),
        in_specs=[pl.BlockSpec((tm, 256), lambda i: (i, SEG_CDQ // 256)),
                  pl.BlockSpec((tm, LANES), lambda i: (i, SEG_CDKV // LANES)),
                  pl.BlockSpec((tm, LANES), lambda i: (i, SEG_CKR // LANES)),
                  pl.BlockSpec((tm, LANES), lambda i: (i, 0)), pl.BlockSpec((tm, LANES), lambda i: (i, 0)),
                  full((256, C_HEADS * LANES)), full((KV_RANK, C_HEADS * LANES)), full((KV_RANK, C_HEADS * LANES)),
                  full((1, 256)), full((1, LANES)), full((1, LANES)), full((1, LANES)), full((1, LANES))],
        out_specs=[pl.BlockSpec((tm, C_HEADS * LANES), lambda i: (i, 0)),
                   pl.BlockSpec((tm, C_HEADS * LANES), lambda i: (i, 0)),
                   pl.BlockSpec((tm, C_HEADS * LANES), lambda i: (i, 0))],
        out_shape=[jax.ShapeDtypeStruct((N, C_HEADS * LANES), BF16),
                   jax.ShapeDtypeStruct((N, C_HEADS * LANES), BF16),
                   jax.ShapeDtypeStruct((N, C_HEADS * LANES), BF16)],
        compiler_params=_params("parallel"),
        name="mla_prep",
    )(proj2, proj2, proj2, rope_cs, rope_sn, wq, wk, wv,
      _pad_lanes(q_a_norm, 256), kv_a_norm.reshape(1, -1), _pad_lanes(qk_norm[0], LANES),
      _pad_lanes(qk_norm[1, :C_NOPE], LANES), gkr)


def _mla_attn_kernel(q_ref, k_ref, v_ref, causal_ref, o_ref, s_s, m_s, acc_s):
    qi = pl.program_id(2)
    m_s[...] = jnp.full(m_s.shape, NEG, F32)
    acc_s[...] = jnp.zeros(acc_s.shape, F32)
    heads = [slice(hh * LANES, (hh + 1) * LANES) for hh in range(2)]
    per_head = AT // RB

    def key_rows(kc):
        return pl.ds(pl.multiple_of(kc * AT, AT), AT)

    def scores(kc):
        return [_dot_nt(q_ref[0, :, sl], k_ref[0, key_rows(kc), sl]) for sl in heads]

    def values(kc):
        return [v_ref[0, key_rows(kc), sl] for sl in heads]

    _softmax_pv(scores(qi), values(qi), lambda r: causal_ref[pl.ds((r % per_head) * RB, RB), :], m_s, acc_s)

    for h, s in enumerate(scores(0)):
        s_s[h] = s

    def body(kc, carry):
        s_next = scores(kc + 1)
        _softmax_pv([s_s[0], s_s[1]], values(kc), None, m_s, acc_s)
        for h, s in enumerate(s_next):
            s_s[h] = s
        return carry

    lax.fori_loop(0, qi, body, 0)
    lane = lax.broadcasted_iota(jnp.int32, (AT, LANES), 1)
    outs = []
    for hh in range(2):
        a = acc_s[hh * AT:(hh + 1) * AT, :]
        outs.append(jnp.where(lane < C_V, a / a[:, C_V:C_V + 1], 0.0))
    o_ref[0] = outs[0] + pltpu.roll(outs[1], C_V, axis=1)


def _mla_attn(q, k, v):
    B, T, _ = q.shape
    causal = np.where(np.arange(AT)[:, None] >= np.arange(AT)[None, :], 0.0, NEG).astype(np.float32)
    pair = pl.BlockSpec((1, T, 2 * LANES), lambda b, p, i: (b, 0, p))
    return pl.pallas_call(
        _mla_attn_kernel,
        grid=(B, C_HEADS // 2, T // AT),
        in_specs=[pl.BlockSpec((1, AT, 2 * LANES), lambda b, p, i: (b, i, p)), pair, pair,
                  pl.BlockSpec((AT, AT), lambda b, p, i: (0, 0))],
        out_specs=pl.BlockSpec((1, AT, LANES), lambda b, p, i: (b, i, p)),
        out_shape=jax.ShapeDtypeStruct((B, T, C_HEADS * C_V), F32),
        scratch_shapes=[pltpu.VMEM((2, AT, AT), F32), pltpu.VMEM((2 * AT, LANES), F32),
                        pltpu.VMEM((2 * AT, LANES), F32)],
        compiler_params=_params("parallel", "parallel", "arbitrary"),
        name="mla_attn",
    )(q, k, v, jnp.asarray(causal))


def _outproj_kernel(x_ref, ya_ref, yb_ref, yc_ref, wa_ref, wb_ref, wc_ref, ga_ref, gc_ref, g2_ref,
                    wr_ref, br_ref, x1_ref, h2_ref, ids_ref, wts_ref, cnt_ref):
    ya = _rms_lanes(ya_ref[...], ga_ref[...], A_HEADS * HEAD_DIM).astype(BF16)
    yc = _rms_lanes(yc_ref[...], gc_ref[...], C_HEADS * C_V).astype(BF16)
    x1 = (x_ref[...] + _dot(ya, wa_ref[...]) + _dot(yb_ref[...].astype(BF16), wb_ref[...])
          + _dot(yc, wc_ref[...]))
    x1_ref[...] = x1
    h2 = _rms_lanes(x1, g2_ref[...], D_MODEL)
    h2_ref[...] = h2
    h2_hi = h2.astype(BF16)
    h2_mid = (h2 - h2_hi.astype(F32)).astype(BF16)
    logits = (_dot(h2_hi, wr_ref[0]) + _dot(h2_mid, wr_ref[0]) + _dot(h2_hi, wr_ref[1])) + br_ref[...]
    tm = logits.shape[0]
    lane = lax.broadcasted_iota(jnp.int32, (tm, LANES), 1)
    is_g = (lane >= N_EXPERTS) & (lane < N_EXPERTS + N_GROUPS)
    gl = jnp.where(is_g, logits, NEG)
    ge = jnp.where(is_g, jnp.exp(gl - jnp.max(gl, axis=-1, keepdims=True)), 0.0)
    gp = jnp.where(is_g, ge / jnp.sum(ge, axis=-1, keepdims=True), -1.0)
    g_top = jnp.max(gp, axis=-1, keepdims=True)
    g_idx = jnp.min(jnp.where(gp == g_top, lane, 2 * LANES), axis=-1, keepdims=True) - N_EXPERTS
    in_grp = (lane < N_EXPERTS) & (lax.shift_right_logical(lane, 3) == g_idx)
    el = jnp.where(in_grp, logits, NEG)
    ee = jnp.where(in_grp, jnp.exp(el - jnp.max(el, axis=-1, keepdims=True)), 0.0)
    ep = jnp.where(in_grp, ee / jnp.sum(ee, axis=-1, keepdims=True), -1.0)
    v1 = jnp.max(ep, axis=-1, keepdims=True)
    i1 = jnp.min(jnp.where(ep == v1, lane, 2 * LANES), axis=-1, keepdims=True)
    ep2 = jnp.where(lane == i1, -1.0, ep)
    v2 = jnp.max(ep2, axis=-1, keepdims=True)
    i2 = jnp.min(jnp.where(ep2 == v2, lane, 2 * LANES), axis=-1, keepdims=True)
    tot = v1 + v2
    ids_ref[...] = jnp.where(lane == 0, i1, jnp.where(lane == 1, i2, 0))
    wts_ref[...] = jnp.where(lane == 0, v1 / tot * g_top, jnp.where(lane == 1, v2 / tot * g_top, 0.0))
    per_expert = jnp.sum(jnp.where((lane == i1) | (lane == i2), 1.0, 0.0), axis=0, keepdims=True)
    cnt_ref[...] = jnp.broadcast_to(per_expert, cnt_ref.shape)


def _outproj(x2, ya, yb, yc, w_out, nsa_out_norm, mla_out_norm, norm2_g, w_group, b_group, w_expert, b_expert):
    N = x2.shape[0]
    tm = 256
    amap = np.full((A_HEADS * LANES,), -1, np.int64)
    for hd in range(A_HEADS):
        amap[hd * LANES:hd * LANES + HEAD_DIM] = hd * HEAD_DIM + np.arange(HEAD_DIM)
    wa = _take_rows(w_out[:384], amap).astype(BF16)
    wb = w_out[384:640].astype(BF16)
    wc = w_out[640:].astype(BF16)
    ga = _take_cols(nsa_out_norm.reshape(1, -1), amap)
    wr = jnp.pad(jnp.concatenate([w_expert, w_group], axis=1), ((0, 0), (0, LANES - N_EXPERTS - N_GROUPS)))
    wr_hi = wr.astype(BF16)
    wr = jnp.stack([wr_hi, (wr - wr_hi.astype(F32)).astype(BF16)])
    br = _pad_lanes(jnp.concatenate([b_expert, b_group]), LANES)

    def full(shape):
        return pl.BlockSpec(shape, lambda i: (0,) * len(shape))

    def rows(w):
        return pl.BlockSpec((tm, w), lambda i: (i, 0))

    return pl.pallas_call(
        _outproj_kernel,
        grid=(N // tm,),
        in_specs=[rows(D_MODEL), rows(A_HEADS * LANES), rows(256), rows(384),
                  full((A_HEADS * LANES, D_MODEL)), full((256, D_MODEL)), full((384, D_MODEL)),
                  full((1, A_HEADS * LANES)), full((1, 384)), full((1, D_MODEL)),
                  full((2, D_MODEL, LANES)), full((1, LANES))],
        out_specs=[rows(D_MODEL), rows(D_MODEL), rows(LANES), rows(LANES),
                   pl.BlockSpec((8, LANES), lambda i: (i, 0))],
        out_shape=[jax.ShapeDtypeStruct((N, D_MODEL), F32), jax.ShapeDtypeStruct((N, D_MODEL), F32),
                   jax.ShapeDtypeStruct((N, LANES), jnp.int32), jax.ShapeDtypeStruct((N, LANES), F32),
                   jax.ShapeDtypeStruct((N // tm * 8, LANES), F32)],
        compiler_params=_params("parallel"),
        name="outproj_router",
    )(x2, ya, yb, yc, wa, wb, wc, ga, mla_out_norm.reshape(1, -1), norm2_g.reshape(1, -1), wr, br)


def _rank_kernel(ids_ref, off_ref, tril_ref, pos_ref, carry_s):
    i = pl.program_id(0)

    @pl.when(i == 0)
    def _():
        carry_s[...] = jnp.broadcast_to(off_ref[...], carry_s.shape)

    ids = ids_ref[...]
    tm = ids.shape[0]
    lane = lax.broadcasted_iota(jnp.int32, (tm, LANES), 1)
    i1 = ids[:, 0:1]
    i2 = ids[:, 1:2]
    oh = jnp.where((lane == i1) | (lane == i2), 1.0, 0.0)
    incl = _dot(tril_ref[...], oh.astype(BF16)) + carry_s[0:1, :]
    excl = incl - oh
    p1 = jnp.sum(jnp.where(lane == i1, excl, 0.0), axis=-1, keepdims=True)
    p2 = jnp.sum(jnp.where(lane == i2, excl, 0.0), axis=-1, keepdims=True)
    pos_ref[...] = jnp.where(lane == 0, p1, jnp.where(lane == 1, p2, 0.0)).astype(jnp.int32)
    carry_s[...] = jnp.broadcast_to(incl[tm - 1:tm, :], carry_s.shape)


def _moe_positions(ids, off_row):
    N = ids.shape[0]
    tm = 512
    tril = jnp.asarray(np.tril(np.ones((tm, tm))), BF16)
    return pl.pallas_call(
        _rank_kernel,
        grid=(N // tm,),
        in_specs=[pl.BlockSpec((tm, LANES), lambda i: (i, 0)), pl.BlockSpec((1, LANES), lambda i: (0, 0)),
                  pl.BlockSpec((tm, tm), lambda i: (0, 0))],
        out_specs=pl.BlockSpec((tm, LANES), lambda i: (i, 0)),
        out_shape=jax.ShapeDtypeStruct((N, LANES), jnp.int32),
        scratch_shapes=[pltpu.VMEM((8, LANES), F32)],
        compiler_params=_params("arbitrary"),
        name="moe_positions",
    )(ids, off_row, tril)


DISPATCH_TM = 1024


DMA_UNROLL = 8


def _dispatch_kernel(off_ref, tp_ref, nu_ref, p1_ref, p2_ref, h_ref, xs_ref, zero_s, sem, zsem):
    n_tiles = xs_ref.shape[0] // ROW_TILE

    @pl.when(pl.program_id(0) == 0)
    def _zero_fill():
        zero_s[...] = jnp.zeros(zero_s.shape, F32)

        def tile_copy(row0):
            return pltpu.make_async_copy(
                zero_s, xs_ref.at[pl.ds(pl.multiple_of(row0, ROW_TILE), ROW_TILE), :], zsem)

        for e in range(N_EXPERTS):
            @pl.when(tp_ref[e] > 0)
            def _start(e=e):
                tile_copy(off_ref[e] + (tp_ref[e] - 1) * ROW_TILE).start()

        def tail_start(g, carry):
            tile_copy(g * ROW_TILE).start()
            return carry

        lax.fori_loop(nu_ref[0], n_tiles, tail_start, 0)
        for e in range(N_EXPERTS):
            @pl.when(tp_ref[e] > 0)
            def _wait():
                tile_copy(0).wait()

        def tail_wait(g, carry):
            tile_copy(0).wait()
            return carry

        lax.fori_loop(nu_ref[0], n_tiles, tail_wait, 0)

    def row_copy(i, pos):
        return pltpu.make_async_copy(h_ref.at[pl.ds(i, 1), :], xs_ref.at[pl.ds(pos, 1), :], sem)

    def issue(j, carry):
        for u in range(DMA_UNROLL):
            i = j * DMA_UNROLL + u
            row_copy(i, p1_ref[i]).start(priority=0)
            row_copy(i, p2_ref[i]).start(priority=1)
        return carry

    lax.fori_loop(0, DISPATCH_TM // DMA_UNROLL, issue, 0)
    whole = pltpu.make_async_copy(h_ref, xs_ref.at[pl.ds(0, DISPATCH_TM), :], sem)
    whole.wait()
    whole.wait()


def _moe_dispatch(h2, off, tiles_per, n_used, pos1, pos2, n_rows):
    N = h2.shape[0]
    smem1d = pl.BlockSpec((DISPATCH_TM,), lambda i: (i,), memory_space=pltpu.SMEM)
    smem = pl.BlockSpec(memory_space=pltpu.SMEM)
    return pl.pallas_call(
        _dispatch_kernel,
        grid=(N // DISPATCH_TM,),
        in_specs=[smem, smem, smem, smem1d, smem1d,
                  pl.BlockSpec((DISPATCH_TM, D_MODEL), lambda i: (i, 0))],
        out_specs=pl.BlockSpec(memory_space=pl.ANY),
        out_shape=jax.ShapeDtypeStruct((n_rows, D_MODEL), F32),
        scratch_shapes=[pltpu.VMEM((ROW_TILE, D_MODEL), F32), pltpu.SemaphoreType.DMA(()),
                        pltpu.SemaphoreType.DMA(())],
        compiler_params=_params("arbitrary"),
        name="moe_dispatch",
    )(off, tiles_per, n_used, pos1, pos2, h2)


def _expert_kernel(te_ref, nu_ref, x_ref, wg_ref, wu_ref, wd_ref, y_ref):
    g = pl.program_id(0)

    @pl.when(g < nu_ref[0])
    def _():
        x = x_ref[...].astype(BF16)
        a = _dot(x, wg_ref[0, 0].astype(BF16))
        u = _dot(x, wu_ref[0, 0].astype(BF16))
        h = (a * jax.nn.sigmoid(a) * u).astype(BF16)
        y_ref[...] = _dot(h, wd_ref[0, 0].astype(BF16))

    @pl.when(g >= nu_ref[0])
    def _():
        y_ref[...] = jnp.zeros(y_ref.shape, F32)


def _moe_experts(xs, tile_expert, n_used, layer, w_gate, w_up, w_down):
    R = xs.shape[0]
    G = R // ROW_TILE

    def row_map(g, te, nu):
        return (jnp.minimum(g, nu[0] - 1), 0)

    def w_map(g, te, nu):
        return (layer, te[jnp.minimum(g, nu[0] - 1)], 0, 0)

    return pl.pallas_call(
        _expert_kernel,
        grid_spec=pltpu.PrefetchScalarGridSpec(
            num_scalar_prefetch=2,
            grid=(G,),
            in_specs=[pl.BlockSpec((ROW_TILE, D_MODEL), row_map),
                      pl.BlockSpec((1, 1, D_MODEL, D_EXPERT), w_map),
                      pl.BlockSpec((1, 1, D_MODEL, D_EXPERT), w_map),
                      pl.BlockSpec((1, 1, D_EXPERT, D_MODEL), w_map)],
            out_specs=pl.BlockSpec((ROW_TILE, D_MODEL), lambda g, te, nu: (g, 0))),
        out_shape=jax.ShapeDtypeStruct((R, D_MODEL), F32),
        compiler_params=_params("arbitrary"),
        name="moe_experts",
    )(tile_expert, n_used, xs, w_gate, w_up, w_down)


COMBINE_TM = 256


def _combine_kernel(p1_ref, p2_ref, x1_ref, wts_ref, ys_ref, o_ref, b1_s, b2_s, sem):
    def row_copy(buf, i, pos):
        return pltpu.make_async_copy(ys_ref.at[pl.ds(pos, 1), :], buf.at[pl.ds(i, 1), :], sem)

    def issue(j, carry):
        for u in range(DMA_UNROLL):
            i = j * DMA_UNROLL + u
            row_copy(b1_s, i, p1_ref[i]).start(priority=0)
            row_copy(b2_s, i, p2_ref[i]).start(priority=1)
        return carry

    lax.fori_loop(0, COMBINE_TM // DMA_UNROLL, issue, 0)
    pltpu.make_async_copy(ys_ref.at[pl.ds(0, COMBINE_TM), :], b1_s, sem).wait()
    pltpu.make_async_copy(ys_ref.at[pl.ds(0, COMBINE_TM), :], b2_s, sem).wait()
    w = wts_ref[...]
    o_ref[...] = x1_ref[...] + w[:, 0:1] * b1_s[...] + w[:, 1:2] * b2_s[...]


def _moe_combine(x1, wts, ys, pos1, pos2):
    N = x1.shape[0]
    smem1d = pl.BlockSpec((COMBINE_TM,), lambda i: (i,), memory_space=pltpu.SMEM)
    return pl.pallas_call(
        _combine_kernel,
        grid=(N // COMBINE_TM,),
        in_specs=[smem1d, smem1d,
                  pl.BlockSpec((COMBINE_TM, D_MODEL), lambda i: (i, 0)),
                  pl.BlockSpec((COMBINE_TM, LANES), lambda i: (i, 0)),
                  pl.BlockSpec(memory_space=pl.ANY)],
        out_specs=pl.BlockSpec((COMBINE_TM, D_MODEL), lambda i: (i, 0)),
        out_shape=jax.ShapeDtypeStruct((N, D_MODEL), F32),
        scratch_shapes=[pltpu.VMEM((COMBINE_TM, D_MODEL), F32), pltpu.VMEM((COMBINE_TM, D_MODEL), F32),
                        pltpu.SemaphoreType.DMA(())],
        compiler_params=_params("arbitrary"),
        name="moe_combine",
    )(pos1, pos2, x1, wts, ys)


def _moe(x1, h2, ids, wts, tile_counts, layer, w_gate, w_up, w_down):
    N = x1.shape[0]
    n_rows = 2 * N + N_EXPERTS * ROW_TILE
    counts = jnp.sum(tile_counts[::8, :N_EXPERTS], axis=0).astype(jnp.int32)
    tiles_per = (counts + ROW_TILE - 1) // ROW_TILE
    tile_end = jnp.cumsum(tiles_per)
    off = ((tile_end - tiles_per) * ROW_TILE).astype(jnp.int32)
    n_used = tile_end[-1:].astype(jnp.int32)
    tile_ids = jnp.arange(n_rows // ROW_TILE, dtype=jnp.int32)
    tile_expert = jnp.minimum(jnp.sum(tile_end[None, :] <= tile_ids[:, None], axis=1), N_EXPERTS - 1).astype(jnp.int32)
    pos = _moe_positions(ids, _pad_lanes(off.astype(F32), LANES))
    pos1, pos2 = pos[:, 0], pos[:, 1]
    xs = _moe_dispatch(h2, off, tiles_per.astype(jnp.int32), n_used, pos1, pos2, n_rows)
    ys = _moe_experts(xs, tile_expert, n_used, layer, w_gate, w_up, w_down)
    return _moe_combine(x1, wts, ys, pos1, pos2)


def kernel(x, positions, rel_bias, norm1_g, w_in, nsa_cmp_pe, nsa_cmp_w1, nsa_cmp_w2, nsa_q_norm, nsa_k_norm, nsa_out_norm, mlstm_conv, mlstm_i_bias, mlstm_f_bias, mlstm_out_norm, mla_q_a_norm, mla_w_uq, mla_kv_a_norm, mla_w_ukv, mla_qk_norm, mla_out_norm, w_out, norm2_g, moe_w_group, moe_b_group, moe_w_expert, moe_b_expert, moe_w_gate, moe_w_up, moe_w_down):
    B, T, D = x.shape
    N = B * T
    depth = w_in.shape[0]
    colmap = _proj_colmap()
    tiles, biasc = _bias_tables(rel_bias, T)
    rope_cs, rope_sn = _rope_tables(positions.reshape(N, 1))
    x2 = x.reshape(N, D)
    for l in range(depth):
        w_proj = _take_cols(w_in[l], colmap).astype(BF16)
        proj2 = _inproj(x2, norm1_g[l].reshape(1, D), w_proj)
        proj = proj2.reshape(B, T, D_PROJ)
        ya = _nsa(proj, tiles, biasc, nsa_cmp_pe[l], nsa_cmp_w1[l], nsa_cmp_w2[l], nsa_q_norm[l], nsa_k_norm[l])
        yb = _mlstm4(proj, mlstm_conv[l], mlstm_i_bias[l], mlstm_f_bias[l], mlstm_out_norm[l])
        q, k, v = _mla_prep(proj2, rope_cs, rope_sn, mla_q_a_norm[l], mla_w_uq[l], mla_kv_a_norm[l], mla_w_ukv[l], mla_qk_norm[l])
        yc = _mla_attn(q.reshape(B, T, -1), k.reshape(B, T, -1), v.reshape(B, T, -1))
        x1, h2, ids, wts, tile_counts = _outproj(x2, ya.reshape(N, -1), yb.reshape(N, -1), yc.reshape(N, -1), w_out[l],
                                    nsa_out_norm[l], mla_out_norm[l], norm2_g[l], moe_w_group[l], moe_b_group[l],
                                    moe_w_expert[l], moe_b_expert[l])
        x2 = _moe(x1, h2, ids, wts, tile_counts, l, moe_w_gate, moe_w_up, moe_w_down)
    return x2.reshape(B, T, D)
```

```python
import functools
import math

import numpy as np
import jax
import jax.numpy as jnp
from jax import lax
from jax.experimental import pallas as pl
from jax.experimental.pallas import tpu as pltpu

F32 = jnp.float32
BF16 = jnp.bfloat16

D_MODEL = 1024
HEAD_DIM = 64
LANES = 128
A_HEADS, A_KV_HEADS, A_GROUP = 6, 2, 3
CMP_BLOCK, CMP_STRIDE, SEL_BLOCK, N_SELECT, WINDOW = 32, 16, 64, 16, 512
FORCE = 1e4
B_HEADS, CONV_K, MLSTM_CHUNK = 4, 4, 64
C_HEADS, C_NOPE, C_ROPE, C_V = 6, 64, 32, 64
Q_RANK, KV_RANK = 192, 128
ROPE_THETA = 10000.0
REL_BUCKETS, REL_MAX_DIST = 32, 128
N_GROUPS, EXPERTS_PER_GROUP, N_EXPERTS, D_EXPERT = 4, 8, 32, 256
EPS = 1e-6
NEG = -1e30

LOG2E = 1.4426950408889634

AT = 256
RB = 128
ROW_TILE = 256
VMEM_LIMIT = 48 * 1024 * 1024

SEG_AQ, SEG_AKV, SEG_GATE, SEG_BQK, SEG_BV, SEG_BO = 0, 768, 2304, 2560, 3072, 3328
SEG_CDQ, SEG_CDKV, SEG_CKR, SEG_BIF, D_PROJ = 3584, 3840, 3968, 4096, 4608


def _params(*sem):
    return pltpu.CompilerParams(dimension_semantics=sem, vmem_limit_bytes=VMEM_LIMIT)


def _dot(a, b):
    return jnp.dot(a, b, preferred_element_type=F32)


def _dot_nt(a, b):
    return lax.dot_general(a, b, (((1,), (1,)), ((), ())), preferred_element_type=F32)


def _dot_tn(a, b):
    return lax.dot_general(a, b, (((0,), (0,)), ((), ())), preferred_element_type=F32)


def _split3(x):
    hi = x.astype(BF16)
    r1 = x - hi.astype(F32)
    mid = r1.astype(BF16)
    lo = (r1 - mid.astype(F32)).astype(BF16)
    return hi, mid, lo


def _dot_exact_rhs(x, m_bf16):
    hi, mid, lo = _split3(x)
    return _dot(hi, m_bf16) + _dot(mid, m_bf16) + _dot(lo, m_bf16)


def _dot_f32(x, w):
    xh = x.astype(BF16)
    xm = (x - xh.astype(F32)).astype(BF16)
    wh = w.astype(BF16)
    wm = (w - wh.astype(F32)).astype(BF16)
    return _dot(xh, wh) + _dot(xm, wh) + _dot(xh, wm)


def _bucket_np(dist):
    n = np.maximum(dist, 0)
    exact = REL_BUCKETS // 2
    nf = np.maximum(n, 1).astype(np.float64)
    large = exact + (np.log(nf / exact) / math.log(REL_MAX_DIST / exact) * (REL_BUCKETS - exact)).astype(np.int64)
    large = np.minimum(large, REL_BUCKETS - 1)
    return np.where(n < exact, n, large).astype(np.int32)


def _bucket_tables(T):
    r = np.arange(AT)[:, None]
    c = np.arange(AT)[None, :]
    diag = np.where(r >= c, _bucket_np(r - c), -1)
    prev = _bucket_np(AT + r - c)
    far = np.where(c > r, _bucket_np(2 * AT + r - c), -1)
    tiles = np.stack([diag, prev, far]).astype(np.int32)
    tq = np.arange(T)[:, None]
    n = np.arange(LANES)[None, :]
    dist_c = tq - (n * CMP_STRIDE + CMP_BLOCK - 1)
    cmp_tbl = np.where(dist_c >= 0, _bucket_np(dist_c), -1).astype(np.int32)
    return tiles, cmp_tbl


def _proj_colmap():
    cm = np.full((D_PROJ,), -1, np.int64)
    for hd in range(A_HEADS):
        cm[SEG_AQ + hd * LANES: SEG_AQ + hd * LANES + HEAD_DIM] = hd * HEAD_DIM + np.arange(HEAD_DIM)
    for s in range(12):
        cm[SEG_AKV + s * LANES: SEG_AKV + s * LANES + HEAD_DIM] = 384 + s * HEAD_DIM + np.arange(HEAD_DIM)
    for h in range(A_KV_HEADS):
        cm[SEG_GATE + h * LANES: SEG_GATE + h * LANES + 9] = 1152 + h * 9 + np.arange(9)
    cm[SEG_BQK: SEG_BQK + 512] = 1170 + np.arange(512)
    cm[SEG_BV: SEG_BV + 256] = 1682 + np.arange(256)
    cm[SEG_BIF: SEG_BIF + 2 * B_HEADS * HEAD_DIM] = 1938 + np.arange(2 * B_HEADS * HEAD_DIM) // HEAD_DIM
    cm[SEG_BO: SEG_BO + 256] = 1946 + np.arange(256)
    cm[SEG_CDQ: SEG_CDQ + Q_RANK] = 2202 + np.arange(Q_RANK)
    cm[SEG_CDKV: SEG_CDKV + KV_RANK] = 2394 + np.arange(KV_RANK)
    cm[SEG_CKR + C_NOPE: SEG_CKR + C_NOPE + C_ROPE] = 2522 + np.arange(C_ROPE)
    return cm


def _take_cols(w, colmap):
    wz = jnp.concatenate([w, jnp.zeros((w.shape[0], 1), w.dtype)], axis=1)
    return wz[:, np.where(colmap < 0, w.shape[1], colmap)]


def _take_rows(w, rowmap):
    return _take_cols(w.T, rowmap).T


def _pad_lanes(v, n):
    v = v.reshape(1, -1)
    return jnp.pad(v, ((0, 0), (0, n - v.shape[1])))


def _bias_kernel(rb_ref, bt_ref, bc_ref, tiles_ref, cmp_ref):
    hd = pl.program_id(0)
    bt = bt_ref[...]
    bc = bc_ref[...]
    t = jnp.full(bt.shape, NEG, F32)
    c = jnp.full(bc.shape, NEG, F32)
    far = rb_ref[REL_BUCKETS - 1, hd]
    for b in range(REL_BUCKETS):
        val = rb_ref[b, hd]
        t = jnp.where(bt == b, (val - far) * LOG2E, t)
        c = jnp.where(bc == b, val * LOG2E, c)
    tiles_ref[0, 0:3] = t
    tiles_ref[0, 3] = jnp.full((AT, AT), NEG, F32)
    cmp_ref[0] = c


def _bias_tables(rel_bias, T):
    bt, bc = _bucket_tables(T)
    return pl.pallas_call(
        _bias_kernel,
        grid=(A_HEADS,),
        in_specs=[pl.BlockSpec(memory_space=pltpu.SMEM),
                  pl.BlockSpec((3, AT, AT), lambda h: (0, 0, 0)),
                  pl.BlockSpec((T, LANES), lambda h: (0, 0))],
        out_specs=[pl.BlockSpec((1, 4, AT, AT), lambda h: (h, 0, 0, 0)),
                   pl.BlockSpec((1, T, LANES), lambda h: (h, 0, 0))],
        out_shape=[jax.ShapeDtypeStruct((A_HEADS, 4, AT, AT), F32),
                   jax.ShapeDtypeStruct((A_HEADS, T, LANES), F32)],
        compiler_params=_params("arbitrary"),
        name="bias_tables",
    )(rel_bias, jnp.asarray(bt), jnp.asarray(bc))


def _inproj_kernel(x_ref, g_ref, w_ref, o_ref):
    x = x_ref[...]
    ms = jnp.mean(x * x, axis=-1, keepdims=True)
    h = (x * lax.rsqrt(ms + EPS) * g_ref[...]).astype(BF16)
    for j in range(0, D_PROJ, 384):
        o_ref[:, j:j + 384] = _dot(h, w_ref[:, j:j + 384])


def _inproj(x2, g, w):
    N = x2.shape[0]
    tm = 512
    return pl.pallas_call(
        _inproj_kernel,
        grid=(N // tm,),
        in_specs=[pl.BlockSpec((tm, D_MODEL), lambda i: (i, 0)),
                  pl.BlockSpec((1, D_MODEL), lambda i: (0, 0)),
                  pl.BlockSpec((D_MODEL, D_PROJ), lambda i: (0, 0))],
        out_specs=pl.BlockSpec((tm, D_PROJ), lambda i: (i, 0)),
        out_shape=jax.ShapeDtypeStruct((N, D_PROJ), F32),
        compiler_params=_params("parallel"),
        name="inproj",
    )(x2, g, w)


def _rms_lanes(x, g, width):
    ms = jnp.sum(x * x, axis=-1, keepdims=True) * (1.0 / width)
    return x * lax.rsqrt(ms + EPS) * g


def _qk_halves(q_r, k_blk):
    half_rows = q_r.shape[0] // 2
    return [_dot_nt(q_r[h0:h0 + half_rows, :], k_blk) for h0 in range(0, q_r.shape[0], half_rows)]


def _flash_chunk(q_r, k_blk, v_blk, bias_fn, m_r, acc_r):
    _softmax_pv(_qk_halves(q_r, k_blk), v_blk, bias_fn, m_r, acc_r)


def _softmax_pv(s_halves, v_blk, bias_fn, m_r, acc_r):
    half_rows = m_r.shape[0] // 2
    for hi_, h0 in enumerate(range(0, m_r.shape[0], half_rows)):
        s_all = s_halves[hi_]
        ps, alphas = [], []
        for r in range(half_rows // RB):
            rows = slice(h0 + r * RB, h0 + (r + 1) * RB)
            s = s_all[r * RB:(r + 1) * RB]
            if bias_fn is not None:
                s = s + bias_fn((h0 + r * RB) // RB)
            lo, hi = s[:, :LANES], s[:, LANES:]
            mo = m_r[rows, :]
            mn = jnp.maximum(mo, jnp.max(jnp.maximum(lo, hi), axis=-1, keepdims=True))
            ps.append(jnp.exp2(jnp.concatenate([lo - mn, hi - mn], axis=1)).astype(BF16))
            alphas.append(jnp.exp2(mo - mn))
            m_r[rows, :] = mn
        pv = _dot(jnp.concatenate(ps, axis=0), v_blk[hi_] if isinstance(v_blk, (list, tuple)) else v_blk)
        for r in range(half_rows // RB):
            rows = slice(h0 + r * RB, h0 + (r + 1) * RB)
            acc_r[rows, :] = alphas[r] * acc_r[rows, :] + pv[r * RB:(r + 1) * RB]


def _nsa_kernel(q_ref, k0_ref, v0_ref, k1_ref, v1_ref, k2_ref, v2_ref, gate_ref, tiles_ref, biasc_ref,
                ovt_ref, place_ref, w1_ref, w2_ref, pe_ref, qg_ref, kg_ref, o_ref,
                kc_s, vc_s, k1_s, v1_s, k2_s, v2_s, qx_s, qw_s, s_s, m_s, acc_s, mw_s, accw_s):
    qi = pl.program_id(2)
    kg = kg_ref[...]
    n_cmp_pad = kc_s.shape[0]
    per_g = AT // RB

    @pl.when(qi == 0)
    def _prep():
        T = k1_s.shape[0]
        lane_t = lax.broadcasted_iota(jnp.int32, (T, LANES), 1)
        row_t = lax.broadcasted_iota(jnp.int32, (T, LANES), 0)
        block_onehot = lax.shift_right_logical(row_t, 6) == lane_t - HEAD_DIM
        ones_lane = lane_t == HEAD_DIM
        k1_s[...] = jnp.where(block_onehot, 1.0, _rms_lanes(k1_ref[0], kg, HEAD_DIM)).astype(BF16)
        v1_s[...] = jnp.where(ones_lane, 1.0, v1_ref[0]).astype(BF16)
        k2_s[...] = _rms_lanes(k2_ref[0], kg, HEAD_DIM).astype(BF16)
        v2_s[...] = jnp.where(ones_lane, 1.0, v2_ref[0]).astype(BF16)
        half = CMP_BLOCK // 2
        for s, (src, dst) in enumerate(((k0_ref, kc_s), (v0_ref, vc_s))):
            a = jnp.zeros((n_cmp_pad, HEAD_DIM), F32)
            bm = jnp.zeros((n_cmp_pad, HEAD_DIM), F32)
            for j in range(half):
                xj = src[0, pl.ds(j, n_cmp_pad, stride=CMP_STRIDE), :][:, :HEAD_DIM]
                a = a + _dot_f32(xj + pe_ref[s, j:j + 1, :], w1_ref[s, j])
                bm = bm + _dot_f32(xj + pe_ref[s, half + j:half + j + 1, :], w1_ref[s, half + j])
            pre = a + pltpu.roll(bm, n_cmp_pad - 1, axis=0)
            hid = pre * jax.nn.sigmoid(pre)
            comp = _dot_f32(hid, w2_ref[s])
            if s == 0:
                comp = _rms_lanes(comp, kg, HEAD_DIM)
            dst[...] = comp.astype(BF16)

    qscale = HEAD_DIM ** -0.5 * LOG2E
    q = q_ref[0]
    qg = qg_ref[...]
    qs = jnp.concatenate(
        [_rms_lanes(q[:, g * LANES:(g + 1) * LANES], qg, HEAD_DIM) * qscale for g in range(A_GROUP)],
        axis=0).astype(BF16)
    qw_s[...] = qs

    def near_bias(delta, kind):
        kd = jnp.where(qi >= delta, kind, 3)
        return lambda r: tiles_ref[r // per_g, kd, pl.ds((r % per_g) * RB, RB), :]

    def key_rows(kc):
        return pl.ds(pl.multiple_of(kc * AT, AT), AT)

    mw_s[...] = jnp.full(mw_s.shape, NEG, F32)
    accw_s[...] = jnp.zeros(accw_s.shape, F32)
    for delta, kind in ((0, 0), (1, 1), (2, 2)):
        rows_k = key_rows(jnp.maximum(qi - delta, 0))
        _flash_chunk(qw_s, k2_s[rows_k, :], v2_s[rows_k, :], near_bias(delta, kind), mw_s, accw_s)

    biasc = biasc_ref[...].reshape(A_GROUP * AT, LANES)
    valid = biasc > 0.5 * NEG
    lcm = _dot_nt(qs, kc_s[...]) + biasc
    mc = jnp.max(lcm, axis=-1, keepdims=True)
    pc = jnp.where(valid, jnp.exp2(lcm - mc), 0.0)
    lc = jnp.sum(pc, axis=-1, keepdims=True)
    pc = pc * jnp.where(lc > 0.0, 1.0 / lc, 0.0)
    o_c = _dot(pc.astype(BF16), vc_s[...])

    ovt = ovt_ref[...]
    p_hi, p_mid, p_lo = _split3(pc[0:AT] + pc[AT:2 * AT] + pc[2 * AT:3 * AT])
    ps3 = _dot_nt(ovt, jnp.concatenate([p_hi, p_mid, p_lo], axis=0))
    ps = ps3[:, :AT] + ps3[:, AT:2 * AT] + ps3[:, 2 * AT:]
    n_sel = ps.shape[0]
    sid = lax.broadcasted_iota(jnp.int32, (n_sel, AT), 0)
    qcol = lax.broadcasted_iota(jnp.int32, (n_sel, AT), 1)
    cur = qi * (AT // SEL_BLOCK) + lax.shift_right_logical(qcol, 6)
    forced = (sid == 0) | (sid == cur) | (sid == cur - 1)
    psf = jnp.where(forced, ps + FORCE, ps)
    psf = jnp.where(sid <= cur, psf, -1.0)
    rank = jnp.zeros((n_sel, AT), F32)
    for sp in range(n_sel):
        other = psf[sp:sp + 1, :]
        beats = (other > psf) | ((other == psf) & (sid > sp))
        rank = rank + jnp.where(beats, 1.0, 0.0)
    keep = (rank < float(min(N_SELECT, n_sel))) & (psf >= 0.0)
    selneg = _dot_tn(jnp.where(keep, 0.0, NEG).astype(BF16), place_ref[...]).astype(BF16)
    lane_q = lax.broadcasted_iota(jnp.int32, (AT, LANES), 1)
    for g in range(A_GROUP):
        qx_s[g * AT:(g + 1) * AT, :] = jnp.where(lane_q < HEAD_DIM, qs[g * AT:(g + 1) * AT], selneg)

    m_s[...] = jnp.full(m_s.shape, NEG, F32)
    acc_s[...] = jnp.zeros(acc_s.shape, F32)
    for delta, kind in ((0, 0), (1, 1)):
        rows_k = key_rows(jnp.maximum(qi - delta, 0))
        _flash_chunk(qx_s, k1_s[rows_k, :], v1_s[rows_k, :], near_bias(delta, kind), m_s, acc_s)

    for h, s in enumerate(_qk_halves(qx_s, k1_s[key_rows(0), :])):
        s_s[h] = s

    def far_body(kc, carry):
        s_next = _qk_halves(qx_s, k1_s[key_rows(kc + 1), :])
        _softmax_pv([s_s[0], s_s[1]], v1_s[key_rows(kc), :], None, m_s, acc_s)
        for h, s in enumerate(s_next):
            s_s[h] = s
        return carry

    lax.fori_loop(0, jnp.maximum(qi - 1, 0), far_body, 0)

    gt = jax.nn.sigmoid(gate_ref[0])
    for g in range(A_GROUP):
        rows = slice(g * AT, (g + 1) * AT)
        a_s = acc_s[rows, :]
        a_w = accw_s[rows, :]
        o = (gt[:, 3 * g:3 * g + 1] * o_c[rows]
             + gt[:, 3 * g + 1:3 * g + 2] * (a_s / a_s[:, HEAD_DIM:HEAD_DIM + 1])
             + gt[:, 3 * g + 2:3 * g + 3] * (a_w / a_w[:, HEAD_DIM:HEAD_DIM + 1]))
        o_ref[0, :, g * LANES:(g + 1) * LANES] = jnp.where(lane_q < HEAD_DIM, o, 0.0)


def _nsa(proj, tiles, biasc, cmp_pe, cmp_w1, cmp_w2, q_norm, k_norm):
    B, T, _ = proj.shape
    nq = T // AT
    n_cmp_pad = T // CMP_STRIDE
    n_sel = T // SEL_BLOCK
    assert n_sel <= LANES - HEAD_DIM - 1 and n_cmp_pad == LANES
    starts = np.arange(n_cmp_pad) * CMP_STRIDE
    sid = np.arange(n_sel)
    overlap = ((starts[:, None] < (sid[None, :] + 1) * SEL_BLOCK)
               & (starts[:, None] + CMP_BLOCK > sid[None, :] * SEL_BLOCK)
               & (starts[:, None] + CMP_BLOCK <= T))
    place = np.zeros((n_sel, LANES), np.float32)
    place[sid, HEAD_DIM + sid] = 1.0
    w2p = jnp.pad(cmp_w2, ((0, 0), (0, 0), (0, LANES - HEAD_DIM)))

    def slab(br, kv):
        base = SEG_AKV // LANES + (br * 2 + kv) * 2
        return pl.BlockSpec((1, T, LANES), lambda b, h, i: (b, 0, base + h))

    def full(shape):
        return pl.BlockSpec(shape, lambda b, h, i: (0,) * len(shape))

    rows = A_GROUP * AT
    return pl.pallas_call(
        _nsa_kernel,
        grid=(B, A_KV_HEADS, nq),
        in_specs=[pl.BlockSpec((1, AT, A_GROUP * LANES), lambda b, h, i: (b, i, h)),
                  slab(0, 0), slab(0, 1), slab(1, 0), slab(1, 1), slab(2, 0), slab(2, 1),
                  pl.BlockSpec((1, AT, LANES), lambda b, h, i: (b, i, SEG_GATE // LANES + h)),
                  pl.BlockSpec((A_GROUP, 4, AT, AT), lambda b, h, i: (h, 0, 0, 0)),
                  pl.BlockSpec((A_GROUP, AT, LANES), lambda b, h, i: (h, i, 0)),
                  full((n_sel, n_cmp_pad)), full((n_sel, LANES)),
                  full((2, CMP_BLOCK, HEAD_DIM, HEAD_DIM)), full((2, HEAD_DIM, LANES)),
                  full((2, CMP_BLOCK, HEAD_DIM)), full((1, LANES)), full((1, LANES))],
        out_specs=pl.BlockSpec((1, AT, A_GROUP * LANES), lambda b, h, i: (b, i, h)),
        out_shape=jax.ShapeDtypeStruct((B, T, A_HEADS * LANES), F32),
        scratch_shapes=[pltpu.VMEM((n_cmp_pad, LANES), BF16), pltpu.VMEM((n_cmp_pad, LANES), BF16),
                        pltpu.VMEM((T, LANES), BF16), pltpu.VMEM((T, LANES), BF16),
                        pltpu.VMEM((T, LANES), BF16), pltpu.VMEM((T, LANES), BF16),
                        pltpu.VMEM((rows, LANES), BF16), pltpu.VMEM((rows, LANES), BF16),
                        pltpu.VMEM((2, rows // 2, AT), F32),
                        pltpu.VMEM((rows, LANES), F32), pltpu.VMEM((rows, LANES), F32),
                        pltpu.VMEM((rows, LANES), F32), pltpu.VMEM((rows, LANES), F32)],
        compiler_params=_params("parallel", "parallel", "arbitrary"),
        name="nsa",
    )(proj, proj, proj, proj, proj, proj, proj, proj, tiles, biasc,
      jnp.asarray(overlap.T, BF16), jnp.asarray(place, BF16), cmp_w1, w2p, cmp_pe,
      _pad_lanes(q_norm, LANES), _pad_lanes(k_norm, LANES))


MLSTM_NB = 2
MLSTM_TB = 512


def _mlstm4_kernel(qk_ref, v_ref, o_ref, gi_ref, gf_ref, cw_ref, bi_ref, bf_ref, gn_ref, tril_ref, y_ref,
                   xpad_s, c_s, n_s, m_s):
    nb, tb = qk_ref.shape[0], qk_ref.shape[1]
    L = MLSTM_CHUNK
    W = B_HEADS * HEAD_DIM

    @pl.when(pl.program_id(1) == 0)
    def _init():
        xpad_s[:, tb:tb + 8, :] = jnp.zeros((nb, 8, xpad_s.shape[2]), F32)
        c_s[...] = jnp.zeros(c_s.shape, F32)
        n_s[...] = jnp.zeros(n_s.shape, F32)
        m_s[...] = jnp.zeros(m_s.shape, F32)

    for bb in range(nb):
        xpad_s[bb, 0:8, :] = xpad_s[bb, tb:tb + 8, :]
        xpad_s[bb, 8:8 + tb, :] = qk_ref[bb]
    lane_head = lax.shift_right_logical(lax.broadcasted_iota(jnp.int32, (1, W), 1), 6)
    seg = [lane_head == a for a in range(B_HEADS)]
    row_l = lax.broadcasted_iota(jnp.int32, (L, W), 0)
    lane_l = lax.broadcasted_iota(jnp.int32, (L, W), 1) & (HEAD_DIM - 1)
    tri = lane_l <= row_l
    diag = lane_l == row_l
    blockdiag = (lax.shift_right_logical(lax.broadcasted_iota(jnp.int32, (W, W), 0), 6)
                 == lax.shift_right_logical(lax.broadcasted_iota(jnp.int32, (W, W), 1), 6))
    kscale = HEAD_DIM ** -0.5

    def seg_reduce(x, op, fill):
        out = None
        for a in range(B_HEADS):
            r = op(jnp.where(seg[a], x, fill)[:, (a // 2) * LANES:(a // 2 + 1) * LANES], axis=-1, keepdims=True)
            out = r if out is None else jnp.where(seg[a], r, out)
        return out

    def chunk_one(bb, c):
        r0 = pl.multiple_of(c * L, L)
        win = xpad_s[bb, pl.ds(r0, L + 8), :]
        conv = cw_ref[CONV_K - 1:CONV_K, :] * win[8:8 + L]
        for j in range(CONV_K - 1):
            conv = conv + cw_ref[j:j + 1, :] * pltpu.roll(win, CONV_K - 1 - j, axis=0)[8:8 + L]
        qk = conv * jax.nn.sigmoid(conv)
        q4 = qk[:, :W]
        k4 = qk[:, W:] * kscale
        v4 = v_ref[bb, pl.ds(r0, L), :]
        i4 = gi_ref[bb, pl.ds(r0, L), :] + bi_ref[...]
        logf = jax.nn.log_sigmoid(gf_ref[bb, pl.ds(r0, L), :] + bf_ref[...])
        lh, lm, ll = _split3(logf)
        cum = _dot(tril_ref[...], jnp.concatenate([lh, lm, ll], axis=1))
        b4 = cum[:, :W] + cum[:, W:2 * W] + cum[:, 2 * W:]
        c_old, n_old, m_prev = c_s[bb], n_s[bb], m_s[bb][0:1, :]
        r4 = jnp.sum(jnp.where(diag, i4 - b4, 0.0), axis=0, keepdims=True)
        dmat = jnp.where(tri, b4 + r4, -jnp.inf)
        m_inter = b4 + m_prev
        m_j = jnp.maximum(m_inter, seg_reduce(dmat, jnp.max, -jnp.inf))
        isc = jnp.exp(m_inter - m_j)
        k_heads = jnp.concatenate([jnp.where(seg[a], k4, 0.0) for a in range(B_HEADS)], axis=0)
        v_heads = jnp.concatenate([jnp.where(seg[a], v4, 0.0) for a in range(B_HEADS)], axis=0)
        s = _dot_nt(q4, k_heads) * jnp.exp(dmat - m_j)
        den = seg_reduce(s, jnp.sum, 0.0) + isc * seg_reduce(q4 * n_old, jnp.sum, 0.0)
        num = _dot(s, v_heads) + isc * _dot_nt(q4, c_old)
        h = num / jnp.maximum(jnp.abs(den), jnp.exp(-m_j))
        b_last = b4[L - 1:L, :]
        w_log = b_last - b4 + i4
        m_new = jnp.maximum(b_last + m_prev, jnp.max(w_log, axis=0, keepdims=True))
        w4 = jnp.exp(w_log - m_new)
        dec = jnp.exp(b_last + m_prev - m_new)
        c_s[bb] = dec * c_old + jnp.where(blockdiag, _dot_tn(v4 * w4, k4), 0.0)
        n_s[bb] = dec * n_old + jnp.sum(w4 * k4, axis=0, keepdims=True)
        m_s[bb] = jnp.broadcast_to(m_new, m_s.shape[1:])
        ms = seg_reduce(h * h, jnp.sum, 0.0) * (1.0 / HEAD_DIM)
        hn = h * lax.rsqrt(ms + EPS) * gn_ref[...]
        y_ref[bb, pl.ds(r0, L), :] = jax.nn.sigmoid(o_ref[bb, pl.ds(r0, L), :]) * hn

    def chunk(c, carry):
        for bb in range(nb):
            chunk_one(bb, c)
        return carry

    lax.fori_loop(0, tb // L, chunk, 0)


def _mlstm4(proj, conv_w, i_bias, f_bias, out_norm):
    B, T, _ = proj.shape
    W = B_HEADS * HEAD_DIM
    nb, tb = MLSTM_NB, MLSTM_TB
    tril = jnp.asarray(np.tril(np.ones((MLSTM_CHUNK, MLSTM_CHUNK))), BF16)

    def full(shape):
        return pl.BlockSpec(shape, lambda b, t: (0,) * len(shape))

    def seg(width, off):
        return pl.BlockSpec((nb, tb, width), lambda b, t: (b, t, off // width))

    return pl.pallas_call(
        _mlstm4_kernel,
        grid=(B // nb, T // tb),
        in_specs=[seg(2 * W, SEG_BQK), seg(W, SEG_BV), seg(W, SEG_BO), seg(W, SEG_BIF), seg(W, SEG_BIF + W),
                  full((CONV_K, 2 * W)), full((1, W)), full((1, W)), full((1, W)),
                  full((MLSTM_CHUNK, MLSTM_CHUNK))],
        out_specs=pl.BlockSpec((nb, tb, W), lambda b, t: (b, t, 0)),
        out_shape=jax.ShapeDtypeStruct((B, T, W), F32),
        scratch_shapes=[pltpu.VMEM((nb, tb + 8, 2 * W), F32),
                        pltpu.VMEM((nb, W, W), F32),
                        pltpu.VMEM((nb, 1, W), F32),
                        pltpu.VMEM((nb, 8, W), F32)],
        compiler_params=_params("parallel", "arbitrary"),
        name="mlstm",
    )(proj, proj, proj, proj, proj, conv_w, jnp.repeat(i_bias, HEAD_DIM).reshape(1, W),
      jnp.repeat(f_bias, HEAD_DIM).reshape(1, W), out_norm.reshape(1, W), tril)


def _rope_kernel(pos_ref, invf_ref, sgn_ref, cs_ref, sn_ref):
    ang = pos_ref[...].astype(F32) * invf_ref[...]
    cs_ref[...] = jnp.cos(ang)
    sn_ref[...] = jnp.sin(ang) * sgn_ref[...]


def _rope_tables(pos):
    N = pos.shape[0]
    tm = 1024
    half = C_ROPE // 2
    inv = 1.0 / (ROPE_THETA ** (np.arange(half, dtype=np.float32) * 2.0 / C_ROPE))
    invf = np.zeros((1, LANES), np.float32)
    sgn = np.zeros((1, LANES), np.float32)
    invf[0, C_NOPE:C_NOPE + C_ROPE] = np.concatenate([inv, inv])
    sgn[0, C_NOPE:C_NOPE + half] = -1.0
    sgn[0, C_NOPE + half:C_NOPE + C_ROPE] = 1.0
    row = pl.BlockSpec((1, LANES), lambda i: (0, 0))
    tile = pl.BlockSpec((tm, LANES), lambda i: (i, 0))
    return pl.pallas_call(
        _rope_kernel,
        grid=(N // tm,),
        in_specs=[pl.BlockSpec((tm, 1), lambda i: (i, 0)), row, row],
        out_specs=[tile, tile],
        out_shape=[jax.ShapeDtypeStruct((N, LANES), F32), jax.ShapeDtypeStruct((N, LANES), F32)],
        compiler_params=_params("parallel"),
        name="rope_tables",
    )(pos, jnp.asarray(invf), jnp.asarray(sgn))


def _mla_prep_kernel(dq_ref, dkv_ref, kr_ref, cs_ref, sn_ref, wq_ref, wk_ref, wv_ref, qa_ref, kva_ref,
                     gq_ref, gk_ref, gkr_ref, q_out, k_out, v_out):
    lane = lax.broadcasted_iota(jnp.int32, (1, LANES), 1)
    nope = lane < C_NOPE
    ropem = (lane >= C_NOPE) & (lane < C_NOPE + C_ROPE)
    first = lane < C_NOPE + C_ROPE // 2
    cs = cs_ref[...]
    sn = sn_ref[...]

    def rope(x):
        partner = jnp.where(first, pltpu.roll(x, LANES - C_ROPE // 2, axis=1), pltpu.roll(x, C_ROPE // 2, axis=1))
        return x * cs + partner * sn

    scale = (C_NOPE + C_ROPE) ** -0.5 * LOG2E
    qn = _rms_lanes(dq_ref[...], qa_ref[...], Q_RANK).astype(BF16)
    q = _dot(qn, wq_ref[...])
    kvn = _rms_lanes(dkv_ref[...], kva_ref[...], KV_RANK).astype(BF16)
    kk = _dot(kvn, wk_ref[...])
    vv = _dot(kvn, wv_ref[...])
    lane_v = lax.broadcasted_iota(jnp.int32, vv.shape, 1)
    v_out[...] = jnp.where((lane_v & (LANES - 1)) == C_V, 1.0, vv).astype(BF16)
    kr = kr_ref[...]
    krn = kr * lax.rsqrt(jnp.sum(kr * kr, axis=-1, keepdims=True) * (1.0 / C_ROPE) + EPS) * gkr_ref[...]
    krr = rope(krn)
    for h in range(C_HEADS):
        sl = slice(h * LANES, (h + 1) * LANES)
        qh = q[:, sl]
        sq = qh * qh
        msn = jnp.sum(jnp.where(nope, sq, 0.0), axis=-1, keepdims=True) * (1.0 / C_NOPE)
        msr = jnp.sum(jnp.where(ropem, sq, 0.0), axis=-1, keepdims=True) * (1.0 / C_ROPE)
        qhn = qh * jnp.where(nope, lax.rsqrt(msn + EPS), lax.rsqrt(msr + EPS)) * gq_ref[...]
        q_out[:, sl] = (rope(qhn) * scale).astype(BF16)
        kh = kk[:, sl]
        msk = jnp.sum(kh * kh, axis=-1, keepdims=True) * (1.0 / C_NOPE)
        k_out[:, sl] = (kh * lax.rsqrt(msk + EPS) * gk_ref[...] + krr).astype(BF16)


def _mla_prep(proj2, rope_cs, rope_sn, q_a_norm, w_uq, kv_a_norm, w_ukv, qk_norm):
    N = proj2.shape[0]
    tm = 512
    qd = C_NOPE + C_ROPE
    qmap = np.full((C_HEADS * LANES,), -1, np.int64)
    kmap = np.full((C_HEADS * LANES,), -1, np.int64)
    vmap = np.full((C_HEADS * LANES,), -1, np.int64)
    for h in range(C_HEADS):
        qmap[h * LANES:h * LANES + qd] = h * qd + np.arange(qd)
        kmap[h * LANES:h * LANES + C_NOPE] = h * (C_NOPE + C_V) + np.arange(C_NOPE)
        vmap[h * LANES:h * LANES + C_V] = h * (C_NOPE + C_V) + C_NOPE + np.arange(C_V)
    wq = jnp.pad(_take_cols(w_uq, qmap), ((0, 256 - Q_RANK), (0, 0))).astype(BF16)
    wk = _take_cols(w_ukv, kmap).astype(BF16)
    wv = _take_cols(w_ukv, vmap).astype(BF16)
    gkr = jnp.pad(qk_norm[1, C_NOPE:].reshape(1, -1), ((0, 0), (C_NOPE, LANES - C_NOPE - C_ROPE)))

    def full(shape):
        return pl.BlockSpec(shape, lambda i: (0,) * len(shape))

    return pl.pallas_call(
        _mla_prep_kernel,
        grid=(N // tm,),
        in_specs=[pl.BlockSpec((tm, 256), lambda i: (i, SEG_CDQ // 256)),
                  pl.BlockSpec((tm, LANES), lambda i: (i, SEG_CDKV // LANES)),
                  pl.BlockSpec((tm, LANES), lambda i: (i, SEG_CKR // LANES)),
                  pl.BlockSpec((tm, LANES), lambda i: (i, 0)), pl.BlockSpec((tm, LANES), lambda i: (i, 0)),
                  full((256, C_HEADS * LANES)), full((KV_RANK, C_HEADS * LANES)), full((KV_RANK, C_HEADS * LANES)),
                  full((1, 256)), full((1, LANES)), full((1, LANES)), full((1, LANES)), full((1, LANES))],
        out_specs=[pl.BlockSpec((tm, C_HEADS * LANES), lambda i: (i, 0)),
                   pl.BlockSpec((tm, C_HEADS * LANES), lambda i: (i, 0)),
                   pl.BlockSpec((tm, C_HEADS * LANES), lambda i: (i, 0))],
        out_shape=[jax.ShapeDtypeStruct((N, C_HEADS * LANES), BF16),
                   jax.ShapeDtypeStruct((N, C_HEADS * LANES), BF16),
                   jax.ShapeDtypeStruct((N, C_HEADS * LANES), BF16)],
        compiler_params=_params("parallel"),
        name="mla_prep",
    )(proj2, proj2, proj2, rope_cs, rope_sn, wq, wk, wv,
      _pad_lanes(q_a_norm, 256), kv_a_norm.reshape(1, -1), _pad_lanes(qk_norm[0], LANES),
      _pad_lanes(qk_norm[1, :C_NOPE], LANES), gkr)


MLA_Q = 2 * AT


def _mla_attn_kernel(q_ref, k_ref, v_ref, causal_ref, o_ref, s_s, m_s, acc_s):
    qi = pl.program_id(2)
    m_s[...] = jnp.full(m_s.shape, NEG, F32)
    acc_s[...] = jnp.zeros(acc_s.shape, F32)
    heads = [slice(hh * LANES, (hh + 1) * LANES) for hh in range(2)]
    per_head = MLA_Q // RB
    tiles_per_q = MLA_Q // AT

    def key_rows(kc):
        return pl.ds(pl.multiple_of(kc * AT, AT), AT)

    def scores(kc):
        return [_dot_nt(q_ref[0, :, sl], k_ref[0, key_rows(kc), sl]) for sl in heads]

    def values(kc):
        return [v_ref[0, key_rows(kc), sl] for sl in heads]

    for d in range(tiles_per_q):
        kc = tiles_per_q * qi + d
        _softmax_pv(scores(kc), values(kc),
                    lambda r, d=d: causal_ref[d, pl.ds((r % per_head) * RB, RB), :], m_s, acc_s)

    for h, s in enumerate(scores(0)):
        s_s[h] = s

    def body(kc, carry):
        s_next = scores(kc + 1)
        _softmax_pv([s_s[0], s_s[1]], values(kc), None, m_s, acc_s)
        for h, s in enumerate(s_next):
            s_s[h] = s
        return carry

    lax.fori_loop(0, tiles_per_q * qi, body, 0)
    lane = lax.broadcasted_iota(jnp.int32, (MLA_Q, LANES), 1)
    outs = []
    for hh in range(2):
        a = acc_s[hh * MLA_Q:(hh + 1) * MLA_Q, :]
        outs.append(jnp.where(lane < C_V, a / a[:, C_V:C_V + 1], 0.0))
    o_ref[0] = outs[0] + pltpu.roll(outs[1], C_V, axis=1)


def _mla_attn(q, k, v):
    B, T, _ = q.shape
    qpos = np.arange(MLA_Q)[None, :, None]
    kpos = (np.arange(MLA_Q // AT)[:, None, None] * AT + np.arange(AT)[None, None, :])
    causal = np.where(qpos >= kpos, 0.0, NEG).astype(np.float32)
    pair = pl.BlockSpec((1, T, 2 * LANES), lambda b, p, i: (b, 0, p))
    return pl.pallas_call(
        _mla_attn_kernel,
        grid=(B, C_HEADS // 2, T // MLA_Q),
        in_specs=[pl.BlockSpec((1, MLA_Q, 2 * LANES), lambda b, p, i: (b, i, p)), pair, pair,
                  pl.BlockSpec((MLA_Q // AT, MLA_Q, AT), lambda b, p, i: (0, 0, 0))],
        out_specs=pl.BlockSpec((1, MLA_Q, LANES), lambda b, p, i: (b, i, p)),
        out_shape=jax.ShapeDtypeStruct((B, T, C_HEADS * C_V), F32),
        scratch_shapes=[pltpu.VMEM((2, MLA_Q, AT), F32), pltpu.VMEM((2 * MLA_Q, LANES), F32),
                        pltpu.VMEM((2 * MLA_Q, LANES), F32)],
        compiler_params=_params("parallel", "parallel", "arbitrary"),
        name="mla_attn",
    )(q, k, v, jnp.asarray(causal))


def _outproj_kernel(x_ref, ya_ref, yb_ref, yc_ref, wa_ref, wb_ref, wc_ref, ga_ref, gc_ref, g2_ref,
                    wr_ref, br_ref, x1_ref, h2_ref, ids_ref, wts_ref, cnt_ref):
    ya = _rms_lanes(ya_ref[...], ga_ref[...], A_HEADS * HEAD_DIM).astype(BF16)
    yc = _rms_lanes(yc_ref[...], gc_ref[...], C_HEADS * C_V).astype(BF16)
    x1 = (x_ref[...] + _dot(ya, wa_ref[...]) + _dot(yb_ref[...].astype(BF16), wb_ref[...])
          + _dot(yc, wc_ref[...]))
    x1_ref[...] = x1
    h2 = _rms_lanes(x1, g2_ref[...], D_MODEL)
    h2_ref[...] = h2
    h2_hi = h2.astype(BF16)
    h2_mid = (h2 - h2_hi.astype(F32)).astype(BF16)
    logits = (_dot(h2_hi, wr_ref[0]) + _dot(h2_mid, wr_ref[0]) + _dot(h2_hi, wr_ref[1])) + br_ref[...]
    tm = logits.shape[0]
    lane = lax.broadcasted_iota(jnp.int32, (tm, LANES), 1)
    is_g = (lane >= N_EXPERTS) & (lane < N_EXPERTS + N_GROUPS)
    gl = jnp.where(is_g, logits, NEG)
    ge = jnp.where(is_g, jnp.exp(gl - jnp.max(gl, axis=-1, keepdims=True)), 0.0)
    gp = jnp.where(is_g, ge / jnp.sum(ge, axis=-1, keepdims=True), -1.0)
    g_top = jnp.max(gp, axis=-1, keepdims=True)
    g_idx = jnp.min(jnp.where(gp == g_top, lane, 2 * LANES), axis=-1, keepdims=True) - N_EXPERTS
    in_grp = (lane < N_EXPERTS) & (lax.shift_right_logical(lane, 3) == g_idx)
    el = jnp.where(in_grp, logits, NEG)
    ee = jnp.where(in_grp, jnp.exp(el - jnp.max(el, axis=-1, keepdims=True)), 0.0)
    ep = jnp.where(in_grp, ee / jnp.sum(ee, axis=-1, keepdims=True), -1.0)
    v1 = jnp.max(ep, axis=-1, keepdims=True)
    i1 = jnp.min(jnp.where(ep == v1, lane, 2 * LANES), axis=-1, keepdims=True)
    ep2 = jnp.where(lane == i1, -1.0, ep)
    v2 = jnp.max(ep2, axis=-1, keepdims=True)
    i2 = jnp.min(jnp.where(ep2 == v2, lane, 2 * LANES), axis=-1, keepdims=True)
    tot = v1 + v2
    ids_ref[...] = jnp.where(lane == 0, i1, jnp.where(lane == 1, i2, 0))
    wts_ref[...] = jnp.where(lane == 0, v1 / tot * g_top, jnp.where(lane == 1, v2 / tot * g_top, 0.0))
    per_expert = jnp.sum(jnp.where((lane == i1) | (lane == i2), 1.0, 0.0), axis=0, keepdims=True)
    cnt_ref[...] = jnp.broadcast_to(per_expert, cnt_ref.shape)


def _outproj(x2, ya, yb, yc, w_out, nsa_out_norm, mla_out_norm, norm2_g, w_group, b_group, w_expert, b_expert):
    N = x2.shape[0]
    tm = 512
    amap = np.full((A_HEADS * LANES,), -1, np.int64)
    for hd in range(A_HEADS):
        amap[hd * LANES:hd * LANES + HEAD_DIM] = hd * HEAD_DIM + np.arange(HEAD_DIM)
    wa = _take_rows(w_out[:384], amap).astype(BF16)
    wb = w_out[384:640].astype(BF16)
    wc = w_out[640:].astype(BF16)
    ga = _take_cols(nsa_out_norm.reshape(1, -1), amap)
    wr = jnp.pad(jnp.concatenate([w_expert, w_group], axis=1), ((0, 0), (0, LANES - N_EXPERTS - N_GROUPS)))
    wr_hi = wr.astype(BF16)
    wr = jnp.stack([wr_hi, (wr - wr_hi.astype(F32)).astype(BF16)])
    br = _pad_lanes(jnp.concatenate([b_expert, b_group]), LANES)

    def full(shape):
        return pl.BlockSpec(shape, lambda i: (0,) * len(shape))

    def rows(w):
        return pl.BlockSpec((tm, w), lambda i: (i, 0))

    return pl.pallas_call(
        _outproj_kernel,
        grid=(N // tm,),
        in_specs=[rows(D_MODEL), rows(A_HEADS * LANES), rows(256), rows(384),
                  full((A_HEADS * LANES, D_MODEL)), full((256, D_MODEL)), full((384, D_MODEL)),
                  full((1, A_HEADS * LANES)), full((1, 384)), full((1, D_MODEL)),
                  full((2, D_MODEL, LANES)), full((1, LANES))],
        out_specs=[rows(D_MODEL), rows(D_MODEL), rows(LANES), rows(LANES),
                   pl.BlockSpec((8, LANES), lambda i: (i, 0))],
        out_shape=[jax.ShapeDtypeStruct((N, D_MODEL), F32), jax.ShapeDtypeStruct((N, D_MODEL), F32),
                   jax.ShapeDtypeStruct((N, LANES), jnp.int32), jax.ShapeDtypeStruct((N, LANES), F32),
                   jax.ShapeDtypeStruct((N // tm * 8, LANES), F32)],
        compiler_params=_params("parallel"),
        name="outproj_router",
    )(x2, ya, yb, yc, wa, wb, wc, ga, mla_out_norm.reshape(1, -1), norm2_g.reshape(1, -1), wr, br)


def _rank_kernel(ids_ref, off_ref, tril_ref, pos_ref, carry_s):
    i = pl.program_id(0)

    @pl.when(i == 0)
    def _():
        carry_s[...] = jnp.broadcast_to(off_ref[...], carry_s.shape)

    ids = ids_ref[...]
    tm = ids.shape[0]
    lane = lax.broadcasted_iota(jnp.int32, (tm, LANES), 1)
    i1 = ids[:, 0:1]
    i2 = ids[:, 1:2]
    oh = jnp.where((lane == i1) | (lane == i2), 1.0, 0.0)
    incl = _dot(tril_ref[...], oh.astype(BF16)) + carry_s[0:1, :]
    excl = incl - oh
    p1 = jnp.sum(jnp.where(lane == i1, excl, 0.0), axis=-1, keepdims=True)
    p2 = jnp.sum(jnp.where(lane == i2, excl, 0.0), axis=-1, keepdims=True)
    pos_ref[...] = jnp.where(lane == 0, p1, jnp.where(lane == 1, p2, 0.0)).astype(jnp.int32)
    carry_s[...] = jnp.broadcast_to(incl[tm - 1:tm, :], carry_s.shape)


def _moe_positions(ids, off_row):
    N = ids.shape[0]
    tm = 512
    tril = jnp.asarray(np.tril(np.ones((tm, tm))), BF16)
    return pl.pallas_call(
        _rank_kernel,
        grid=(N // tm,),
        in_specs=[pl.BlockSpec((tm, LANES), lambda i: (i, 0)), pl.BlockSpec((1, LANES), lambda i: (0, 0)),
                  pl.BlockSpec((tm, tm), lambda i: (0, 0))],
        out_specs=pl.BlockSpec((tm, LANES), lambda i: (i, 0)),
        out_shape=jax.ShapeDtypeStruct((N, LANES), jnp.int32),
        scratch_shapes=[pltpu.VMEM((8, LANES), F32)],
        compiler_params=_params("arbitrary"),
        name="moe_positions",
    )(ids, off_row, tril)


DISPATCH_TM = 1024


DMA_UNROLL = 8


def _dispatch_kernel(off_ref, tp_ref, nu_ref, p1_ref, p2_ref, h_ref, xs_ref, zero_s, sem, zsem):
    n_tiles = xs_ref.shape[0] // ROW_TILE

    @pl.when(pl.program_id(0) == 0)
    def _zero_fill():
        zero_s[...] = jnp.zeros(zero_s.shape, F32)

        def tile_copy(row0):
            return pltpu.make_async_copy(
                zero_s, xs_ref.at[pl.ds(pl.multiple_of(row0, ROW_TILE), ROW_TILE), :], zsem)

        for e in range(N_EXPERTS):
            @pl.when(tp_ref[e] > 0)
            def _start(e=e):
                tile_copy(off_ref[e] + (tp_ref[e] - 1) * ROW_TILE).start()

        def tail_start(g, carry):
            tile_copy(g * ROW_TILE).start()
            return carry

        lax.fori_loop(nu_ref[0], n_tiles, tail_start, 0)
        for e in range(N_EXPERTS):
            @pl.when(tp_ref[e] > 0)
            def _wait():
                tile_copy(0).wait()

        def tail_wait(g, carry):
            tile_copy(0).wait()
            return carry

        lax.fori_loop(nu_ref[0], n_tiles, tail_wait, 0)

    def row_copy(i, pos):
        return pltpu.make_async_copy(h_ref.at[pl.ds(i, 1), :], xs_ref.at[pl.ds(pos, 1), :], sem)

    def issue(j, carry):
        for u in range(DMA_UNROLL):
            i = j * DMA_UNROLL + u
            row_copy(i, p1_ref[i]).start(priority=0)
            row_copy(i, p2_ref[i]).start(priority=1)
        return carry

    lax.fori_loop(0, DISPATCH_TM // DMA_UNROLL, issue, 0)
    whole = pltpu.make_async_copy(h_ref, xs_ref.at[pl.ds(0, DISPATCH_TM), :], sem)
    whole.wait()
    whole.wait()


def _moe_dispatch(h2, off, tiles_per, n_used, pos1, pos2, n_rows):
    N = h2.shape[0]
    smem1d = pl.BlockSpec((DISPATCH_TM,), lambda i: (i,), memory_space=pltpu.SMEM)
    smem = pl.BlockSpec(memory_space=pltpu.SMEM)
    return pl.pallas_call(
        _dispatch_kernel,
        grid=(N // DISPATCH_TM,),
        in_specs=[smem, smem, smem, smem1d, smem1d,
                  pl.BlockSpec((DISPATCH_TM, D_MODEL), lambda i: (i, 0))],
        out_specs=pl.BlockSpec(memory_space=pl.ANY),
        out_shape=jax.ShapeDtypeStruct((n_rows, D_MODEL), F32),
        scratch_shapes=[pltpu.VMEM((ROW_TILE, D_MODEL), F32), pltpu.SemaphoreType.DMA(()),
                        pltpu.SemaphoreType.DMA(())],
        compiler_params=_params("arbitrary"),
        name="moe_dispatch",
    )(off, tiles_per, n_used, pos1, pos2, h2)


def _expert_kernel(te_ref, nu_ref, x_ref, wg_ref, wu_ref, wd_ref, y_ref):
    g = pl.program_id(0)

    @pl.when(g < nu_ref[0])
    def _():
        x = x_ref[...].astype(BF16)
        a = _dot(x, wg_ref[0, 0].astype(BF16))
        u = _dot(x, wu_ref[0, 0].astype(BF16))
        h = (a * jax.nn.sigmoid(a) * u).astype(BF16)
        y_ref[...] = _dot(h, wd_ref[0, 0].astype(BF16))

    @pl.when(g >= nu_ref[0])
    def _():
        y_ref[...] = jnp.zeros(y_ref.shape, F32)


def _moe_experts(xs, tile_expert, n_used, layer, w_gate, w_up, w_down):
    R = xs.shape[0]
    G = R // ROW_TILE

    def row_map(g, te, nu):
        return (jnp.minimum(g, nu[0] - 1), 0)

    def w_map(g, te, nu):
        return (layer, te[jnp.minimum(g, nu[0] - 1)], 0, 0)

    return pl.pallas_call(
        _expert_kernel,
        grid_spec=pltpu.PrefetchScalarGridSpec(
            num_scalar_prefetch=2,
            grid=(G,),
            in_specs=[pl.BlockSpec((ROW_TILE, D_MODEL), row_map),
                      pl.BlockSpec((1, 1, D_MODEL, D_EXPERT), w_map),
                      pl.BlockSpec((1, 1, D_MODEL, D_EXPERT), w_map),
                      pl.BlockSpec((1, 1, D_EXPERT, D_MODEL), w_map)],
            out_specs=pl.BlockSpec((ROW_TILE, D_MODEL), lambda g, te, nu: (g, 0))),
        out_shape=jax.ShapeDtypeStruct((R, D_MODEL), F32),
        compiler_params=_params("arbitrary"),
        name="moe_experts",
    )(tile_expert, n_used, xs, w_gate, w_up, w_down)


COMBINE_TM = 256


def _combine_kernel(p1_ref, p2_ref, x1_ref, wts_ref, ys_ref, o_ref, b1_s, b2_s, sem):
    def row_copy(buf, i, pos):
        return pltpu.make_async_copy(ys_ref.at[pl.ds(pos, 1), :], buf.at[pl.ds(i, 1), :], sem)

    def issue(j, carry):
        for u in range(DMA_UNROLL):
            i = j * DMA_UNROLL + u
            row_copy(b1_s, i, p1_ref[i]).start(priority=0)
            row_copy(b2_s, i, p2_ref[i]).start(priority=1)
        return carry

    lax.fori_loop(0, COMBINE_TM // DMA_UNROLL, issue, 0)
    pltpu.make_async_copy(ys_ref.at[pl.ds(0, COMBINE_TM), :], b1_s, sem).wait()
    pltpu.make_async_copy(ys_ref.at[pl.ds(0, COMBINE_TM), :], b2_s, sem).wait()
    w = wts_ref[...]
    o_ref[...] = x1_ref[...] + w[:, 0:1] * b1_s[...] + w[:, 1:2] * b2_s[...]


def _moe_combine(x1, wts, ys, pos1, pos2):
    N = x1.shape[0]
    smem1d = pl.BlockSpec((COMBINE_TM,), lambda i: (i,), memory_space=pltpu.SMEM)
    return pl.pallas_call(
        _combine_kernel,
        grid=(N // COMBINE_TM,),
        in_specs=[smem1d, smem1d,
                  pl.BlockSpec((COMBINE_TM, D_MODEL), lambda i: (i, 0)),
                  pl.BlockSpec((COMBINE_TM, LANES), lambda i: (i, 0)),
                  pl.BlockSpec(memory_space=pl.ANY)],
        out_specs=pl.BlockSpec((COMBINE_TM, D_MODEL), lambda i: (i, 0)),
        out_shape=jax.ShapeDtypeStruct((N, D_MODEL), F32),
        scratch_shapes=[pltpu.VMEM((COMBINE_TM, D_MODEL), F32), pltpu.VMEM((COMBINE_TM, D_MODEL), F32),
                        pltpu.SemaphoreType.DMA(())],
        compiler_params=_params("arbitrary"),
        name="moe_combine",
    )(pos1, pos2, x1, wts, ys)


def _moe(x1, h2, ids, wts, tile_counts, layer, w_gate, w_up, w_down):
    N = x1.shape[0]
    n_rows = 2 * N + N_EXPERTS * ROW_TILE
    counts = jnp.sum(tile_counts[::8, :N_EXPERTS], axis=0).astype(jnp.int32)
    tiles_per = (counts + ROW_TILE - 1) // ROW_TILE
    tile_end = jnp.cumsum(tiles_per)
    off = ((tile_end - tiles_per) * ROW_TILE).astype(jnp.int32)
    n_used = tile_end[-1:].astype(jnp.int32)
    tile_ids = jnp.arange(n_rows // ROW_TILE, dtype=jnp.int32)
    tile_expert = jnp.minimum(jnp.sum(tile_end[None, :] <= tile_ids[:, None], axis=1), N_EXPERTS - 1).astype(jnp.int32)
    pos = _moe_positions(ids, _pad_lanes(off.astype(F32), LANES))
    pos1, pos2 = pos[:, 0], pos[:, 1]
    xs = _moe_dispatch(h2, off, tiles_per.astype(jnp.int32), n_used, pos1, pos2, n_rows)
    ys = _moe_experts(xs, tile_expert, n_used, layer, w_gate, w_up, w_down)
    return _moe_combine(x1, wts, ys, pos1, pos2)


def kernel(x, positions, rel_bias, norm1_g, w_in, nsa_cmp_pe, nsa_cmp_w1, nsa_cmp_w2, nsa_q_norm, nsa_k_norm, nsa_out_norm, mlstm_conv, mlstm_i_bias, mlstm_f_bias, mlstm_out_norm, mla_q_a_norm, mla_w_uq, mla_kv_a_norm, mla_w_ukv, mla_qk_norm, mla_out_norm, w_out, norm2_g, moe_w_group, moe_b_group, moe_w_expert, moe_b_expert, moe_w_gate, moe_w_up, moe_w_down):
    B, T, D = x.shape
    N = B * T
    depth = w_in.shape[0]
    colmap = _proj_colmap()
    tiles, biasc = _bias_tables(rel_bias, T)
    rope_cs, rope_sn = _rope_tables(positions.reshape(N, 1))
    x2 = x.reshape(N, D)
    for l in range(depth):
        w_proj = _take_cols(w_in[l], colmap).astype(BF16)
        proj2 = _inproj(x2, norm1_g[l].reshape(1, D), w_proj)
        proj = proj2.reshape(B, T, D_PROJ)
        ya = _nsa(proj, tiles, biasc, nsa_cmp_pe[l], nsa_cmp_w1[l], nsa_cmp_w2[l], nsa_q_norm[l], nsa_k_norm[l])
        yb = _mlstm4(proj, mlstm_conv[l], mlstm_i_bias[l], mlstm_f_bias[l], mlstm_out_norm[l])
        q, k, v = _mla_prep(proj2, rope_cs, rope_sn, mla_q_a_norm[l], mla_w_uq[l], mla_kv_a_norm[l], mla_w_ukv[l], mla_qk_norm[l])
        yc = _mla_attn(q.reshape(B, T, -1), k.reshape(B, T, -1), v.reshape(B, T, -1))
        x1, h2, ids, wts, tile_counts = _outproj(x2, ya.reshape(N, -1), yb.reshape(N, -1), yc.reshape(N, -1), w_out[l],
                                    nsa_out_norm[l], mla_out_norm[l], norm2_g[l], moe_w_group[l], moe_b_group[l],
                                    moe_w_expert[l], moe_b_expert[l])
        x2 = _moe(x1, h2, ids, wts, tile_counts, l, moe_w_gate, moe_w_up, moe_w_down)
    return x2.reshape(B, T, D)
```

```python
import functools
import math

import numpy as np
import jax
import jax.numpy as jnp
from jax import lax
from jax.experimental import pallas as pl
from jax.experimental.pallas import tpu as pltpu

F32 = jnp.float32
BF16 = jnp.bfloat16

D_MODEL = 1024
HEAD_DIM = 64
LANES = 128
A_HEADS, A_KV_HEADS, A_GROUP = 6, 2, 3
CMP_BLOCK, CMP_STRIDE, SEL_BLOCK, N_SELECT, WINDOW = 32, 16, 64, 16, 512
FORCE = 1e4
B_HEADS, CONV_K, MLSTM_CHUNK = 4, 4, 64
C_HEADS, C_NOPE, C_ROPE, C_V = 6, 64, 32, 64
Q_RANK, KV_RANK = 192, 128
ROPE_THETA = 10000.0
REL_BUCKETS, REL_MAX_DIST = 32, 128
N_GROUPS, EXPERTS_PER_GROUP, N_EXPERTS, D_EXPERT = 4, 8, 32, 256
EPS = 1e-6
NEG = -1e30

LOG2E = 1.4426950408889634

AT = 256
RB = 128
ROW_TILE = 256
VMEM_LIMIT = 48 * 1024 * 1024

SEG_AQ, SEG_AKV, SEG_GATE, SEG_BQK, SEG_BV, SEG_BO = 0, 768, 2304, 2560, 3072, 3328
SEG_CDQ, SEG_CDKV, SEG_CKR, SEG_BIF, D_PROJ = 3584, 3840, 3968, 4096, 4608


def _params(*sem):
    return pltpu.CompilerParams(dimension_semantics=sem, vmem_limit_bytes=VMEM_LIMIT)


def _dot(a, b):
    return jnp.dot(a, b, preferred_element_type=F32)


def _dot_nt(a, b):
    return lax.dot_general(a, b, (((1,), (1,)), ((), ())), preferred_element_type=F32)


def _dot_tn(a, b):
    return lax.dot_general(a, b, (((0,), (0,)), ((), ())), preferred_element_type=F32)


def _split3(x):
    hi = x.astype(BF16)
    r1 = x - hi.astype(F32)
    mid = r1.astype(BF16)
    lo = (r1 - mid.astype(F32)).astype(BF16)
    return hi, mid, lo


def _dot_exact_rhs(x, m_bf16):
    hi, mid, lo = _split3(x)
    return _dot(hi, m_bf16) + _dot(mid, m_bf16) + _dot(lo, m_bf16)


def _pack_bf16_pairs(a, b):
    ua = lax.bitcast_convert_type(a.astype(BF16).astype(F32), jnp.uint32)
    ub = lax.bitcast_convert_type(b.astype(BF16).astype(F32), jnp.uint32)
    return lax.shift_right_logical(ua, jnp.uint32(16)) | (ub & jnp.uint32(0xFFFF0000))


def _unpack_bf16_pairs(w):
    a = lax.bitcast_convert_type(lax.shift_left(w, jnp.uint32(16)), F32)
    b = lax.bitcast_convert_type(w & jnp.uint32(0xFFFF0000), F32)
    return a.astype(BF16), b.astype(BF16)


def _dot_f32(x, w):
    xh = x.astype(BF16)
    xm = (x - xh.astype(F32)).astype(BF16)
    wh = w.astype(BF16)
    wm = (w - wh.astype(F32)).astype(BF16)
    return _dot(xh, wh) + _dot(xm, wh) + _dot(xh, wm)


def _bucket_np(dist):
    n = np.maximum(dist, 0)
    exact = REL_BUCKETS // 2
    nf = np.maximum(n, 1).astype(np.float64)
    large = exact + (np.log(nf / exact) / math.log(REL_MAX_DIST / exact) * (REL_BUCKETS - exact)).astype(np.int64)
    large = np.minimum(large, REL_BUCKETS - 1)
    return np.where(n < exact, n, large).astype(np.int32)


def _bucket_tables(T):
    r = np.arange(AT)[:, None]
    c = np.arange(AT)[None, :]
    diag = np.where(r >= c, _bucket_np(r - c), -1)
    prev = _bucket_np(AT + r - c)
    far = np.where(c > r, _bucket_np(2 * AT + r - c), -1)
    tiles = np.stack([diag, prev, far]).astype(np.int32)
    tq = np.arange(T)[:, None]
    n = np.arange(LANES)[None, :]
    dist_c = tq - (n * CMP_STRIDE + CMP_BLOCK - 1)
    cmp_tbl = np.where(dist_c >= 0, _bucket_np(dist_c), -1).astype(np.int32)
    return tiles, cmp_tbl


def _proj_colmap():
    cm = np.full((D_PROJ,), -1, np.int64)
    for hd in range(A_HEADS):
        cm[SEG_AQ + hd * LANES: SEG_AQ + hd * LANES + HEAD_DIM] = hd * HEAD_DIM + np.arange(HEAD_DIM)
    for s in range(12):
        cm[SEG_AKV + s * LANES: SEG_AKV + s * LANES + HEAD_DIM] = 384 + s * HEAD_DIM + np.arange(HEAD_DIM)
    for h in range(A_KV_HEADS):
        cm[SEG_GATE + h * LANES: SEG_GATE + h * LANES + 9] = 1152 + h * 9 + np.arange(9)
    cm[SEG_BQK: SEG_BQK + 512] = 1170 + np.arange(512)
    cm[SEG_BV: SEG_BV + 256] = 1682 + np.arange(256)
    cm[SEG_BIF: SEG_BIF + 2 * B_HEADS * HEAD_DIM] = 1938 + np.arange(2 * B_HEADS * HEAD_DIM) // HEAD_DIM
    cm[SEG_BO: SEG_BO + 256] = 1946 + np.arange(256)
    cm[SEG_CDQ: SEG_CDQ + Q_RANK] = 2202 + np.arange(Q_RANK)
    cm[SEG_CDKV: SEG_CDKV + KV_RANK] = 2394 + np.arange(KV_RANK)
    cm[SEG_CKR + C_NOPE: SEG_CKR + C_NOPE + C_ROPE] = 2522 + np.arange(C_ROPE)
    return cm


def _take_cols(w, colmap):
    wz = jnp.concatenate([w, jnp.zeros((w.shape[0], 1), w.dtype)], axis=1)
    return wz[:, np.where(colmap < 0, w.shape[1], colmap)]


def _take_rows(w, rowmap):
    return _take_cols(w.T, rowmap).T


def _pad_lanes(v, n):
    v = v.reshape(1, -1)
    return jnp.pad(v, ((0, 0), (0, n - v.shape[1])))


def _bias_kernel(rb_ref, bt_ref, bc_ref, tiles_ref, cmp_ref):
    hd = pl.program_id(0)
    bt = bt_ref[...]
    bc = bc_ref[...]
    t = jnp.full(bt.shape, NEG, F32)
    c = jnp.full(bc.shape, NEG, F32)
    far = rb_ref[REL_BUCKETS - 1, hd]
    for b in range(REL_BUCKETS):
        val = rb_ref[b, hd]
        t = jnp.where(bt == b, (val - far) * LOG2E, t)
        c = jnp.where(bc == b, val * LOG2E, c)
    tiles_ref[0, 0:3] = t
    tiles_ref[0, 3] = jnp.full((AT, AT), NEG, F32)
    cmp_ref[0] = c


def _bias_tables(rel_bias, T):
    bt, bc = _bucket_tables(T)
    return pl.pallas_call(
        _bias_kernel,
        grid=(A_HEADS,),
        in_specs=[pl.BlockSpec(memory_space=pltpu.SMEM),
                  pl.BlockSpec((3, AT, AT), lambda h: (0, 0, 0)),
                  pl.BlockSpec((T, LANES), lambda h: (0, 0))],
        out_specs=[pl.BlockSpec((1, 4, AT, AT), lambda h: (h, 0, 0, 0)),
                   pl.BlockSpec((1, T, LANES), lambda h: (h, 0, 0))],
        out_shape=[jax.ShapeDtypeStruct((A_HEADS, 4, AT, AT), F32),
                   jax.ShapeDtypeStruct((A_HEADS, T, LANES), F32)],
        compiler_params=_params("arbitrary"),
        name="bias_tables",
    )(rel_bias, jnp.asarray(bt), jnp.asarray(bc))


def _inproj_kernel(x_ref, g_ref, w_ref, o_ref):
    x = x_ref[...]
    ms = jnp.mean(x * x, axis=-1, keepdims=True)
    h = (x * lax.rsqrt(ms + EPS) * g_ref[...]).astype(BF16)
    for j in range(0, D_PROJ, 384):
        o_ref[:, j:j + 384] = _dot(h, w_ref[:, j:j + 384])


def _inproj(x2, g, w):
    N = x2.shape[0]
    tm = 512
    return pl.pallas_call(
        _inproj_kernel,
        grid=(N // tm,),
        in_specs=[pl.BlockSpec((tm, D_MODEL), lambda i: (i, 0)),
                  pl.BlockSpec((1, D_MODEL), lambda i: (0, 0)),
                  pl.BlockSpec((D_MODEL, D_PROJ), lambda i: (0, 0))],
        out_specs=pl.BlockSpec((tm, D_PROJ), lambda i: (i, 0)),
        out_shape=jax.ShapeDtypeStruct((N, D_PROJ), F32),
        compiler_params=_params("parallel"),
        name="inproj",
    )(x2, g, w)


def _rms_lanes(x, g, width):
    ms = jnp.sum(x * x, axis=-1, keepdims=True) * (1.0 / width)
    return x * lax.rsqrt(ms + EPS) * g


def _qk_halves(q_r, k_blk):
    half_rows = q_r.shape[0] // 2
    return [_dot_nt(q_r[h0:h0 + half_rows, :], k_blk) for h0 in range(0, q_r.shape[0], half_rows)]


def _flash_chunk(q_r, k_blk, v_blk, bias_fn, m_r, acc_r):
    _softmax_pv(_qk_halves(q_r, k_blk), v_blk, bias_fn, m_r, acc_r)


def _softmax_pv(s_halves, v_blk, bias_fn, m_r, acc_r):
    half_rows = m_r.shape[0] // 2
    for hi_, h0 in enumerate(range(0, m_r.shape[0], half_rows)):
        s_all = s_halves[hi_]
        ps, alphas = [], []
        for r in range(half_rows // RB):
            rows = slice(h0 + r * RB, h0 + (r + 1) * RB)
            s = s_all[r * RB:(r + 1) * RB]
            if bias_fn is not None:
                s = s + bias_fn((h0 + r * RB) // RB)
            lo, hi = s[:, :LANES], s[:, LANES:]
            mo = m_r[rows, :]
            mn = jnp.maximum(mo, jnp.max(jnp.maximum(lo, hi), axis=-1, keepdims=True))
            ps.append(jnp.exp2(jnp.concatenate([lo - mn, hi - mn], axis=1)).astype(BF16))
            alphas.append(jnp.exp2(mo - mn))
            m_r[rows, :] = mn
        pv = _dot(jnp.concatenate(ps, axis=0), v_blk[hi_] if isinstance(v_blk, (list, tuple)) else v_blk)
        for r in range(half_rows // RB):
            rows = slice(h0 + r * RB, h0 + (r + 1) * RB)
            acc_r[rows, :] = alphas[r] * acc_r[rows, :] + pv[r * RB:(r + 1) * RB]


def _nsa_kernel(q_ref, k0_ref, v0_ref, k1_ref, v1_ref, k2_ref, v2_ref, gate_ref, tiles_ref, biasc_ref,
                ovt_ref, place_ref, w1_ref, w2_ref, pe_ref, qg_ref, kg_ref, o_ref,
                kc_s, vc_s, k1_s, v1_s, k2_s, v2_s, qx_s, qw_s, s_s, m_s, acc_s, mw_s, accw_s):
    qi = pl.program_id(2)
    kg = kg_ref[...]
    n_cmp_pad = kc_s.shape[0]
    per_g = AT // RB

    @pl.when(qi == 0)
    def _prep():
        T = k1_s.shape[0]
        lane_t = lax.broadcasted_iota(jnp.int32, (T, LANES), 1)
        row_t = lax.broadcasted_iota(jnp.int32, (T, LANES), 0)
        block_onehot = lax.shift_right_logical(row_t, 6) == lane_t - HEAD_DIM
        ones_lane = lane_t == HEAD_DIM
        k1_s[...] = jnp.where(block_onehot, 1.0, _rms_lanes(k1_ref[0], kg, HEAD_DIM)).astype(BF16)
        v1_s[...] = jnp.where(ones_lane, 1.0, v1_ref[0]).astype(BF16)
        k2_s[...] = _rms_lanes(k2_ref[0], kg, HEAD_DIM).astype(BF16)
        v2_s[...] = jnp.where(ones_lane, 1.0, v2_ref[0]).astype(BF16)
        half = CMP_BLOCK // 2
        for s, (src, dst) in enumerate(((k0_ref, kc_s), (v0_ref, vc_s))):
            a = jnp.zeros((n_cmp_pad, HEAD_DIM), F32)
            bm = jnp.zeros((n_cmp_pad, HEAD_DIM), F32)
            for j in range(half):
                xj = src[0, pl.ds(j, n_cmp_pad, stride=CMP_STRIDE), :][:, :HEAD_DIM]
                a = a + _dot_f32(xj + pe_ref[s, j:j + 1, :], w1_ref[s, j])
                bm = bm + _dot_f32(xj + pe_ref[s, half + j:half + j + 1, :], w1_ref[s, half + j])
            pre = a + pltpu.roll(bm, n_cmp_pad - 1, axis=0)
            hid = pre * jax.nn.sigmoid(pre)
            comp = _dot_f32(hid, w2_ref[s])
            if s == 0:
                comp = _rms_lanes(comp, kg, HEAD_DIM)
            dst[...] = comp.astype(BF16)

    qscale = HEAD_DIM ** -0.5 * LOG2E
    q = q_ref[0]
    qg = qg_ref[...]
    qs = jnp.concatenate(
        [_rms_lanes(q[:, g * LANES:(g + 1) * LANES], qg, HEAD_DIM) * qscale for g in range(A_GROUP)],
        axis=0).astype(BF16)
    qw_s[...] = qs

    def near_bias(delta, kind):
        kd = jnp.where(qi >= delta, kind, 3)
        return lambda r: tiles_ref[r // per_g, kd, pl.ds((r % per_g) * RB, RB), :]

    def key_rows(kc):
        return pl.ds(pl.multiple_of(kc * AT, AT), AT)

    mw_s[...] = jnp.full(mw_s.shape, NEG, F32)
    accw_s[...] = jnp.zeros(accw_s.shape, F32)
    for delta, kind in ((0, 0), (1, 1), (2, 2)):
        rows_k = key_rows(jnp.maximum(qi - delta, 0))
        _flash_chunk(qw_s, k2_s[rows_k, :], v2_s[rows_k, :], near_bias(delta, kind), mw_s, accw_s)

    biasc = biasc_ref[...].reshape(A_GROUP * AT, LANES)
    valid = biasc > 0.5 * NEG
    lcm = _dot_nt(qs, kc_s[...]) + biasc
    mc = jnp.max(lcm, axis=-1, keepdims=True)
    pc = jnp.where(valid, jnp.exp2(lcm - mc), 0.0)
    lc = jnp.sum(pc, axis=-1, keepdims=True)
    pc = pc * jnp.where(lc > 0.0, 1.0 / lc, 0.0)
    o_c = _dot(pc.astype(BF16), vc_s[...])

    ovt = ovt_ref[...]
    p_hi, p_mid, p_lo = _split3(pc[0:AT] + pc[AT:2 * AT] + pc[2 * AT:3 * AT])
    ps3 = _dot_nt(ovt, jnp.concatenate([p_hi, p_mid, p_lo], axis=0))
    ps = ps3[:, :AT] + ps3[:, AT:2 * AT] + ps3[:, 2 * AT:]
    n_sel = ps.shape[0]
    sid = lax.broadcasted_iota(jnp.int32, (n_sel, AT), 0)
    qcol = lax.broadcasted_iota(jnp.int32, (n_sel, AT), 1)
    cur = qi * (AT // SEL_BLOCK) + lax.shift_right_logical(qcol, 6)
    forced = (sid == 0) | (sid == cur) | (sid == cur - 1)
    psf = jnp.where(forced, ps + FORCE, ps)
    psf = jnp.where(sid <= cur, psf, -1.0)
    rank = jnp.zeros((n_sel, AT), F32)
    for sp in range(n_sel):
        other = psf[sp:sp + 1, :]
        beats = (other > psf) | ((other == psf) & (sid > sp))
        rank = rank + jnp.where(beats, 1.0, 0.0)
    keep = (rank < float(min(N_SELECT, n_sel))) & (psf >= 0.0)
    selneg = _dot_tn(jnp.where(keep, 0.0, NEG).astype(BF16), place_ref[...]).astype(BF16)
    lane_q = lax.broadcasted_iota(jnp.int32, (AT, LANES), 1)
    for g in range(A_GROUP):
        qx_s[g * AT:(g + 1) * AT, :] = jnp.where(lane_q < HEAD_DIM, qs[g * AT:(g + 1) * AT], selneg)

    m_s[...] = jnp.full(m_s.shape, NEG, F32)
    acc_s[...] = jnp.zeros(acc_s.shape, F32)
    for delta, kind in ((0, 0), (1, 1)):
        rows_k = key_rows(jnp.maximum(qi - delta, 0))
        _flash_chunk(qx_s, k1_s[rows_k, :], v1_s[rows_k, :], near_bias(delta, kind), m_s, acc_s)

    for h, s in enumerate(_qk_halves(qx_s, k1_s[key_rows(0), :])):
        s_s[h] = s

    def far_body(kc, carry):
        s_next = _qk_halves(qx_s, k1_s[key_rows(kc + 1), :])
        _softmax_pv([s_s[0], s_s[1]], v1_s[key_rows(kc), :], None, m_s, acc_s)
        for h, s in enumerate(s_next):
            s_s[h] = s
        return carry

    lax.fori_loop(0, jnp.maximum(qi - 1, 0), far_body, 0)

    gt = jax.nn.sigmoid(gate_ref[0])
    for g in range(A_GROUP):
        rows = slice(g * AT, (g + 1) * AT)
        a_s = acc_s[rows, :]
        a_w = accw_s[rows, :]
        o = (gt[:, 3 * g:3 * g + 1] * o_c[rows]
             + gt[:, 3 * g + 1:3 * g + 2] * (a_s / a_s[:, HEAD_DIM:HEAD_DIM + 1])
             + gt[:, 3 * g + 2:3 * g + 3] * (a_w / a_w[:, HEAD_DIM:HEAD_DIM + 1]))
        o_ref[0, :, g * LANES:(g + 1) * LANES] = jnp.where(lane_q < HEAD_DIM, o, 0.0)


def _nsa(proj, tiles, biasc, cmp_pe, cmp_w1, cmp_w2, q_norm, k_norm):
    B, T, _ = proj.shape
    nq = T // AT
    n_cmp_pad = T // CMP_STRIDE
    n_sel = T // SEL_BLOCK
    assert n_sel <= LANES - HEAD_DIM - 1 and n_cmp_pad == LANES
    starts = np.arange(n_cmp_pad) * CMP_STRIDE
    sid = np.arange(n_sel)
    overlap = ((starts[:, None] < (sid[None, :] + 1) * SEL_BLOCK)
               & (starts[:, None] + CMP_BLOCK > sid[None, :] * SEL_BLOCK)
               & (starts[:, None] + CMP_BLOCK <= T))
    place = np.zeros((n_sel, LANES), np.float32)
    place[sid, HEAD_DIM + sid] = 1.0
    w2p = jnp.pad(cmp_w2, ((0, 0), (0, 0), (0, LANES - HEAD_DIM)))

    def slab(br, kv):
        base = SEG_AKV // LANES + (br * 2 + kv) * 2
        return pl.BlockSpec((1, T, LANES), lambda h, b, i: (b, 0, base + h))

    def full(shape):
        return pl.BlockSpec(shape, lambda h, b, i: (0,) * len(shape))

    rows = A_GROUP * AT
    return pl.pallas_call(
        _nsa_kernel,
        grid=(A_KV_HEADS, B, nq),
        in_specs=[pl.BlockSpec((1, AT, A_GROUP * LANES), lambda h, b, i: (b, i, h)),
                  slab(0, 0), slab(0, 1), slab(1, 0), slab(1, 1), slab(2, 0), slab(2, 1),
                  pl.BlockSpec((1, AT, LANES), lambda h, b, i: (b, i, SEG_GATE // LANES + h)),
                  pl.BlockSpec((A_GROUP, 4, AT, AT), lambda h, b, i: (h, 0, 0, 0)),
                  pl.BlockSpec((A_GROUP, AT, LANES), lambda h, b, i: (h, i, 0)),
                  full((n_sel, n_cmp_pad)), full((n_sel, LANES)),
                  full((2, CMP_BLOCK, HEAD_DIM, HEAD_DIM)), full((2, HEAD_DIM, LANES)),
                  full((2, CMP_BLOCK, HEAD_DIM)), full((1, LANES)), full((1, LANES))],
        out_specs=pl.BlockSpec((1, AT, A_GROUP * LANES), lambda h, b, i: (b, i, h)),
        out_shape=jax.ShapeDtypeStruct((B, T, A_HEADS * LANES), F32),
        scratch_shapes=[pltpu.VMEM((n_cmp_pad, LANES), BF16), pltpu.VMEM((n_cmp_pad, LANES), BF16),
                        pltpu.VMEM((T, LANES), BF16), pltpu.VMEM((T, LANES), BF16),
                        pltpu.VMEM((T, LANES), BF16), pltpu.VMEM((T, LANES), BF16),
                        pltpu.VMEM((rows, LANES), BF16), pltpu.VMEM((rows, LANES), BF16),
                        pltpu.VMEM((2, rows // 2, AT), F32),
                        pltpu.VMEM((rows, LANES), F32), pltpu.VMEM((rows, LANES), F32),
                        pltpu.VMEM((rows, LANES), F32), pltpu.VMEM((rows, LANES), F32)],
        compiler_params=_params("parallel", "parallel", "arbitrary"),
        name="nsa",
    )(proj, proj, proj, proj, proj, proj, proj, proj, tiles, biasc,
      jnp.asarray(overlap.T, BF16), jnp.asarray(place, BF16), cmp_w1, w2p, cmp_pe,
      _pad_lanes(q_norm, LANES), _pad_lanes(k_norm, LANES))


MLSTM_NB = 2
MLSTM_TB = 512


def _mlstm4_kernel(qk_ref, v_ref, o_ref, gi_ref, gf_ref, cw_ref, bi_ref, bf_ref, gn_ref, tril_ref, y_ref,
                   xpad_s, c_s, n_s, m_s):
    nb, tb = qk_ref.shape[0], qk_ref.shape[1]
    L = MLSTM_CHUNK
    W = B_HEADS * HEAD_DIM

    @pl.when(pl.program_id(1) == 0)
    def _init():
        xpad_s[:, tb:tb + 8, :] = jnp.zeros((nb, 8, xpad_s.shape[2]), F32)
        c_s[...] = jnp.zeros(c_s.shape, F32)
        n_s[...] = jnp.zeros(n_s.shape, F32)
        m_s[...] = jnp.zeros(m_s.shape, F32)

    for bb in range(nb):
        xpad_s[bb, 0:8, :] = xpad_s[bb, tb:tb + 8, :]
        xpad_s[bb, 8:8 + tb, :] = qk_ref[bb]
    lane_head = lax.shift_right_logical(lax.broadcasted_iota(jnp.int32, (1, W), 1), 6)
    seg = [lane_head == a for a in range(B_HEADS)]
    row_l = lax.broadcasted_iota(jnp.int32, (L, W), 0)
    lane_l = lax.broadcasted_iota(jnp.int32, (L, W), 1) & (HEAD_DIM - 1)
    tri = lane_l <= row_l
    diag = lane_l == row_l
    blockdiag = (lax.shift_right_logical(lax.broadcasted_iota(jnp.int32, (W, W), 0), 6)
                 == lax.shift_right_logical(lax.broadcasted_iota(jnp.int32, (W, W), 1), 6))
    kscale = HEAD_DIM ** -0.5

    def seg_reduce(x, op, fill):
        out = None
        for a in range(B_HEADS):
            r = op(jnp.where(seg[a], x, fill)[:, (a // 2) * LANES:(a // 2 + 1) * LANES], axis=-1, keepdims=True)
            out = r if out is None else jnp.where(seg[a], r, out)
        return out

    def chunk_one(bb, c):
        r0 = pl.multiple_of(c * L, L)
        win = xpad_s[bb, pl.ds(r0, L + 8), :]
        conv = cw_ref[CONV_K - 1:CONV_K, :] * win[8:8 + L]
        for j in range(CONV_K - 1):
            conv = conv + cw_ref[j:j + 1, :] * pltpu.roll(win, CONV_K - 1 - j, axis=0)[8:8 + L]
        qk = conv * jax.nn.sigmoid(conv)
        q4 = qk[:, :W]
        k4 = qk[:, W:] * kscale
        v4 = v_ref[bb, pl.ds(r0, L), :]
        i4 = gi_ref[bb, pl.ds(r0, L), :] + bi_ref[...]
        logf = jax.nn.log_sigmoid(gf_ref[bb, pl.ds(r0, L), :] + bf_ref[...])
        lh, lm, ll = _split3(logf)
        cum = _dot(tril_ref[...], jnp.concatenate([lh, lm, ll], axis=1))
        b4 = cum[:, :W] + cum[:, W:2 * W] + cum[:, 2 * W:]
        c_old, n_old, m_prev = c_s[bb], n_s[bb], m_s[bb][0:1, :]
        r4 = jnp.sum(jnp.where(diag, i4 - b4, 0.0), axis=0, keepdims=True)
        dmat = jnp.where(tri, b4 + r4, -jnp.inf)
        m_inter = b4 + m_prev
        m_j = jnp.maximum(m_inter, seg_reduce(dmat, jnp.max, -jnp.inf))
        isc = jnp.exp(m_inter - m_j)
        k_heads = jnp.concatenate([jnp.where(seg[a], k4, 0.0) for a in range(B_HEADS)], axis=0)
        v_heads = jnp.concatenate([jnp.where(seg[a], v4, 0.0) for a in range(B_HEADS)], axis=0)
        s = _dot_nt(q4, k_heads) * jnp.exp(dmat - m_j)
        den = seg_reduce(s, jnp.sum, 0.0) + isc * seg_reduce(q4 * n_old, jnp.sum, 0.0)
        num = _dot(s, v_heads) + isc * _dot_nt(q4, c_old)
        h = num / jnp.maximum(jnp.abs(den), jnp.exp(-m_j))
        b_last = b4[L - 1:L, :]
        w_log = b_last - b4 + i4
        m_new = jnp.maximum(b_last + m_prev, jnp.max(w_log, axis=0, keepdims=True))
        w4 = jnp.exp(w_log - m_new)
        dec = jnp.exp(b_last + m_prev - m_new)
        c_s[bb] = dec * c_old + jnp.where(blockdiag, _dot_tn(v4 * w4, k4), 0.0)
        n_s[bb] = dec * n_old + jnp.sum(w4 * k4, axis=0, keepdims=True)
        m_s[bb] = jnp.broadcast_to(m_new, m_s.shape[1:])
        ms = seg_reduce(h * h, jnp.sum, 0.0) * (1.0 / HEAD_DIM)
        hn = h * lax.rsqrt(ms + EPS) * gn_ref[...]
        y_ref[bb, pl.ds(r0, L), :] = jax.nn.sigmoid(o_ref[bb, pl.ds(r0, L), :]) * hn

    def chunk(c, carry):
        for bb in range(nb):
            chunk_one(bb, c)
        return carry

    lax.fori_loop(0, tb // L, chunk, 0)


def _mlstm4(proj, conv_w, i_bias, f_bias, out_norm):
    B, T, _ = proj.shape
    W = B_HEADS * HEAD_DIM
    nb, tb = MLSTM_NB, MLSTM_TB
    tril = jnp.asarray(np.tril(np.ones((MLSTM_CHUNK, MLSTM_CHUNK))), BF16)

    def full(shape):
        return pl.BlockSpec(shape, lambda b, t: (0,) * len(shape))

    def seg(width, off):
        return pl.BlockSpec((nb, tb, width), lambda b, t: (b, t, off // width))

    return pl.pallas_call(
        _mlstm4_kernel,
        grid=(B // nb, T // tb),
        in_specs=[seg(2 * W, SEG_BQK), seg(W, SEG_BV), seg(W, SEG_BO), seg(W, SEG_BIF), seg(W, SEG_BIF + W),
                  full((CONV_K, 2 * W)), full((1, W)), full((1, W)), full((1, W)),
                  full((MLSTM_CHUNK, MLSTM_CHUNK))],
        out_specs=pl.BlockSpec((nb, tb, W), lambda b, t: (b, t, 0)),
        out_shape=jax.ShapeDtypeStruct((B, T, W), F32),
        scratch_shapes=[pltpu.VMEM((nb, tb + 8, 2 * W), F32),
                        pltpu.VMEM((nb, W, W), F32),
                        pltpu.VMEM((nb, 1, W), F32),
                        pltpu.VMEM((nb, 8, W), F32)],
        compiler_params=_params("parallel", "arbitrary"),
        name="mlstm",
    )(proj, proj, proj, proj, proj, conv_w, jnp.repeat(i_bias, HEAD_DIM).reshape(1, W),
      jnp.repeat(f_bias, HEAD_DIM).reshape(1, W), out_norm.reshape(1, W), tril)


def _rope_kernel(pos_ref, invf_ref, sgn_ref, cs_ref, sn_ref):
    ang = pos_ref[...].astype(F32) * invf_ref[...]
    cs_ref[...] = jnp.cos(ang)
    sn_ref[...] = jnp.sin(ang) * sgn_ref[...]


def _rope_tables(pos):
    N = pos.shape[0]
    tm = 1024
    half = C_ROPE // 2
    inv = 1.0 / (ROPE_THETA ** (np.arange(half, dtype=np.float32) * 2.0 / C_ROPE))
    invf = np.zeros((1, LANES), np.float32)
    sgn = np.zeros((1, LANES), np.float32)
    invf[0, C_NOPE:C_NOPE + C_ROPE] = np.concatenate([inv, inv])
    sgn[0, C_NOPE:C_NOPE + half] = -1.0
    sgn[0, C_NOPE + half:C_NOPE + C_ROPE] = 1.0
    row = pl.BlockSpec((1, LANES), lambda i: (0, 0))
    tile = pl.BlockSpec((tm, LANES), lambda i: (i, 0))
    return pl.pallas_call(
        _rope_kernel,
        grid=(N // tm,),
        in_specs=[pl.BlockSpec((tm, 1), lambda i: (i, 0)), row, row],
        out_specs=[tile, tile],
        out_shape=[jax.ShapeDtypeStruct((N, LANES), F32), jax.ShapeDtypeStruct((N, LANES), F32)],
        compiler_params=_params("parallel"),
        name="rope_tables",
    )(pos, jnp.asarray(invf), jnp.asarray(sgn))


def _mla_prep_kernel(dq_ref, dkv_ref, kr_ref, cs_ref, sn_ref, wq_ref, wk_ref, wv_ref, qa_ref, kva_ref,
                     gq_ref, gk_ref, gkr_ref, q_out, k_out, v_out):
    lane = lax.broadcasted_iota(jnp.int32, (1, LANES), 1)
    nope = lane < C_NOPE
    ropem = (lane >= C_NOPE) & (lane < C_NOPE + C_ROPE)
    first = lane < C_NOPE + C_ROPE // 2
    cs = cs_ref[...]
    sn = sn_ref[...]

    def rope(x):
        partner = jnp.where(first, pltpu.roll(x, LANES - C_ROPE // 2, axis=1), pltpu.roll(x, C_ROPE // 2, axis=1))
        return x * cs + partner * sn

    scale = (C_NOPE + C_ROPE) ** -0.5 * LOG2E
    qn = _rms_lanes(dq_ref[...], qa_ref[...], Q_RANK).astype(BF16)
    q = _dot(qn, wq_ref[...])
    kvn = _rms_lanes(dkv_ref[...], kva_ref[...], KV_RANK).astype(BF16)
    kk = _dot(kvn, wk_ref[...])
    vv = _dot(kvn, wv_ref[...])
    lane_v = lax.broadcasted_iota(jnp.int32, vv.shape, 1)
    v_out[...] = jnp.where((lane_v & (LANES - 1)) == C_V, 1.0, vv).astype(BF16)
    kr = kr_ref[...]
    krn = kr * lax.rsqrt(jnp.sum(kr * kr, axis=-1, keepdims=True) * (1.0 / C_ROPE) + EPS) * gkr_ref[...]
    krr = rope(krn)
    for h in range(C_HEADS):
        sl = slice(h * LANES, (h + 1) * LANES)
        qh = q[:, sl]
        sq = qh * qh
        msn = jnp.sum(jnp.where(nope, sq, 0.0), axis=-1, keepdims=True) * (1.0 / C_NOPE)
        msr = jnp.sum(jnp.where(ropem, sq, 0.0), axis=-1, keepdims=True) * (1.0 / C_ROPE)
        qhn = qh * jnp.where(nope, lax.rsqrt(msn + EPS), lax.rsqrt(msr + EPS)) * gq_ref[...]
        q_out[:, sl] = (rope(qhn) * scale).astype(BF16)
        kh = kk[:, sl]
        msk = jnp.sum(kh * kh, axis=-1, keepdims=True) * (1.0 / C_NOPE)
        k_out[:, sl] = (kh * lax.rsqrt(msk + EPS) * gk_ref[...] + krr).astype(BF16)


def _mla_prep(proj2, rope_cs, rope_sn, q_a_norm, w_uq, kv_a_norm, w_ukv, qk_norm):
    N = proj2.shape[0]
    tm = 512
    qd = C_NOPE + C_ROPE
    qmap = np.full((C_HEADS * LANES,), -1, np.int64)
    kmap = np.full((C_HEADS * LANES,), -1, np.int64)
    vmap = np.full((C_HEADS * LANES,), -1, np.int64)
    for h in range(C_HEADS):
        qmap[h * LANES:h * LANES + qd] = h * qd + np.arange(qd)
        kmap[h * LANES:h * LANES + C_NOPE] = h * (C_NOPE + C_V) + np.arange(C_NOPE)
        vmap[h * LANES:h * LANES + C_V] = h * (C_NOPE + C_V) + C_NOPE + np.arange(C_V)
    wq = jnp.pad(_take_cols(w_uq, qmap), ((0, 256 - Q_RANK), (0, 0))).astype(BF16)
    wk = _take_cols(w_ukv, kmap).astype(BF16)
    wv = _take_cols(w_ukv, vmap).astype(BF16)
    gkr = jnp.pad(qk_norm[1, C_NOPE:].reshape(1, -1), ((0, 0), (C_NOPE, LANES - C_NOPE - C_ROPE)))

    def full(shape):
        return pl.BlockSpec(shape, lambda i: (0,) * len(shape))

    return pl.pallas_call(
        _mla_prep_kernel,
        grid=(N // tm,),
        in_specs=[pl.BlockSpec((tm, 256), lambda i: (i, SEG_CDQ // 256)),
                  pl.BlockSpec((tm, LANES), lambda i: (i, SEG_CDKV // LANES)),
                  pl.BlockSpec((tm, LANES), lambda i: (i, SEG_CKR // LANES)),
                  pl.BlockSpec((tm, LANES), lambda i: (i, 0)), pl.BlockSpec((tm, LANES), lambda i: (i, 0)),
                  full((256, C_HEADS * LANES)), full((KV_RANK, C_HEADS * LANES)), full((KV_RANK, C_HEADS * LANES)),
                  full((1, 256)), full((1, LANES)), full((1, LANES)), full((1, LANES)), full((1, LANES))],
        out_specs=[pl.BlockSpec((tm, C_HEADS * LANES), lambda i: (i, 0)),
                   pl.BlockSpec((tm, C_HEADS * LANES), lambda i: (i, 0)),
                   pl.BlockSpec((tm, C_HEADS * LANES), lambda i: (i, 0))],
        out_shape=[jax.ShapeDtypeStruct((N, C_HEADS * LANES), BF16),
                   jax.ShapeDtypeStruct((N, C_HEADS * LANES), BF16),
                   jax.ShapeDtypeStruct((N, C_HEADS * LANES), BF16)],
        compiler_params=_params("parallel"),
        name="mla_prep",
    )(proj2, proj2, proj2, rope_cs, rope_sn, wq, wk, wv,
      _pad_lanes(q_a_norm, 256), kv_a_norm.reshape(1, -1), _pad_lanes(qk_norm[0], LANES),
      _pad_lanes(qk_norm[1, :C_NOPE], LANES), gkr)


MLA_Q = 2 * AT


def _mla_attn_kernel(q_ref, k_ref, v_ref, causal_ref, o_ref, s_s, m_s, acc_s):
    qi = pl.program_id(2)
    m_s[...] = jnp.full(m_s.shape, NEG, F32)
    acc_s[...] = jnp.zeros(acc_s.shape, F32)
    heads = [slice(hh * LANES, (hh + 1) * LANES) for hh in range(2)]
    per_head = MLA_Q // RB
    tiles_per_q = MLA_Q // AT

    def key_rows(kc):
        return pl.ds(pl.multiple_of(kc * AT, AT), AT)

    def scores(kc):
        return [_dot_nt(q_ref[0, :, sl], k_ref[0, key_rows(kc), sl]) for sl in heads]

    def values(kc):
        return [v_ref[0, key_rows(kc), sl] for sl in heads]

    for d in range(tiles_per_q):
        kc = tiles_per_q * qi + d
        _softmax_pv(scores(kc), values(kc),
                    lambda r, d=d: causal_ref[d, pl.ds((r % per_head) * RB, RB), :], m_s, acc_s)

    for h, s in enumerate(scores(0)):
        s_s[h] = s

    def body(kc, carry):
        s_next = scores(kc + 1)
        _softmax_pv([s_s[0], s_s[1]], values(kc), None, m_s, acc_s)
        for h, s in enumerate(s_next):
            s_s[h] = s
        return carry

    lax.fori_loop(0, tiles_per_q * qi, body, 0)
    lane = lax.broadcasted_iota(jnp.int32, (MLA_Q, LANES), 1)
    outs = []
    for hh in range(2):
        a = acc_s[hh * MLA_Q:(hh + 1) * MLA_Q, :]
        outs.append(jnp.where(lane < C_V, a / a[:, C_V:C_V + 1], 0.0))
    o_ref[0] = outs[0] + pltpu.roll(outs[1], C_V, axis=1)


def _mla_attn(q, k, v):
    B, T, _ = q.shape
    qpos = np.arange(MLA_Q)[None, :, None]
    kpos = (np.arange(MLA_Q // AT)[:, None, None] * AT + np.arange(AT)[None, None, :])
    causal = np.where(qpos >= kpos, 0.0, NEG).astype(np.float32)
    pair = pl.BlockSpec((1, T, 2 * LANES), lambda b, p, i: (b, 0, p))
    return pl.pallas_call(
        _mla_attn_kernel,
        grid=(B, C_HEADS // 2, T // MLA_Q),
        in_specs=[pl.BlockSpec((1, MLA_Q, 2 * LANES), lambda b, p, i: (b, i, p)), pair, pair,
                  pl.BlockSpec((MLA_Q // AT, MLA_Q, AT), lambda b, p, i: (0, 0, 0))],
        out_specs=pl.BlockSpec((1, MLA_Q, LANES), lambda b, p, i: (b, i, p)),
        out_shape=jax.ShapeDtypeStruct((B, T, C_HEADS * C_V), F32),
        scratch_shapes=[pltpu.VMEM((2, MLA_Q, AT), F32), pltpu.VMEM((2 * MLA_Q, LANES), F32),
                        pltpu.VMEM((2 * MLA_Q, LANES), F32)],
        compiler_params=_params("parallel", "parallel", "arbitrary"),
        name="mla_attn",
    )(q, k, v, jnp.asarray(causal))


def _outproj_kernel(x_ref, ya_ref, yb_ref, yc_ref, wa_ref, wb_ref, wc_ref, ga_ref, gc_ref, g2_ref,
                    wr_ref, br_ref, x1_ref, h2_ref, ids_ref, wts_ref, cnt_ref):
    ya = _rms_lanes(ya_ref[...], ga_ref[...], A_HEADS * HEAD_DIM).astype(BF16)
    yc = _rms_lanes(yc_ref[...], gc_ref[...], C_HEADS * C_V).astype(BF16)
    x1 = (x_ref[...] + _dot(ya, wa_ref[...]) + _dot(yb_ref[...].astype(BF16), wb_ref[...])
          + _dot(yc, wc_ref[...]))
    x1_ref[...] = x1
    h2 = _rms_lanes(x1, g2_ref[...], D_MODEL)
    h2_ref[...] = _pack_bf16_pairs(h2[:, :D_MODEL // 2], h2[:, D_MODEL // 2:])
    h2_hi = h2.astype(BF16)
    h2_mid = (h2 - h2_hi.astype(F32)).astype(BF16)
    logits = (_dot(h2_hi, wr_ref[0]) + _dot(h2_mid, wr_ref[0]) + _dot(h2_hi, wr_ref[1])) + br_ref[...]
    tm = logits.shape[0]
    lane = lax.broadcasted_iota(jnp.int32, (tm, LANES), 1)
    is_g = (lane >= N_EXPERTS) & (lane < N_EXPERTS + N_GROUPS)
    gl = jnp.where(is_g, logits, NEG)
    ge = jnp.where(is_g, jnp.exp(gl - jnp.max(gl, axis=-1, keepdims=True)), 0.0)
    gp = jnp.where(is_g, ge / jnp.sum(ge, axis=-1, keepdims=True), -1.0)
    g_top = jnp.max(gp, axis=-1, keepdims=True)
    g_idx = jnp.min(jnp.where(gp == g_top, lane, 2 * LANES), axis=-1, keepdims=True) - N_EXPERTS
    in_grp = (lane < N_EXPERTS) & (lax.shift_right_logical(lane, 3) == g_idx)
    el = jnp.where(in_grp, logits, NEG)
    ee = jnp.where(in_grp, jnp.exp(el - jnp.max(el, axis=-1, keepdims=True)), 0.0)
    ep = jnp.where(in_grp, ee / jnp.sum(ee, axis=-1, keepdims=True), -1.0)
    v1 = jnp.max(ep, axis=-1, keepdims=True)
    i1 = jnp.min(jnp.where(ep == v1, lane, 2 * LANES), axis=-1, keepdims=True)
    ep2 = jnp.where(lane == i1, -1.0, ep)
    v2 = jnp.max(ep2, axis=-1, keepdims=True)
    i2 = jnp.min(jnp.where(ep2 == v2, lane, 2 * LANES), axis=-1, keepdims=True)
    tot = v1 + v2
    ids_ref[...] = jnp.where(lane == 0, i1, jnp.where(lane == 1, i2, 0))
    wts_ref[...] = jnp.where(lane == 0, v1 / tot * g_top, jnp.where(lane == 1, v2 / tot * g_top, 0.0))
    per_expert = jnp.sum(jnp.where((lane == i1) | (lane == i2), 1.0, 0.0), axis=0, keepdims=True)
    cnt_ref[...] = jnp.broadcast_to(per_expert, cnt_ref.shape)


def _outproj(x2, ya, yb, yc, w_out, nsa_out_norm, mla_out_norm, norm2_g, w_group, b_group, w_expert, b_expert):
    N = x2.shape[0]
    tm = 512
    amap = np.full((A_HEADS * LANES,), -1, np.int64)
    for hd in range(A_HEADS):
        amap[hd * LANES:hd * LANES + HEAD_DIM] = hd * HEAD_DIM + np.arange(HEAD_DIM)
    wa = _take_rows(w_out[:384], amap).astype(BF16)
    wb = w_out[384:640].astype(BF16)
    wc = w_out[640:].astype(BF16)
    ga = _take_cols(nsa_out_norm.reshape(1, -1), amap)
    wr = jnp.pad(jnp.concatenate([w_expert, w_group], axis=1), ((0, 0), (0, LANES - N_EXPERTS - N_GROUPS)))
    wr_hi = wr.astype(BF16)
    wr = jnp.stack([wr_hi, (wr - wr_hi.astype(F32)).astype(BF16)])
    br = _pad_lanes(jnp.concatenate([b_expert, b_group]), LANES)

    def full(shape):
        return pl.BlockSpec(shape, lambda i: (0,) * len(shape))

    def rows(w):
        return pl.BlockSpec((tm, w), lambda i: (i, 0))

    return pl.pallas_call(
        _outproj_kernel,
        grid=(N // tm,),
        in_specs=[rows(D_MODEL), rows(A_HEADS * LANES), rows(256), rows(384),
                  full((A_HEADS * LANES, D_MODEL)), full((256, D_MODEL)), full((384, D_MODEL)),
                  full((1, A_HEADS * LANES)), full((1, 384)), full((1, D_MODEL)),
                  full((2, D_MODEL, LANES)), full((1, LANES))],
        out_specs=[rows(D_MODEL), rows(D_MODEL // 2), rows(LANES), rows(LANES),
                   pl.BlockSpec((8, LANES), lambda i: (i, 0))],
        out_shape=[jax.ShapeDtypeStruct((N, D_MODEL), F32), jax.ShapeDtypeStruct((N, D_MODEL // 2), jnp.uint32),
                   jax.ShapeDtypeStruct((N, LANES), jnp.int32), jax.ShapeDtypeStruct((N, LANES), F32),
                   jax.ShapeDtypeStruct((N // tm * 8, LANES), F32)],
        compiler_params=_params("parallel"),
        name="outproj_router",
    )(x2, ya, yb, yc, wa, wb, wc, ga, mla_out_norm.reshape(1, -1), norm2_g.reshape(1, -1), wr, br)


def _rank_kernel(ids_ref, off_ref, tril_ref, pos_ref, carry_s):
    i = pl.program_id(0)

    @pl.when(i == 0)
    def _():
        carry_s[...] = jnp.broadcast_to(off_ref[...], carry_s.shape)

    ids = ids_ref[...]
    tm = ids.shape[0]
    lane = lax.broadcasted_iota(jnp.int32, (tm, LANES), 1)
    i1 = ids[:, 0:1]
    i2 = ids[:, 1:2]
    oh = jnp.where((lane == i1) | (lane == i2), 1.0, 0.0)
    incl = _dot(tril_ref[...], oh.astype(BF16)) + carry_s[0:1, :]
    excl = incl - oh
    p1 = jnp.sum(jnp.where(lane == i1, excl, 0.0), axis=-1, keepdims=True)
    p2 = jnp.sum(jnp.where(lane == i2, excl, 0.0), axis=-1, keepdims=True)
    pos_ref[...] = jnp.where(lane == 0, p1, jnp.where(lane == 1, p2, 0.0)).astype(jnp.int32)
    carry_s[...] = jnp.broadcast_to(incl[tm - 1:tm, :], carry_s.shape)


def _moe_positions(ids, off_row):
    N = ids.shape[0]
    tm = 512
    tril = jnp.asarray(np.tril(np.ones((tm, tm))), BF16)
    return pl.pallas_call(
        _rank_kernel,
        grid=(N // tm,),
        in_specs=[pl.BlockSpec((tm, LANES), lambda i: (i, 0)), pl.BlockSpec((1, LANES), lambda i: (0, 0)),
                  pl.BlockSpec((tm, tm), lambda i: (0, 0))],
        out_specs=pl.BlockSpec((tm, LANES), lambda i: (i, 0)),
        out_shape=jax.ShapeDtypeStruct((N, LANES), jnp.int32),
        scratch_shapes=[pltpu.VMEM((8, LANES), F32)],
        compiler_params=_params("arbitrary"),
        name="moe_positions",
    )(ids, off_row, tril)


DISPATCH_TM = 1024


DMA_UNROLL = 8


def _dispatch_kernel(off_ref, tp_ref, nu_ref, p1_ref, p2_ref, h_ref, xs_ref, zero_s, sem, zsem):
    n_tiles = xs_ref.shape[0] // ROW_TILE

    @pl.when(pl.program_id(0) == 0)
    def _zero_fill():
        zero_s[...] = jnp.zeros(zero_s.shape, zero_s.dtype)

        def tile_copy(row0):
            return pltpu.make_async_copy(
                zero_s, xs_ref.at[pl.ds(pl.multiple_of(row0, ROW_TILE), ROW_TILE), :], zsem)

        for e in range(N_EXPERTS):
            @pl.when(tp_ref[e] > 0)
            def _start(e=e):
                tile_copy(off_ref[e] + (tp_ref[e] - 1) * ROW_TILE).start()

        def tail_start(g, carry):
            tile_copy(g * ROW_TILE).start()
            return carry

        lax.fori_loop(nu_ref[0], n_tiles, tail_start, 0)
        for e in range(N_EXPERTS):
            @pl.when(tp_ref[e] > 0)
            def _wait():
                tile_copy(0).wait()

        def tail_wait(g, carry):
            tile_copy(0).wait()
            return carry

        lax.fori_loop(nu_ref[0], n_tiles, tail_wait, 0)

    def row_copy(i, pos):
        return pltpu.make_async_copy(h_ref.at[pl.ds(i, 1), :], xs_ref.at[pl.ds(pos, 1), :], sem)

    def issue(j, carry):
        for u in range(DMA_UNROLL):
            i = j * DMA_UNROLL + u
            row_copy(i, p1_ref[i]).start(priority=0)
            row_copy(i, p2_ref[i]).start(priority=1)
        return carry

    lax.fori_loop(0, DISPATCH_TM // DMA_UNROLL, issue, 0)
    whole = pltpu.make_async_copy(h_ref, xs_ref.at[pl.ds(0, DISPATCH_TM), :], sem)
    whole.wait()
    whole.wait()


def _moe_dispatch(h2, off, tiles_per, n_used, pos1, pos2, n_rows):
    N = h2.shape[0]
    smem1d = pl.BlockSpec((DISPATCH_TM,), lambda i: (i,), memory_space=pltpu.SMEM)
    smem = pl.BlockSpec(memory_space=pltpu.SMEM)
    return pl.pallas_call(
        _dispatch_kernel,
        grid=(N // DISPATCH_TM,),
        in_specs=[smem, smem, smem, smem1d, smem1d,
                  pl.BlockSpec((DISPATCH_TM, D_MODEL // 2), lambda i: (i, 0))],
        out_specs=pl.BlockSpec(memory_space=pl.ANY),
        out_shape=jax.ShapeDtypeStruct((n_rows, D_MODEL // 2), jnp.uint32),
        scratch_shapes=[pltpu.VMEM((ROW_TILE, D_MODEL // 2), jnp.uint32), pltpu.SemaphoreType.DMA(()),
                        pltpu.SemaphoreType.DMA(())],
        compiler_params=_params("arbitrary"),
        name="moe_dispatch",
    )(off, tiles_per, n_used, pos1, pos2, h2)


def _expert_kernel(te_ref, nu_ref, x_ref, wg_ref, wu_ref, wd_ref, y_ref):
    g = pl.program_id(0)

    @pl.when(g < nu_ref[0])
    def _():
        x = jnp.concatenate(_unpack_bf16_pairs(x_ref[...]), axis=1)
        a = _dot(x, wg_ref[0, 0].astype(BF16))
        u = _dot(x, wu_ref[0, 0].astype(BF16))
        h = (a * jax.nn.sigmoid(a) * u).astype(BF16)
        y_ref[...] = _dot(h, wd_ref[0, 0].astype(BF16))

    @pl.when(g >= nu_ref[0])
    def _():
        y_ref[...] = jnp.zeros(y_ref.shape, F32)


def _moe_experts(xs, tile_expert, n_used, layer, w_gate, w_up, w_down):
    R = xs.shape[0]
    G = R // ROW_TILE

    def row_map(g, te, nu):
        return (jnp.minimum(g, nu[0] - 1), 0)

    def w_map(g, te, nu):
        return (layer, te[jnp.minimum(g, nu[0] - 1)], 0, 0)

    return pl.pallas_call(
        _expert_kernel,
        grid_spec=pltpu.PrefetchScalarGridSpec(
            num_scalar_prefetch=2,
            grid=(G,),
            in_specs=[pl.BlockSpec((ROW_TILE, D_MODEL // 2), row_map),
                      pl.BlockSpec((1, 1, D_MODEL, D_EXPERT), w_map),
                      pl.BlockSpec((1, 1, D_MODEL, D_EXPERT), w_map),
                      pl.BlockSpec((1, 1, D_EXPERT, D_MODEL), w_map)],
            out_specs=pl.BlockSpec((ROW_TILE, D_MODEL), lambda g, te, nu: (g, 0))),
        out_shape=jax.ShapeDtypeStruct((R, D_MODEL), F32),
        compiler_params=_params("arbitrary"),
        name="moe_experts",
    )(tile_expert, n_used, xs, w_gate, w_up, w_down)


COMBINE_TM = 256


def _combine_kernel(p1_ref, p2_ref, x1_ref, wts_ref, ys_ref, o_ref, b1_s, b2_s, sem):
    def row_copy(buf, i, pos):
        return pltpu.make_async_copy(ys_ref.at[pl.ds(pos, 1), :], buf.at[pl.ds(i, 1), :], sem)

    def issue(j, carry):
        for u in range(DMA_UNROLL):
            i = j * DMA_UNROLL + u
            row_copy(b1_s, i, p1_ref[i]).start(priority=0)
            row_copy(b2_s, i, p2_ref[i]).start(priority=1)
        return carry

    lax.fori_loop(0, COMBINE_TM // DMA_UNROLL, issue, 0)
    pltpu.make_async_copy(ys_ref.at[pl.ds(0, COMBINE_TM), :], b1_s, sem).wait()
    pltpu.make_async_copy(ys_ref.at[pl.ds(0, COMBINE_TM), :], b2_s, sem).wait()
    w = wts_ref[...]
    o_ref[...] = x1_ref[...] + w[:, 0:1] * b1_s[...] + w[:, 1:2] * b2_s[...]


def _moe_combine(x1, wts, ys, pos1, pos2):
    N = x1.shape[0]
    smem1d = pl.BlockSpec((COMBINE_TM,), lambda i: (i,), memory_space=pltpu.SMEM)
    return pl.pallas_call(
        _combine_kernel,
        grid=(N // COMBINE_TM,),
        in_specs=[smem1d, smem1d,
                  pl.BlockSpec((COMBINE_TM, D_MODEL), lambda i: (i, 0)),
                  pl.BlockSpec((COMBINE_TM, LANES), lambda i: (i, 0)),
                  pl.BlockSpec(memory_space=pl.ANY)],
        out_specs=pl.BlockSpec((COMBINE_TM, D_MODEL), lambda i: (i, 0)),
        out_shape=jax.ShapeDtypeStruct((N, D_MODEL), F32),
        scratch_shapes=[pltpu.VMEM((COMBINE_TM, D_MODEL), F32), pltpu.VMEM((COMBINE_TM, D_MODEL), F32),
                        pltpu.SemaphoreType.DMA(())],
        compiler_params=_params("arbitrary"),
        name="moe_combine",
    )(pos1, pos2, x1, wts, ys)


def _moe(x1, h2, ids, wts, tile_counts, layer, w_gate, w_up, w_down):
    N = x1.shape[0]
    n_rows = 2 * N + N_EXPERTS * ROW_TILE
    counts = jnp.sum(tile_counts[::8, :N_EXPERTS], axis=0).astype(jnp.int32)
    tiles_per = (counts + ROW_TILE - 1) // ROW_TILE
    tile_end = jnp.cumsum(tiles_per)
    off = ((tile_end - tiles_per) * ROW_TILE).astype(jnp.int32)
    n_used = tile_end[-1:].astype(jnp.int32)
    tile_ids = jnp.arange(n_rows // ROW_TILE, dtype=jnp.int32)
    tile_expert = jnp.minimum(jnp.sum(tile_end[None, :] <= tile_ids[:, None], axis=1), N_EXPERTS - 1).astype(jnp.int32)
    pos = _moe_positions(ids, _pad_lanes(off.astype(F32), LANES))
    pos1, pos2 = pos[:, 0], pos[:, 1]
    xs = _moe_dispatch(h2, off, tiles_per.astype(jnp.int32), n_used, pos1, pos2, n_rows)
    ys = _moe_experts(xs, tile_expert, n_used, layer, w_gate, w_up, w_down)
    return _moe_combine(x1, wts, ys, pos1, pos2)


def kernel(x, positions, rel_bias, norm1_g, w_in, nsa_cmp_pe, nsa_cmp_w1, nsa_cmp_w2, nsa_q_norm, nsa_k_norm, nsa_out_norm, mlstm_conv, mlstm_i_bias, mlstm_f_bias, mlstm_out_norm, mla_q_a_norm, mla_w_uq, mla_kv_a_norm, mla_w_ukv, mla_qk_norm, mla_out_norm, w_out, norm2_g, moe_w_group, moe_b_group, moe_w_expert, moe_b_expert, moe_w_gate, moe_w_up, moe_w_down):
    B, T, D = x.shape
    N = B * T
    depth = w_in.shape[0]
    colmap = _proj_colmap()
    tiles, biasc = _bias_tables(rel_bias, T)
    rope_cs, rope_sn = _rope_tables(positions.reshape(N, 1))
    x2 = x.reshape(N, D)
    for l in range(depth):
        w_proj = _take_cols(w_in[l], colmap).astype(BF16)
        proj2 = _inproj(x2, norm1_g[l].reshape(1, D), w_proj)
        proj = proj2.reshape(B, T, D_PROJ)
        ya = _nsa(proj, tiles, biasc, nsa_cmp_pe[l], nsa_cmp_w1[l], nsa_cmp_w2[l], nsa_q_norm[l], nsa_k_norm[l])
        yb = _mlstm4(proj, mlstm_conv[l], mlstm_i_bias[l], mlstm_f_bias[l], mlstm_out_norm[l])
        q, k, v = _mla_prep(proj2, rope_cs, rope_sn, mla_q_a_norm[l], mla_w_uq[l], mla_kv_a_norm[l], mla_w_ukv[l], mla_qk_norm[l])
        yc = _mla_attn(q.reshape(B, T, -1), k.reshape(B, T, -1), v.reshape(B, T, -1))
        x1, h2, ids, wts, tile_counts = _outproj(x2, ya.reshape(N, -1), yb.reshape(N, -1), yc.reshape(N, -1), w_out[l],
                                    nsa_out_norm[l], mla_out_norm[l], norm2_g[l], moe_w_group[l], moe_b_group[l],
                                    moe_w_expert[l], moe_b_expert[l])
        x2 = _moe(x1, h2, ids, wts, tile_counts, l, moe_w_gate, moe_w_up, moe_w_down)
    return x2.reshape(B, T, D)
```

```python
import functools
import math

import numpy as np
import jax
import jax.numpy as jnp
from jax import lax
from jax.experimental import pallas as pl
from jax.experimental.pallas import tpu as pltpu

F32 = jnp.float32
BF16 = jnp.bfloat16

D_MODEL = 1024
HEAD_DIM = 64
LANES = 128
A_HEADS, A_KV_HEADS, A_GROUP = 6, 2, 3
CMP_BLOCK, CMP_STRIDE, SEL_BLOCK, N_SELECT, WINDOW = 32, 16, 64, 16, 512
FORCE = 1e4
B_HEADS, CONV_K, MLSTM_CHUNK = 4, 4, 64
C_HEADS, C_NOPE, C_ROPE, C_V = 6, 64, 32, 64
Q_RANK, KV_RANK = 192, 128
ROPE_THETA = 10000.0
REL_BUCKETS, REL_MAX_DIST = 32, 128
N_GROUPS, EXPERTS_PER_GROUP, N_EXPERTS, D_EXPERT = 4, 8, 32, 256
EPS = 1e-6
NEG = -1e30

LOG2E = 1.4426950408889634

AT = 256
RB = 128
ROW_TILE = 512
VMEM_LIMIT = 48 * 1024 * 1024

SEG_AQ, SEG_AKV, SEG_GATE, SEG_BQK, SEG_BV, SEG_BO = 0, 768, 2304, 2560, 3072, 3328
SEG_CDQ, SEG_CDKV, SEG_CKR, SEG_BIF, D_PROJ = 3584, 3840, 3968, 4096, 4608


def _params(*sem):
    return pltpu.CompilerParams(dimension_semantics=sem, vmem_limit_bytes=VMEM_LIMIT)


def _dot(a, b):
    return jnp.dot(a, b, preferred_element_type=F32)


def _dot_nt(a, b):
    return lax.dot_general(a, b, (((1,), (1,)), ((), ())), preferred_element_type=F32)


def _dot_tn(a, b):
    return lax.dot_general(a, b, (((0,), (0,)), ((), ())), preferred_element_type=F32)


def _split3(x):
    hi = x.astype(BF16)
    r1 = x - hi.astype(F32)
    mid = r1.astype(BF16)
    lo = (r1 - mid.astype(F32)).astype(BF16)
    return hi, mid, lo


def _dot_exact_rhs(x, m_bf16):
    hi, mid, lo = _split3(x)
    return _dot(hi, m_bf16) + _dot(mid, m_bf16) + _dot(lo, m_bf16)


def _pack_bf16_pairs(a, b):
    ua = lax.bitcast_convert_type(a.astype(BF16).astype(F32), jnp.uint32)
    ub = lax.bitcast_convert_type(b.astype(BF16).astype(F32), jnp.uint32)
    return lax.shift_right_logical(ua, jnp.uint32(16)) | (ub & jnp.uint32(0xFFFF0000))


def _unpack_bf16_pairs(w):
    a = lax.bitcast_convert_type(lax.shift_left(w, jnp.uint32(16)), F32)
    b = lax.bitcast_convert_type(w & jnp.uint32(0xFFFF0000), F32)
    return a.astype(BF16), b.astype(BF16)


def _dot_f32(x, w):
    xh = x.astype(BF16)
    xm = (x - xh.astype(F32)).astype(BF16)
    wh = w.astype(BF16)
    wm = (w - wh.astype(F32)).astype(BF16)
    return _dot(xh, wh) + _dot(xm, wh) + _dot(xh, wm)


def _bucket_np(dist):
    n = np.maximum(dist, 0)
    exact = REL_BUCKETS // 2
    nf = np.maximum(n, 1).astype(np.float64)
    large = exact + (np.log(nf / exact) / math.log(REL_MAX_DIST / exact) * (REL_BUCKETS - exact)).astype(np.int64)
    large = np.minimum(large, REL_BUCKETS - 1)
    return np.where(n < exact, n, large).astype(np.int32)


def _bucket_tables(T):
    r = np.arange(AT)[:, None]
    c = np.arange(AT)[None, :]
    diag = np.where(r >= c, _bucket_np(r - c), -1)
    prev = _bucket_np(AT + r - c)
    far = np.where(c > r, _bucket_np(2 * AT + r - c), -1)
    tiles = np.stack([diag, prev, far]).astype(np.int32)
    tq = np.arange(T)[:, None]
    n = np.arange(LANES)[None, :]
    dist_c = tq - (n * CMP_STRIDE + CMP_BLOCK - 1)
    cmp_tbl = np.where(dist_c >= 0, _bucket_np(dist_c), -1).astype(np.int32)
    return tiles, cmp_tbl


def _proj_colmap():
    cm = np.full((D_PROJ,), -1, np.int64)
    for hd in range(A_HEADS):
        cm[SEG_AQ + hd * LANES: SEG_AQ + hd * LANES + HEAD_DIM] = hd * HEAD_DIM + np.arange(HEAD_DIM)
    for s in range(12):
        cm[SEG_AKV + s * LANES: SEG_AKV + s * LANES + HEAD_DIM] = 384 + s * HEAD_DIM + np.arange(HEAD_DIM)
    for h in range(A_KV_HEADS):
        cm[SEG_GATE + h * LANES + HEAD_DIM: SEG_GATE + h * LANES + HEAD_DIM + 9] = 1152 + h * 9 + np.arange(9)
    cm[SEG_BQK: SEG_BQK + 512] = 1170 + np.arange(512)
    cm[SEG_BV: SEG_BV + 256] = 1682 + np.arange(256)
    cm[SEG_BIF: SEG_BIF + 2 * B_HEADS * HEAD_DIM] = 1938 + np.arange(2 * B_HEADS * HEAD_DIM) // HEAD_DIM
    cm[SEG_BO: SEG_BO + 256] = 1946 + np.arange(256)
    cm[SEG_CDQ: SEG_CDQ + Q_RANK] = 2202 + np.arange(Q_RANK)
    cm[SEG_CDKV: SEG_CDKV + KV_RANK] = 2394 + np.arange(KV_RANK)
    cm[SEG_CKR + C_NOPE: SEG_CKR + C_NOPE + C_ROPE] = 2522 + np.arange(C_ROPE)
    return cm


def _take_cols(w, colmap):
    wz = jnp.concatenate([w, jnp.zeros((w.shape[0], 1), w.dtype)], axis=1)
    return wz[:, np.where(colmap < 0, w.shape[1], colmap)]


def _take_rows(w, rowmap):
    return _take_cols(w.T, rowmap).T


def _pad_lanes(v, n):
    v = v.reshape(1, -1)
    return jnp.pad(v, ((0, 0), (0, n - v.shape[1])))


def _bias_kernel(rb_ref, bt_ref, bc_ref, tiles_ref, cmp_ref):
    hd = pl.program_id(0)
    bt = bt_ref[...]
    bc = bc_ref[...]
    t = jnp.full(bt.shape, NEG, F32)
    c = jnp.full(bc.shape, NEG, F32)
    far = rb_ref[REL_BUCKETS - 1, hd]
    for b in range(REL_BUCKETS):
        val = rb_ref[b, hd]
        t = jnp.where(bt == b, (val - far) * LOG2E, t)
        c = jnp.where(bc == b, val * LOG2E, c)
    tiles_ref[0, 0:3] = t
    tiles_ref[0, 3] = jnp.full((AT, AT), NEG, F32)
    cmp_ref[0] = c


def _bias_tables(rel_bias, T):
    bt, bc = _bucket_tables(T)
    return pl.pallas_call(
        _bias_kernel,
        grid=(A_HEADS,),
        in_specs=[pl.BlockSpec(memory_space=pltpu.SMEM),
                  pl.BlockSpec((3, AT, AT), lambda h: (0, 0, 0)),
                  pl.BlockSpec((T, LANES), lambda h: (0, 0))],
        out_specs=[pl.BlockSpec((1, 4, AT, AT), lambda h: (h, 0, 0, 0)),
                   pl.BlockSpec((1, T, LANES), lambda h: (h, 0, 0))],
        out_shape=[jax.ShapeDtypeStruct((A_HEADS, 4, AT, AT), F32),
                   jax.ShapeDtypeStruct((A_HEADS, T, LANES), F32)],
        compiler_params=_params("arbitrary"),
        name="bias_tables",
    )(rel_bias, jnp.asarray(bt), jnp.asarray(bc))


def _inproj_kernel(x_ref, g_ref, w_ref, o_ref):
    x = x_ref[...]
    ms = jnp.mean(x * x, axis=-1, keepdims=True)
    h = (x * lax.rsqrt(ms + EPS) * g_ref[...]).astype(BF16)
    for j in range(0, D_PROJ, 384):
        o_ref[:, j:j + 384] = _dot(h, w_ref[:, j:j + 384])


def _inproj(x2, g, w):
    N = x2.shape[0]
    tm = 512
    return pl.pallas_call(
        _inproj_kernel,
        grid=(N // tm,),
        in_specs=[pl.BlockSpec((tm, D_MODEL), lambda i: (i, 0)),
                  pl.BlockSpec((1, D_MODEL), lambda i: (0, 0)),
                  pl.BlockSpec((D_MODEL, D_PROJ), lambda i: (0, 0))],
        out_specs=pl.BlockSpec((tm, D_PROJ), lambda i: (i, 0)),
        out_shape=jax.ShapeDtypeStruct((N, D_PROJ), F32),
        compiler_params=_params("parallel"),
        name="inproj",
    )(x2, g, w)


def _rms_lanes(x, g, width):
    ms = jnp.sum(x * x, axis=-1, keepdims=True) * (1.0 / width)
    return x * lax.rsqrt(ms + EPS) * g


def _qk_halves(q_r, k_blk):
    half_rows = q_r.shape[0] // 2
    return [_dot_nt(q_r[h0:h0 + half_rows, :], k_blk) for h0 in range(0, q_r.shape[0], half_rows)]


def _flash_chunk(q_r, k_blk, v_blk, bias_fn, m_r, acc_r):
    _softmax_pv(_qk_halves(q_r, k_blk), v_blk, bias_fn, m_r, acc_r)


def _softmax_pv(s_halves, v_blk, bias_fn, m_r, acc_r):
    half_rows = m_r.shape[0] // 2
    for hi_, h0 in enumerate(range(0, m_r.shape[0], half_rows)):
        s_all = s_halves[hi_]
        ps, alphas = [], []
        for r in range(half_rows // RB):
            rows = slice(h0 + r * RB, h0 + (r + 1) * RB)
            s = s_all[r * RB:(r + 1) * RB]
            if bias_fn is not None:
                s = s + bias_fn((h0 + r * RB) // RB)
            lo, hi = s[:, :LANES], s[:, LANES:]
            mo = m_r[rows, :]
            mn = jnp.maximum(mo, jnp.max(jnp.maximum(lo, hi), axis=-1, keepdims=True))
            ps.append(jnp.exp2(jnp.concatenate([lo - mn, hi - mn], axis=1)).astype(BF16))
            alphas.append(jnp.exp2(mo - mn))
            m_r[rows, :] = mn
        pv = _dot(jnp.concatenate(ps, axis=0), v_blk[hi_] if isinstance(v_blk, (list, tuple)) else v_blk)
        for r in range(half_rows // RB):
            rows = slice(h0 + r * RB, h0 + (r + 1) * RB)
            acc_r[rows, :] = alphas[r] * acc_r[rows, :] + pv[r * RB:(r + 1) * RB]


def _nsa_kernel(q_ref, k0_ref, v0_ref, k1_ref, v1_ref, k2_ref, v2_ref, gate_ref, tiles_ref, biasc_ref,
                ovt_ref, place_ref, w1_ref, w2_ref, pe_ref, qg_ref, kg_ref, o_ref,
                kc_s, vc_s, k1_s, v1_s, k2_s, v2_s, qx_s, qw_s, s_s, m_s, acc_s, mw_s, accw_s):
    qi = pl.program_id(2)
    kg = kg_ref[...]
    n_cmp_pad = kc_s.shape[0]
    per_g = AT // RB

    @pl.when(qi == 0)
    def _prep():
        T = k1_s.shape[0]
        lane_t = lax.broadcasted_iota(jnp.int32, (T, LANES), 1)
        row_t = lax.broadcasted_iota(jnp.int32, (T, LANES), 0)
        block_onehot = lax.shift_right_logical(row_t, 6) == lane_t - HEAD_DIM
        ones_lane = (lane_t >= HEAD_DIM) & (lane_t < HEAD_DIM + 3 * A_GROUP)
        k1_s[...] = jnp.where(block_onehot, 1.0, _rms_lanes(k1_ref[0], kg, HEAD_DIM)).astype(BF16)
        v1_s[...] = jnp.where(ones_lane, 1.0, v1_ref[0]).astype(BF16)
        k2_s[...] = _rms_lanes(k2_ref[0], kg, HEAD_DIM).astype(BF16)
        v2_s[...] = jnp.where(ones_lane, 1.0, v2_ref[0]).astype(BF16)
        half = CMP_BLOCK // 2
        for s, (src, dst) in enumerate(((k0_ref, kc_s), (v0_ref, vc_s))):
            a = jnp.zeros((n_cmp_pad, HEAD_DIM), F32)
            bm = jnp.zeros((n_cmp_pad, HEAD_DIM), F32)
            for j in range(half):
                xj = src[0, pl.ds(j, n_cmp_pad, stride=CMP_STRIDE), :][:, :HEAD_DIM]
                a = a + _dot_f32(xj + pe_ref[s, j:j + 1, :], w1_ref[s, j])
                bm = bm + _dot_f32(xj + pe_ref[s, half + j:half + j + 1, :], w1_ref[s, half + j])
            pre = a + pltpu.roll(bm, n_cmp_pad - 1, axis=0)
            hid = pre * jax.nn.sigmoid(pre)
            comp = _dot_f32(hid, w2_ref[s])
            if s == 0:
                comp = _rms_lanes(comp, kg, HEAD_DIM)
            dst[...] = comp.astype(BF16)

    qscale = HEAD_DIM ** -0.5 * LOG2E
    q = q_ref[0]
    qg = qg_ref[...]
    qs = jnp.concatenate(
        [_rms_lanes(q[:, g * LANES:(g + 1) * LANES], qg, HEAD_DIM) * qscale for g in range(A_GROUP)],
        axis=0).astype(BF16)
    qw_s[...] = qs

    def near_bias(delta, kind):
        kd = jnp.where(qi >= delta, kind, 3)
        return lambda r: tiles_ref[r // per_g, kd, pl.ds((r % per_g) * RB, RB), :]

    def key_rows(kc):
        return pl.ds(pl.multiple_of(kc * AT, AT), AT)

    mw_s[...] = jnp.full(mw_s.shape, NEG, F32)
    accw_s[...] = jnp.zeros(accw_s.shape, F32)
    for delta, kind in ((0, 0), (1, 1), (2, 2)):
        rows_k = key_rows(jnp.maximum(qi - delta, 0))
        _flash_chunk(qw_s, k2_s[rows_k, :], v2_s[rows_k, :], near_bias(delta, kind), mw_s, accw_s)

    biasc = biasc_ref[...].reshape(A_GROUP * AT, LANES)
    valid = biasc > 0.5 * NEG
    lcm = _dot_nt(qs, kc_s[...]) + biasc
    mc = jnp.max(lcm, axis=-1, keepdims=True)
    pc = jnp.where(valid, jnp.exp2(lcm - mc), 0.0)
    lc = jnp.sum(pc, axis=-1, keepdims=True)
    pc = pc * jnp.where(lc > 0.0, 1.0 / lc, 0.0)
    o_c = _dot(pc.astype(BF16), vc_s[...])

    ovt = ovt_ref[...]
    p_hi, p_mid, p_lo = _split3(pc[0:AT] + pc[AT:2 * AT] + pc[2 * AT:3 * AT])
    ps3 = _dot_nt(ovt, jnp.concatenate([p_hi, p_mid, p_lo], axis=0))
    ps = ps3[:, :AT] + ps3[:, AT:2 * AT] + ps3[:, 2 * AT:]
    n_sel = ps.shape[0]
    sid = lax.broadcasted_iota(jnp.int32, (n_sel, AT), 0)
    qcol = lax.broadcasted_iota(jnp.int32, (n_sel, AT), 1)
    cur = qi * (AT // SEL_BLOCK) + lax.shift_right_logical(qcol, 6)
    forced = (sid == 0) | (sid == cur) | (sid == cur - 1)
    psf = jnp.where(forced, ps + FORCE, ps)
    psf = jnp.where(sid <= cur, psf, -1.0)
    rank = jnp.zeros((n_sel, AT), F32)
    for sp in range(n_sel):
        other = psf[sp:sp + 1, :]
        beats = (other > psf) | ((other == psf) & (sid > sp))
        rank = rank + jnp.where(beats, 1.0, 0.0)
    keep = (rank < float(min(N_SELECT, n_sel))) & (psf >= 0.0)
    selneg = _dot_tn(jnp.where(keep, 0.0, NEG).astype(BF16), place_ref[...]).astype(BF16)
    lane_q = lax.broadcasted_iota(jnp.int32, (AT, LANES), 1)
    for g in range(A_GROUP):
        qx_s[g * AT:(g + 1) * AT, :] = jnp.where(lane_q < HEAD_DIM, qs[g * AT:(g + 1) * AT], selneg)

    m_s[...] = jnp.full(m_s.shape, NEG, F32)
    acc_s[...] = jnp.zeros(acc_s.shape, F32)
    for delta, kind in ((0, 0), (1, 1)):
        rows_k = key_rows(jnp.maximum(qi - delta, 0))
        _flash_chunk(qx_s, k1_s[rows_k, :], v1_s[rows_k, :], near_bias(delta, kind), m_s, acc_s)

    for h, s in enumerate(_qk_halves(qx_s, k1_s[key_rows(0), :])):
        s_s[h] = s

    def far_body(kc, carry):
        s_next = _qk_halves(qx_s, k1_s[key_rows(kc + 1), :])
        _softmax_pv([s_s[0], s_s[1]], v1_s[key_rows(kc), :], None, m_s, acc_s)
        for h, s in enumerate(s_next):
            s_s[h] = s
        return carry

    lax.fori_loop(0, jnp.maximum(qi - 1, 0), far_body, 0)

    gt = jax.nn.sigmoid(gate_ref[0])
    gate_lanes = (lane_q >= HEAD_DIM) & (lane_q < HEAD_DIM + 3 * A_GROUP)
    for g in range(A_GROUP):
        rows = slice(g * AT, (g + 1) * AT)
        a_s = acc_s[rows, :]
        a_w = accw_s[rows, :]
        c_s = gt / jnp.where(gate_lanes, a_s, 1.0)
        c_w = gt / jnp.where(gate_lanes, a_w, 1.0)
        j = HEAD_DIM + 3 * g
        o = gt[:, j:j + 1] * o_c[rows] + c_s[:, j + 1:j + 2] * a_s + c_w[:, j + 2:j + 3] * a_w
        o_ref[0, :, g * LANES:(g + 1) * LANES] = jnp.where(lane_q < HEAD_DIM, o, 0.0)


def _nsa(proj, tiles, biasc, cmp_pe, cmp_w1, cmp_w2, q_norm, k_norm):
    B, T, _ = proj.shape
    nq = T // AT
    n_cmp_pad = T // CMP_STRIDE
    n_sel = T // SEL_BLOCK
    assert n_sel <= LANES - HEAD_DIM - 1 and n_cmp_pad == LANES
    starts = np.arange(n_cmp_pad) * CMP_STRIDE
    sid = np.arange(n_sel)
    overlap = ((starts[:, None] < (sid[None, :] + 1) * SEL_BLOCK)
               & (starts[:, None] + CMP_BLOCK > sid[None, :] * SEL_BLOCK)
               & (starts[:, None] + CMP_BLOCK <= T))
    place = np.zeros((n_sel, LANES), np.float32)
    place[sid, HEAD_DIM + sid] = 1.0
    w2p = jnp.pad(cmp_w2, ((0, 0), (0, 0), (0, LANES - HEAD_DIM)))

    def slab(br, kv):
        base = SEG_AKV // LANES + (br * 2 + kv) * 2
        return pl.BlockSpec((1, T, LANES), lambda h, b, i: (b, 0, base + h))

    def full(shape):
        return pl.BlockSpec(shape, lambda h, b, i: (0,) * len(shape))

    rows = A_GROUP * AT
    return pl.pallas_call(
        _nsa_kernel,
        grid=(A_KV_HEADS, B, nq),
        in_specs=[pl.BlockSpec((1, AT, A_GROUP * LANES), lambda h, b, i: (b, i, h)),
                  slab(0, 0), slab(0, 1), slab(1, 0), slab(1, 1), slab(2, 0), slab(2, 1),
                  pl.BlockSpec((1, AT, LANES), lambda h, b, i: (b, i, SEG_GATE // LANES + h)),
                  pl.BlockSpec((A_GROUP, 4, AT, AT), lambda h, b, i: (h, 0, 0, 0)),
                  pl.BlockSpec((A_GROUP, AT, LANES), lambda h, b, i: (h, i, 0)),
                  full((n_sel, n_cmp_pad)), full((n_sel, LANES)),
                  full((2, CMP_BLOCK, HEAD_DIM, HEAD_DIM)), full((2, HEAD_DIM, LANES)),
                  full((2, CMP_BLOCK, HEAD_DIM)), full((1, LANES)), full((1, LANES))],
        out_specs=pl.BlockSpec((1, AT, A_GROUP * LANES), lambda h, b, i: (b, i, h)),
        out_shape=jax.ShapeDtypeStruct((B, T, A_HEADS * LANES), F32),
        scratch_shapes=[pltpu.VMEM((n_cmp_pad, LANES), BF16), pltpu.VMEM((n_cmp_pad, LANES), BF16),
                        pltpu.VMEM((T, LANES), BF16), pltpu.VMEM((T, LANES), BF16),
                        pltpu.VMEM((T, LANES), BF16), pltpu.VMEM((T, LANES), BF16),
                        pltpu.VMEM((rows, LANES), BF16), pltpu.VMEM((rows, LANES), BF16),
                        pltpu.VMEM((2, rows // 2, AT), F32),
                        pltpu.VMEM((rows, LANES), F32), pltpu.VMEM((rows, LANES), F32),
                        pltpu.VMEM((rows, LANES), F32), pltpu.VMEM((rows, LANES), F32)],
        compiler_params=_params("parallel", "parallel", "arbitrary"),
        name="nsa",
    )(proj, proj, proj, proj, proj, proj, proj, proj, tiles, biasc,
      jnp.asarray(overlap.T, BF16), jnp.asarray(place, BF16), cmp_w1, w2p, cmp_pe,
      _pad_lanes(q_norm, LANES), _pad_lanes(k_norm, LANES))


MLSTM_NB = 2
MLSTM_TB = 512


def _mlstm4_kernel(qk_ref, v_ref, o_ref, gi_ref, gf_ref, cw_ref, bi_ref, bf_ref, gn_ref, tril_ref, y_ref,
                   xpad_s, c_s, n_s, m_s):
    nb, tb = qk_ref.shape[0], qk_ref.shape[1]
    L = MLSTM_CHUNK
    W = B_HEADS * HEAD_DIM

    @pl.when(pl.program_id(1) == 0)
    def _init():
        xpad_s[:, tb:tb + 8, :] = jnp.zeros((nb, 8, xpad_s.shape[2]), F32)
        c_s[...] = jnp.zeros(c_s.shape, F32)
        n_s[...] = jnp.zeros(n_s.shape, F32)
        m_s[...] = jnp.zeros(m_s.shape, F32)

    for bb in range(nb):
        xpad_s[bb, 0:8, :] = xpad_s[bb, tb:tb + 8, :]
        xpad_s[bb, 8:8 + tb, :] = qk_ref[bb]
    lane_head = lax.shift_right_logical(lax.broadcasted_iota(jnp.int32, (1, W), 1), 6)
    seg = [lane_head == a for a in range(B_HEADS)]
    row_l = lax.broadcasted_iota(jnp.int32, (L, W), 0)
    lane_l = lax.broadcasted_iota(jnp.int32, (L, W), 1) & (HEAD_DIM - 1)
    tri = lane_l <= row_l
    diag = lane_l == row_l
    blockdiag = (lax.shift_right_logical(lax.broadcasted_iota(jnp.int32, (W, W), 0), 6)
                 == lax.shift_right_logical(lax.broadcasted_iota(jnp.int32, (W, W), 1), 6))
    kscale = HEAD_DIM ** -0.5

    def seg_reduce(x, op, fill):
        out = None
        for a in range(B_HEADS):
            r = op(jnp.where(seg[a], x, fill)[:, (a // 2) * LANES:(a // 2 + 1) * LANES], axis=-1, keepdims=True)
            out = r if out is None else jnp.where(seg[a], r, out)
        return out

    def chunk_one(bb, c):
        r0 = pl.multiple_of(c * L, L)
        win = xpad_s[bb, pl.ds(r0, L + 8), :]
        conv = cw_ref[CONV_K - 1:CONV_K, :] * win[8:8 + L]
        for j in range(CONV_K - 1):
            conv = conv + cw_ref[j:j + 1, :] * pltpu.roll(win, CONV_K - 1 - j, axis=0)[8:8 + L]
        qk = conv * jax.nn.sigmoid(conv)
        q4 = qk[:, :W]
        k4 = qk[:, W:] * kscale
        v4 = v_ref[bb, pl.ds(r0, L), :]
        i4 = gi_ref[bb, pl.ds(r0, L), :] + bi_ref[...]
        logf = jax.nn.log_sigmoid(gf_ref[bb, pl.ds(r0, L), :] + bf_ref[...])
        lh, lm, ll = _split3(logf)
        cum = _dot(tril_ref[...], jnp.concatenate([lh, lm, ll], axis=1))
        b4 = cum[:, :W] + cum[:, W:2 * W] + cum[:, 2 * W:]
        c_old, n_old, m_prev = c_s[bb], n_s[bb], m_s[bb][0:1, :]
        r4 = jnp.sum(jnp.where(diag, i4 - b4, 0.0), axis=0, keepdims=True)
        dmat = jnp.where(tri, b4 + r4, -jnp.inf)
        m_inter = b4 + m_prev
        m_j = jnp.maximum(m_inter, seg_reduce(dmat, jnp.max, -jnp.inf))
        isc = jnp.exp(m_inter - m_j)
        k_heads = jnp.concatenate([jnp.where(seg[a], k4, 0.0) for a in range(B_HEADS)], axis=0)
        v_heads = jnp.concatenate([jnp.where(seg[a], v4, 0.0) for a in range(B_HEADS)], axis=0)
        s = _dot_nt(q4, k_heads) * jnp.exp(dmat - m_j)
        den = seg_reduce(s, jnp.sum, 0.0) + isc * seg_reduce(q4 * n_old, jnp.sum, 0.0)
        num = _dot(s, v_heads) + isc * _dot_nt(q4, c_old)
        h = num / jnp.maximum(jnp.abs(den), jnp.exp(-m_j))
        b_last = b4[L - 1:L, :]
        w_log = b_last - b4 + i4
        m_new = jnp.maximum(b_last + m_prev, jnp.max(w_log, axis=0, keepdims=True))
        w4 = jnp.exp(w_log - m_new)
        dec = jnp.exp(b_last + m_prev - m_new)
        c_s[bb] = dec * c_old + jnp.where(blockdiag, _dot_tn(v4 * w4, k4), 0.0)
        n_s[bb] = dec * n_old + jnp.sum(w4 * k4, axis=0, keepdims=True)
        m_s[bb] = jnp.broadcast_to(m_new, m_s.shape[1:])
        ms = seg_reduce(h * h, jnp.sum, 0.0) * (1.0 / HEAD_DIM)
        hn = h * lax.rsqrt(ms + EPS) * gn_ref[...]
        y_ref[bb, pl.ds(r0, L), :] = jax.nn.sigmoid(o_ref[bb, pl.ds(r0, L), :]) * hn

    def chunk(c, carry):
        for bb in range(nb):
            chunk_one(bb, c)
        return carry

    lax.fori_loop(0, tb // L, chunk, 0)


def _mlstm4(proj, conv_w, i_bias, f_bias, out_norm):
    B, T, _ = proj.shape
    W = B_HEADS * HEAD_DIM
    nb, tb = MLSTM_NB, MLSTM_TB
    tril = jnp.asarray(np.tril(np.ones((MLSTM_CHUNK, MLSTM_CHUNK))), BF16)

    def full(shape):
        return pl.BlockSpec(shape, lambda b, t: (0,) * len(shape))

    def seg(width, off):
        return pl.BlockSpec((nb, tb, width), lambda b, t: (b, t, off // width))

    return pl.pallas_call(
        _mlstm4_kernel,
        grid=(B // nb, T // tb),
        in_specs=[seg(2 * W, SEG_BQK), seg(W, SEG_BV), seg(W, SEG_BO), seg(W, SEG_BIF), seg(W, SEG_BIF + W),
                  full((CONV_K, 2 * W)), full((1, W)), full((1, W)), full((1, W)),
                  full((MLSTM_CHUNK, MLSTM_CHUNK))],
        out_specs=pl.BlockSpec((nb, tb, W), lambda b, t: (b, t, 0)),
        out_shape=jax.ShapeDtypeStruct((B, T, W), F32),
        scratch_shapes=[pltpu.VMEM((nb, tb + 8, 2 * W), F32),
                        pltpu.VMEM((nb, W, W), F32),
                        pltpu.VMEM((nb, 1, W), F32),
                        pltpu.VMEM((nb, 8, W), F32)],
        compiler_params=_params("parallel", "arbitrary"),
        name="mlstm",
    )(proj, proj, proj, proj, proj, conv_w, jnp.repeat(i_bias, HEAD_DIM).reshape(1, W),
      jnp.repeat(f_bias, HEAD_DIM).reshape(1, W), out_norm.reshape(1, W), tril)


def _rope_kernel(pos_ref, invf_ref, sgn_ref, cs_ref, sn_ref):
    ang = pos_ref[...].astype(F32) * invf_ref[...]
    cs_ref[...] = jnp.cos(ang)
    sn_ref[...] = jnp.sin(ang) * sgn_ref[...]


def _rope_tables(pos):
    N = pos.shape[0]
    tm = 1024
    half = C_ROPE // 2
    inv = 1.0 / (ROPE_THETA ** (np.arange(half, dtype=np.float32) * 2.0 / C_ROPE))
    invf = np.zeros((1, LANES), np.float32)
    sgn = np.zeros((1, LANES), np.float32)
    invf[0, C_NOPE:C_NOPE + C_ROPE] = np.concatenate([inv, inv])
    sgn[0, C_NOPE:C_NOPE + half] = -1.0
    sgn[0, C_NOPE + half:C_NOPE + C_ROPE] = 1.0
    row = pl.BlockSpec((1, LANES), lambda i: (0, 0))
    tile = pl.BlockSpec((tm, LANES), lambda i: (i, 0))
    return pl.pallas_call(
        _rope_kernel,
        grid=(N // tm,),
        in_specs=[pl.BlockSpec((tm, 1), lambda i: (i, 0)), row, row],
        out_specs=[tile, tile],
        out_shape=[jax.ShapeDtypeStruct((N, LANES), F32), jax.ShapeDtypeStruct((N, LANES), F32)],
        compiler_params=_params("parallel"),
        name="rope_tables",
    )(pos, jnp.asarray(invf), jnp.asarray(sgn))


def _mla_prep_kernel(dq_ref, dkv_ref, kr_ref, cs_ref, sn_ref, wq_ref, wk_ref, wv_ref, qa_ref, kva_ref,
                     gq_ref, gk_ref, gkr_ref, q_out, k_out, v_out):
    lane = lax.broadcasted_iota(jnp.int32, (1, LANES), 1)
    nope = lane < C_NOPE
    ropem = (lane >= C_NOPE) & (lane < C_NOPE + C_ROPE)
    first = lane < C_NOPE + C_ROPE // 2
    cs = cs_ref[...]
    sn = sn_ref[...]

    def rope(x):
        partner = jnp.where(first, pltpu.roll(x, LANES - C_ROPE // 2, axis=1), pltpu.roll(x, C_ROPE // 2, axis=1))
        return x * cs + partner * sn

    scale = (C_NOPE + C_ROPE) ** -0.5 * LOG2E
    qn = _rms_lanes(dq_ref[...], qa_ref[...], Q_RANK).astype(BF16)
    q = _dot(qn, wq_ref[...])
    kvn = _rms_lanes(dkv_ref[...], kva_ref[...], KV_RANK).astype(BF16)
    kk = _dot(kvn, wk_ref[...])
    vv = _dot(kvn, wv_ref[...])
    lane_v = lax.broadcasted_iota(jnp.int32, vv.shape, 1)
    v_out[...] = jnp.where((lane_v & (LANES - 1)) == C_V, 1.0, vv).astype(BF16)
    kr = kr_ref[...]
    krn = kr * lax.rsqrt(jnp.sum(kr * kr, axis=-1, keepdims=True) * (1.0 / C_ROPE) + EPS) * gkr_ref[...]
    krr = rope(krn)
    for h in range(C_HEADS):
        sl = slice(h * LANES, (h + 1) * LANES)
        qh = q[:, sl]
        sq = qh * qh
        msn = jnp.sum(jnp.where(nope, sq, 0.0), axis=-1, keepdims=True) * (1.0 / C_NOPE)
        msr = jnp.sum(jnp.where(ropem, sq, 0.0), axis=-1, keepdims=True) * (1.0 / C_ROPE)
        qhn = qh * jnp.where(nope, lax.rsqrt(msn + EPS), lax.rsqrt(msr + EPS)) * gq_ref[...]
        q_out[:, sl] = (rope(qhn) * scale).astype(BF16)
        kh = kk[:, sl]
        msk = jnp.sum(kh * kh, axis=-1, keepdims=True) * (1.0 / C_NOPE)
        k_out[:, sl] = (kh * lax.rsqrt(msk + EPS) * gk_ref[...] + krr).astype(BF16)


def _mla_prep(proj2, rope_cs, rope_sn, q_a_norm, w_uq, kv_a_norm, w_ukv, qk_norm):
    N = proj2.shape[0]
    tm = 512
    qd = C_NOPE + C_ROPE
    qmap = np.full((C_HEADS * LANES,), -1, np.int64)
    kmap = np.full((C_HEADS * LANES,), -1, np.int64)
    vmap = np.full((C_HEADS * LANES,), -1, np.int64)
    for h in range(C_HEADS):
        qmap[h * LANES:h * LANES + qd] = h * qd + np.arange(qd)
        kmap[h * LANES:h * LANES + C_NOPE] = h * (C_NOPE + C_V) + np.arange(C_NOPE)
        vmap[h * LANES:h * LANES + C_V] = h * (C_NOPE + C_V) + C_NOPE + np.arange(C_V)
    wq = jnp.pad(_take_cols(w_uq, qmap), ((0, 256 - Q_RANK), (0, 0))).astype(BF16)
    wk = _take_cols(w_ukv, kmap).astype(BF16)
    wv = _take_cols(w_ukv, vmap).astype(BF16)
    gkr = jnp.pad(qk_norm[1, C_NOPE:].reshape(1, -1), ((0, 0), (C_NOPE, LANES - C_NOPE - C_ROPE)))

    def full(shape):
        return pl.BlockSpec(shape, lambda i: (0,) * len(shape))

    return pl.pallas_call(
        _mla_prep_kernel,
        grid=(N // tm,),
        in_specs=[pl.BlockSpec((tm, 256), lambda i: (i, SEG_CDQ // 256)),
                  pl.BlockSpec((tm, LANES), lambda i: (i, SEG_CDKV // LANES)),
                  pl.BlockSpec((tm, LANES), lambda i: (i, SEG_CKR // LANES)),
                  pl.BlockSpec((tm, LANES), lambda i: (i, 0)), pl.BlockSpec((tm, LANES), lambda i: (i, 0)),
                  full((256, C_HEADS * LANES)), full((KV_RANK, C_HEADS * LANES)), full((KV_RANK, C_HEADS * LANES)),
                  full((1, 256)), full((1, LANES)), full((1, LANES)), full((1, LANES)), full((1, LANES))],
        out_specs=[pl.BlockSpec((tm, C_HEADS * LANES), lambda i: (i, 0)),
                   pl.BlockSpec((tm, C_HEADS * LANES), lambda i: (i, 0)),
                   pl.BlockSpec((tm, C_HEADS * LANES), lambda i: (i, 0))],
        out_shape=[jax.ShapeDtypeStruct((N, C_HEADS * LANES), BF16),
                   jax.ShapeDtypeStruct((N, C_HEADS * LANES), BF16),
                   jax.ShapeDtypeStruct((N, C_HEADS * LANES), BF16)],
        compiler_params=_params("parallel"),
        name="mla_prep",
    )(proj2, proj2, proj2, rope_cs, rope_sn, wq, wk, wv,
      _pad_lanes(q_a_norm, 256), kv_a_norm.reshape(1, -1), _pad_lanes(qk_norm[0], LANES),
      _pad_lanes(qk_norm[1, :C_NOPE], LANES), gkr)


MLA_Q = 2 * AT


def _mla_attn_kernel(q_ref, k_ref, v_ref, causal_ref, o_ref, s_s, m_s, acc_s):
    qi = pl.program_id(2)
    m_s[...] = jnp.full(m_s.shape, NEG, F32)
    acc_s[...] = jnp.zeros(acc_s.shape, F32)
    heads = [slice(hh * LANES, (hh + 1) * LANES) for hh in range(2)]
    per_head = MLA_Q // RB
    tiles_per_q = MLA_Q // AT

    def key_rows(kc):
        return pl.ds(pl.multiple_of(kc * AT, AT), AT)

    def scores(kc):
        return [_dot_nt(q_ref[0, :, sl], k_ref[0, key_rows(kc), sl]) for sl in heads]

    def values(kc):
        return [v_ref[0, key_rows(kc), sl] for sl in heads]

    for d in range(tiles_per_q):
        kc = tiles_per_q * qi + d
        _softmax_pv(scores(kc), values(kc),
                    lambda r, d=d: causal_ref[d, pl.ds((r % per_head) * RB, RB), :], m_s, acc_s)

    for h, s in enumerate(scores(0)):
        s_s[h] = s

    def body(kc, carry):
        s_next = scores(kc + 1)
        _softmax_pv([s_s[0], s_s[1]], values(kc), None, m_s, acc_s)
        for h, s in enumerate(s_next):
            s_s[h] = s
        return carry

    lax.fori_loop(0, tiles_per_q * qi, body, 0)
    lane = lax.broadcasted_iota(jnp.int32, (MLA_Q, LANES), 1)
    outs = []
    for hh in range(2):
        a = acc_s[hh * MLA_Q:(hh + 1) * MLA_Q, :]
        outs.append(jnp.where(lane < C_V, a / a[:, C_V:C_V + 1], 0.0))
    o_ref[0] = outs[0] + pltpu.roll(outs[1], C_V, axis=1)


def _mla_attn(q, k, v):
    B, T, _ = q.shape
    qpos = np.arange(MLA_Q)[None, :, None]
    kpos = (np.arange(MLA_Q // AT)[:, None, None] * AT + np.arange(AT)[None, None, :])
    causal = np.where(qpos >= kpos, 0.0, NEG).astype(np.float32)
    pair = pl.BlockSpec((1, T, 2 * LANES), lambda b, p, i: (b, 0, p))
    return pl.pallas_call(
        _mla_attn_kernel,
        grid=(B, C_HEADS // 2, T // MLA_Q),
        in_specs=[pl.BlockSpec((1, MLA_Q, 2 * LANES), lambda b, p, i: (b, i, p)), pair, pair,
                  pl.BlockSpec((MLA_Q // AT, MLA_Q, AT), lambda b, p, i: (0, 0, 0))],
        out_specs=pl.BlockSpec((1, MLA_Q, LANES), lambda b, p, i: (b, i, p)),
        out_shape=jax.ShapeDtypeStruct((B, T, C_HEADS * C_V), F32),
        scratch_shapes=[pltpu.VMEM((2, MLA_Q, AT), F32), pltpu.VMEM((2 * MLA_Q, LANES), F32),
                        pltpu.VMEM((2 * MLA_Q, LANES), F32)],
        compiler_params=_params("parallel", "parallel", "arbitrary"),
        name="mla_attn",
    )(q, k, v, jnp.asarray(causal))


def _outproj_kernel(x_ref, ya_ref, yb_ref, yc_ref, wa_ref, wb_ref, wc_ref, ga_ref, gc_ref, g2_ref,
                    wr_ref, br_ref, x1_ref, h2_ref, ids_ref, wts_ref, cnt_ref):
    ya = _rms_lanes(ya_ref[...], ga_ref[...], A_HEADS * HEAD_DIM).astype(BF16)
    yc = _rms_lanes(yc_ref[...], gc_ref[...], C_HEADS * C_V).astype(BF16)
    x1 = (x_ref[...] + _dot(ya, wa_ref[...]) + _dot(yb_ref[...].astype(BF16), wb_ref[...])
          + _dot(yc, wc_ref[...]))
    x1_ref[...] = x1
    h2 = _rms_lanes(x1, g2_ref[...], D_MODEL)
    h2_ref[...] = _pack_bf16_pairs(h2[:, :D_MODEL // 2], h2[:, D_MODEL // 2:])
    h2_hi = h2.astype(BF16)
    h2_mid = (h2 - h2_hi.astype(F32)).astype(BF16)
    logits = (_dot(h2_hi, wr_ref[0]) + _dot(h2_mid, wr_ref[0]) + _dot(h2_hi, wr_ref[1])) + br_ref[...]
    tm = logits.shape[0]
    lane = lax.broadcasted_iota(jnp.int32, (tm, LANES), 1)
    is_g = (lane >= N_EXPERTS) & (lane < N_EXPERTS + N_GROUPS)
    gl = jnp.where(is_g, logits, NEG)
    ge = jnp.where(is_g, jnp.exp(gl - jnp.max(gl, axis=-1, keepdims=True)), 0.0)
    gp = jnp.where(is_g, ge / jnp.sum(ge, axis=-1, keepdims=True), -1.0)
    g_top = jnp.max(gp, axis=-1, keepdims=True)
    g_idx = jnp.min(jnp.where(gp == g_top, lane, 2 * LANES), axis=-1, keepdims=True) - N_EXPERTS
    in_grp = (lane < N_EXPERTS) & (lax.shift_right_logical(lane, 3) == g_idx)
    el = jnp.where(in_grp, logits, NEG)
    ee = jnp.where(in_grp, jnp.exp(el - jnp.max(el, axis=-1, keepdims=True)), 0.0)
    ep = jnp.where(in_grp, ee / jnp.sum(ee, axis=-1, keepdims=True), -1.0)
    v1 = jnp.max(ep, axis=-1, keepdims=True)
    i1 = jnp.min(jnp.where(ep == v1, lane, 2 * LANES), axis=-1, keepdims=True)
    ep2 = jnp.where(lane == i1, -1.0, ep)
    v2 = jnp.max(ep2, axis=-1, keepdims=True)
    i2 = jnp.min(jnp.where(ep2 == v2, lane, 2 * LANES), axis=-1, keepdims=True)
    tot = v1 + v2
    ids_ref[...] = jnp.where(lane == 0, i1, jnp.where(lane == 1, i2, 0))
    wts_ref[...] = jnp.where(lane == 0, v1 / tot * g_top, jnp.where(lane == 1, v2 / tot * g_top, 0.0))
    per_expert = jnp.sum(jnp.where((lane == i1) | (lane == i2), 1.0, 0.0), axis=0, keepdims=True)
    cnt_ref[...] = jnp.broadcast_to(per_expert, cnt_ref.shape)


def _outproj(x2, ya, yb, yc, w_out, nsa_out_norm, mla_out_norm, norm2_g, w_group, b_group, w_expert, b_expert):
    N = x2.shape[0]
    tm = 512
    amap = np.full((A_HEADS * LANES,), -1, np.int64)
    for hd in range(A_HEADS):
        amap[hd * LANES:hd * LANES + HEAD_DIM] = hd * HEAD_DIM + np.arange(HEAD_DIM)
    wa = _take_rows(w_out[:384], amap).astype(BF16)
    wb = w_out[384:640].astype(BF16)
    wc = w_out[640:].astype(BF16)
    ga = _take_cols(nsa_out_norm.reshape(1, -1), amap)
    wr = jnp.pad(jnp.concatenate([w_expert, w_group], axis=1), ((0, 0), (0, LANES - N_EXPERTS - N_GROUPS)))
    wr_hi = wr.astype(BF16)
    wr = jnp.stack([wr_hi, (wr - wr_hi.astype(F32)).astype(BF16)])
    br = _pad_lanes(jnp.concatenate([b_expert, b_group]), LANES)

    def full(shape):
        return pl.BlockSpec(shape, lambda i: (0,) * len(shape))

    def rows(w):
        return pl.BlockSpec((tm, w), lambda i: (i, 0))

    return pl.pallas_call(
        _outproj_kernel,
        grid=(N // tm,),
        in_specs=[rows(D_MODEL), rows(A_HEADS * LANES), rows(256), rows(384),
                  full((A_HEADS * LANES, D_MODEL)), full((256, D_MODEL)), full((384, D_MODEL)),
                  full((1, A_HEADS * LANES)), full((1, 384)), full((1, D_MODEL)),
                  full((2, D_MODEL, LANES)), full((1, LANES))],
        out_specs=[rows(D_MODEL), rows(D_MODEL // 2), rows(LANES), rows(LANES),
                   pl.BlockSpec((8, LANES), lambda i: (i, 0))],
        out_shape=[jax.ShapeDtypeStruct((N, D_MODEL), F32), jax.ShapeDtypeStruct((N, D_MODEL // 2), jnp.uint32),
                   jax.ShapeDtypeStruct((N, LANES), jnp.int32), jax.ShapeDtypeStruct((N, LANES), F32),
                   jax.ShapeDtypeStruct((N // tm * 8, LANES), F32)],
        compiler_params=_params("parallel"),
        name="outproj_router",
    )(x2, ya, yb, yc, wa, wb, wc, ga, mla_out_norm.reshape(1, -1), norm2_g.reshape(1, -1), wr, br)


def _rank_kernel(ids_ref, off_ref, tril_ref, pos_ref, carry_s):
    i = pl.program_id(0)

    @pl.when(i == 0)
    def _():
        carry_s[...] = jnp.broadcast_to(off_ref[...], carry_s.shape)

    ids = ids_ref[...]
    tm = ids.shape[0]
    lane = lax.broadcasted_iota(jnp.int32, (tm, LANES), 1)
    i1 = ids[:, 0:1]
    i2 = ids[:, 1:2]
    oh = jnp.where((lane == i1) | (lane == i2), 1.0, 0.0)
    incl = _dot(tril_ref[...], oh.astype(BF16)) + carry_s[0:1, :]
    excl = incl - oh
    p1 = jnp.sum(jnp.where(lane == i1, excl, 0.0), axis=-1, keepdims=True)
    p2 = jnp.sum(jnp.where(lane == i2, excl, 0.0), axis=-1, keepdims=True)
    pos_ref[...] = jnp.where(lane == 0, p1, jnp.where(lane == 1, p2, 0.0)).astype(jnp.int32)
    carry_s[...] = jnp.broadcast_to(incl[tm - 1:tm, :], carry_s.shape)


def _moe_positions(ids, off_row):
    N = ids.shape[0]
    tm = 512
    tril = jnp.asarray(np.tril(np.ones((tm, tm))), BF16)
    return pl.pallas_call(
        _rank_kernel,
        grid=(N // tm,),
        in_specs=[pl.BlockSpec((tm, LANES), lambda i: (i, 0)), pl.BlockSpec((1, LANES), lambda i: (0, 0)),
                  pl.BlockSpec((tm, tm), lambda i: (0, 0))],
        out_specs=pl.BlockSpec((tm, LANES), lambda i: (i, 0)),
        out_shape=jax.ShapeDtypeStruct((N, LANES), jnp.int32),
        scratch_shapes=[pltpu.VMEM((8, LANES), F32)],
        compiler_params=_params("arbitrary"),
        name="moe_positions",
    )(ids, off_row, tril)


DISPATCH_TM = 1024


DMA_UNROLL = 8


def _dispatch_kernel(off_ref, tp_ref, nu_ref, p1_ref, p2_ref, h_ref, xs_ref, zero_s, sem, zsem):
    n_tiles = xs_ref.shape[0] // ROW_TILE

    @pl.when(pl.program_id(0) == 0)
    def _zero_fill():
        zero_s[...] = jnp.zeros(zero_s.shape, zero_s.dtype)

        def tile_copy(row0):
            return pltpu.make_async_copy(
                zero_s, xs_ref.at[pl.ds(pl.multiple_of(row0, ROW_TILE), ROW_TILE), :], zsem)

        for e in range(N_EXPERTS):
            @pl.when(tp_ref[e] > 0)
            def _start(e=e):
                tile_copy(off_ref[e] + (tp_ref[e] - 1) * ROW_TILE).start()

        def tail_start(g, carry):
            tile_copy(g * ROW_TILE).start()
            return carry

        lax.fori_loop(nu_ref[0], n_tiles, tail_start, 0)
        for e in range(N_EXPERTS):
            @pl.when(tp_ref[e] > 0)
            def _wait():
                tile_copy(0).wait()

        def tail_wait(g, carry):
            tile_copy(0).wait()
            return carry

        lax.fori_loop(nu_ref[0], n_tiles, tail_wait, 0)

    def row_copy(i, pos):
        return pltpu.make_async_copy(h_ref.at[pl.ds(i, 1), :], xs_ref.at[pl.ds(pos, 1), :], sem)

    def issue(j, carry):
        for u in range(DMA_UNROLL):
            i = j * DMA_UNROLL + u
            row_copy(i, p1_ref[i]).start(priority=0)
            row_copy(i, p2_ref[i]).start(priority=1)
        return carry

    lax.fori_loop(0, DISPATCH_TM // DMA_UNROLL, issue, 0)
    whole = pltpu.make_async_copy(h_ref, xs_ref.at[pl.ds(0, DISPATCH_TM), :], sem)
    whole.wait()
    whole.wait()


def _moe_dispatch(h2, off, tiles_per, n_used, pos1, pos2, n_rows):
    N = h2.shape[0]
    smem1d = pl.BlockSpec((DISPATCH_TM,), lambda i: (i,), memory_space=pltpu.SMEM)
    smem = pl.BlockSpec(memory_space=pltpu.SMEM)
    return pl.pallas_call(
        _dispatch_kernel,
        grid=(N // DISPATCH_TM,),
        in_specs=[smem, smem, smem, smem1d, smem1d,
                  pl.BlockSpec((DISPATCH_TM, D_MODEL // 2), lambda i: (i, 0))],
        out_specs=pl.BlockSpec(memory_space=pl.ANY),
        out_shape=jax.ShapeDtypeStruct((n_rows, D_MODEL // 2), jnp.uint32),
        scratch_shapes=[pltpu.VMEM((ROW_TILE, D_MODEL // 2), jnp.uint32), pltpu.SemaphoreType.DMA(()),
                        pltpu.SemaphoreType.DMA(())],
        compiler_params=_params("arbitrary"),
        name="moe_dispatch",
    )(off, tiles_per, n_used, pos1, pos2, h2)


def _expert_kernel(te_ref, nu_ref, x_ref, wg_ref, wu_ref, wd_ref, y_ref):
    g = pl.program_id(0)

    @pl.when(g < nu_ref[0])
    def _():
        x = jnp.concatenate(_unpack_bf16_pairs(x_ref[...]), axis=1)
        a = _dot(x, wg_ref[0, 0].astype(BF16))
        u = _dot(x, wu_ref[0, 0].astype(BF16))
        h = (a * jax.nn.sigmoid(a) * u).astype(BF16)
        y_ref[...] = _dot(h, wd_ref[0, 0].astype(BF16))

    @pl.when(g >= nu_ref[0])
    def _():
        y_ref[...] = jnp.zeros(y_ref.shape, F32)


def _moe_experts(xs, tile_expert, n_used, layer, w_gate, w_up, w_down):
    R = xs.shape[0]
    G = R // ROW_TILE

    def row_map(g, te, nu):
        return (jnp.minimum(g, nu[0] - 1), 0)

    def w_map(g, te, nu):
        return (layer, te[jnp.minimum(g, nu[0] - 1)], 0, 0)

    return pl.pallas_call(
        _expert_kernel,
        grid_spec=pltpu.PrefetchScalarGridSpec(
            num_scalar_prefetch=2,
            grid=(G,),
            in_specs=[pl.BlockSpec((ROW_TILE, D_MODEL // 2), row_map),
                      pl.BlockSpec((1, 1, D_MODEL, D_EXPERT), w_map),
                      pl.BlockSpec((1, 1, D_MODEL, D_EXPERT), w_map),
                      pl.BlockSpec((1, 1, D_EXPERT, D_MODEL), w_map)],
            out_specs=pl.BlockSpec((ROW_TILE, D_MODEL), lambda g, te, nu: (g, 0))),
        out_shape=jax.ShapeDtypeStruct((R, D_MODEL), F32),
        compiler_params=_params("arbitrary"),
        name="moe_experts",
    )(tile_expert, n_used, xs, w_gate, w_up, w_down)


COMBINE_TM = 256


def _combine_kernel(p1_ref, p2_ref, x1_ref, wts_ref, ys_ref, o_ref, b1_s, b2_s, sem):
    def row_copy(buf, i, pos):
        return pltpu.make_async_copy(ys_ref.at[pl.ds(pos, 1), :], buf.at[pl.ds(i, 1), :], sem)

    def issue(j, carry):
        for u in range(DMA_UNROLL):
            i = j * DMA_UNROLL + u
            row_copy(b1_s, i, p1_ref[i]).start(priority=0)
            row_copy(b2_s, i, p2_ref[i]).start(priority=1)
        return carry

    lax.fori_loop(0, COMBINE_TM // DMA_UNROLL, issue, 0)
    pltpu.make_async_copy(ys_ref.at[pl.ds(0, COMBINE_TM), :], b1_s, sem).wait()
    pltpu.make_async_copy(ys_ref.at[pl.ds(0, COMBINE_TM), :], b2_s, sem).wait()
    w = wts_ref[...]
    o_ref[...] = x1_ref[...] + w[:, 0:1] * b1_s[...] + w[:, 1:2] * b2_s[...]


def _moe_combine(x1, wts, ys, pos1, pos2):
    N = x1.shape[0]
    smem1d = pl.BlockSpec((COMBINE_TM,), lambda i: (i,), memory_space=pltpu.SMEM)
    return pl.pallas_call(
        _combine_kernel,
        grid=(N // COMBINE_TM,),
        in_specs=[smem1d, smem1d,
                  pl.BlockSpec((COMBINE_TM, D_MODEL), lambda i: (i, 0)),
                  pl.BlockSpec((COMBINE_TM, LANES), lambda i: (i, 0)),
                  pl.BlockSpec(memory_space=pl.ANY)],
        out_specs=pl.BlockSpec((COMBINE_TM, D_MODEL), lambda i: (i, 0)),
        out_shape=jax.ShapeDtypeStruct((N, D_MODEL), F32),
        scratch_shapes=[pltpu.VMEM((COMBINE_TM, D_MODEL), F32), pltpu.VMEM((COMBINE_TM, D_MODEL), F32),
                        pltpu.SemaphoreType.DMA(())],
        compiler_params=_params("arbitrary"),
        name="moe_combine",
    )(pos1, pos2, x1, wts, ys)


def _moe(x1, h2, ids, wts, tile_counts, layer, w_gate, w_up, w_down):
    N = x1.shape[0]
    n_rows = 2 * N + N_EXPERTS * ROW_TILE
    counts = jnp.sum(tile_counts[::8, :N_EXPERTS], axis=0).astype(jnp.int32)
    tiles_per = (counts + ROW_TILE - 1) // ROW_TILE
    tile_end = jnp.cumsum(tiles_per)
    off = ((tile_end - tiles_per) * ROW_TILE).astype(jnp.int32)
    n_used = tile_end[-1:].astype(jnp.int32)
    tile_ids = jnp.arange(n_rows // ROW_TILE, dtype=jnp.int32)
    tile_expert = jnp.minimum(jnp.sum(tile_end[None, :] <= tile_ids[:, None], axis=1), N_EXPERTS - 1).astype(jnp.int32)
    pos = _moe_positions(ids, _pad_lanes(off.astype(F32), LANES))
    pos1, pos2 = pos[:, 0], pos[:, 1]
    xs = _moe_dispatch(h2, off, tiles_per.astype(jnp.int32), n_used, pos1, pos2, n_rows)
    ys = _moe_experts(xs, tile_expert, n_used, layer, w_gate, w_up, w_down)
    return _moe_combine(x1, wts, ys, pos1, pos2)


def kernel(x, positions, rel_bias, norm1_g, w_in, nsa_cmp_pe, nsa_cmp_w1, nsa_cmp_w2, nsa_q_norm, nsa_k_norm, nsa_out_norm, mlstm_conv, mlstm_i_bias, mlstm_f_bias, mlstm_out_norm, mla_q_a_norm, mla_w_uq, mla_kv_a_norm, mla_w_ukv, mla_qk_norm, mla_out_norm, w_out, norm2_g, moe_w_group, moe_b_group, moe_w_expert, moe_b_expert, moe_w_gate, moe_w_up, moe_w_down):
    B, T, D = x.shape
    N = B * T
    depth = w_in.shape[0]
    colmap = _proj_colmap()
    tiles, biasc = _bias_tables(rel_bias, T)
    rope_cs, rope_sn = _rope_tables(positions.reshape(N, 1))
    x2 = x.reshape(N, D)
    for l in range(depth):
        w_proj = _take_cols(w_in[l], colmap).astype(BF16)
        proj2 = _inproj(x2, norm1_g[l].reshape(1, D), w_proj)
        proj = proj2.reshape(B, T, D_PROJ)
        ya = _nsa(proj, tiles, biasc, nsa_cmp_pe[l], nsa_cmp_w1[l], nsa_cmp_w2[l], nsa_q_norm[l], nsa_k_norm[l])
        yb = _mlstm4(proj, mlstm_conv[l], mlstm_i_bias[l], mlstm_f_bias[l], mlstm_out_norm[l])
        q, k, v = _mla_prep(proj2, rope_cs, rope_sn, mla_q_a_norm[l], mla_w_uq[l], mla_kv_a_norm[l], mla_w_ukv[l], mla_qk_norm[l])
        yc = _mla_attn(q.reshape(B, T, -1), k.reshape(B, T, -1), v.reshape(B, T, -1))
        x1, h2, ids, wts, tile_counts = _outproj(x2, ya.reshape(N, -1), yb.reshape(N, -1), yc.reshape(N, -1), w_out[l],
                                    nsa_out_norm[l], mla_out_norm[l], norm2_g[l], moe_w_group[l], moe_b_group[l],
                                    moe_w_expert[l], moe_b_expert[l])
        x2 = _moe(x1, h2, ids, wts, tile_counts, l, moe_w_gate, moe_w_up, moe_w_down)
    return x2.reshape(B, T, D)
```

```python
import functools
import math

import numpy as np
import jax
import jax.numpy as jnp
from jax import lax
from jax.experimental import pallas as pl
from jax.experimental.pallas import tpu as pltpu

F32 = jnp.float32
BF16 = jnp.bfloat16

D_MODEL = 1024
HEAD_DIM = 64
LANES = 128
A_HEADS, A_KV_HEADS, A_GROUP = 6, 2, 3
CMP_BLOCK, CMP_STRIDE, SEL_BLOCK, N_SELECT, WINDOW = 32, 16, 64, 16, 512
FORCE = 1e4
B_HEADS, CONV_K, MLSTM_CHUNK = 4, 4, 64
C_HEADS, C_NOPE, C_ROPE, C_V = 6, 64, 32, 64
Q_RANK, KV_RANK = 192, 128
ROPE_THETA = 10000.0
REL_BUCKETS, REL_MAX_DIST = 32, 128
N_GROUPS, EXPERTS_PER_GROUP, N_EXPERTS, D_EXPERT = 4, 8, 32, 256
EPS = 1e-6
NEG = -1e30

LOG2E = 1.4426950408889634

AT = 256
RB = 128
ROW_TILE = 512
VMEM_LIMIT = 48 * 1024 * 1024

SEG_AQ, SEG_AKV, SEG_GATE, SEG_BQK, SEG_BV, SEG_BO = 0, 768, 2304, 2560, 3072, 3328
SEG_CDQ, SEG_CDKV, SEG_CKR, SEG_BIF, D_PROJ = 3584, 3840, 3968, 4096, 4608


def _params(*sem):
    return pltpu.CompilerParams(dimension_semantics=sem, vmem_limit_bytes=VMEM_LIMIT)


def _dot(a, b):
    return jnp.dot(a, b, preferred_element_type=F32)


def _dot_nt(a, b):
    return lax.dot_general(a, b, (((1,), (1,)), ((), ())), preferred_element_type=F32)


def _dot_tn(a, b):
    return lax.dot_general(a, b, (((0,), (0,)), ((), ())), preferred_element_type=F32)


def _split3(x):
    hi = x.astype(BF16)
    r1 = x - hi.astype(F32)
    mid = r1.astype(BF16)
    lo = (r1 - mid.astype(F32)).astype(BF16)
    return hi, mid, lo


def _pack_bf16_pairs(a, b):
    ua = lax.bitcast_convert_type(a.astype(BF16).astype(F32), jnp.uint32)
    ub = lax.bitcast_convert_type(b.astype(BF16).astype(F32), jnp.uint32)
    return lax.shift_right_logical(ua, jnp.uint32(16)) | (ub & jnp.uint32(0xFFFF0000))


def _unpack_bf16_pairs(w):
    a = lax.bitcast_convert_type(lax.shift_left(w, jnp.uint32(16)), F32)
    b = lax.bitcast_convert_type(w & jnp.uint32(0xFFFF0000), F32)
    return a.astype(BF16), b.astype(BF16)


def _bucket_np(dist):
    n = np.maximum(dist, 0)
    exact = REL_BUCKETS // 2
    nf = np.maximum(n, 1).astype(np.float64)
    large = exact + (np.log(nf / exact) / math.log(REL_MAX_DIST / exact) * (REL_BUCKETS - exact)).astype(np.int64)
    large = np.minimum(large, REL_BUCKETS - 1)
    return np.where(n < exact, n, large).astype(np.int32)


def _bucket_tables(T):
    r = np.arange(AT)[:, None]
    c = np.arange(AT)[None, :]
    diag = np.where(r >= c, _bucket_np(r - c), -1)
    prev = _bucket_np(AT + r - c)
    far = np.where(c > r, _bucket_np(2 * AT + r - c), -1)
    tiles = np.stack([diag, prev, far]).astype(np.int32)
    tq = np.arange(T)[:, None]
    n = np.arange(LANES)[None, :]
    dist_c = tq - (n * CMP_STRIDE + CMP_BLOCK - 1)
    cmp_tbl = np.where(dist_c >= 0, _bucket_np(dist_c), -1).astype(np.int32)
    return tiles, cmp_tbl


def _proj_colmap():
    cm = np.full((D_PROJ,), -1, np.int64)
    for hd in range(A_HEADS):
        cm[SEG_AQ + hd * LANES: SEG_AQ + hd * LANES + HEAD_DIM] = hd * HEAD_DIM + np.arange(HEAD_DIM)
    for s in range(12):
        cm[SEG_AKV + s * LANES: SEG_AKV + s * LANES + HEAD_DIM] = 384 + s * HEAD_DIM + np.arange(HEAD_DIM)
    for h in range(A_KV_HEADS):
        cm[SEG_GATE + h * LANES + HEAD_DIM: SEG_GATE + h * LANES + HEAD_DIM + 9] = 1152 + h * 9 + np.arange(9)
    cm[SEG_BQK: SEG_BQK + 512] = 1170 + np.arange(512)
    cm[SEG_BV: SEG_BV + 256] = 1682 + np.arange(256)
    cm[SEG_BIF: SEG_BIF + 2 * B_HEADS * HEAD_DIM] = 1938 + np.arange(2 * B_HEADS * HEAD_DIM) // HEAD_DIM
    cm[SEG_BO: SEG_BO + 256] = 1946 + np.arange(256)
    cm[SEG_CDQ: SEG_CDQ + Q_RANK] = 2202 + np.arange(Q_RANK)
    cm[SEG_CDKV: SEG_CDKV + KV_RANK] = 2394 + np.arange(KV_RANK)
    cm[SEG_CKR + C_NOPE: SEG_CKR + C_NOPE + C_ROPE] = 2522 + np.arange(C_ROPE)
    return cm


def _take_cols(w, colmap):
    wz = jnp.concatenate([w, jnp.zeros((w.shape[0], 1), w.dtype)], axis=1)
    return wz[:, np.where(colmap < 0, w.shape[1], colmap)]


def _take_rows(w, rowmap):
    return _take_cols(w.T, rowmap).T


def _pad_lanes(v, n):
    v = v.reshape(1, -1)
    return jnp.pad(v, ((0, 0), (0, n - v.shape[1])))


def _bias_kernel(rb_ref, bt_ref, bc_ref, tiles_ref, cmp_ref):
    hd = pl.program_id(0)
    bt = bt_ref[...]
    bc = bc_ref[...]
    t = jnp.full(bt.shape, NEG, F32)
    c = jnp.full(bc.shape, NEG, F32)
    far = rb_ref[REL_BUCKETS - 1, hd]
    for b in range(REL_BUCKETS):
        val = rb_ref[b, hd]
        t = jnp.where(bt == b, (val - far) * LOG2E, t)
        c = jnp.where(bc == b, val * LOG2E, c)
    tiles_ref[0, 0:3] = t
    tiles_ref[0, 3] = jnp.full((AT, AT), NEG, F32)
    cmp_ref[0] = c


def _bias_tables(rel_bias, T):
    bt, bc = _bucket_tables(T)
    return pl.pallas_call(
        _bias_kernel,
        grid=(A_HEADS,),
        in_specs=[pl.BlockSpec(memory_space=pltpu.SMEM),
                  pl.BlockSpec((3, AT, AT), lambda h: (0, 0, 0)),
                  pl.BlockSpec((T, LANES), lambda h: (0, 0))],
        out_specs=[pl.BlockSpec((1, 4, AT, AT), lambda h: (h, 0, 0, 0)),
                   pl.BlockSpec((1, T, LANES), lambda h: (h, 0, 0))],
        out_shape=[jax.ShapeDtypeStruct((A_HEADS, 4, AT, AT), F32),
                   jax.ShapeDtypeStruct((A_HEADS, T, LANES), F32)],
        compiler_params=_params("arbitrary"),
        name="bias_tables",
    )(rel_bias, jnp.asarray(bt), jnp.asarray(bc))


def _inproj_kernel(x_ref, g_ref, w_ref, o_ref):
    x = x_ref[...]
    ms = jnp.mean(x * x, axis=-1, keepdims=True)
    h = (x * lax.rsqrt(ms + EPS) * g_ref[...]).astype(BF16)
    for j in range(0, D_PROJ, 384):
        o_ref[:, j:j + 384] = _dot(h, w_ref[:, j:j + 384])


def _inproj(x2, g, w):
    N = x2.shape[0]
    tm = 512
    return pl.pallas_call(
        _inproj_kernel,
        grid=(N // tm,),
        in_specs=[pl.BlockSpec((tm, D_MODEL), lambda i: (i, 0)),
                  pl.BlockSpec((1, D_MODEL), lambda i: (0, 0)),
                  pl.BlockSpec((D_MODEL, D_PROJ), lambda i: (0, 0))],
        out_specs=pl.BlockSpec((tm, D_PROJ), lambda i: (i, 0)),
        out_shape=jax.ShapeDtypeStruct((N, D_PROJ), F32),
        compiler_params=_params("parallel"),
        name="inproj",
    )(x2, g, w)


def _rms_lanes(x, g, width):
    ms = jnp.sum(x * x, axis=-1, keepdims=True) * (1.0 / width)
    return x * lax.rsqrt(ms + EPS) * g


def _qk_halves(q_r, k_blk):
    half_rows = q_r.shape[0] // 2
    return [_dot_nt(q_r[h0:h0 + half_rows, :], k_blk) for h0 in range(0, q_r.shape[0], half_rows)]


def _flash_chunk(q_r, k_blk, v_blk, bias_fn, m_r, acc_r):
    _softmax_pv(_qk_halves(q_r, k_blk), v_blk, bias_fn, m_r, acc_r)


def _softmax_pv(s_halves, v_blk, bias_fn, m_r, acc_r):
    half_rows = m_r.shape[0] // 2
    assert half_rows % RB == 0
    for hi_, h0 in enumerate(range(0, m_r.shape[0], half_rows)):
        s_all = s_halves[hi_]
        ps, alphas = [], []
        for r in range(half_rows // RB):
            rows = slice(h0 + r * RB, h0 + (r + 1) * RB)
            s = s_all[r * RB:(r + 1) * RB]
            if bias_fn is not None:
                s = s + bias_fn((h0 + r * RB) // RB)
            lo, hi = s[:, :LANES], s[:, LANES:]
            mo = m_r[rows, :]
            mn = jnp.maximum(mo, jnp.max(jnp.maximum(lo, hi), axis=-1, keepdims=True))
            ps.append(jnp.exp2(jnp.concatenate([lo - mn, hi - mn], axis=1)).astype(BF16))
            alphas.append(jnp.exp2(mo - mn))
            m_r[rows, :] = mn
        pv = _dot(jnp.concatenate(ps, axis=0), v_blk[hi_] if isinstance(v_blk, (list, tuple)) else v_blk)
        for r in range(half_rows // RB):
            rows = slice(h0 + r * RB, h0 + (r + 1) * RB)
            acc_r[rows, :] = alphas[r] * acc_r[rows, :] + pv[r * RB:(r + 1) * RB]


def _nsa_kernel(q_ref, k0_ref, v0_ref, k1_ref, v1_ref, k2_ref, v2_ref, gate_ref, tiles_ref, biasc_ref,
                ovt_ref, place_ref, w1_ref, w2_ref, pe_ref, qg_ref, kg_ref, o_ref,
                kc_s, vc_s, k1_s, v1_s, k2_s, v2_s, qx_s, qw_s, s_s, m_s, acc_s, mw_s, accw_s):
    qi = pl.program_id(2)
    kg = kg_ref[...]
    n_cmp_pad = kc_s.shape[0]
    per_g = AT // RB

    @pl.when(qi == 0)
    def _prep():
        T = k1_s.shape[0]
        lane_t = lax.broadcasted_iota(jnp.int32, (T, LANES), 1)
        row_t = lax.broadcasted_iota(jnp.int32, (T, LANES), 0)
        block_onehot = lax.shift_right_logical(row_t, 6) == lane_t - HEAD_DIM
        ones_lane = (lane_t >= HEAD_DIM) & (lane_t < HEAD_DIM + 3 * A_GROUP)
        k1_s[...] = jnp.where(block_onehot, 1.0, _rms_lanes(k1_ref[0], kg, HEAD_DIM)).astype(BF16)
        v1_s[...] = jnp.where(ones_lane, 1.0, v1_ref[0]).astype(BF16)
        k2_s[...] = _rms_lanes(k2_ref[0], kg, HEAD_DIM).astype(BF16)
        v2_s[...] = jnp.where(ones_lane, 1.0, v2_ref[0]).astype(BF16)
        half = CMP_BLOCK // 2
        for s, (src, dst) in enumerate(((k0_ref, kc_s), (v0_ref, vc_s))):
            a = jnp.zeros((n_cmp_pad, HEAD_DIM), F32)
            bm = jnp.zeros((n_cmp_pad, HEAD_DIM), F32)
            for j in range(half):
                xj = src[0, pl.ds(j, n_cmp_pad, stride=CMP_STRIDE), :][:, :HEAD_DIM]
                a = a + _dot((xj + pe_ref[s, j:j + 1, :]).astype(BF16), w1_ref[s, j])
                bm = bm + _dot((xj + pe_ref[s, half + j:half + j + 1, :]).astype(BF16), w1_ref[s, half + j])
            pre = a + pltpu.roll(bm, n_cmp_pad - 1, axis=0)
            hid = pre * jax.nn.sigmoid(pre)
            comp = _dot(hid.astype(BF16), w2_ref[s])
            if s == 0:
                comp = _rms_lanes(comp, kg, HEAD_DIM)
            dst[...] = comp.astype(BF16)

    qscale = HEAD_DIM ** -0.5 * LOG2E
    q = q_ref[0]
    qg = qg_ref[...]
    qs = jnp.concatenate(
        [_rms_lanes(q[:, g * LANES:(g + 1) * LANES], qg, HEAD_DIM) * qscale for g in range(A_GROUP)],
        axis=0).astype(BF16)
    qw_s[...] = qs

    def near_bias(delta, kind):
        kd = jnp.where(qi >= delta, kind, 3)
        return lambda r: tiles_ref[r // per_g, kd, pl.ds((r % per_g) * RB, RB), :]

    def key_rows(kc):
        return pl.ds(pl.multiple_of(kc * AT, AT), AT)

    mw_s[...] = jnp.full(mw_s.shape, NEG, F32)
    accw_s[...] = jnp.zeros(accw_s.shape, F32)
    for delta, kind in ((0, 0), (1, 1), (2, 2)):
        rows_k = key_rows(jnp.maximum(qi - delta, 0))
        _flash_chunk(qw_s, k2_s[rows_k, :], v2_s[rows_k, :], near_bias(delta, kind), mw_s, accw_s)

    biasc = biasc_ref[...].reshape(A_GROUP * AT, LANES)
    valid = biasc > 0.5 * NEG
    lcm = _dot_nt(qs, kc_s[...]) + biasc
    mc = jnp.max(lcm, axis=-1, keepdims=True)
    pc = jnp.where(valid, jnp.exp2(lcm - mc), 0.0)
    lc = jnp.sum(pc, axis=-1, keepdims=True)
    pc = pc * jnp.where(lc > 0.0, 1.0 / lc, 0.0)
    o_c = _dot(pc.astype(BF16), vc_s[...])

    ovt = ovt_ref[...]
    p_hi, p_mid, p_lo = _split3(pc[0:AT] + pc[AT:2 * AT] + pc[2 * AT:3 * AT])
    ps3 = _dot_nt(ovt, jnp.concatenate([p_hi, p_mid, p_lo], axis=0))
    ps = ps3[:, :AT] + ps3[:, AT:2 * AT] + ps3[:, 2 * AT:]
    n_sel = ps.shape[0]
    sid = lax.broadcasted_iota(jnp.int32, (n_sel, AT), 0)
    qcol = lax.broadcasted_iota(jnp.int32, (n_sel, AT), 1)
    cur = qi * (AT // SEL_BLOCK) + lax.shift_right_logical(qcol, 6)
    forced = (sid == 0) | (sid == cur) | (sid == cur - 1)
    psf = jnp.where(forced, ps + FORCE, ps)
    psf = jnp.where(sid <= cur, psf, -1.0)
    rank = jnp.zeros((n_sel, AT), F32)
    for sp in range(n_sel):
        other = psf[sp:sp + 1, :]
        beats = (other > psf) | ((other == psf) & (sid > sp))
        rank = rank + jnp.where(beats, 1.0, 0.0)
    keep = (rank < float(min(N_SELECT, n_sel))) & (psf >= 0.0)
    selneg = _dot_tn(jnp.where(keep, 0.0, NEG).astype(BF16), place_ref[...]).astype(BF16)
    lane_q = lax.broadcasted_iota(jnp.int32, (AT, LANES), 1)
    for g in range(A_GROUP):
        qx_s[g * AT:(g + 1) * AT, :] = jnp.where(lane_q < HEAD_DIM, qs[g * AT:(g + 1) * AT], selneg)

    m_s[...] = jnp.full(m_s.shape, NEG, F32)
    acc_s[...] = jnp.zeros(acc_s.shape, F32)
    for delta, kind in ((0, 0), (1, 1)):
        rows_k = key_rows(jnp.maximum(qi - delta, 0))
        _flash_chunk(qx_s, k1_s[rows_k, :], v1_s[rows_k, :], near_bias(delta, kind), m_s, acc_s)

    for h, s in enumerate(_qk_halves(qx_s, k1_s[key_rows(0), :])):
        s_s[h] = s

    def far_body(kc, carry):
        s_next = _qk_halves(qx_s, k1_s[key_rows(kc + 1), :])
        _softmax_pv([s_s[0], s_s[1]], v1_s[key_rows(kc), :], None, m_s, acc_s)
        for h, s in enumerate(s_next):
            s_s[h] = s
        return carry

    lax.fori_loop(0, jnp.maximum(qi - 1, 0), far_body, 0)

    gt = jax.nn.sigmoid(gate_ref[0])
    gate_lanes = (lane_q >= HEAD_DIM) & (lane_q < HEAD_DIM + 3 * A_GROUP)
    for g in range(A_GROUP):
        rows = slice(g * AT, (g + 1) * AT)
        a_s = acc_s[rows, :]
        a_w = accw_s[rows, :]
        c_s = gt / jnp.where(gate_lanes, a_s, 1.0)
        c_w = gt / jnp.where(gate_lanes, a_w, 1.0)
        j = HEAD_DIM + 3 * g
        o = gt[:, j:j + 1] * o_c[rows] + c_s[:, j + 1:j + 2] * a_s + c_w[:, j + 2:j + 3] * a_w
        o_ref[0, :, g * LANES:(g + 1) * LANES] = jnp.where(lane_q < HEAD_DIM, o, 0.0)


def _nsa(proj, tiles, biasc, cmp_pe, cmp_w1, cmp_w2, q_norm, k_norm):
    B, T, _ = proj.shape
    nq = T // AT
    n_cmp_pad = T // CMP_STRIDE
    n_sel = T // SEL_BLOCK
    assert n_sel <= LANES - HEAD_DIM - 1 and n_cmp_pad == LANES
    starts = np.arange(n_cmp_pad) * CMP_STRIDE
    sid = np.arange(n_sel)
    overlap = ((starts[:, None] < (sid[None, :] + 1) * SEL_BLOCK)
               & (starts[:, None] + CMP_BLOCK > sid[None, :] * SEL_BLOCK)
               & (starts[:, None] + CMP_BLOCK <= T))
    place = np.zeros((n_sel, LANES), np.float32)
    place[sid, HEAD_DIM + sid] = 1.0
    w2p = jnp.pad(cmp_w2, ((0, 0), (0, 0), (0, LANES - HEAD_DIM)))

    def slab(br, kv):
        base = SEG_AKV // LANES + (br * 2 + kv) * 2
        return pl.BlockSpec((1, T, LANES), lambda h, b, i: (b, 0, base + h))

    def full(shape):
        return pl.BlockSpec(shape, lambda h, b, i: (0,) * len(shape))

    rows = A_GROUP * AT
    return pl.pallas_call(
        _nsa_kernel,
        grid=(A_KV_HEADS, B, nq),
        in_specs=[pl.BlockSpec((1, AT, A_GROUP * LANES), lambda h, b, i: (b, i, h)),
                  slab(0, 0), slab(0, 1), slab(1, 0), slab(1, 1), slab(2, 0), slab(2, 1),
                  pl.BlockSpec((1, AT, LANES), lambda h, b, i: (b, i, SEG_GATE // LANES + h)),
                  pl.BlockSpec((A_GROUP, 4, AT, AT), lambda h, b, i: (h, 0, 0, 0)),
                  pl.BlockSpec((A_GROUP, AT, LANES), lambda h, b, i: (h, i, 0)),
                  full((n_sel, n_cmp_pad)), full((n_sel, LANES)),
                  full((2, CMP_BLOCK, HEAD_DIM, HEAD_DIM)), full((2, HEAD_DIM, LANES)),
                  full((2, CMP_BLOCK, HEAD_DIM)), full((1, LANES)), full((1, LANES))],
        out_specs=pl.BlockSpec((1, AT, A_GROUP * LANES), lambda h, b, i: (b, i, h)),
        out_shape=jax.ShapeDtypeStruct((B, T, A_HEADS * LANES), F32),
        scratch_shapes=[pltpu.VMEM((n_cmp_pad, LANES), BF16), pltpu.VMEM((n_cmp_pad, LANES), BF16),
                        pltpu.VMEM((T, LANES), BF16), pltpu.VMEM((T, LANES), BF16),
                        pltpu.VMEM((T, LANES), BF16), pltpu.VMEM((T, LANES), BF16),
                        pltpu.VMEM((rows, LANES), BF16), pltpu.VMEM((rows, LANES), BF16),
                        pltpu.VMEM((2, rows // 2, AT), F32),
                        pltpu.VMEM((rows, LANES), F32), pltpu.VMEM((rows, LANES), F32),
                        pltpu.VMEM((rows, LANES), F32), pltpu.VMEM((rows, LANES), F32)],
        compiler_params=_params("parallel", "parallel", "arbitrary"),
        name="nsa",
    )(proj, proj, proj, proj, proj, proj, proj, proj, tiles, biasc,
      jnp.asarray(overlap.T, BF16), jnp.asarray(place, BF16), cmp_w1.astype(BF16), w2p.astype(BF16), cmp_pe,
      _pad_lanes(q_norm, LANES), _pad_lanes(k_norm, LANES))


MLSTM_NB = 2
MLSTM_TB = 512


def _mlstm4_kernel(qk_ref, v_ref, o_ref, gi_ref, gf_ref, cw_ref, bi_ref, bf_ref, gn_ref, tril_ref, y_ref,
                   xpad_s, c_s, n_s, m_s):
    nb, tb = qk_ref.shape[0], qk_ref.shape[1]
    L = MLSTM_CHUNK
    W = B_HEADS * HEAD_DIM

    @pl.when(pl.program_id(1) == 0)
    def _init():
        xpad_s[:, tb:tb + 8, :] = jnp.zeros((nb, 8, xpad_s.shape[2]), F32)
        c_s[...] = jnp.zeros(c_s.shape, F32)
        n_s[...] = jnp.zeros(n_s.shape, F32)
        m_s[...] = jnp.zeros(m_s.shape, F32)

    for bb in range(nb):
        xpad_s[bb, 0:8, :] = xpad_s[bb, tb:tb + 8, :]
        xpad_s[bb, 8:8 + tb, :] = qk_ref[bb]
    lane_head = lax.shift_right_logical(lax.broadcasted_iota(jnp.int32, (1, W), 1), 6)
    seg = [lane_head == a for a in range(B_HEADS)]
    row_l = lax.broadcasted_iota(jnp.int32, (L, W), 0)
    lane_l = lax.broadcasted_iota(jnp.int32, (L, W), 1) & (HEAD_DIM - 1)
    tri = lane_l <= row_l
    diag = lane_l == row_l
    blockdiag = (lax.shift_right_logical(lax.broadcasted_iota(jnp.int32, (W, W), 0), 6)
                 == lax.shift_right_logical(lax.broadcasted_iota(jnp.int32, (W, W), 1), 6))
    kscale = HEAD_DIM ** -0.5

    def seg_reduce(x, op, fill):
        out = None
        for a in range(B_HEADS):
            r = op(jnp.where(seg[a], x, fill)[:, (a // 2) * LANES:(a // 2 + 1) * LANES], axis=-1, keepdims=True)
            out = r if out is None else jnp.where(seg[a], r, out)
        return out

    def chunk_one(bb, c):
        r0 = pl.multiple_of(c * L, L)
        win = xpad_s[bb, pl.ds(r0, L + 8), :]
        conv = cw_ref[CONV_K - 1:CONV_K, :] * win[8:8 + L]
        for j in range(CONV_K - 1):
            conv = conv + cw_ref[j:j + 1, :] * pltpu.roll(win, CONV_K - 1 - j, axis=0)[8:8 + L]
        qk = conv * jax.nn.sigmoid(conv)
        q4 = qk[:, :W]
        k4 = qk[:, W:] * kscale
        v4 = v_ref[bb, pl.ds(r0, L), :]
        i4 = gi_ref[bb, pl.ds(r0, L), :] + bi_ref[...]
        logf = jax.nn.log_sigmoid(gf_ref[bb, pl.ds(r0, L), :] + bf_ref[...])
        lh, lm, ll = _split3(logf)
        cum = _dot(tril_ref[...], jnp.concatenate([lh, lm, ll], axis=1))
        b4 = cum[:, :W] + cum[:, W:2 * W] + cum[:, 2 * W:]
        c_old, n_old, m_prev = c_s[bb], n_s[bb], m_s[bb][0:1, :]
        r4 = jnp.sum(jnp.where(diag, i4 - b4, 0.0), axis=0, keepdims=True)
        dmat = jnp.where(tri, b4 + r4, -jnp.inf)
        m_inter = b4 + m_prev
        m_j = jnp.maximum(m_inter, seg_reduce(dmat, jnp.max, -jnp.inf))
        isc = jnp.exp(m_inter - m_j)
        k_heads = jnp.concatenate([jnp.where(seg[a], k4, 0.0) for a in range(B_HEADS)], axis=0)
        v_heads = jnp.concatenate([jnp.where(seg[a], v4, 0.0) for a in range(B_HEADS)], axis=0)
        s = _dot_nt(q4, k_heads) * jnp.exp(dmat - m_j)
        den = seg_reduce(s, jnp.sum, 0.0) + isc * seg_reduce(q4 * n_old, jnp.sum, 0.0)
        num = _dot(s, v_heads) + isc * _dot_nt(q4, c_old)
        h = num / jnp.maximum(jnp.abs(den), jnp.exp(-m_j))
        b_last = b4[L - 1:L, :]
        w_log = b_last - b4 + i4
        m_new = jnp.maximum(b_last + m_prev, jnp.max(w_log, axis=0, keepdims=True))
        w4 = jnp.exp(w_log - m_new)
        dec = jnp.exp(b_last + m_prev - m_new)
        c_s[bb] = dec * c_old + jnp.where(blockdiag, _dot_tn(v4 * w4, k4), 0.0)
        n_s[bb] = dec * n_old + jnp.sum(w4 * k4, axis=0, keepdims=True)
        m_s[bb] = jnp.broadcast_to(m_new, m_s.shape[1:])
        ms = seg_reduce(h * h, jnp.sum, 0.0) * (1.0 / HEAD_DIM)
        hn = h * lax.rsqrt(ms + EPS) * gn_ref[...]
        y_ref[bb, pl.ds(r0, L), :] = jax.nn.sigmoid(o_ref[bb, pl.ds(r0, L), :]) * hn

    def chunk(c, carry):
        for bb in range(nb):
            chunk_one(bb, c)
        return carry

    lax.fori_loop(0, tb // L, chunk, 0)


def _mlstm4(proj, conv_w, i_bias, f_bias, out_norm):
    B, T, _ = proj.shape
    W = B_HEADS * HEAD_DIM
    nb, tb = MLSTM_NB, MLSTM_TB
    tril = jnp.asarray(np.tril(np.ones((MLSTM_CHUNK, MLSTM_CHUNK))), BF16)

    def full(shape):
        return pl.BlockSpec(shape, lambda b, t: (0,) * len(shape))

    def seg(width, off):
        return pl.BlockSpec((nb, tb, width), lambda b, t: (b, t, off // width))

    return pl.pallas_call(
        _mlstm4_kernel,
        grid=(B // nb, T // tb),
        in_specs=[seg(2 * W, SEG_BQK), seg(W, SEG_BV), seg(W, SEG_BO), seg(W, SEG_BIF), seg(W, SEG_BIF + W),
                  full((CONV_K, 2 * W)), full((1, W)), full((1, W)), full((1, W)),
                  full((MLSTM_CHUNK, MLSTM_CHUNK))],
        out_specs=pl.BlockSpec((nb, tb, W), lambda b, t: (b, t, 0)),
        out_shape=jax.ShapeDtypeStruct((B, T, W), F32),
        scratch_shapes=[pltpu.VMEM((nb, tb + 8, 2 * W), F32),
                        pltpu.VMEM((nb, W, W), F32),
                        pltpu.VMEM((nb, 1, W), F32),
                        pltpu.VMEM((nb, 8, W), F32)],
        compiler_params=_params("parallel", "arbitrary"),
        name="mlstm",
    )(proj, proj, proj, proj, proj, conv_w, jnp.repeat(i_bias, HEAD_DIM).reshape(1, W),
      jnp.repeat(f_bias, HEAD_DIM).reshape(1, W), out_norm.reshape(1, W), tril)


def _rope_kernel(pos_ref, invf_ref, sgn_ref, cs_ref, sn_ref):
    ang = pos_ref[...].astype(F32) * invf_ref[...]
    cs_ref[...] = jnp.cos(ang)
    sn_ref[...] = jnp.sin(ang) * sgn_ref[...]


def _rope_tables(pos):
    N = pos.shape[0]
    tm = 1024
    half = C_ROPE // 2
    inv = 1.0 / (ROPE_THETA ** (np.arange(half, dtype=np.float32) * 2.0 / C_ROPE))
    invf = np.zeros((1, LANES), np.float32)
    sgn = np.zeros((1, LANES), np.float32)
    invf[0, C_NOPE:C_NOPE + C_ROPE] = np.concatenate([inv, inv])
    sgn[0, C_NOPE:C_NOPE + half] = -1.0
    sgn[0, C_NOPE + half:C_NOPE + C_ROPE] = 1.0
    row = pl.BlockSpec((1, LANES), lambda i: (0, 0))
    tile = pl.BlockSpec((tm, LANES), lambda i: (i, 0))
    return pl.pallas_call(
        _rope_kernel,
        grid=(N // tm,),
        in_specs=[pl.BlockSpec((tm, 1), lambda i: (i, 0)), row, row],
        out_specs=[tile, tile],
        out_shape=[jax.ShapeDtypeStruct((N, LANES), F32), jax.ShapeDtypeStruct((N, LANES), F32)],
        compiler_params=_params("parallel"),
        name="rope_tables",
    )(pos, jnp.asarray(invf), jnp.asarray(sgn))


def _mla_prep_kernel(dq_ref, dkv_ref, kr_ref, cs_ref, sn_ref, wq_ref, wk_ref, wv_ref, qa_ref, kva_ref,
                     gq_ref, gk_ref, gkr_ref, q_out, k_out, v_out):
    lane = lax.broadcasted_iota(jnp.int32, (1, LANES), 1)
    nope = lane < C_NOPE
    ropem = (lane >= C_NOPE) & (lane < C_NOPE + C_ROPE)
    first = lane < C_NOPE + C_ROPE // 2
    cs = cs_ref[...]
    sn = sn_ref[...]

    def rope(x):
        partner = jnp.where(first, pltpu.roll(x, LANES - C_ROPE // 2, axis=1), pltpu.roll(x, C_ROPE // 2, axis=1))
        return x * cs + partner * sn

    scale = (C_NOPE + C_ROPE) ** -0.5 * LOG2E
    qn = _rms_lanes(dq_ref[...], qa_ref[...], Q_RANK).astype(BF16)
    q = _dot(qn, wq_ref[...])
    kvn = _rms_lanes(dkv_ref[...], kva_ref[...], KV_RANK).astype(BF16)
    kk = _dot(kvn, wk_ref[...])
    vv = _dot(kvn, wv_ref[...])
    lane_v = lax.broadcasted_iota(jnp.int32, vv.shape, 1)
    v_out[...] = jnp.where((lane_v & (LANES - 1)) == C_V, 1.0, vv).astype(BF16)
    kr = kr_ref[...]
    krn = kr * lax.rsqrt(jnp.sum(kr * kr, axis=-1, keepdims=True) * (1.0 / C_ROPE) + EPS) * gkr_ref[...]
    krr = rope(krn)
    for h in range(C_HEADS):
        sl = slice(h * LANES, (h + 1) * LANES)
        qh = q[:, sl]
        sq = qh * qh
        msn = jnp.sum(jnp.where(nope, sq, 0.0), axis=-1, keepdims=True) * (1.0 / C_NOPE)
        msr = jnp.sum(jnp.where(ropem, sq, 0.0), axis=-1, keepdims=True) * (1.0 / C_ROPE)
        qhn = qh * jnp.where(nope, lax.rsqrt(msn + EPS), lax.rsqrt(msr + EPS)) * gq_ref[...]
        q_out[:, sl] = (rope(qhn) * scale).astype(BF16)
        kh = kk[:, sl]
        msk = jnp.sum(kh * kh, axis=-1, keepdims=True) * (1.0 / C_NOPE)
        k_out[:, sl] = (kh * lax.rsqrt(msk + EPS) * gk_ref[...] + krr).astype(BF16)


def _mla_prep(proj2, rope_cs, rope_sn, q_a_norm, w_uq, kv_a_norm, w_ukv, qk_norm):
    N = proj2.shape[0]
    tm = 512
    qd = C_NOPE + C_ROPE
    qmap = np.full((C_HEADS * LANES,), -1, np.int64)
    kmap = np.full((C_HEADS * LANES,), -1, np.int64)
    vmap = np.full((C_HEADS * LANES,), -1, np.int64)
    for h in range(C_HEADS):
        qmap[h * LANES:h * LANES + qd] = h * qd + np.arange(qd)
        kmap[h * LANES:h * LANES + C_NOPE] = h * (C_NOPE + C_V) + np.arange(C_NOPE)
        vmap[h * LANES:h * LANES + C_V] = h * (C_NOPE + C_V) + C_NOPE + np.arange(C_V)
    wq = jnp.pad(_take_cols(w_uq, qmap), ((0, 256 - Q_RANK), (0, 0))).astype(BF16)
    wk = _take_cols(w_ukv, kmap).astype(BF16)
    wv = _take_cols(w_ukv, vmap).astype(BF16)
    gkr = jnp.pad(qk_norm[1, C_NOPE:].reshape(1, -1), ((0, 0), (C_NOPE, LANES - C_NOPE - C_ROPE)))

    def full(shape):
        return pl.BlockSpec(shape, lambda i: (0,) * len(shape))

    return pl.pallas_call(
        _mla_prep_kernel,
        grid=(N // tm,),
        in_specs=[pl.BlockSpec((tm, 256), lambda i: (i, SEG_CDQ // 256)),
                  pl.BlockSpec((tm, LANES), lambda i: (i, SEG_CDKV // LANES)),
                  pl.BlockSpec((tm, LANES), lambda i: (i, SEG_CKR // LANES)),
                  pl.BlockSpec((tm, LANES), lambda i: (i, 0)), pl.BlockSpec((tm, LANES), lambda i: (i, 0)),
                  full((256, C_HEADS * LANES)), full((KV_RANK, C_HEADS * LANES)), full((KV_RANK, C_HEADS * LANES)),
                  full((1, 256)), full((1, LANES)), full((1, LANES)), full((1, LANES)), full((1, LANES))],
        out_specs=[pl.BlockSpec((tm, C_HEADS * LANES), lambda i: (i, 0)),
                   pl.BlockSpec((tm, C_HEADS * LANES), lambda i: (i, 0)),
                   pl.BlockSpec((tm, C_HEADS * LANES), lambda i: (i, 0))],
        out_shape=[jax.ShapeDtypeStruct((N, C_HEADS * LANES), BF16),
                   jax.ShapeDtypeStruct((N, C_HEADS * LANES), BF16),
                   jax.ShapeDtypeStruct((N, C_HEADS * LANES), BF16)],
        compiler_params=_params("parallel"),
        name="mla_prep",
    )(proj2, proj2, proj2, rope_cs, rope_sn, wq, wk, wv,
      _pad_lanes(q_a_norm, 256), kv_a_norm.reshape(1, -1), _pad_lanes(qk_norm[0], LANES),
      _pad_lanes(qk_norm[1, :C_NOPE], LANES), gkr)


MLA_Q = 2 * AT


def _mla_attn_kernel(q_ref, k_ref, v_ref, causal_ref, o_ref, s_s, m_s, acc_s):
    qi = pl.program_id(2)
    m_s[...] = jnp.full(m_s.shape, NEG, F32)
    acc_s[...] = jnp.zeros(acc_s.shape, F32)
    heads = [slice(hh * LANES, (hh + 1) * LANES) for hh in range(2)]
    per_head = MLA_Q // RB
    tiles_per_q = MLA_Q // AT

    def key_rows(kc):
        return pl.ds(pl.multiple_of(kc * AT, AT), AT)

    def scores(kc):
        return [_dot_nt(q_ref[0, :, sl], k_ref[0, key_rows(kc), sl]) for sl in heads]

    def values(kc):
        return [v_ref[0, key_rows(kc), sl] for sl in heads]

    for d in range(tiles_per_q):
        kc = tiles_per_q * qi + d
        _softmax_pv(scores(kc), values(kc),
                    lambda r, d=d: causal_ref[d, pl.ds((r % per_head) * RB, RB), :], m_s, acc_s)

    for h, s in enumerate(scores(0)):
        s_s[h] = s

    def body(kc, carry):
        s_next = scores(kc + 1)
        _softmax_pv([s_s[0], s_s[1]], values(kc), None, m_s, acc_s)
        for h, s in enumerate(s_next):
            s_s[h] = s
        return carry

    lax.fori_loop(0, tiles_per_q * qi, body, 0)
    lane = lax.broadcasted_iota(jnp.int32, (MLA_Q, LANES), 1)
    outs = []
    for hh in range(2):
        a = acc_s[hh * MLA_Q:(hh + 1) * MLA_Q, :]
        outs.append(jnp.where(lane < C_V, a / a[:, C_V:C_V + 1], 0.0))
    o_ref[0] = outs[0] + pltpu.roll(outs[1], C_V, axis=1)


def _mla_attn(q, k, v):
    B, T, _ = q.shape
    qpos = np.arange(MLA_Q)[None, :, None]
    kpos = (np.arange(MLA_Q // AT)[:, None, None] * AT + np.arange(AT)[None, None, :])
    causal = np.where(qpos >= kpos, 0.0, NEG).astype(np.float32)
    pair = pl.BlockSpec((1, T, 2 * LANES), lambda b, p, i: (b, 0, p))
    return pl.pallas_call(
        _mla_attn_kernel,
        grid=(B, C_HEADS // 2, T // MLA_Q),
        in_specs=[pl.BlockSpec((1, MLA_Q, 2 * LANES), lambda b, p, i: (b, i, p)), pair, pair,
                  pl.BlockSpec((MLA_Q // AT, MLA_Q, AT), lambda b, p, i: (0, 0, 0))],
        out_specs=pl.BlockSpec((1, MLA_Q, LANES), lambda b, p, i: (b, i, p)),
        out_shape=jax.ShapeDtypeStruct((B, T, C_HEADS * C_V), F32),
        scratch_shapes=[pltpu.VMEM((2, MLA_Q, AT), F32), pltpu.VMEM((2 * MLA_Q, LANES), F32),
                        pltpu.VMEM((2 * MLA_Q, LANES), F32)],
        compiler_params=_params("parallel", "parallel", "arbitrary"),
        name="mla_attn",
    )(q, k, v, jnp.asarray(causal))


def _outproj_kernel(x_ref, ya_ref, yb_ref, yc_ref, wa_ref, wb_ref, wc_ref, ga_ref, gc_ref, g2_ref,
                    wr_ref, br_ref, x1_ref, h2_ref, ids_ref, wts_ref, cnt_ref):
    ya = _rms_lanes(ya_ref[...], ga_ref[...], A_HEADS * HEAD_DIM).astype(BF16)
    yc = _rms_lanes(yc_ref[...], gc_ref[...], C_HEADS * C_V).astype(BF16)
    x1 = (x_ref[...] + _dot(ya, wa_ref[...]) + _dot(yb_ref[...].astype(BF16), wb_ref[...])
          + _dot(yc, wc_ref[...]))
    x1_ref[...] = x1
    h2 = _rms_lanes(x1, g2_ref[...], D_MODEL)
    h2_ref[...] = _pack_bf16_pairs(h2[:, :D_MODEL // 2], h2[:, D_MODEL // 2:])
    h2_hi = h2.astype(BF16)
    h2_mid = (h2 - h2_hi.astype(F32)).astype(BF16)
    logits = (_dot(h2_hi, wr_ref[0]) + _dot(h2_mid, wr_ref[0]) + _dot(h2_hi, wr_ref[1])) + br_ref[...]
    tm = logits.shape[0]
    lane = lax.broadcasted_iota(jnp.int32, (tm, LANES), 1)
    is_g = (lane >= N_EXPERTS) & (lane < N_EXPERTS + N_GROUPS)
    gl = jnp.where(is_g, logits, NEG)
    ge = jnp.where(is_g, jnp.exp(gl - jnp.max(gl, axis=-1, keepdims=True)), 0.0)
    gp = jnp.where(is_g, ge / jnp.sum(ge, axis=-1, keepdims=True), -1.0)
    g_top = jnp.max(gp, axis=-1, keepdims=True)
    g_idx = jnp.min(jnp.where(gp == g_top, lane, 2 * LANES), axis=-1, keepdims=True) - N_EXPERTS
    in_grp = (lane < N_EXPERTS) & (lax.shift_right_logical(lane, 3) == g_idx)
    el = jnp.where(in_grp, logits, NEG)
    ee = jnp.where(in_grp, jnp.exp(el - jnp.max(el, axis=-1, keepdims=True)), 0.0)
    ep = jnp.where(in_grp, ee / jnp.sum(ee, axis=-1, keepdims=True), -1.0)
    v1 = jnp.max(ep, axis=-1, keepdims=True)
    i1 = jnp.min(jnp.where(ep == v1, lane, 2 * LANES), axis=-1, keepdims=True)
    ep2 = jnp.where(lane == i1, -1.0, ep)
    v2 = jnp.max(ep2, axis=-1, keepdims=True)
    i2 = jnp.min(jnp.where(ep2 == v2, lane, 2 * LANES), axis=-1, keepdims=True)
    tot = v1 + v2
    ids_ref[...] = jnp.where(lane == 0, i1, jnp.where(lane == 1, i2, 0))
    wts_ref[...] = jnp.where(lane == 0, v1 / tot * g_top, jnp.where(lane == 1, v2 / tot * g_top, 0.0))
    per_expert = jnp.sum(jnp.where((lane == i1) | (lane == i2), 1.0, 0.0), axis=0, keepdims=True)
    cnt_ref[...] = jnp.broadcast_to(per_expert, cnt_ref.shape)


def _outproj(x2, ya, yb, yc, w_out, nsa_out_norm, mla_out_norm, norm2_g, w_group, b_group, w_expert, b_expert):
    N = x2.shape[0]
    tm = 512
    amap = np.full((A_HEADS * LANES,), -1, np.int64)
    for hd in range(A_HEADS):
        amap[hd * LANES:hd * LANES + HEAD_DIM] = hd * HEAD_DIM + np.arange(HEAD_DIM)
    wa = _take_rows(w_out[:384], amap).astype(BF16)
    wb = w_out[384:640].astype(BF16)
    wc = w_out[640:].astype(BF16)
    ga = _take_cols(nsa_out_norm.reshape(1, -1), amap)
    wr = jnp.pad(jnp.concatenate([w_expert, w_group], axis=1), ((0, 0), (0, LANES - N_EXPERTS - N_GROUPS)))
    wr_hi = wr.astype(BF16)
    wr = jnp.stack([wr_hi, (wr - wr_hi.astype(F32)).astype(BF16)])
    br = _pad_lanes(jnp.concatenate([b_expert, b_group]), LANES)

    def full(shape):
        return pl.BlockSpec(shape, lambda i: (0,) * len(shape))

    def rows(w):
        return pl.BlockSpec((tm, w), lambda i: (i, 0))

    return pl.pallas_call(
        _outproj_kernel,
        grid=(N // tm,),
        in_specs=[rows(D_MODEL), rows(A_HEADS * LANES), rows(256), rows(384),
                  full((A_HEADS * LANES, D_MODEL)), full((256, D_MODEL)), full((384, D_MODEL)),
                  full((1, A_HEADS * LANES)), full((1, 384)), full((1, D_MODEL)),
                  full((2, D_MODEL, LANES)), full((1, LANES))],
        out_specs=[rows(D_MODEL), rows(D_MODEL // 2), rows(LANES), rows(LANES),
                   pl.BlockSpec((8, LANES), lambda i: (i, 0))],
        out_shape=[jax.ShapeDtypeStruct((N, D_MODEL), F32), jax.ShapeDtypeStruct((N, D_MODEL // 2), jnp.uint32),
                   jax.ShapeDtypeStruct((N, LANES), jnp.int32), jax.ShapeDtypeStruct((N, LANES), F32),
                   jax.ShapeDtypeStruct((N // tm * 8, LANES), F32)],
        compiler_params=_params("parallel"),
        name="outproj_router",
    )(x2, ya, yb, yc, wa, wb, wc, ga, mla_out_norm.reshape(1, -1), norm2_g.reshape(1, -1), wr, br)


def _rank_kernel(ids_ref, off_ref, tril_ref, pos_ref, carry_s):
    i = pl.program_id(0)

    @pl.when(i == 0)
    def _():
        carry_s[...] = jnp.broadcast_to(off_ref[...], carry_s.shape)

    ids = ids_ref[...]
    tm = ids.shape[0]
    lane = lax.broadcasted_iota(jnp.int32, (tm, LANES), 1)
    i1 = ids[:, 0:1]
    i2 = ids[:, 1:2]
    oh = jnp.where((lane == i1) | (lane == i2), 1.0, 0.0)
    incl = _dot(tril_ref[...], oh.astype(BF16)) + carry_s[0:1, :]
    excl = incl - oh
    p1 = jnp.sum(jnp.where(lane == i1, excl, 0.0), axis=-1, keepdims=True)
    p2 = jnp.sum(jnp.where(lane == i2, excl, 0.0), axis=-1, keepdims=True)
    pos_ref[...] = jnp.where(lane == 0, p1, jnp.where(lane == 1, p2, 0.0)).astype(jnp.int32)
    carry_s[...] = jnp.broadcast_to(incl[tm - 1:tm, :], carry_s.shape)


def _moe_positions(ids, off_row):
    N = ids.shape[0]
    tm = 512
    tril = jnp.asarray(np.tril(np.ones((tm, tm))), BF16)
    return pl.pallas_call(
        _rank_kernel,
        grid=(N // tm,),
        in_specs=[pl.BlockSpec((tm, LANES), lambda i: (i, 0)), pl.BlockSpec((1, LANES), lambda i: (0, 0)),
                  pl.BlockSpec((tm, tm), lambda i: (0, 0))],
        out_specs=pl.BlockSpec((tm, LANES), lambda i: (i, 0)),
        out_shape=jax.ShapeDtypeStruct((N, LANES), jnp.int32),
        scratch_shapes=[pltpu.VMEM((8, LANES), F32)],
        compiler_params=_params("arbitrary"),
        name="moe_positions",
    )(ids, off_row, tril)


DISPATCH_TM = 2048


DMA_UNROLL = 8


def _dispatch_kernel(off_ref, tp_ref, nu_ref, p1_ref, p2_ref, h_ref, xs_ref, zero_s, sem, zsem):
    n_tiles = xs_ref.shape[0] // ROW_TILE

    @pl.when(pl.program_id(0) == 0)
    def _zero_fill():
        zero_s[...] = jnp.zeros(zero_s.shape, zero_s.dtype)

        def tile_copy(row0):
            return pltpu.make_async_copy(
                zero_s, xs_ref.at[pl.ds(pl.multiple_of(row0, ROW_TILE), ROW_TILE), :], zsem)

        for e in range(N_EXPERTS):
            @pl.when(tp_ref[e] > 0)
            def _start(e=e):
                tile_copy(off_ref[e] + (tp_ref[e] - 1) * ROW_TILE).start()

        def tail_start(g, carry):
            tile_copy(g * ROW_TILE).start()
            return carry

        lax.fori_loop(nu_ref[0], n_tiles, tail_start, 0)
        for e in range(N_EXPERTS):
            @pl.when(tp_ref[e] > 0)
            def _wait():
                tile_copy(0).wait()

        def tail_wait(g, carry):
            tile_copy(0).wait()
            return carry

        lax.fori_loop(nu_ref[0], n_tiles, tail_wait, 0)

    def row_copy(i, pos):
        return pltpu.make_async_copy(h_ref.at[pl.ds(i, 1), :], xs_ref.at[pl.ds(pos, 1), :], sem)

    def issue(j, carry):
        for u in range(DMA_UNROLL):
            i = j * DMA_UNROLL + u
            row_copy(i, p1_ref[i]).start(priority=0)
            row_copy(i, p2_ref[i]).start(priority=1)
        return carry

    lax.fori_loop(0, DISPATCH_TM // DMA_UNROLL, issue, 0)
    whole = pltpu.make_async_copy(h_ref, xs_ref.at[pl.ds(0, DISPATCH_TM), :], sem)
    whole.wait()
    whole.wait()


def _moe_dispatch(h2, off, tiles_per, n_used, pos1, pos2, n_rows):
    N = h2.shape[0]
    smem1d = pl.BlockSpec((DISPATCH_TM,), lambda i: (i,), memory_space=pltpu.SMEM)
    smem = pl.BlockSpec(memory_space=pltpu.SMEM)
    return pl.pallas_call(
        _dispatch_kernel,
        grid=(N // DISPATCH_TM,),
        in_specs=[smem, smem, smem, smem1d, smem1d,
                  pl.BlockSpec((DISPATCH_TM, D_MODEL // 2), lambda i: (i, 0))],
        out_specs=pl.BlockSpec(memory_space=pl.ANY),
        out_shape=jax.ShapeDtypeStruct((n_rows, D_MODEL // 2), jnp.uint32),
        scratch_shapes=[pltpu.VMEM((ROW_TILE, D_MODEL // 2), jnp.uint32), pltpu.SemaphoreType.DMA(()),
                        pltpu.SemaphoreType.DMA(())],
        compiler_params=_params("arbitrary"),
        name="moe_dispatch",
    )(off, tiles_per, n_used, pos1, pos2, h2)


def _expert_kernel(te_ref, nu_ref, x_ref, wg_ref, wu_ref, wd_ref, y_ref):
    g = pl.program_id(0)

    @pl.when(g < nu_ref[0])
    def _():
        x = jnp.concatenate(_unpack_bf16_pairs(x_ref[...]), axis=1)
        a = _dot(x, wg_ref[0, 0].astype(BF16))
        u = _dot(x, wu_ref[0, 0].astype(BF16))
        h = (a * jax.nn.sigmoid(a) * u).astype(BF16)
        y_ref[...] = _dot(h, wd_ref[0, 0].astype(BF16))

    @pl.when(g >= nu_ref[0])
    def _():
        y_ref[...] = jnp.zeros(y_ref.shape, F32)


def _moe_experts(xs, tile_expert, n_used, layer, w_gate, w_up, w_down):
    R = xs.shape[0]
    G = R // ROW_TILE

    def row_map(g, te, nu):
        return (jnp.minimum(g, nu[0] - 1), 0)

    def w_map(g, te, nu):
        return (layer, te[jnp.minimum(g, nu[0] - 1)], 0, 0)

    return pl.pallas_call(
        _expert_kernel,
        grid_spec=pltpu.PrefetchScalarGridSpec(
            num_scalar_prefetch=2,
            grid=(G,),
            in_specs=[pl.BlockSpec((ROW_TILE, D_MODEL // 2), row_map),
                      pl.BlockSpec((1, 1, D_MODEL, D_EXPERT), w_map),
                      pl.BlockSpec((1, 1, D_MODEL, D_EXPERT), w_map),
                      pl.BlockSpec((1, 1, D_EXPERT, D_MODEL), w_map)],
            out_specs=pl.BlockSpec((ROW_TILE, D_MODEL), lambda g, te, nu: (g, 0))),
        out_shape=jax.ShapeDtypeStruct((R, D_MODEL), F32),
        compiler_params=_params("arbitrary"),
        name="moe_experts",
    )(tile_expert, n_used, xs, w_gate, w_up, w_down)


COMBINE_TM = 512


def _combine_kernel(p1_ref, p2_ref, x1_ref, wts_ref, ys_ref, o_ref, b1_s, b2_s, sem):
    def row_copy(buf, i, pos):
        return pltpu.make_async_copy(ys_ref.at[pl.ds(pos, 1), :], buf.at[pl.ds(i, 1), :], sem)

    def issue(j, carry):
        for u in range(DMA_UNROLL):
            i = j * DMA_UNROLL + u
            row_copy(b1_s, i, p1_ref[i]).start(priority=0)
            row_copy(b2_s, i, p2_ref[i]).start(priority=1)
        return carry

    lax.fori_loop(0, COMBINE_TM // DMA_UNROLL, issue, 0)
    pltpu.make_async_copy(ys_ref.at[pl.ds(0, COMBINE_TM), :], b1_s, sem).wait()
    pltpu.make_async_copy(ys_ref.at[pl.ds(0, COMBINE_TM), :], b2_s, sem).wait()
    w = wts_ref[...]
    o_ref[...] = x1_ref[...] + w[:, 0:1] * b1_s[...] + w[:, 1:2] * b2_s[...]


def _moe_combine(x1, wts, ys, pos1, pos2):
    N = x1.shape[0]
    smem1d = pl.BlockSpec((COMBINE_TM,), lambda i: (i,), memory_space=pltpu.SMEM)
    return pl.pallas_call(
        _combine_kernel,
        grid=(N // COMBINE_TM,),
        in_specs=[smem1d, smem1d,
                  pl.BlockSpec((COMBINE_TM, D_MODEL), lambda i: (i, 0)),
                  pl.BlockSpec((COMBINE_TM, LANES), lambda i: (i, 0)),
                  pl.BlockSpec(memory_space=pl.ANY)],
        out_specs=pl.BlockSpec((COMBINE_TM, D_MODEL), lambda i: (i, 0)),
        out_shape=jax.ShapeDtypeStruct((N, D_MODEL), F32),
        scratch_shapes=[pltpu.VMEM((COMBINE_TM, D_MODEL), F32), pltpu.VMEM((COMBINE_TM, D_MODEL), F32),
                        pltpu.SemaphoreType.DMA(())],
        compiler_params=_params("arbitrary"),
        name="moe_combine",
    )(pos1, pos2, x1, wts, ys)


def _moe(x1, h2, ids, wts, tile_counts, layer, w_gate, w_up, w_down):
    N = x1.shape[0]
    n_rows = 2 * N + N_EXPERTS * ROW_TILE
    counts = jnp.sum(tile_counts[::8, :N_EXPERTS], axis=0).astype(jnp.int32)
    tiles_per = (counts + ROW_TILE - 1) // ROW_TILE
    tile_end = jnp.cumsum(tiles_per)
    off = ((tile_end - tiles_per) * ROW_TILE).astype(jnp.int32)
    n_used = tile_end[-1:].astype(jnp.int32)
    tile_ids = jnp.arange(n_rows // ROW_TILE, dtype=jnp.int32)
    tile_expert = jnp.minimum(jnp.sum(tile_end[None, :] <= tile_ids[:, None], axis=1), N_EXPERTS - 1).astype(jnp.int32)
    pos = _moe_positions(ids, _pad_lanes(off.astype(F32), LANES))
    pos1, pos2 = pos[:, 0], pos[:, 1]
    xs = _moe_dispatch(h2, off, tiles_per.astype(jnp.int32), n_used, pos1, pos2, n_rows)
    ys = _moe_experts(xs, tile_expert, n_used, layer, w_gate, w_up, w_down)
    return _moe_combine(x1, wts, ys, pos1, pos2)


def kernel(x, positions, rel_bias, norm1_g, w_in, nsa_cmp_pe, nsa_cmp_w1, nsa_cmp_w2, nsa_q_norm, nsa_k_norm, nsa_out_norm, mlstm_conv, mlstm_i_bias, mlstm_f_bias, mlstm_out_norm, mla_q_a_norm, mla_w_uq, mla_kv_a_norm, mla_w_ukv, mla_qk_norm, mla_out_norm, w_out, norm2_g, moe_w_group, moe_b_group, moe_w_expert, moe_b_expert, moe_w_gate, moe_w_up, moe_w_down):
    B, T, D = x.shape
    N = B * T
    depth = w_in.shape[0]
    colmap = _proj_colmap()
    tiles, biasc = _bias_tables(rel_bias, T)
    rope_cs, rope_sn = _rope_tables(positions.reshape(N, 1))
    x2 = x.reshape(N, D)
    for l in range(depth):
        w_proj = _take_cols(w_in[l], colmap).astype(BF16)
        proj2 = _inproj(x2, norm1_g[l].reshape(1, D), w_proj)
        proj = proj2.reshape(B, T, D_PROJ)
        ya = _nsa(proj, tiles, biasc, nsa_cmp_pe[l], nsa_cmp_w1[l], nsa_cmp_w2[l], nsa_q_norm[l], nsa_k_norm[l])
        yb = _mlstm4(proj, mlstm_conv[l], mlstm_i_bias[l], mlstm_f_bias[l], mlstm_out_norm[l])
        q, k, v = _mla_prep(proj2, rope_cs, rope_sn, mla_q_a_norm[l], mla_w_uq[l], mla_kv_a_norm[l], mla_w_ukv[l], mla_qk_norm[l])
        yc = _mla_attn(q.reshape(B, T, -1), k.reshape(B, T, -1), v.reshape(B, T, -1))
        x1, h2, ids, wts, tile_counts = _outproj(x2, ya.reshape(N, -1), yb.reshape(N, -1), yc.reshape(N, -1), w_out[l],
                                    nsa_out_norm[l], mla_out_norm[l], norm2_g[l], moe_w_group[l], moe_b_group[l],
                                    moe_w_expert[l], moe_b_expert[l])
        x2 = _moe(x1, h2, ids, wts, tile_counts, l, moe_w_gate, moe_w_up, moe_w_down)
    return x2.reshape(B, T, D)
```

```python
import functools
import math

import numpy as np
import jax
import jax.numpy as jnp
from jax import lax
from jax.experimental import pallas as pl
from jax.experimental.pallas import tpu as pltpu

F32 = jnp.float32
BF16 = jnp.bfloat16

D_MODEL = 1024
HEAD_DIM = 64
LANES = 128
A_HEADS, A_KV_HEADS, A_GROUP = 6, 2, 3
CMP_BLOCK, CMP_STRIDE, SEL_BLOCK, N_SELECT, WINDOW = 32, 16, 64, 16, 512
FORCE = 1e4
B_HEADS, CONV_K, MLSTM_CHUNK = 4, 4, 64
C_HEADS, C_NOPE, C_ROPE, C_V = 6, 64, 32, 64
Q_RANK, KV_RANK = 192, 128
ROPE_THETA = 10000.0
REL_BUCKETS, REL_MAX_DIST = 32, 128
N_GROUPS, EXPERTS_PER_GROUP, N_EXPERTS, D_EXPERT = 4, 8, 32, 256
EPS = 1e-6
NEG = -1e30

LOG2E = 1.4426950408889634

AT = 256
RB = 128
ROW_TILE = 512
VMEM_LIMIT = 48 * 1024 * 1024

SEG_AQ, SEG_AKV, SEG_GATE, SEG_BQK, SEG_BV, SEG_BO = 0, 768, 2304, 2560, 3072, 3328
SEG_CDQ, SEG_CDKV, SEG_CKR, SEG_BIF, D_PROJ = 3584, 3840, 3968, 4096, 4608


def _params(*sem):
    return pltpu.CompilerParams(dimension_semantics=sem, vmem_limit_bytes=VMEM_LIMIT)


def _dot(a, b):
    return jnp.dot(a, b, preferred_element_type=F32)


def _dot_nt(a, b):
    return lax.dot_general(a, b, (((1,), (1,)), ((), ())), preferred_element_type=F32)


def _dot_tn(a, b):
    return lax.dot_general(a, b, (((0,), (0,)), ((), ())), preferred_element_type=F32)


def _split3(x):
    hi = x.astype(BF16)
    r1 = x - hi.astype(F32)
    mid = r1.astype(BF16)
    lo = (r1 - mid.astype(F32)).astype(BF16)
    return hi, mid, lo


def _pack_bf16_pairs(a, b):
    ua = lax.bitcast_convert_type(a.astype(BF16).astype(F32), jnp.uint32)
    ub = lax.bitcast_convert_type(b.astype(BF16).astype(F32), jnp.uint32)
    return lax.shift_right_logical(ua, jnp.uint32(16)) | (ub & jnp.uint32(0xFFFF0000))


def _unpack_bf16_pairs(w):
    a = lax.bitcast_convert_type(lax.shift_left(w, jnp.uint32(16)), F32)
    b = lax.bitcast_convert_type(w & jnp.uint32(0xFFFF0000), F32)
    return a.astype(BF16), b.astype(BF16)


def _bucket_np(dist):
    n = np.maximum(dist, 0)
    exact = REL_BUCKETS // 2
    nf = np.maximum(n, 1).astype(np.float64)
    large = exact + (np.log(nf / exact) / math.log(REL_MAX_DIST / exact) * (REL_BUCKETS - exact)).astype(np.int64)
    large = np.minimum(large, REL_BUCKETS - 1)
    return np.where(n < exact, n, large).astype(np.int32)


def _bucket_tables(T):
    r = np.arange(AT)[:, None]
    c = np.arange(AT)[None, :]
    diag = np.where(r >= c, _bucket_np(r - c), -1)
    prev = _bucket_np(AT + r - c)
    far = np.where(c > r, _bucket_np(2 * AT + r - c), -1)
    tiles = np.stack([diag, prev, far]).astype(np.int32)
    tq = np.arange(T)[:, None]
    n = np.arange(LANES)[None, :]
    dist_c = tq - (n * CMP_STRIDE + CMP_BLOCK - 1)
    cmp_tbl = np.where(dist_c >= 0, _bucket_np(dist_c), -1).astype(np.int32)
    return tiles, cmp_tbl


def _proj_colmap():
    cm = np.full((D_PROJ,), -1, np.int64)
    for hd in range(A_HEADS):
        cm[SEG_AQ + hd * LANES: SEG_AQ + hd * LANES + HEAD_DIM] = hd * HEAD_DIM + np.arange(HEAD_DIM)
    for s in range(12):
        cm[SEG_AKV + s * LANES: SEG_AKV + s * LANES + HEAD_DIM] = 384 + s * HEAD_DIM + np.arange(HEAD_DIM)
    for h in range(A_KV_HEADS):
        cm[SEG_GATE + h * LANES + HEAD_DIM: SEG_GATE + h * LANES + HEAD_DIM + 9] = 1152 + h * 9 + np.arange(9)
    cm[SEG_BQK: SEG_BQK + 512] = 1170 + np.arange(512)
    cm[SEG_BV: SEG_BV + 256] = 1682 + np.arange(256)
    cm[SEG_BIF: SEG_BIF + 2 * B_HEADS * HEAD_DIM] = 1938 + np.arange(2 * B_HEADS * HEAD_DIM) // HEAD_DIM
    cm[SEG_BO: SEG_BO + 256] = 1946 + np.arange(256)
    cm[SEG_CDQ: SEG_CDQ + Q_RANK] = 2202 + np.arange(Q_RANK)
    cm[SEG_CDKV: SEG_CDKV + KV_RANK] = 2394 + np.arange(KV_RANK)
    cm[SEG_CKR + C_NOPE: SEG_CKR + C_NOPE + C_ROPE] = 2522 + np.arange(C_ROPE)
    return cm


def _take_cols(w, colmap):
    wz = jnp.concatenate([w, jnp.zeros((w.shape[0], 1), w.dtype)], axis=1)
    return wz[:, np.where(colmap < 0, w.shape[1], colmap)]


def _take_rows(w, rowmap):
    return _take_cols(w.T, rowmap).T


def _pad_lanes(v, n):
    v = v.reshape(1, -1)
    return jnp.pad(v, ((0, 0), (0, n - v.shape[1])))


def _bias_kernel(rb_ref, bt_ref, bc_ref, tiles_ref, cmp_ref):
    hd = pl.program_id(0)
    bt = bt_ref[...]
    bc = bc_ref[...]
    t = jnp.full(bt.shape, NEG, F32)
    c = jnp.full(bc.shape, NEG, F32)
    far = rb_ref[REL_BUCKETS - 1, hd]
    for b in range(REL_BUCKETS):
        val = rb_ref[b, hd]
        t = jnp.where(bt == b, (val - far) * LOG2E, t)
        c = jnp.where(bc == b, val * LOG2E, c)
    tiles_ref[0, 0:3] = t
    tiles_ref[0, 3] = jnp.full((AT, AT), NEG, F32)
    cmp_ref[0] = c


def _bias_tables(rel_bias, T):
    bt, bc = _bucket_tables(T)
    return pl.pallas_call(
        _bias_kernel,
        grid=(A_HEADS,),
        in_specs=[pl.BlockSpec(memory_space=pltpu.SMEM),
                  pl.BlockSpec((3, AT, AT), lambda h: (0, 0, 0)),
                  pl.BlockSpec((T, LANES), lambda h: (0, 0))],
        out_specs=[pl.BlockSpec((1, 4, AT, AT), lambda h: (h, 0, 0, 0)),
                   pl.BlockSpec((1, T, LANES), lambda h: (h, 0, 0))],
        out_shape=[jax.ShapeDtypeStruct((A_HEADS, 4, AT, AT), F32),
                   jax.ShapeDtypeStruct((A_HEADS, T, LANES), F32)],
        compiler_params=_params("arbitrary"),
        name="bias_tables",
    )(rel_bias, jnp.asarray(bt), jnp.asarray(bc))


def _inproj_kernel(x_ref, g_ref, w_ref, o_ref):
    x = x_ref[...]
    ms = jnp.mean(x * x, axis=-1, keepdims=True)
    h = (x * lax.rsqrt(ms + EPS) * g_ref[...]).astype(BF16)
    for j in range(0, D_PROJ, 384):
        o_ref[:, j:j + 384] = _dot(h, w_ref[:, j:j + 384])


def _inproj(x2, g, w):
    N = x2.shape[0]
    tm = 512
    return pl.pallas_call(
        _inproj_kernel,
        grid=(N // tm,),
        in_specs=[pl.BlockSpec((tm, D_MODEL), lambda i: (i, 0)),
                  pl.BlockSpec((1, D_MODEL), lambda i: (0, 0)),
                  pl.BlockSpec((D_MODEL, D_PROJ), lambda i: (0, 0))],
        out_specs=pl.BlockSpec((tm, D_PROJ), lambda i: (i, 0)),
        out_shape=jax.ShapeDtypeStruct((N, D_PROJ), F32),
        compiler_params=_params("parallel"),
        name="inproj",
    )(x2, g, w)


def _rms_lanes(x, g, width):
    ms = jnp.sum(x * x, axis=-1, keepdims=True) * (1.0 / width)
    return x * lax.rsqrt(ms + EPS) * g


def _qk_halves(q_r, k_blk):
    half_rows = q_r.shape[0] // 2
    return [_dot_nt(q_r[h0:h0 + half_rows, :], k_blk) for h0 in range(0, q_r.shape[0], half_rows)]


def _flash_chunk(q_r, k_blk, v_blk, bias_fn, m_r, acc_r):
    _softmax_pv(_qk_halves(q_r, k_blk), v_blk, bias_fn, m_r, acc_r)


def _softmax_pv(s_halves, v_blk, bias_fn, m_r, acc_r):
    half_rows = m_r.shape[0] // 2
    assert half_rows % RB == 0
    for hi_, h0 in enumerate(range(0, m_r.shape[0], half_rows)):
        s_all = s_halves[hi_]
        ps, alphas = [], []
        for r in range(half_rows // RB):
            rows = slice(h0 + r * RB, h0 + (r + 1) * RB)
            s = s_all[r * RB:(r + 1) * RB]
            if bias_fn is not None:
                s = s + bias_fn((h0 + r * RB) // RB)
            lo, hi = s[:, :LANES], s[:, LANES:]
            mo = m_r[rows, :]
            mn = jnp.maximum(mo, jnp.max(jnp.maximum(lo, hi), axis=-1, keepdims=True))
            ps.append(jnp.exp2(jnp.concatenate([lo - mn, hi - mn], axis=1)).astype(BF16))
            alphas.append(jnp.exp2(mo - mn))
            m_r[rows, :] = mn
        pv = _dot(jnp.concatenate(ps, axis=0), v_blk[hi_] if isinstance(v_blk, (list, tuple)) else v_blk)
        for r in range(half_rows // RB):
            rows = slice(h0 + r * RB, h0 + (r + 1) * RB)
            acc_r[rows, :] = alphas[r] * acc_r[rows, :] + pv[r * RB:(r + 1) * RB]


def _nsa_kernel(q_ref, k0_ref, v0_ref, k1_ref, v1_ref, k2_ref, v2_ref, gate_ref, tiles_ref, biasc_ref,
                ovt_ref, place_ref, w1_ref, w2_ref, pe_ref, qg_ref, kg_ref, o_ref,
                kc_s, vc_s, k1_s, v1_s, k2_s, v2_s, qx_s, qw_s, s_s, m_s, acc_s, mw_s, accw_s):
    qi = pl.program_id(2)
    kg = kg_ref[...]
    n_cmp_pad = kc_s.shape[0]
    per_g = AT // RB

    @pl.when(qi == 0)
    def _prep():
        T = k1_s.shape[0]
        lane_t = lax.broadcasted_iota(jnp.int32, (T, LANES), 1)
        row_t = lax.broadcasted_iota(jnp.int32, (T, LANES), 0)
        block_onehot = lax.shift_right_logical(row_t, 6) == lane_t - HEAD_DIM
        ones_lane = (lane_t >= HEAD_DIM) & (lane_t < HEAD_DIM + 3 * A_GROUP)
        k1_s[...] = jnp.where(block_onehot, 1.0, _rms_lanes(k1_ref[0], kg, HEAD_DIM)).astype(BF16)
        v1_s[...] = jnp.where(ones_lane, 1.0, v1_ref[0]).astype(BF16)
        k2_s[...] = _rms_lanes(k2_ref[0], kg, HEAD_DIM).astype(BF16)
        v2_s[...] = jnp.where(ones_lane, 1.0, v2_ref[0]).astype(BF16)
        half = CMP_BLOCK // 2
        for s, (src, dst) in enumerate(((k0_ref, kc_s), (v0_ref, vc_s))):
            a = jnp.zeros((n_cmp_pad, HEAD_DIM), F32)
            bm = jnp.zeros((n_cmp_pad, HEAD_DIM), F32)
            for j in range(half):
                xj = src[0, pl.ds(j, n_cmp_pad, stride=CMP_STRIDE), :][:, :HEAD_DIM]
                a = a + _dot((xj + pe_ref[s, j:j + 1, :]).astype(BF16), w1_ref[s, j])
                bm = bm + _dot((xj + pe_ref[s, half + j:half + j + 1, :]).astype(BF16), w1_ref[s, half + j])
            pre = a + pltpu.roll(bm, n_cmp_pad - 1, axis=0)
            hid = pre * jax.nn.sigmoid(pre)
            comp = _dot(hid.astype(BF16), w2_ref[s])
            if s == 0:
                comp = _rms_lanes(comp, kg, HEAD_DIM)
            dst[...] = comp.astype(BF16)

    qscale = HEAD_DIM ** -0.5 * LOG2E
    q = q_ref[0]
    qg = qg_ref[...]
    qs = jnp.concatenate(
        [_rms_lanes(q[:, g * LANES:(g + 1) * LANES], qg, HEAD_DIM) * qscale for g in range(A_GROUP)],
        axis=0).astype(BF16)
    qw_s[...] = qs

    def near_bias(delta, kind):
        kd = jnp.where(qi >= delta, kind, 3)
        return lambda r: tiles_ref[r // per_g, kd, pl.ds((r % per_g) * RB, RB), :]

    def key_rows(kc):
        return pl.ds(pl.multiple_of(kc * AT, AT), AT)

    mw_s[...] = jnp.full(mw_s.shape, NEG, F32)
    accw_s[...] = jnp.zeros(accw_s.shape, F32)
    for delta, kind in ((0, 0), (1, 1), (2, 2)):
        rows_k = key_rows(jnp.maximum(qi - delta, 0))
        _flash_chunk(qw_s, k2_s[rows_k, :], v2_s[rows_k, :], near_bias(delta, kind), mw_s, accw_s)

    biasc = biasc_ref[...].reshape(A_GROUP * AT, LANES)
    valid = biasc > 0.5 * NEG
    lcm = _dot_nt(qs, kc_s[...]) + biasc
    mc = jnp.max(lcm, axis=-1, keepdims=True)
    pc = jnp.where(valid, jnp.exp2(lcm - mc), 0.0)
    lc = jnp.sum(pc, axis=-1, keepdims=True)
    pc = pc * jnp.where(lc > 0.0, 1.0 / lc, 0.0)
    o_c = _dot(pc.astype(BF16), vc_s[...])

    ovt = ovt_ref[...]
    p_hi, p_mid, p_lo = _split3(pc[0:AT] + pc[AT:2 * AT] + pc[2 * AT:3 * AT])
    ps3 = _dot_nt(ovt, jnp.concatenate([p_hi, p_mid, p_lo], axis=0))
    ps = ps3[:, :AT] + ps3[:, AT:2 * AT] + ps3[:, 2 * AT:]
    n_sel = ps.shape[0]
    sid = lax.broadcasted_iota(jnp.int32, (n_sel, AT), 0)
    qcol = lax.broadcasted_iota(jnp.int32, (n_sel, AT), 1)
    cur = qi * (AT // SEL_BLOCK) + lax.shift_right_logical(qcol, 6)
    forced = (sid == 0) | (sid == cur) | (sid == cur - 1)
    psf = jnp.where(forced, ps + FORCE, ps)
    psf = jnp.where(sid <= cur, psf, -1.0)
    rank = jnp.zeros((n_sel, AT), F32)
    for sp in range(n_sel):
        other = psf[sp:sp + 1, :]
        beats = (other > psf) | ((other == psf) & (sid > sp))
        rank = rank + jnp.where(beats, 1.0, 0.0)
    keep = (rank < float(min(N_SELECT, n_sel))) & (psf >= 0.0)
    selneg = _dot_tn(jnp.where(keep, 0.0, NEG).astype(BF16), place_ref[...]).astype(BF16)
    lane_q = lax.broadcasted_iota(jnp.int32, (AT, LANES), 1)
    for g in range(A_GROUP):
        qx_s[g * AT:(g + 1) * AT, :] = jnp.where(lane_q < HEAD_DIM, qs[g * AT:(g + 1) * AT], selneg)

    m_s[...] = jnp.full(m_s.shape, NEG, F32)
    acc_s[...] = jnp.zeros(acc_s.shape, F32)
    for delta, kind in ((0, 0), (1, 1)):
        rows_k = key_rows(jnp.maximum(qi - delta, 0))
        _flash_chunk(qx_s, k1_s[rows_k, :], v1_s[rows_k, :], near_bias(delta, kind), m_s, acc_s)

    for h, s in enumerate(_qk_halves(qx_s, k1_s[key_rows(0), :])):
        s_s[h] = s

    def far_body(kc, carry):
        s_next = _qk_halves(qx_s, k1_s[key_rows(kc + 1), :])
        _softmax_pv([s_s[0], s_s[1]], v1_s[key_rows(kc), :], None, m_s, acc_s)
        for h, s in enumerate(s_next):
            s_s[h] = s
        return carry

    lax.fori_loop(0, jnp.maximum(qi - 1, 0), far_body, 0)

    gt = jax.nn.sigmoid(gate_ref[0])
    gate_lanes = (lane_q >= HEAD_DIM) & (lane_q < HEAD_DIM + 3 * A_GROUP)
    for g in range(A_GROUP):
        rows = slice(g * AT, (g + 1) * AT)
        a_s = acc_s[rows, :]
        a_w = accw_s[rows, :]
        c_s = gt / jnp.where(gate_lanes, a_s, 1.0)
        c_w = gt / jnp.where(gate_lanes, a_w, 1.0)
        j = HEAD_DIM + 3 * g
        o = gt[:, j:j + 1] * o_c[rows] + c_s[:, j + 1:j + 2] * a_s + c_w[:, j + 2:j + 3] * a_w
        o_ref[0, :, g * LANES:(g + 1) * LANES] = jnp.where(lane_q < HEAD_DIM, o, 0.0)


def _nsa(proj, tiles, biasc, cmp_pe, cmp_w1, cmp_w2, q_norm, k_norm):
    B, T, _ = proj.shape
    nq = T // AT
    n_cmp_pad = T // CMP_STRIDE
    n_sel = T // SEL_BLOCK
    assert n_sel <= LANES - HEAD_DIM - 1 and n_cmp_pad == LANES
    starts = np.arange(n_cmp_pad) * CMP_STRIDE
    sid = np.arange(n_sel)
    overlap = ((starts[:, None] < (sid[None, :] + 1) * SEL_BLOCK)
               & (starts[:, None] + CMP_BLOCK > sid[None, :] * SEL_BLOCK)
               & (starts[:, None] + CMP_BLOCK <= T))
    place = np.zeros((n_sel, LANES), np.float32)
    place[sid, HEAD_DIM + sid] = 1.0
    w2p = jnp.pad(cmp_w2, ((0, 0), (0, 0), (0, LANES - HEAD_DIM)))

    def slab(br, kv):
        base = SEG_AKV // LANES + (br * 2 + kv) * 2
        return pl.BlockSpec((1, T, LANES), lambda h, b, i: (b, 0, base + h))

    def full(shape):
        return pl.BlockSpec(shape, lambda h, b, i: (0,) * len(shape))

    rows = A_GROUP * AT
    return pl.pallas_call(
        _nsa_kernel,
        grid=(A_KV_HEADS, B, nq),
        in_specs=[pl.BlockSpec((1, AT, A_GROUP * LANES), lambda h, b, i: (b, i, h)),
                  slab(0, 0), slab(0, 1), slab(1, 0), slab(1, 1), slab(2, 0), slab(2, 1),
                  pl.BlockSpec((1, AT, LANES), lambda h, b, i: (b, i, SEG_GATE // LANES + h)),
                  pl.BlockSpec((A_GROUP, 4, AT, AT), lambda h, b, i: (h, 0, 0, 0)),
                  pl.BlockSpec((A_GROUP, AT, LANES), lambda h, b, i: (h, i, 0)),
                  full((n_sel, n_cmp_pad)), full((n_sel, LANES)),
                  full((2, CMP_BLOCK, HEAD_DIM, HEAD_DIM)), full((2, HEAD_DIM, LANES)),
                  full((2, CMP_BLOCK, HEAD_DIM)), full((1, LANES)), full((1, LANES))],
        out_specs=pl.BlockSpec((1, AT, A_GROUP * LANES), lambda h, b, i: (b, i, h)),
        out_shape=jax.ShapeDtypeStruct((B, T, A_HEADS * LANES), F32),
        scratch_shapes=[pltpu.VMEM((n_cmp_pad, LANES), BF16), pltpu.VMEM((n_cmp_pad, LANES), BF16),
                        pltpu.VMEM((T, LANES), BF16), pltpu.VMEM((T, LANES), BF16),
                        pltpu.VMEM((T, LANES), BF16), pltpu.VMEM((T, LANES), BF16),
                        pltpu.VMEM((rows, LANES), BF16), pltpu.VMEM((rows, LANES), BF16),
                        pltpu.VMEM((2, rows // 2, AT), F32),
                        pltpu.VMEM((rows, LANES), F32), pltpu.VMEM((rows, LANES), F32),
                        pltpu.VMEM((rows, LANES), F32), pltpu.VMEM((rows, LANES), F32)],
        compiler_params=_params("parallel", "parallel", "arbitrary"),
        name="nsa",
    )(proj, proj, proj, proj, proj, proj, proj, proj, tiles, biasc,
      jnp.asarray(overlap.T, BF16), jnp.asarray(place, BF16), cmp_w1.astype(BF16), w2p.astype(BF16), cmp_pe,
      _pad_lanes(q_norm, LANES), _pad_lanes(k_norm, LANES))


MLSTM_NB = 2
MLSTM_TB = 512


def _mlstm4_kernel(qk_ref, v_ref, o_ref, gi_ref, gf_ref, cw_ref, bi_ref, bf_ref, gn_ref, tril_ref, y_ref,
                   xpad_s, c_s, n_s, m_s):
    nb, tb = qk_ref.shape[0], qk_ref.shape[1]
    L = MLSTM_CHUNK
    W = B_HEADS * HEAD_DIM

    @pl.when(pl.program_id(1) == 0)
    def _init():
        xpad_s[:, tb:tb + 8, :] = jnp.zeros((nb, 8, xpad_s.shape[2]), F32)
        c_s[...] = jnp.zeros(c_s.shape, F32)
        n_s[...] = jnp.zeros(n_s.shape, F32)
        m_s[...] = jnp.zeros(m_s.shape, F32)

    for bb in range(nb):
        xpad_s[bb, 0:8, :] = xpad_s[bb, tb:tb + 8, :]
        xpad_s[bb, 8:8 + tb, :] = qk_ref[bb]
    lane_head = lax.shift_right_logical(lax.broadcasted_iota(jnp.int32, (1, W), 1), 6)
    seg = [lane_head == a for a in range(B_HEADS)]
    row_l = lax.broadcasted_iota(jnp.int32, (L, W), 0)
    lane_l = lax.broadcasted_iota(jnp.int32, (L, W), 1) & (HEAD_DIM - 1)
    tri = lane_l <= row_l
    diag = lane_l == row_l
    blockdiag = (lax.shift_right_logical(lax.broadcasted_iota(jnp.int32, (W, W), 0), 6)
                 == lax.shift_right_logical(lax.broadcasted_iota(jnp.int32, (W, W), 1), 6))
    kscale = HEAD_DIM ** -0.5

    def seg_reduce(x, op, fill):
        out = None
        for a in range(B_HEADS):
            r = op(jnp.where(seg[a], x, fill)[:, (a // 2) * LANES:(a // 2 + 1) * LANES], axis=-1, keepdims=True)
            out = r if out is None else jnp.where(seg[a], r, out)
        return out

    def chunk_one(bb, c):
        r0 = pl.multiple_of(c * L, L)
        win = xpad_s[bb, pl.ds(r0, L + 8), :]
        conv = cw_ref[CONV_K - 1:CONV_K, :] * win[8:8 + L]
        for j in range(CONV_K - 1):
            conv = conv + cw_ref[j:j + 1, :] * pltpu.roll(win, CONV_K - 1 - j, axis=0)[8:8 + L]
        qk = conv * jax.nn.sigmoid(conv)
        q4 = qk[:, :W]
        k4 = qk[:, W:] * kscale
        v4 = v_ref[bb, pl.ds(r0, L), :]
        i4 = gi_ref[bb, pl.ds(r0, L), :] + bi_ref[...]
        logf = jax.nn.log_sigmoid(gf_ref[bb, pl.ds(r0, L), :] + bf_ref[...])
        lh, lm, ll = _split3(logf)
        cum = _dot(tril_ref[...], jnp.concatenate([lh, lm, ll], axis=1))
        b4 = cum[:, :W] + cum[:, W:2 * W] + cum[:, 2 * W:]
        c_old, n_old, m_prev = c_s[bb], n_s[bb], m_s[bb][0:1, :]
        r4 = jnp.sum(jnp.where(diag, i4 - b4, 0.0), axis=0, keepdims=True)
        dmat = jnp.where(tri, b4 + r4, -jnp.inf)
        m_inter = b4 + m_prev
        m_j = jnp.maximum(m_inter, seg_reduce(dmat, jnp.max, -jnp.inf))
        isc = jnp.exp(m_inter - m_j)
        k_heads = jnp.concatenate([jnp.where(seg[a], k4, 0.0) for a in range(B_HEADS)], axis=0)
        v_heads = jnp.concatenate([jnp.where(seg[a], v4, 0.0) for a in range(B_HEADS)], axis=0)
        s = _dot_nt(q4, k_heads) * jnp.exp(dmat - m_j)
        den = seg_reduce(s, jnp.sum, 0.0) + isc * seg_reduce(q4 * n_old, jnp.sum, 0.0)
        num = _dot(s, v_heads) + isc * _dot_nt(q4, c_old)
        h = num / jnp.maximum(jnp.abs(den), jnp.exp(-m_j))
        b_last = b4[L - 1:L, :]
        w_log = b_last - b4 + i4
        m_new = jnp.maximum(b_last + m_prev, jnp.max(w_log, axis=0, keepdims=True))
        w4 = jnp.exp(w_log - m_new)
        dec = jnp.exp(b_last + m_prev - m_new)
        c_s[bb] = dec * c_old + jnp.where(blockdiag, _dot_tn(v4 * w4, k4), 0.0)
        n_s[bb] = dec * n_old + jnp.sum(w4 * k4, axis=0, keepdims=True)
        m_s[bb] = jnp.broadcast_to(m_new, m_s.shape[1:])
        ms = seg_reduce(h * h, jnp.sum, 0.0) * (1.0 / HEAD_DIM)
        hn = h * lax.rsqrt(ms + EPS) * gn_ref[...]
        y_ref[bb, pl.ds(r0, L), :] = jax.nn.sigmoid(o_ref[bb, pl.ds(r0, L), :]) * hn

    def chunk(c, carry):
        for bb in range(nb):
            chunk_one(bb, c)
        return carry

    lax.fori_loop(0, tb // L, chunk, 0)


def _mlstm4(proj, conv_w, i_bias, f_bias, out_norm):
    B, T, _ = proj.shape
    W = B_HEADS * HEAD_DIM
    nb, tb = MLSTM_NB, MLSTM_TB
    tril = jnp.asarray(np.tril(np.ones((MLSTM_CHUNK, MLSTM_CHUNK))), BF16)

    def full(shape):
        return pl.BlockSpec(shape, lambda b, t: (0,) * len(shape))

    def seg(width, off):
        return pl.BlockSpec((nb, tb, width), lambda b, t: (b, t, off // width))

    return pl.pallas_call(
        _mlstm4_kernel,
        grid=(B // nb, T // tb),
        in_specs=[seg(2 * W, SEG_BQK), seg(W, SEG_BV), seg(W, SEG_BO), seg(W, SEG_BIF), seg(W, SEG_BIF + W),
                  full((CONV_K, 2 * W)), full((1, W)), full((1, W)), full((1, W)),
                  full((MLSTM_CHUNK, MLSTM_CHUNK))],
        out_specs=pl.BlockSpec((nb, tb, W), lambda b, t: (b, t, 0)),
        out_shape=jax.ShapeDtypeStruct((B, T, W), F32),
        scratch_shapes=[pltpu.VMEM((nb, tb + 8, 2 * W), F32),
                        pltpu.VMEM((nb, W, W), F32),
                        pltpu.VMEM((nb, 1, W), F32),
                        pltpu.VMEM((nb, 8, W), F32)],
        compiler_params=_params("parallel", "arbitrary"),
        name="mlstm",
    )(proj, proj, proj, proj, proj, conv_w, jnp.repeat(i_bias, HEAD_DIM).reshape(1, W),
      jnp.repeat(f_bias, HEAD_DIM).reshape(1, W), out_norm.reshape(1, W), tril)


def _rope_kernel(pos_ref, invf_ref, sgn_ref, cs_ref, sn_ref):
    ang = pos_ref[...].astype(F32) * invf_ref[...]
    cs_ref[...] = jnp.cos(ang)
    sn_ref[...] = jnp.sin(ang) * sgn_ref[...]


def _rope_tables(pos):
    N = pos.shape[0]
    tm = 1024
    half = C_ROPE // 2
    inv = 1.0 / (ROPE_THETA ** (np.arange(half, dtype=np.float32) * 2.0 / C_ROPE))
    invf = np.zeros((1, LANES), np.float32)
    sgn = np.zeros((1, LANES), np.float32)
    invf[0, C_NOPE:C_NOPE + C_ROPE] = np.concatenate([inv, inv])
    sgn[0, C_NOPE:C_NOPE + half] = -1.0
    sgn[0, C_NOPE + half:C_NOPE + C_ROPE] = 1.0
    row = pl.BlockSpec((1, LANES), lambda i: (0, 0))
    tile = pl.BlockSpec((tm, LANES), lambda i: (i, 0))
    return pl.pallas_call(
        _rope_kernel,
        grid=(N // tm,),
        in_specs=[pl.BlockSpec((tm, 1), lambda i: (i, 0)), row, row],
        out_specs=[tile, tile],
        out_shape=[jax.ShapeDtypeStruct((N, LANES), F32), jax.ShapeDtypeStruct((N, LANES), F32)],
        compiler_params=_params("parallel"),
        name="rope_tables",
    )(pos, jnp.asarray(invf), jnp.asarray(sgn))


def _mla_prep_kernel(dq_ref, dkv_ref, kr_ref, cs_ref, sn_ref, wq_ref, wk_ref, wv_ref, qa_ref, kva_ref,
                     gq_ref, gk_ref, gkr_ref, q_out, k_out, v_out):
    lane = lax.broadcasted_iota(jnp.int32, (1, LANES), 1)
    nope = lane < C_NOPE
    ropem = (lane >= C_NOPE) & (lane < C_NOPE + C_ROPE)
    first = lane < C_NOPE + C_ROPE // 2
    cs = cs_ref[...]
    sn = sn_ref[...]

    def rope(x):
        partner = jnp.where(first, pltpu.roll(x, LANES - C_ROPE // 2, axis=1), pltpu.roll(x, C_ROPE // 2, axis=1))
        return x * cs + partner * sn

    scale = (C_NOPE + C_ROPE) ** -0.5 * LOG2E
    qn = _rms_lanes(dq_ref[...], qa_ref[...], Q_RANK).astype(BF16)
    q = _dot(qn, wq_ref[...])
    kvn = _rms_lanes(dkv_ref[...], kva_ref[...], KV_RANK).astype(BF16)
    kk = _dot(kvn, wk_ref[...])
    vv = _dot(kvn, wv_ref[...])
    lane_v = lax.broadcasted_iota(jnp.int32, vv.shape, 1)
    v_out[...] = jnp.where((lane_v & (LANES - 1)) == C_V, 1.0, vv).astype(BF16)
    kr = kr_ref[...]
    krn = kr * lax.rsqrt(jnp.sum(kr * kr, axis=-1, keepdims=True) * (1.0 / C_ROPE) + EPS) * gkr_ref[...]
    krr = rope(krn)
    for h in range(C_HEADS):
        sl = slice(h * LANES, (h + 1) * LANES)
        qh = q[:, sl]
        sq = qh * qh
        msn = jnp.sum(jnp.where(nope, sq, 0.0), axis=-1, keepdims=True) * (1.0 / C_NOPE)
        msr = jnp.sum(jnp.where(ropem, sq, 0.0), axis=-1, keepdims=True) * (1.0 / C_ROPE)
        qhn = qh * jnp.where(nope, lax.rsqrt(msn + EPS), lax.rsqrt(msr + EPS)) * gq_ref[...]
        q_out[:, sl] = (rope(qhn) * scale).astype(BF16)
        kh = kk[:, sl]
        msk = jnp.sum(kh * kh, axis=-1, keepdims=True) * (1.0 / C_NOPE)
        k_out[:, sl] = (kh * lax.rsqrt(msk + EPS) * gk_ref[...] + krr).astype(BF16)


def _mla_prep(proj2, rope_cs, rope_sn, q_a_norm, w_uq, kv_a_norm, w_ukv, qk_norm):
    N = proj2.shape[0]
    tm = 512
    qd = C_NOPE + C_ROPE
    qmap = np.full((C_HEADS * LANES,), -1, np.int64)
    kmap = np.full((C_HEADS * LANES,), -1, np.int64)
    vmap = np.full((C_HEADS * LANES,), -1, np.int64)
    for h in range(C_HEADS):
        qmap[h * LANES:h * LANES + qd] = h * qd + np.arange(qd)
        kmap[h * LANES:h * LANES + C_NOPE] = h * (C_NOPE + C_V) + np.arange(C_NOPE)
        vmap[h * LANES:h * LANES + C_V] = h * (C_NOPE + C_V) + C_NOPE + np.arange(C_V)
    wq = jnp.pad(_take_cols(w_uq, qmap), ((0, 256 - Q_RANK), (0, 0))).astype(BF16)
    wk = _take_cols(w_ukv, kmap).astype(BF16)
    wv = _take_cols(w_ukv, vmap).astype(BF16)
    gkr = jnp.pad(qk_norm[1, C_NOPE:].reshape(1, -1), ((0, 0), (C_NOPE, LANES - C_NOPE - C_ROPE)))

    def full(shape):
        return pl.BlockSpec(shape, lambda i: (0,) * len(shape))

    return pl.pallas_call(
        _mla_prep_kernel,
        grid=(N // tm,),
        in_specs=[pl.BlockSpec((tm, 256), lambda i: (i, SEG_CDQ // 256)),
                  pl.BlockSpec((tm, LANES), lambda i: (i, SEG_CDKV // LANES)),
                  pl.BlockSpec((tm, LANES), lambda i: (i, SEG_CKR // LANES)),
                  pl.BlockSpec((tm, LANES), lambda i: (i, 0)), pl.BlockSpec((tm, LANES), lambda i: (i, 0)),
                  full((256, C_HEADS * LANES)), full((KV_RANK, C_HEADS * LANES)), full((KV_RANK, C_HEADS * LANES)),
                  full((1, 256)), full((1, LANES)), full((1, LANES)), full((1, LANES)), full((1, LANES))],
        out_specs=[pl.BlockSpec((tm, C_HEADS * LANES), lambda i: (i, 0)),
                   pl.BlockSpec((tm, C_HEADS * LANES), lambda i: (i, 0)),
                   pl.BlockSpec((tm, C_HEADS * LANES), lambda i: (i, 0))],
        out_shape=[jax.ShapeDtypeStruct((N, C_HEADS * LANES), BF16),
                   jax.ShapeDtypeStruct((N, C_HEADS * LANES), BF16),
                   jax.ShapeDtypeStruct((N, C_HEADS * LANES), BF16)],
        compiler_params=_params("parallel"),
        name="mla_prep",
    )(proj2, proj2, proj2, rope_cs, rope_sn, wq, wk, wv,
      _pad_lanes(q_a_norm, 256), kv_a_norm.reshape(1, -1), _pad_lanes(qk_norm[0], LANES),
      _pad_lanes(qk_norm[1, :C_NOPE], LANES), gkr)


MLA_Q = 2 * AT


def _mla_attn_kernel(q_ref, k_ref, v_ref, causal_ref, o_ref, s_s, m_s, acc_s):
    qi = pl.program_id(2)
    m_s[...] = jnp.full(m_s.shape, NEG, F32)
    acc_s[...] = jnp.zeros(acc_s.shape, F32)
    heads = [slice(hh * LANES, (hh + 1) * LANES) for hh in range(2)]
    per_head = MLA_Q // RB
    tiles_per_q = MLA_Q // AT

    def key_rows(kc):
        return pl.ds(pl.multiple_of(kc * AT, AT), AT)

    def scores(kc):
        return [_dot_nt(q_ref[0, :, sl], k_ref[0, key_rows(kc), sl]) for sl in heads]

    def values(kc):
        return [v_ref[0, key_rows(kc), sl] for sl in heads]

    for d in range(tiles_per_q):
        kc = tiles_per_q * qi + d
        _softmax_pv(scores(kc), values(kc),
                    lambda r, d=d: causal_ref[d, pl.ds((r % per_head) * RB, RB), :], m_s, acc_s)

    for h, s in enumerate(scores(0)):
        s_s[h] = s

    def body(kc, carry):
        s_next = scores(kc + 1)
        _softmax_pv([s_s[0], s_s[1]], values(kc), None, m_s, acc_s)
        for h, s in enumerate(s_next):
            s_s[h] = s
        return carry

    lax.fori_loop(0, tiles_per_q * qi, body, 0)
    lane = lax.broadcasted_iota(jnp.int32, (MLA_Q, LANES), 1)
    outs = []
    for hh in range(2):
        a = acc_s[hh * MLA_Q:(hh + 1) * MLA_Q, :]
        outs.append(jnp.where(lane < C_V, a / a[:, C_V:C_V + 1], 0.0))
    o_ref[0] = outs[0] + pltpu.roll(outs[1], C_V, axis=1)


def _mla_attn(q, k, v):
    B, T, _ = q.shape
    qpos = np.arange(MLA_Q)[None, :, None]
    kpos = (np.arange(MLA_Q // AT)[:, None, None] * AT + np.arange(AT)[None, None, :])
    causal = np.where(qpos >= kpos, 0.0, NEG).astype(np.float32)
    pair = pl.BlockSpec((1, T, 2 * LANES), lambda b, p, i: (b, 0, p))
    return pl.pallas_call(
        _mla_attn_kernel,
        grid=(B, C_HEADS // 2, T // MLA_Q),
        in_specs=[pl.BlockSpec((1, MLA_Q, 2 * LANES), lambda b, p, i: (b, i, p)), pair, pair,
                  pl.BlockSpec((MLA_Q // AT, MLA_Q, AT), lambda b, p, i: (0, 0, 0))],
        out_specs=pl.BlockSpec((1, MLA_Q, LANES), lambda b, p, i: (b, i, p)),
        out_shape=jax.ShapeDtypeStruct((B, T, C_HEADS * C_V), F32),
        scratch_shapes=[pltpu.VMEM((2, MLA_Q, AT), F32), pltpu.VMEM((2 * MLA_Q, LANES), F32),
                        pltpu.VMEM((2 * MLA_Q, LANES), F32)],
        compiler_params=_params("parallel", "parallel", "arbitrary"),
        name="mla_attn",
    )(q, k, v, jnp.asarray(causal))


def _outproj_kernel(x_ref, ya_ref, yb_ref, yc_ref, wa_ref, wb_ref, wc_ref, ga_ref, gc_ref, g2_ref,
                    wr_ref, br_ref, x1_ref, h2_ref, ids_ref, wts_ref, cnt_ref):
    ya = _rms_lanes(ya_ref[...], ga_ref[...], A_HEADS * HEAD_DIM).astype(BF16)
    yc = _rms_lanes(yc_ref[...], gc_ref[...], C_HEADS * C_V).astype(BF16)
    x1 = (x_ref[...] + _dot(ya, wa_ref[...]) + _dot(yb_ref[...].astype(BF16), wb_ref[...])
          + _dot(yc, wc_ref[...]))
    x1_ref[...] = x1
    h2 = _rms_lanes(x1, g2_ref[...], D_MODEL)
    h2_ref[...] = _pack_bf16_pairs(h2[:, :D_MODEL // 2], h2[:, D_MODEL // 2:])
    h2_hi = h2.astype(BF16)
    h2_mid = (h2 - h2_hi.astype(F32)).astype(BF16)
    logits = (_dot(h2_hi, wr_ref[0]) + _dot(h2_mid, wr_ref[0]) + _dot(h2_hi, wr_ref[1])) + br_ref[...]
    tm = logits.shape[0]
    lane = lax.broadcasted_iota(jnp.int32, (tm, LANES), 1)
    is_g = (lane >= N_EXPERTS) & (lane < N_EXPERTS + N_GROUPS)
    gl = jnp.where(is_g, logits, NEG)
    ge = jnp.where(is_g, jnp.exp(gl - jnp.max(gl, axis=-1, keepdims=True)), 0.0)
    gp = jnp.where(is_g, ge / jnp.sum(ge, axis=-1, keepdims=True), -1.0)
    g_top = jnp.max(gp, axis=-1, keepdims=True)
    g_idx = jnp.min(jnp.where(gp == g_top, lane, 2 * LANES), axis=-1, keepdims=True) - N_EXPERTS
    in_grp = (lane < N_EXPERTS) & (lax.shift_right_logical(lane, 3) == g_idx)
    el = jnp.where(in_grp, logits, NEG)
    ee = jnp.where(in_grp, jnp.exp(el - jnp.max(el, axis=-1, keepdims=True)), 0.0)
    ep = jnp.where(in_grp, ee / jnp.sum(ee, axis=-1, keepdims=True), -1.0)
    v1 = jnp.max(ep, axis=-1, keepdims=True)
    i1 = jnp.min(jnp.where(ep == v1, lane, 2 * LANES), axis=-1, keepdims=True)
    ep2 = jnp.where(lane == i1, -1.0, ep)
    v2 = jnp.max(ep2, axis=-1, keepdims=True)
    i2 = jnp.min(jnp.where(ep2 == v2, lane, 2 * LANES), axis=-1, keepdims=True)
    tot = v1 + v2
    ids_ref[...] = jnp.where(lane == 0, i1, jnp.where(lane == 1, i2, 0))
    wts_ref[...] = jnp.where(lane == 0, v1 / tot * g_top, jnp.where(lane == 1, v2 / tot * g_top, 0.0))
    per_expert = jnp.sum(jnp.where((lane == i1) | (lane == i2), 1.0, 0.0), axis=0, keepdims=True)
    cnt_ref[...] = jnp.broadcast_to(per_expert, cnt_ref.shape)


def _outproj(x2, ya, yb, yc, w_out, nsa_out_norm, mla_out_norm, norm2_g, w_group, b_group, w_expert, b_expert):
    N = x2.shape[0]
    tm = 1024
    amap = np.full((A_HEADS * LANES,), -1, np.int64)
    for hd in range(A_HEADS):
        amap[hd * LANES:hd * LANES + HEAD_DIM] = hd * HEAD_DIM + np.arange(HEAD_DIM)
    wa = _take_rows(w_out[:384], amap).astype(BF16)
    wb = w_out[384:640].astype(BF16)
    wc = w_out[640:].astype(BF16)
    ga = _take_cols(nsa_out_norm.reshape(1, -1), amap)
    wr = jnp.pad(jnp.concatenate([w_expert, w_group], axis=1), ((0, 0), (0, LANES - N_EXPERTS - N_GROUPS)))
    wr_hi = wr.astype(BF16)
    wr = jnp.stack([wr_hi, (wr - wr_hi.astype(F32)).astype(BF16)])
    br = _pad_lanes(jnp.concatenate([b_expert, b_group]), LANES)

    def full(shape):
        return pl.BlockSpec(shape, lambda i: (0,) * len(shape))

    def rows(w):
        return pl.BlockSpec((tm, w), lambda i: (i, 0))

    return pl.pallas_call(
        _outproj_kernel,
        grid=(N // tm,),
        in_specs=[rows(D_MODEL), rows(A_HEADS * LANES), rows(256), rows(384),
                  full((A_HEADS * LANES, D_MODEL)), full((256, D_MODEL)), full((384, D_MODEL)),
                  full((1, A_HEADS * LANES)), full((1, 384)), full((1, D_MODEL)),
                  full((2, D_MODEL, LANES)), full((1, LANES))],
        out_specs=[rows(D_MODEL), rows(D_MODEL // 2), rows(LANES), rows(LANES),
                   pl.BlockSpec((8, LANES), lambda i: (i, 0))],
        out_shape=[jax.ShapeDtypeStruct((N, D_MODEL), F32), jax.ShapeDtypeStruct((N, D_MODEL // 2), jnp.uint32),
                   jax.ShapeDtypeStruct((N, LANES), jnp.int32), jax.ShapeDtypeStruct((N, LANES), F32),
                   jax.ShapeDtypeStruct((N // tm * 8, LANES), F32)],
        compiler_params=_params("parallel"),
        name="outproj_router",
    )(x2, ya, yb, yc, wa, wb, wc, ga, mla_out_norm.reshape(1, -1), norm2_g.reshape(1, -1), wr, br)


def _rank_kernel(ids_ref, off_ref, tril_ref, pos_ref, carry_s):
    i = pl.program_id(0)

    @pl.when(i == 0)
    def _():
        carry_s[...] = jnp.broadcast_to(off_ref[...], carry_s.shape)

    ids = ids_ref[...]
    tm = ids.shape[0]
    lane = lax.broadcasted_iota(jnp.int32, (tm, LANES), 1)
    i1 = ids[:, 0:1]
    i2 = ids[:, 1:2]
    oh = jnp.where((lane == i1) | (lane == i2), 1.0, 0.0)
    incl = _dot(tril_ref[...], oh.astype(BF16)) + carry_s[0:1, :]
    excl = incl - oh
    p1 = jnp.sum(jnp.where(lane == i1, excl, 0.0), axis=-1, keepdims=True)
    p2 = jnp.sum(jnp.where(lane == i2, excl, 0.0), axis=-1, keepdims=True)
    pos_ref[...] = jnp.where(lane == 0, p1, jnp.where(lane == 1, p2, 0.0)).astype(jnp.int32)
    carry_s[...] = jnp.broadcast_to(incl[tm - 1:tm, :], carry_s.shape)


def _moe_positions(ids, off_row):
    N = ids.shape[0]
    tm = 512
    tril = jnp.asarray(np.tril(np.ones((tm, tm))), BF16)
    return pl.pallas_call(
        _rank_kernel,
        grid=(N // tm,),
        in_specs=[pl.BlockSpec((tm, LANES), lambda i: (i, 0)), pl.BlockSpec((1, LANES), lambda i: (0, 0)),
                  pl.BlockSpec((tm, tm), lambda i: (0, 0))],
        out_specs=pl.BlockSpec((tm, LANES), lambda i: (i, 0)),
        out_shape=jax.ShapeDtypeStruct((N, LANES), jnp.int32),
        scratch_shapes=[pltpu.VMEM((8, LANES), F32)],
        compiler_params=_params("arbitrary"),
        name="moe_positions",
    )(ids, off_row, tril)


DISPATCH_TM = 2048


DMA_UNROLL = 8


def _dispatch_kernel(off_ref, tp_ref, nu_ref, p1_ref, p2_ref, h_ref, xs_ref, zero_s, sem, zsem):
    n_tiles = xs_ref.shape[0] // ROW_TILE

    @pl.when(pl.program_id(0) == 0)
    def _zero_fill():
        zero_s[...] = jnp.zeros(zero_s.shape, zero_s.dtype)

        def tile_copy(row0):
            return pltpu.make_async_copy(
                zero_s, xs_ref.at[pl.ds(pl.multiple_of(row0, ROW_TILE), ROW_TILE), :], zsem)

        for e in range(N_EXPERTS):
            @pl.when(tp_ref[e] > 0)
            def _start(e=e):
                tile_copy(off_ref[e] + (tp_ref[e] - 1) * ROW_TILE).start()

        def tail_start(g, carry):
            tile_copy(g * ROW_TILE).start()
            return carry

        lax.fori_loop(nu_ref[0], n_tiles, tail_start, 0)
        for e in range(N_EXPERTS):
            @pl.when(tp_ref[e] > 0)
            def _wait():
                tile_copy(0).wait()

        def tail_wait(g, carry):
            tile_copy(0).wait()
            return carry

        lax.fori_loop(nu_ref[0], n_tiles, tail_wait, 0)

    def row_copy(i, pos):
        return pltpu.make_async_copy(h_ref.at[pl.ds(i, 1), :], xs_ref.at[pl.ds(pos, 1), :], sem)

    def issue(j, carry):
        for u in range(DMA_UNROLL):
            i = j * DMA_UNROLL + u
            row_copy(i, p1_ref[i]).start(priority=0)
            row_copy(i, p2_ref[i]).start(priority=1)
        return carry

    lax.fori_loop(0, DISPATCH_TM // DMA_UNROLL, issue, 0)
    whole = pltpu.make_async_copy(h_ref, xs_ref.at[pl.ds(0, DISPATCH_TM), :], sem)
    whole.wait()
    whole.wait()


def _moe_dispatch(h2, off, tiles_per, n_used, pos1, pos2, n_rows):
    N = h2.shape[0]
    smem1d = pl.BlockSpec((DISPATCH_TM,), lambda i: (i,), memory_space=pltpu.SMEM)
    smem = pl.BlockSpec(memory_space=pltpu.SMEM)
    return pl.pallas_call(
        _dispatch_kernel,
        grid=(N // DISPATCH_TM,),
        in_specs=[smem, smem, smem, smem1d, smem1d,
                  pl.BlockSpec((DISPATCH_TM, D_MODEL // 2), lambda i: (i, 0))],
        out_specs=pl.BlockSpec(memory_space=pl.ANY),
        out_shape=jax.ShapeDtypeStruct((n_rows, D_MODEL // 2), jnp.uint32),
        scratch_shapes=[pltpu.VMEM((ROW_TILE, D_MODEL // 2), jnp.uint32), pltpu.SemaphoreType.DMA(()),
                        pltpu.SemaphoreType.DMA(())],
        compiler_params=_params("arbitrary"),
        name="moe_dispatch",
    )(off, tiles_per, n_used, pos1, pos2, h2)


def _expert_kernel(te_ref, nu_ref, x_ref, wg_ref, wu_ref, wd_ref, y_ref):
    g = pl.program_id(0)

    @pl.when(g < nu_ref[0])
    def _():
        x = jnp.concatenate(_unpack_bf16_pairs(x_ref[...]), axis=1)
        a = _dot(x, wg_ref[0, 0].astype(BF16))
        u = _dot(x, wu_ref[0, 0].astype(BF16))
        h = (a * jax.nn.sigmoid(a) * u).astype(BF16)
        y_ref[...] = _dot(h, wd_ref[0, 0].astype(BF16))

    @pl.when(g >= nu_ref[0])
    def _():
        y_ref[...] = jnp.zeros(y_ref.shape, F32)


def _moe_experts(xs, tile_expert, n_used, layer, w_gate, w_up, w_down):
    R = xs.shape[0]
    G = R // ROW_TILE

    def row_map(g, te, nu):
        return (jnp.minimum(g, nu[0] - 1), 0)

    def w_map(g, te, nu):
        return (layer, te[jnp.minimum(g, nu[0] - 1)], 0, 0)

    return pl.pallas_call(
        _expert_kernel,
        grid_spec=pltpu.PrefetchScalarGridSpec(
            num_scalar_prefetch=2,
            grid=(G,),
            in_specs=[pl.BlockSpec((ROW_TILE, D_MODEL // 2), row_map),
                      pl.BlockSpec((1, 1, D_MODEL, D_EXPERT), w_map),
                      pl.BlockSpec((1, 1, D_MODEL, D_EXPERT), w_map),
                      pl.BlockSpec((1, 1, D_EXPERT, D_MODEL), w_map)],
            out_specs=pl.BlockSpec((ROW_TILE, D_MODEL), lambda g, te, nu: (g, 0))),
        out_shape=jax.ShapeDtypeStruct((R, D_MODEL), F32),
        compiler_params=_params("arbitrary"),
        name="moe_experts",
    )(tile_expert, n_used, xs, w_gate, w_up, w_down)


COMBINE_TM = 1024


def _combine_kernel(p1_ref, p2_ref, x1_ref, wts_ref, ys_ref, o_ref, b1_s, b2_s, sem):
    def row_copy(buf, i, pos):
        return pltpu.make_async_copy(ys_ref.at[pl.ds(pos, 1), :], buf.at[pl.ds(i, 1), :], sem)

    def issue(j, carry):
        for u in range(DMA_UNROLL):
            i = j * DMA_UNROLL + u
            row_copy(b1_s, i, p1_ref[i]).start(priority=0)
            row_copy(b2_s, i, p2_ref[i]).start(priority=1)
        return carry

    lax.fori_loop(0, COMBINE_TM // DMA_UNROLL, issue, 0)
    pltpu.make_async_copy(ys_ref.at[pl.ds(0, COMBINE_TM), :], b1_s, sem).wait()
    pltpu.make_async_copy(ys_ref.at[pl.ds(0, COMBINE_TM), :], b2_s, sem).wait()
    w = wts_ref[...]
    o_ref[...] = x1_ref[...] + w[:, 0:1] * b1_s[...] + w[:, 1:2] * b2_s[...]


def _moe_combine(x1, wts, ys, pos1, pos2):
    N = x1.shape[0]
    smem1d = pl.BlockSpec((COMBINE_TM,), lambda i: (i,), memory_space=pltpu.SMEM)
    return pl.pallas_call(
        _combine_kernel,
        grid=(N // COMBINE_TM,),
        in_specs=[smem1d, smem1d,
                  pl.BlockSpec((COMBINE_TM, D_MODEL), lambda i: (i, 0)),
                  pl.BlockSpec((COMBINE_TM, LANES), lambda i: (i, 0)),
                  pl.BlockSpec(memory_space=pl.ANY)],
        out_specs=pl.BlockSpec((COMBINE_TM, D_MODEL), lambda i: (i, 0)),
        out_shape=jax.ShapeDtypeStruct((N, D_MODEL), F32),
        scratch_shapes=[pltpu.VMEM((COMBINE_TM, D_MODEL), F32), pltpu.VMEM((COMBINE_TM, D_MODEL), F32),
                        pltpu.SemaphoreType.DMA(())],
        compiler_params=_params("arbitrary"),
        name="moe_combine",
    )(pos1, pos2, x1, wts, ys)


def _moe(x1, h2, ids, wts, tile_counts, layer, w_gate, w_up, w_down):
    N = x1.shape[0]
    n_rows = 2 * N + N_EXPERTS * ROW_TILE
    counts = jnp.sum(tile_counts[::8, :N_EXPERTS], axis=0).astype(jnp.int32)
    tiles_per = (counts + ROW_TILE - 1) // ROW_TILE
    tile_end = jnp.cumsum(tiles_per)
    off = ((tile_end - tiles_per) * ROW_TILE).astype(jnp.int32)
    n_used = tile_end[-1:].astype(jnp.int32)
    tile_ids = jnp.arange(n_rows // ROW_TILE, dtype=jnp.int32)
    tile_expert = jnp.minimum(jnp.sum(tile_end[None, :] <= tile_ids[:, None], axis=1), N_EXPERTS - 1).astype(jnp.int32)
    pos = _moe_positions(ids, _pad_lanes(off.astype(F32), LANES))
    pos1, pos2 = pos[:, 0], pos[:, 1]
    xs = _moe_dispatch(h2, off, tiles_per.astype(jnp.int32), n_used, pos1, pos2, n_rows)
    ys = _moe_experts(xs, tile_expert, n_used, layer, w_gate, w_up, w_down)
    return _moe_combine(x1, wts, ys, pos1, pos2)


def kernel(x, positions, rel_bias, norm1_g, w_in, nsa_cmp_pe, nsa_cmp_w1, nsa_cmp_w2, nsa_q_norm, nsa_k_norm, nsa_out_norm, mlstm_conv, mlstm_i_bias, mlstm_f_bias, mlstm_out_norm, mla_q_a_norm, mla_w_uq, mla_kv_a_norm, mla_w_ukv, mla_qk_norm, mla_out_norm, w_out, norm2_g, moe_w_group, moe_b_group, moe_w_expert, moe_b_expert, moe_w_gate, moe_w_up, moe_w_down):
    B, T, D = x.shape
    N = B * T
    depth = w_in.shape[0]
    colmap = _proj_colmap()
    tiles, biasc = _bias_tables(rel_bias, T)
    rope_cs, rope_sn = _rope_tables(positions.reshape(N, 1))
    x2 = x.reshape(N, D)
    for l in range(depth):
        w_proj = _take_cols(w_in[l], colmap).astype(BF16)
        proj2 = _inproj(x2, norm1_g[l].reshape(1, D), w_proj)
        proj = proj2.reshape(B, T, D_PROJ)
        ya = _nsa(proj, tiles, biasc, nsa_cmp_pe[l], nsa_cmp_w1[l], nsa_cmp_w2[l], nsa_q_norm[l], nsa_k_norm[l])
        yb = _mlstm4(proj, mlstm_conv[l], mlstm_i_bias[l], mlstm_f_bias[l], mlstm_out_norm[l])
        q, k, v = _mla_prep(proj2, rope_cs, rope_sn, mla_q_a_norm[l], mla_w_uq[l], mla_kv_a_norm[l], mla_w_ukv[l], mla_qk_norm[l])
        yc = _mla_attn(q.reshape(B, T, -1), k.reshape(B, T, -1), v.reshape(B, T, -1))
        x1, h2, ids, wts, tile_counts = _outproj(x2, ya.reshape(N, -1), yb.reshape(N, -1), yc.reshape(N, -1), w_out[l],
                                    nsa_out_norm[l], mla_out_norm[l], norm2_g[l], moe_w_group[l], moe_b_group[l],
                                    moe_w_expert[l], moe_b_expert[l])
        x2 = _moe(x1, h2, ids, wts, tile_counts, l, moe_w_gate, moe_w_up, moe_w_down)
    return x2.reshape(B, T, D)
```

```python
import functools
import math

import numpy as np
import jax
import jax.numpy as jnp
from jax import lax
from jax.experimental import pallas as pl
from jax.experimental.pallas import tpu as pltpu

F32 = jnp.float32
BF16 = jnp.bfloat16

D_MODEL = 1024
HEAD_DIM = 64
LANES = 128
A_HEADS, A_KV_HEADS, A_GROUP = 6, 2, 3
CMP_BLOCK, CMP_STRIDE, SEL_BLOCK, N_SELECT, WINDOW = 32, 16, 64, 16, 512
FORCE = 1e4
B_HEADS, CONV_K, MLSTM_CHUNK = 4, 4, 64
C_HEADS, C_NOPE, C_ROPE, C_V = 6, 64, 32, 64
Q_RANK, KV_RANK = 192, 128
ROPE_THETA = 10000.0
REL_BUCKETS, REL_MAX_DIST = 32, 128
N_GROUPS, EXPERTS_PER_GROUP, N_EXPERTS, D_EXPERT = 4, 8, 32, 256
EPS = 1e-6
NEG = -1e30

LOG2E = 1.4426950408889634

AT = 256
RB = 128
ROW_TILE = 512
VMEM_LIMIT = 48 * 1024 * 1024

SEG_AQ, SEG_AKV, SEG_GATE, SEG_BQK, SEG_BV, SEG_BO = 0, 768, 2304, 2560, 3072, 3328
SEG_CDQ, SEG_CDKV, SEG_CKR, SEG_BIF, D_PROJ = 3584, 3840, 3968, 4096, 4608


def _params(*sem):
    return pltpu.CompilerParams(dimension_semantics=sem, vmem_limit_bytes=VMEM_LIMIT)


def _dot(a, b):
    return jnp.dot(a, b, preferred_element_type=F32)


def _dot_nt(a, b):
    return lax.dot_general(a, b, (((1,), (1,)), ((), ())), preferred_element_type=F32)


def _dot_tn(a, b):
    return lax.dot_general(a, b, (((0,), (0,)), ((), ())), preferred_element_type=F32)


def _split3(x):
    hi = x.astype(BF16)
    r1 = x - hi.astype(F32)
    mid = r1.astype(BF16)
    lo = (r1 - mid.astype(F32)).astype(BF16)
    return hi, mid, lo


def _pack_bf16_pairs(a, b):
    ua = lax.bitcast_convert_type(a.astype(BF16).astype(F32), jnp.uint32)
    ub = lax.bitcast_convert_type(b.astype(BF16).astype(F32), jnp.uint32)
    return lax.shift_right_logical(ua, jnp.uint32(16)) | (ub & jnp.uint32(0xFFFF0000))


def _unpack_bf16_pairs(w):
    a = lax.bitcast_convert_type(lax.shift_left(w, jnp.uint32(16)), F32)
    b = lax.bitcast_convert_type(w & jnp.uint32(0xFFFF0000), F32)
    return a.astype(BF16), b.astype(BF16)


def _bucket_np(dist):
    n = np.maximum(dist, 0)
    exact = REL_BUCKETS // 2
    nf = np.maximum(n, 1).astype(np.float64)
    large = exact + (np.log(nf / exact) / math.log(REL_MAX_DIST / exact) * (REL_BUCKETS - exact)).astype(np.int64)
    large = np.minimum(large, REL_BUCKETS - 1)
    return np.where(n < exact, n, large).astype(np.int32)


def _bucket_tables(T):
    r = np.arange(AT)[:, None]
    c = np.arange(AT)[None, :]
    diag = np.where(r >= c, _bucket_np(r - c), -1)
    prev = _bucket_np(AT + r - c)
    far = np.where(c > r, _bucket_np(2 * AT + r - c), -1)
    tiles = np.stack([diag, prev, far]).astype(np.int32)
    tq = np.arange(T)[:, None]
    n = np.arange(LANES)[None, :]
    dist_c = tq - (n * CMP_STRIDE + CMP_BLOCK - 1)
    cmp_tbl = np.where(dist_c >= 0, _bucket_np(dist_c), -1).astype(np.int32)
    return tiles, cmp_tbl


def _proj_colmap():
    cm = np.full((D_PROJ,), -1, np.int64)
    for hd in range(A_HEADS):
        cm[SEG_AQ + hd * LANES: SEG_AQ + hd * LANES + HEAD_DIM] = hd * HEAD_DIM + np.arange(HEAD_DIM)
    for s in range(12):
        cm[SEG_AKV + s * LANES: SEG_AKV + s * LANES + HEAD_DIM] = 384 + s * HEAD_DIM + np.arange(HEAD_DIM)
    for h in range(A_KV_HEADS):
        cm[SEG_GATE + h * LANES + HEAD_DIM: SEG_GATE + h * LANES + HEAD_DIM + 9] = 1152 + h * 9 + np.arange(9)
    cm[SEG_BQK: SEG_BQK + 512] = 1170 + np.arange(512)
    cm[SEG_BV: SEG_BV + 256] = 1682 + np.arange(256)
    cm[SEG_BIF: SEG_BIF + 2 * B_HEADS * HEAD_DIM] = 1938 + np.arange(2 * B_HEADS * HEAD_DIM) // HEAD_DIM
    cm[SEG_BO: SEG_BO + 256] = 1946 + np.arange(256)
    cm[SEG_CDQ: SEG_CDQ + Q_RANK] = 2202 + np.arange(Q_RANK)
    cm[SEG_CDKV: SEG_CDKV + KV_RANK] = 2394 + np.arange(KV_RANK)
    cm[SEG_CKR + C_NOPE: SEG_CKR + C_NOPE + C_ROPE] = 2522 + np.arange(C_ROPE)
    return cm


def _take_cols(w, colmap):
    wz = jnp.concatenate([w, jnp.zeros((w.shape[0], 1), w.dtype)], axis=1)
    return wz[:, np.where(colmap < 0, w.shape[1], colmap)]


def _take_rows(w, rowmap):
    return _take_cols(w.T, rowmap).T


def _pad_lanes(v, n):
    v = v.reshape(1, -1)
    return jnp.pad(v, ((0, 0), (0, n - v.shape[1])))


def _bias_kernel(rb_ref, bt_ref, bc_ref, tiles_ref, cmp_ref):
    hd = pl.program_id(0)
    bt = bt_ref[...]
    bc = bc_ref[...]
    t = jnp.full(bt.shape, NEG, F32)
    c = jnp.full(bc.shape, NEG, F32)
    far = rb_ref[REL_BUCKETS - 1, hd]
    for b in range(REL_BUCKETS):
        val = rb_ref[b, hd]
        t = jnp.where(bt == b, (val - far) * LOG2E, t)
        c = jnp.where(bc == b, val * LOG2E, c)
    tiles_ref[0, 0:3] = t
    tiles_ref[0, 3] = jnp.full((AT, AT), NEG, F32)
    cmp_ref[0] = c


def _bias_tables(rel_bias, T):
    bt, bc = _bucket_tables(T)
    return pl.pallas_call(
        _bias_kernel,
        grid=(A_HEADS,),
        in_specs=[pl.BlockSpec(memory_space=pltpu.SMEM),
                  pl.BlockSpec((3, AT, AT), lambda h: (0, 0, 0)),
                  pl.BlockSpec((T, LANES), lambda h: (0, 0))],
        out_specs=[pl.BlockSpec((1, 4, AT, AT), lambda h: (h, 0, 0, 0)),
                   pl.BlockSpec((1, T, LANES), lambda h: (h, 0, 0))],
        out_shape=[jax.ShapeDtypeStruct((A_HEADS, 4, AT, AT), F32),
                   jax.ShapeDtypeStruct((A_HEADS, T, LANES), F32)],
        compiler_params=_params("arbitrary"),
        name="bias_tables",
    )(rel_bias, jnp.asarray(bt), jnp.asarray(bc))


def _inproj_kernel(x_ref, g_ref, w_ref, o_ref):
    x = x_ref[...]
    ms = jnp.mean(x * x, axis=-1, keepdims=True)
    h = (x * lax.rsqrt(ms + EPS) * g_ref[...]).astype(BF16)
    for j in range(0, D_PROJ, 384):
        o_ref[:, j:j + 384] = _dot(h, w_ref[:, j:j + 384])


def _inproj(x2, g, w):
    N = x2.shape[0]
    tm = 512
    return pl.pallas_call(
        _inproj_kernel,
        grid=(N // tm,),
        in_specs=[pl.BlockSpec((tm, D_MODEL), lambda i: (i, 0)),
                  pl.BlockSpec((1, D_MODEL), lambda i: (0, 0)),
                  pl.BlockSpec((D_MODEL, D_PROJ), lambda i: (0, 0))],
        out_specs=pl.BlockSpec((tm, D_PROJ), lambda i: (i, 0)),
        out_shape=jax.ShapeDtypeStruct((N, D_PROJ), F32),
        compiler_params=_params("parallel"),
        name="inproj",
    )(x2, g, w)


def _rms_lanes(x, g, width):
    ms = jnp.sum(x * x, axis=-1, keepdims=True) * (1.0 / width)
    return x * lax.rsqrt(ms + EPS) * g


def _qk_halves(q_r, k_blk):
    half_rows = q_r.shape[0] // 2
    return [_dot_nt(q_r[h0:h0 + half_rows, :], k_blk) for h0 in range(0, q_r.shape[0], half_rows)]


def _flash_chunk(q_r, k_blk, v_blk, bias_fn, m_r, acc_r):
    _softmax_pv(_qk_halves(q_r, k_blk), v_blk, bias_fn, m_r, acc_r)


def _softmax_pv(s_halves, v_blk, bias_fn, m_r, acc_r):
    half_rows = m_r.shape[0] // 2
    assert half_rows % RB == 0
    for hi_, h0 in enumerate(range(0, m_r.shape[0], half_rows)):
        s_all = s_halves[hi_]
        ps, alphas = [], []
        for r in range(half_rows // RB):
            rows = slice(h0 + r * RB, h0 + (r + 1) * RB)
            s = s_all[r * RB:(r + 1) * RB]
            if bias_fn is not None:
                s = s + bias_fn((h0 + r * RB) // RB)
            lo, hi = s[:, :LANES], s[:, LANES:]
            mo = m_r[rows, :]
            mn = jnp.maximum(mo, jnp.max(jnp.maximum(lo, hi), axis=-1, keepdims=True))
            ps.append(jnp.exp2(jnp.concatenate([lo - mn, hi - mn], axis=1)).astype(BF16))
            alphas.append(jnp.exp2(mo - mn))
            m_r[rows, :] = mn
        pv = _dot(jnp.concatenate(ps, axis=0), v_blk[hi_] if isinstance(v_blk, (list, tuple)) else v_blk)
        for r in range(half_rows // RB):
            rows = slice(h0 + r * RB, h0 + (r + 1) * RB)
            acc_r[rows, :] = alphas[r] * acc_r[rows, :] + pv[r * RB:(r + 1) * RB]


def _nsa_kernel(q_ref, k0_ref, v0_ref, k1_ref, v1_ref, k2_ref, v2_ref, gate_ref, tiles_ref, biasc_ref,
                ovt_ref, place_ref, w1_ref, w2_ref, pe_ref, qg_ref, kg_ref, o_ref,
                kc_s, vc_s, k1_s, v1_s, k2_s, v2_s, qx_s, qw_s, s_s, m_s, acc_s, mw_s, accw_s):
    qi = pl.program_id(2)
    kg = kg_ref[...]
    n_cmp_pad = kc_s.shape[0]
    per_g = AT // RB

    @pl.when(qi == 0)
    def _prep():
        T = k1_s.shape[0]
        lane_t = lax.broadcasted_iota(jnp.int32, (T, LANES), 1)
        row_t = lax.broadcasted_iota(jnp.int32, (T, LANES), 0)
        block_onehot = lax.shift_right_logical(row_t, 6) == lane_t - HEAD_DIM
        ones_lane = (lane_t >= HEAD_DIM) & (lane_t < HEAD_DIM + 3 * A_GROUP)
        k1_s[...] = jnp.where(block_onehot, 1.0, _rms_lanes(k1_ref[0], kg, HEAD_DIM)).astype(BF16)
        v1_s[...] = jnp.where(ones_lane, 1.0, v1_ref[0]).astype(BF16)
        k2_s[...] = _rms_lanes(k2_ref[0], kg, HEAD_DIM).astype(BF16)
        v2_s[...] = jnp.where(ones_lane, 1.0, v2_ref[0]).astype(BF16)
        half = CMP_BLOCK // 2
        for s, (src, dst) in enumerate(((k0_ref, kc_s), (v0_ref, vc_s))):
            a = jnp.zeros((n_cmp_pad, HEAD_DIM), F32)
            bm = jnp.zeros((n_cmp_pad, HEAD_DIM), F32)
            for j in range(half):
                xj = src[0, pl.ds(j, n_cmp_pad, stride=CMP_STRIDE), :][:, :HEAD_DIM]
                a = a + _dot((xj + pe_ref[s, j:j + 1, :]).astype(BF16), w1_ref[s, j])
                bm = bm + _dot((xj + pe_ref[s, half + j:half + j + 1, :]).astype(BF16), w1_ref[s, half + j])
            pre = a + pltpu.roll(bm, n_cmp_pad - 1, axis=0)
            hid = pre * jax.nn.sigmoid(pre)
            comp = _dot(hid.astype(BF16), w2_ref[s])
            if s == 0:
                comp = _rms_lanes(comp, kg, HEAD_DIM)
            dst[...] = comp.astype(BF16)

    qscale = HEAD_DIM ** -0.5 * LOG2E
    q = q_ref[0]
    qg = qg_ref[...]
    qs = jnp.concatenate(
        [_rms_lanes(q[:, g * LANES:(g + 1) * LANES], qg, HEAD_DIM) * qscale for g in range(A_GROUP)],
        axis=0).astype(BF16)
    qw_s[...] = qs

    def near_bias(delta, kind):
        kd = jnp.where(qi >= delta, kind, 3)
        return lambda r: tiles_ref[r // per_g, kd, pl.ds((r % per_g) * RB, RB), :]

    def key_rows(kc):
        return pl.ds(pl.multiple_of(kc * AT, AT), AT)

    mw_s[...] = jnp.full(mw_s.shape, NEG, F32)
    accw_s[...] = jnp.zeros(accw_s.shape, F32)
    for delta, kind in ((0, 0), (1, 1), (2, 2)):
        rows_k = key_rows(jnp.maximum(qi - delta, 0))
        _flash_chunk(qw_s, k2_s[rows_k, :], v2_s[rows_k, :], near_bias(delta, kind), mw_s, accw_s)

    biasc = biasc_ref[...].reshape(A_GROUP * AT, LANES)
    valid = biasc > 0.5 * NEG
    lcm = _dot_nt(qs, kc_s[...]) + biasc
    mc = jnp.max(lcm, axis=-1, keepdims=True)
    pc = jnp.where(valid, jnp.exp2(lcm - mc), 0.0)
    lc = jnp.sum(pc, axis=-1, keepdims=True)
    pc = pc * jnp.where(lc > 0.0, 1.0 / lc, 0.0)
    o_c = _dot(pc.astype(BF16), vc_s[...])

    ovt = ovt_ref[...]
    p_hi, p_mid, p_lo = _split3(pc[0:AT] + pc[AT:2 * AT] + pc[2 * AT:3 * AT])
    ps3 = _dot_nt(ovt, jnp.concatenate([p_hi, p_mid, p_lo], axis=0))
    ps = ps3[:, :AT] + ps3[:, AT:2 * AT] + ps3[:, 2 * AT:]
    n_sel = ps.shape[0]
    sid = lax.broadcasted_iota(jnp.int32, (n_sel, AT), 0)
    qcol = lax.broadcasted_iota(jnp.int32, (n_sel, AT), 1)
    cur = qi * (AT // SEL_BLOCK) + lax.shift_right_logical(qcol, 6)
    forced = (sid == 0) | (sid == cur) | (sid == cur - 1)
    psf = jnp.where(forced, ps + FORCE, ps)
    psf = jnp.where(sid <= cur, psf, -1.0)
    rank = jnp.zeros((n_sel, AT), F32)
    for sp in range(n_sel):
        other = psf[sp:sp + 1, :]
        beats = (other > psf) | ((other == psf) & (sid > sp))
        rank = rank + jnp.where(beats, 1.0, 0.0)
    keep = (rank < float(min(N_SELECT, n_sel))) & (psf >= 0.0)
    selneg = _dot_tn(jnp.where(keep, 0.0, NEG).astype(BF16), place_ref[...]).astype(BF16)
    lane_q = lax.broadcasted_iota(jnp.int32, (AT, LANES), 1)
    for g in range(A_GROUP):
        qx_s[g * AT:(g + 1) * AT, :] = jnp.where(lane_q < HEAD_DIM, qs[g * AT:(g + 1) * AT], selneg)

    m_s[...] = jnp.full(m_s.shape, NEG, F32)
    acc_s[...] = jnp.zeros(acc_s.shape, F32)
    for delta, kind in ((0, 0), (1, 1)):
        rows_k = key_rows(jnp.maximum(qi - delta, 0))
        _flash_chunk(qx_s, k1_s[rows_k, :], v1_s[rows_k, :], near_bias(delta, kind), m_s, acc_s)

    for h, s in enumerate(_qk_halves(qx_s, k1_s[key_rows(0), :])):
        s_s[h] = s

    def far_body(kc, carry):
        s_next = _qk_halves(qx_s, k1_s[key_rows(kc + 1), :])
        _softmax_pv([s_s[0], s_s[1]], v1_s[key_rows(kc), :], None, m_s, acc_s)
        for h, s in enumerate(s_next):
            s_s[h] = s
        return carry

    lax.fori_loop(0, jnp.maximum(qi - 1, 0), far_body, 0)

    gt = jax.nn.sigmoid(gate_ref[0])
    gate_lanes = (lane_q >= HEAD_DIM) & (lane_q < HEAD_DIM + 3 * A_GROUP)
    for g in range(A_GROUP):
        rows = slice(g * AT, (g + 1) * AT)
        a_s = acc_s[rows, :]
        a_w = accw_s[rows, :]
        c_s = gt / jnp.where(gate_lanes, a_s, 1.0)
        c_w = gt / jnp.where(gate_lanes, a_w, 1.0)
        j = HEAD_DIM + 3 * g
        o = gt[:, j:j + 1] * o_c[rows] + c_s[:, j + 1:j + 2] * a_s + c_w[:, j + 2:j + 3] * a_w
        o_ref[0, :, g * LANES:(g + 1) * LANES] = jnp.where(lane_q < HEAD_DIM, o, 0.0)


def _nsa(proj, tiles, biasc, cmp_pe, cmp_w1, cmp_w2, q_norm, k_norm):
    B, T, _ = proj.shape
    nq = T // AT
    n_cmp_pad = T // CMP_STRIDE
    n_sel = T // SEL_BLOCK
    assert n_sel <= LANES - HEAD_DIM - 1 and n_cmp_pad == LANES
    starts = np.arange(n_cmp_pad) * CMP_STRIDE
    sid = np.arange(n_sel)
    overlap = ((starts[:, None] < (sid[None, :] + 1) * SEL_BLOCK)
               & (starts[:, None] + CMP_BLOCK > sid[None, :] * SEL_BLOCK)
               & (starts[:, None] + CMP_BLOCK <= T))
    place = np.zeros((n_sel, LANES), np.float32)
    place[sid, HEAD_DIM + sid] = 1.0
    w2p = jnp.pad(cmp_w2, ((0, 0), (0, 0), (0, LANES - HEAD_DIM)))

    def slab(br, kv):
        base = SEG_AKV // LANES + (br * 2 + kv) * 2
        return pl.BlockSpec((1, T, LANES), lambda h, b, i: (b, 0, base + h))

    def full(shape):
        return pl.BlockSpec(shape, lambda h, b, i: (0,) * len(shape))

    rows = A_GROUP * AT
    return pl.pallas_call(
        _nsa_kernel,
        grid=(A_KV_HEADS, B, nq),
        in_specs=[pl.BlockSpec((1, AT, A_GROUP * LANES), lambda h, b, i: (b, i, h)),
                  slab(0, 0), slab(0, 1), slab(1, 0), slab(1, 1), slab(2, 0), slab(2, 1),
                  pl.BlockSpec((1, AT, LANES), lambda h, b, i: (b, i, SEG_GATE // LANES + h)),
                  pl.BlockSpec((A_GROUP, 4, AT, AT), lambda h, b, i: (h, 0, 0, 0)),
                  pl.BlockSpec((A_GROUP, AT, LANES), lambda h, b, i: (h, i, 0)),
                  full((n_sel, n_cmp_pad)), full((n_sel, LANES)),
                  full((2, CMP_BLOCK, HEAD_DIM, HEAD_DIM)), full((2, HEAD_DIM, LANES)),
                  full((2, CMP_BLOCK, HEAD_DIM)), full((1, LANES)), full((1, LANES))],
        out_specs=pl.BlockSpec((1, AT, A_GROUP * LANES), lambda h, b, i: (b, i, h)),
        out_shape=jax.ShapeDtypeStruct((B, T, A_HEADS * LANES), F32),
        scratch_shapes=[pltpu.VMEM((n_cmp_pad, LANES), BF16), pltpu.VMEM((n_cmp_pad, LANES), BF16),
                        pltpu.VMEM((T, LANES), BF16), pltpu.VMEM((T, LANES), BF16),
                        pltpu.VMEM((T, LANES), BF16), pltpu.VMEM((T, LANES), BF16),
                        pltpu.VMEM((rows, LANES), BF16), pltpu.VMEM((rows, LANES), BF16),
                        pltpu.VMEM((2, rows // 2, AT), F32),
                        pltpu.VMEM((rows, LANES), F32), pltpu.VMEM((rows, LANES), F32),
                        pltpu.VMEM((rows, LANES), F32), pltpu.VMEM((rows, LANES), F32)],
        compiler_params=_params("parallel", "parallel", "arbitrary"),
        name="nsa",
    )(proj, proj, proj, proj, proj, proj, proj, proj, tiles, biasc,
      jnp.asarray(overlap.T, BF16), jnp.asarray(place, BF16), cmp_w1.astype(BF16), w2p.astype(BF16), cmp_pe,
      _pad_lanes(q_norm, LANES), _pad_lanes(k_norm, LANES))


MLSTM_NB = 4
MLSTM_TB = 512


def _mlstm4_kernel(qk_ref, v_ref, o_ref, gi_ref, gf_ref, cw_ref, bi_ref, bf_ref, gn_ref, tril_ref, y_ref,
                   xpad_s, c_s, n_s, m_s):
    nb, tb = qk_ref.shape[0], qk_ref.shape[1]
    L = MLSTM_CHUNK
    W = B_HEADS * HEAD_DIM

    @pl.when(pl.program_id(1) == 0)
    def _init():
        xpad_s[:, tb:tb + 8, :] = jnp.zeros((nb, 8, xpad_s.shape[2]), F32)
        c_s[...] = jnp.zeros(c_s.shape, F32)
        n_s[...] = jnp.zeros(n_s.shape, F32)
        m_s[...] = jnp.zeros(m_s.shape, F32)

    for bb in range(nb):
        xpad_s[bb, 0:8, :] = xpad_s[bb, tb:tb + 8, :]
        xpad_s[bb, 8:8 + tb, :] = qk_ref[bb]
    lane_head = lax.shift_right_logical(lax.broadcasted_iota(jnp.int32, (1, W), 1), 6)
    seg = [lane_head == a for a in range(B_HEADS)]
    row_l = lax.broadcasted_iota(jnp.int32, (L, W), 0)
    lane_l = lax.broadcasted_iota(jnp.int32, (L, W), 1) & (HEAD_DIM - 1)
    tri = lane_l <= row_l
    diag = lane_l == row_l
    blockdiag = (lax.shift_right_logical(lax.broadcasted_iota(jnp.int32, (W, W), 0), 6)
                 == lax.shift_right_logical(lax.broadcasted_iota(jnp.int32, (W, W), 1), 6))
    kscale = HEAD_DIM ** -0.5

    def seg_reduce(x, op, fill):
        out = None
        for a in range(B_HEADS):
            r = op(jnp.where(seg[a], x, fill)[:, (a // 2) * LANES:(a // 2 + 1) * LANES], axis=-1, keepdims=True)
            out = r if out is None else jnp.where(seg[a], r, out)
        return out

    def chunk_one(bb, c):
        r0 = pl.multiple_of(c * L, L)
        win = xpad_s[bb, pl.ds(r0, L + 8), :]
        conv = cw_ref[CONV_K - 1:CONV_K, :] * win[8:8 + L]
        for j in range(CONV_K - 1):
            conv = conv + cw_ref[j:j + 1, :] * pltpu.roll(win, CONV_K - 1 - j, axis=0)[8:8 + L]
        qk = conv * jax.nn.sigmoid(conv)
        q4 = qk[:, :W]
        k4 = qk[:, W:] * kscale
        v4 = v_ref[bb, pl.ds(r0, L), :]
        i4 = gi_ref[bb, pl.ds(r0, L), :] + bi_ref[...]
        logf = jax.nn.log_sigmoid(gf_ref[bb, pl.ds(r0, L), :] + bf_ref[...])
        lh, lm, ll = _split3(logf)
        cum = _dot(tril_ref[...], jnp.concatenate([lh, lm, ll], axis=1))
        b4 = cum[:, :W] + cum[:, W:2 * W] + cum[:, 2 * W:]
        c_old, n_old, m_prev = c_s[bb], n_s[bb], m_s[bb][0:1, :]
        r4 = jnp.sum(jnp.where(diag, i4 - b4, 0.0), axis=0, keepdims=True)
        dmat = jnp.where(tri, b4 + r4, -jnp.inf)
        m_inter = b4 + m_prev
        m_j = jnp.maximum(m_inter, seg_reduce(dmat, jnp.max, -jnp.inf))
        isc = jnp.exp(m_inter - m_j)
        k_heads = jnp.concatenate([jnp.where(seg[a], k4, 0.0) for a in range(B_HEADS)], axis=0)
        v_heads = jnp.concatenate([jnp.where(seg[a], v4, 0.0) for a in range(B_HEADS)], axis=0)
        s = _dot_nt(q4, k_heads) * jnp.exp(dmat - m_j)
        den = seg_reduce(s, jnp.sum, 0.0) + isc * seg_reduce(q4 * n_old, jnp.sum, 0.0)
        num = _dot(s, v_heads) + isc * _dot_nt(q4, c_old)
        h = num / jnp.maximum(jnp.abs(den), jnp.exp(-m_j))
        b_last = b4[L - 1:L, :]
        w_log = b_last - b4 + i4
        m_new = jnp.maximum(b_last + m_prev, jnp.max(w_log, axis=0, keepdims=True))
        w4 = jnp.exp(w_log - m_new)
        dec = jnp.exp(b_last + m_prev - m_new)
        c_s[bb] = dec * c_old + jnp.where(blockdiag, _dot_tn(v4 * w4, k4), 0.0)
        n_s[bb] = dec * n_old + jnp.sum(w4 * k4, axis=0, keepdims=True)
        m_s[bb] = jnp.broadcast_to(m_new, m_s.shape[1:])
        ms = seg_reduce(h * h, jnp.sum, 0.0) * (1.0 / HEAD_DIM)
        hn = h * lax.rsqrt(ms + EPS) * gn_ref[...]
        y_ref[bb, pl.ds(r0, L), :] = jax.nn.sigmoid(o_ref[bb, pl.ds(r0, L), :]) * hn

    def chunk(c, carry):
        for bb in range(nb):
            chunk_one(bb, c)
        return carry

    lax.fori_loop(0, tb // L, chunk, 0)


def _mlstm4(proj, conv_w, i_bias, f_bias, out_norm):
    B, T, _ = proj.shape
    W = B_HEADS * HEAD_DIM
    nb, tb = MLSTM_NB, MLSTM_TB
    tril = jnp.asarray(np.tril(np.ones((MLSTM_CHUNK, MLSTM_CHUNK))), BF16)

    def full(shape):
        return pl.BlockSpec(shape, lambda b, t: (0,) * len(shape))

    def seg(width, off):
        return pl.BlockSpec((nb, tb, width), lambda b, t: (b, t, off // width))

    return pl.pallas_call(
        _mlstm4_kernel,
        grid=(B // nb, T // tb),
        in_specs=[seg(2 * W, SEG_BQK), seg(W, SEG_BV), seg(W, SEG_BO), seg(W, SEG_BIF), seg(W, SEG_BIF + W),
                  full((CONV_K, 2 * W)), full((1, W)), full((1, W)), full((1, W)),
                  full((MLSTM_CHUNK, MLSTM_CHUNK))],
        out_specs=pl.BlockSpec((nb, tb, W), lambda b, t: (b, t, 0)),
        out_shape=jax.ShapeDtypeStruct((B, T, W), F32),
        scratch_shapes=[pltpu.VMEM((nb, tb + 8, 2 * W), F32),
                        pltpu.VMEM((nb, W, W), F32),
                        pltpu.VMEM((nb, 1, W), F32),
                        pltpu.VMEM((nb, 8, W), F32)],
        compiler_params=_params("parallel", "arbitrary"),
        name="mlstm",
    )(proj, proj, proj, proj, proj, conv_w, jnp.repeat(i_bias, HEAD_DIM).reshape(1, W),
      jnp.repeat(f_bias, HEAD_DIM).reshape(1, W), out_norm.reshape(1, W), tril)


def _rope_kernel(pos_ref, invf_ref, sgn_ref, cs_ref, sn_ref):
    ang = pos_ref[...].astype(F32) * invf_ref[...]
    cs_ref[...] = jnp.cos(ang)
    sn_ref[...] = jnp.sin(ang) * sgn_ref[...]


def _rope_tables(pos):
    N = pos.shape[0]
    tm = 1024
    half = C_ROPE // 2
    inv = 1.0 / (ROPE_THETA ** (np.arange(half, dtype=np.float32) * 2.0 / C_ROPE))
    invf = np.zeros((1, LANES), np.float32)
    sgn = np.zeros((1, LANES), np.float32)
    invf[0, C_NOPE:C_NOPE + C_ROPE] = np.concatenate([inv, inv])
    sgn[0, C_NOPE:C_NOPE + half] = -1.0
    sgn[0, C_NOPE + half:C_NOPE + C_ROPE] = 1.0
    row = pl.BlockSpec((1, LANES), lambda i: (0, 0))
    tile = pl.BlockSpec((tm, LANES), lambda i: (i, 0))
    return pl.pallas_call(
        _rope_kernel,
        grid=(N // tm,),
        in_specs=[pl.BlockSpec((tm, 1), lambda i: (i, 0)), row, row],
        out_specs=[tile, tile],
        out_shape=[jax.ShapeDtypeStruct((N, LANES), F32), jax.ShapeDtypeStruct((N, LANES), F32)],
        compiler_params=_params("parallel"),
        name="rope_tables",
    )(pos, jnp.asarray(invf), jnp.asarray(sgn))


def _mla_prep_kernel(dq_ref, dkv_ref, kr_ref, cs_ref, sn_ref, wq_ref, wk_ref, wv_ref, qa_ref, kva_ref,
                     gq_ref, gk_ref, gkr_ref, q_out, k_out, v_out):
    lane = lax.broadcasted_iota(jnp.int32, (1, LANES), 1)
    nope = lane < C_NOPE
    ropem = (lane >= C_NOPE) & (lane < C_NOPE + C_ROPE)
    first = lane < C_NOPE + C_ROPE // 2
    cs = cs_ref[...]
    sn = sn_ref[...]

    def rope(x):
        partner = jnp.where(first, pltpu.roll(x, LANES - C_ROPE // 2, axis=1), pltpu.roll(x, C_ROPE // 2, axis=1))
        return x * cs + partner * sn

    scale = (C_NOPE + C_ROPE) ** -0.5 * LOG2E
    qn = _rms_lanes(dq_ref[...], qa_ref[...], Q_RANK).astype(BF16)
    q = _dot(qn, wq_ref[...])
    kvn = _rms_lanes(dkv_ref[...], kva_ref[...], KV_RANK).astype(BF16)
    kk = _dot(kvn, wk_ref[...])
    vv = _dot(kvn, wv_ref[...])
    lane_v = lax.broadcasted_iota(jnp.int32, vv.shape, 1)
    v_out[...] = jnp.where((lane_v & (LANES - 1)) == C_V, 1.0, vv).astype(BF16)
    kr = kr_ref[...]
    krn = kr * lax.rsqrt(jnp.sum(kr * kr, axis=-1, keepdims=True) * (1.0 / C_ROPE) + EPS) * gkr_ref[...]
    krr = rope(krn)
    for h in range(C_HEADS):
        sl = slice(h * LANES, (h + 1) * LANES)
        qh = q[:, sl]
        sq = qh * qh
        msn = jnp.sum(jnp.where(nope, sq, 0.0), axis=-1, keepdims=True) * (1.0 / C_NOPE)
        msr = jnp.sum(jnp.where(ropem, sq, 0.0), axis=-1, keepdims=True) * (1.0 / C_ROPE)
        qhn = qh * jnp.where(nope, lax.rsqrt(msn + EPS), lax.rsqrt(msr + EPS)) * gq_ref[...]
        q_out[:, sl] = (rope(qhn) * scale).astype(BF16)
        kh = kk[:, sl]
        msk = jnp.sum(kh * kh, axis=-1, keepdims=True) * (1.0 / C_NOPE)
        k_out[:, sl] = (kh * lax.rsqrt(msk + EPS) * gk_ref[...] + krr).astype(BF16)


def _mla_prep(proj2, rope_cs, rope_sn, q_a_norm, w_uq, kv_a_norm, w_ukv, qk_norm):
    N = proj2.shape[0]
    tm = 512
    qd = C_NOPE + C_ROPE
    qmap = np.full((C_HEADS * LANES,), -1, np.int64)
    kmap = np.full((C_HEADS * LANES,), -1, np.int64)
    vmap = np.full((C_HEADS * LANES,), -1, np.int64)
    for h in range(C_HEADS):
        qmap[h * LANES:h * LANES + qd] = h * qd + np.arange(qd)
        kmap[h * LANES:h * LANES + C_NOPE] = h * (C_NOPE + C_V) + np.arange(C_NOPE)
        vmap[h * LANES:h * LANES + C_V] = h * (C_NOPE + C_V) + C_NOPE + np.arange(C_V)
    wq = jnp.pad(_take_cols(w_uq, qmap), ((0, 256 - Q_RANK), (0, 0))).astype(BF16)
    wk = _take_cols(w_ukv, kmap).astype(BF16)
    wv = _take_cols(w_ukv, vmap).astype(BF16)
    gkr = jnp.pad(qk_norm[1, C_NOPE:].reshape(1, -1), ((0, 0), (C_NOPE, LANES - C_NOPE - C_ROPE)))

    def full(shape):
        return pl.BlockSpec(shape, lambda i: (0,) * len(shape))

    return pl.pallas_call(
        _mla_prep_kernel,
        grid=(N // tm,),
        in_specs=[pl.BlockSpec((tm, 256), lambda i: (i, SEG_CDQ // 256)),
                  pl.BlockSpec((tm, LANES), lambda i: (i, SEG_CDKV // LANES)),
                  pl.BlockSpec((tm, LANES), lambda i: (i, SEG_CKR // LANES)),
                  pl.BlockSpec((tm, LANES), lambda i: (i, 0)), pl.BlockSpec((tm, LANES), lambda i: (i, 0)),
                  full((256, C_HEADS * LANES)), full((KV_RANK, C_HEADS * LANES)), full((KV_RANK, C_HEADS * LANES)),
                  full((1, 256)), full((1, LANES)), full((1, LANES)), full((1, LANES)), full((1, LANES))],
        out_specs=[pl.BlockSpec((tm, C_HEADS * LANES), lambda i: (i, 0)),
                   pl.BlockSpec((tm, C_HEADS * LANES), lambda i: (i, 0)),
                   pl.BlockSpec((tm, C_HEADS * LANES), lambda i: (i, 0))],
        out_shape=[jax.ShapeDtypeStruct((N, C_HEADS * LANES), BF16),
                   jax.ShapeDtypeStruct((N, C_HEADS * LANES), BF16),
                   jax.ShapeDtypeStruct((N, C_HEADS * LANES), BF16)],
        compiler_params=_params("parallel"),
        name="mla_prep",
    )(proj2, proj2, proj2, rope_cs, rope_sn, wq, wk, wv,
      _pad_lanes(q_a_norm, 256), kv_a_norm.reshape(1, -1), _pad_lanes(qk_norm[0], LANES),
      _pad_lanes(qk_norm[1, :C_NOPE], LANES), gkr)


MLA_Q = 2 * AT


def _mla_attn_kernel(q_ref, k_ref, v_ref, causal_ref, o_ref, s_s, m_s, acc_s):
    qi = pl.program_id(2)
    m_s[...] = jnp.full(m_s.shape, NEG, F32)
    acc_s[...] = jnp.zeros(acc_s.shape, F32)
    heads = [slice(hh * LANES, (hh + 1) * LANES) for hh in range(2)]
    per_head = MLA_Q // RB
    tiles_per_q = MLA_Q // AT

    def key_rows(kc):
        return pl.ds(pl.multiple_of(kc * AT, AT), AT)

    def scores(kc):
        return [_dot_nt(q_ref[0, :, sl], k_ref[0, key_rows(kc), sl]) for sl in heads]

    def values(kc):
        return [v_ref[0, key_rows(kc), sl] for sl in heads]

    for d in range(tiles_per_q):
        kc = tiles_per_q * qi + d
        _softmax_pv(scores(kc), values(kc),
                    lambda r, d=d: causal_ref[d, pl.ds((r % per_head) * RB, RB), :], m_s, acc_s)

    for h, s in enumerate(scores(0)):
        s_s[h] = s

    def body(kc, carry):
        s_next = scores(kc + 1)
        _softmax_pv([s_s[0], s_s[1]], values(kc), None, m_s, acc_s)
        for h, s in enumerate(s_next):
            s_s[h] = s
        return carry

    lax.fori_loop(0, tiles_per_q * qi, body, 0)
    lane = lax.broadcasted_iota(jnp.int32, (MLA_Q, LANES), 1)
    outs = []
    for hh in range(2):
        a = acc_s[hh * MLA_Q:(hh + 1) * MLA_Q, :]
        outs.append(jnp.where(lane < C_V, a / a[:, C_V:C_V + 1], 0.0))
    o_ref[0] = outs[0] + pltpu.roll(outs[1], C_V, axis=1)


def _mla_attn(q, k, v):
    B, T, _ = q.shape
    qpos = np.arange(MLA_Q)[None, :, None]
    kpos = (np.arange(MLA_Q // AT)[:, None, None] * AT + np.arange(AT)[None, None, :])
    causal = np.where(qpos >= kpos, 0.0, NEG).astype(np.float32)
    pair = pl.BlockSpec((1, T, 2 * LANES), lambda b, p, i: (b, 0, p))
    return pl.pallas_call(
        _mla_attn_kernel,
        grid=(B, C_HEADS // 2, T // MLA_Q),
        in_specs=[pl.BlockSpec((1, MLA_Q, 2 * LANES), lambda b, p, i: (b, i, p)), pair, pair,
                  pl.BlockSpec((MLA_Q // AT, MLA_Q, AT), lambda b, p, i: (0, 0, 0))],
        out_specs=pl.BlockSpec((1, MLA_Q, LANES), lambda b, p, i: (b, i, p)),
        out_shape=jax.ShapeDtypeStruct((B, T, C_HEADS * C_V), F32),
        scratch_shapes=[pltpu.VMEM((2, MLA_Q, AT), F32), pltpu.VMEM((2 * MLA_Q, LANES), F32),
                        pltpu.VMEM((2 * MLA_Q, LANES), F32)],
        compiler_params=_params("parallel", "parallel", "arbitrary"),
        name="mla_attn",
    )(q, k, v, jnp.asarray(causal))


def _outproj_kernel(x_ref, ya_ref, yb_ref, yc_ref, wa_ref, wb_ref, wc_ref, ga_ref, gc_ref, g2_ref,
                    wr_ref, br_ref, x1_ref, h2_ref, ids_ref, wts_ref, cnt_ref):
    ya = _rms_lanes(ya_ref[...], ga_ref[...], A_HEADS * HEAD_DIM).astype(BF16)
    yc = _rms_lanes(yc_ref[...], gc_ref[...], C_HEADS * C_V).astype(BF16)
    x1 = (x_ref[...] + _dot(ya, wa_ref[...]) + _dot(yb_ref[...].astype(BF16), wb_ref[...])
          + _dot(yc, wc_ref[...]))
    x1_ref[...] = x1
    h2 = _rms_lanes(x1, g2_ref[...], D_MODEL)
    h2_ref[...] = _pack_bf16_pairs(h2[:, :D_MODEL // 2], h2[:, D_MODEL // 2:])
    h2_hi = h2.astype(BF16)
    h2_mid = (h2 - h2_hi.astype(F32)).astype(BF16)
    logits = (_dot(h2_hi, wr_ref[0]) + _dot(h2_mid, wr_ref[0]) + _dot(h2_hi, wr_ref[1])) + br_ref[...]
    tm = logits.shape[0]
    lane = lax.broadcasted_iota(jnp.int32, (tm, LANES), 1)
    is_g = (lane >= N_EXPERTS) & (lane < N_EXPERTS + N_GROUPS)
    gl = jnp.where(is_g, logits, NEG)
    ge = jnp.where(is_g, jnp.exp(gl - jnp.max(gl, axis=-1, keepdims=True)), 0.0)
    gp = jnp.where(is_g, ge / jnp.sum(ge, axis=-1, keepdims=True), -1.0)
    g_top = jnp.max(gp, axis=-1, keepdims=True)
    g_idx = jnp.min(jnp.where(gp == g_top, lane, 2 * LANES), axis=-1, keepdims=True) - N_EXPERTS
    in_grp = (lane < N_EXPERTS) & (lax.shift_right_logical(lane, 3) == g_idx)
    el = jnp.where(in_grp, logits, NEG)
    ee = jnp.where(in_grp, jnp.exp(el - jnp.max(el, axis=-1, keepdims=True)), 0.0)
    ep = jnp.where(in_grp, ee / jnp.sum(ee, axis=-1, keepdims=True), -1.0)
    v1 = jnp.max(ep, axis=-1, keepdims=True)
    i1 = jnp.min(jnp.where(ep == v1, lane, 2 * LANES), axis=-1, keepdims=True)
    ep2 = jnp.where(lane == i1, -1.0, ep)
    v2 = jnp.max(ep2, axis=-1, keepdims=True)
    i2 = jnp.min(jnp.where(ep2 == v2, lane, 2 * LANES), axis=-1, keepdims=True)
    tot = v1 + v2
    ids_ref[...] = jnp.where(lane == 0, i1, jnp.where(lane == 1, i2, 0))
    wts_ref[...] = jnp.where(lane == 0, v1 / tot * g_top, jnp.where(lane == 1, v2 / tot * g_top, 0.0))
    per_expert = jnp.sum(jnp.where((lane == i1) | (lane == i2), 1.0, 0.0), axis=0, keepdims=True)
    cnt_ref[...] = jnp.broadcast_to(per_expert, cnt_ref.shape)


def _outproj(x2, ya, yb, yc, w_out, nsa_out_norm, mla_out_norm, norm2_g, w_group, b_group, w_expert, b_expert):
    N = x2.shape[0]
    tm = 1024
    amap = np.full((A_HEADS * LANES,), -1, np.int64)
    for hd in range(A_HEADS):
        amap[hd * LANES:hd * LANES + HEAD_DIM] = hd * HEAD_DIM + np.arange(HEAD_DIM)
    wa = _take_rows(w_out[:384], amap).astype(BF16)
    wb = w_out[384:640].astype(BF16)
    wc = w_out[640:].astype(BF16)
    ga = _take_cols(nsa_out_norm.reshape(1, -1), amap)
    wr = jnp.pad(jnp.concatenate([w_expert, w_group], axis=1), ((0, 0), (0, LANES - N_EXPERTS - N_GROUPS)))
    wr_hi = wr.astype(BF16)
    wr = jnp.stack([wr_hi, (wr - wr_hi.astype(F32)).astype(BF16)])
    br = _pad_lanes(jnp.concatenate([b_expert, b_group]), LANES)

    def full(shape):
        return pl.BlockSpec(shape, lambda i: (0,) * len(shape))

    def rows(w):
        return pl.BlockSpec((tm, w), lambda i: (i, 0))

    return pl.pallas_call(
        _outproj_kernel,
        grid=(N // tm,),
        in_specs=[rows(D_MODEL), rows(A_HEADS * LANES), rows(256), rows(384),
                  full((A_HEADS * LANES, D_MODEL)), full((256, D_MODEL)), full((384, D_MODEL)),
                  full((1, A_HEADS * LANES)), full((1, 384)), full((1, D_MODEL)),
                  full((2, D_MODEL, LANES)), full((1, LANES))],
        out_specs=[rows(D_MODEL), rows(D_MODEL // 2), rows(LANES), rows(LANES),
                   pl.BlockSpec((8, LANES), lambda i: (i, 0))],
        out_shape=[jax.ShapeDtypeStruct((N, D_MODEL), F32), jax.ShapeDtypeStruct((N, D_MODEL // 2), jnp.uint32),
                   jax.ShapeDtypeStruct((N, LANES), jnp.int32), jax.ShapeDtypeStruct((N, LANES), F32),
                   jax.ShapeDtypeStruct((N // tm * 8, LANES), F32)],
        compiler_params=_params("parallel"),
        name="outproj_router",
    )(x2, ya, yb, yc, wa, wb, wc, ga, mla_out_norm.reshape(1, -1), norm2_g.reshape(1, -1), wr, br)


def _rank_kernel(ids_ref, off_ref, tril_ref, pos_ref, carry_s):
    i = pl.program_id(0)

    @pl.when(i == 0)
    def _():
        carry_s[...] = jnp.broadcast_to(off_ref[...], carry_s.shape)

    ids = ids_ref[...]
    tm = ids.shape[0]
    lane = lax.broadcasted_iota(jnp.int32, (tm, LANES), 1)
    i1 = ids[:, 0:1]
    i2 = ids[:, 1:2]
    oh = jnp.where((lane == i1) | (lane == i2), 1.0, 0.0)
    incl = _dot(tril_ref[...], oh.astype(BF16)) + carry_s[0:1, :]
    excl = incl - oh
    p1 = jnp.sum(jnp.where(lane == i1, excl, 0.0), axis=-1, keepdims=True)
    p2 = jnp.sum(jnp.where(lane == i2, excl, 0.0), axis=-1, keepdims=True)
    pos_ref[...] = jnp.where(lane == 0, p1, jnp.where(lane == 1, p2, 0.0)).astype(jnp.int32)
    carry_s[...] = jnp.broadcast_to(incl[tm - 1:tm, :], carry_s.shape)


def _moe_positions(ids, off_row):
    N = ids.shape[0]
    tm = 512
    tril = jnp.asarray(np.tril(np.ones((tm, tm))), BF16)
    return pl.pallas_call(
        _rank_kernel,
        grid=(N // tm,),
        in_specs=[pl.BlockSpec((tm, LANES), lambda i: (i, 0)), pl.BlockSpec((1, LANES), lambda i: (0, 0)),
                  pl.BlockSpec((tm, tm), lambda i: (0, 0))],
        out_specs=pl.BlockSpec((tm, LANES), lambda i: (i, 0)),
        out_shape=jax.ShapeDtypeStruct((N, LANES), jnp.int32),
        scratch_shapes=[pltpu.VMEM((8, LANES), F32)],
        compiler_params=_params("arbitrary"),
        name="moe_positions",
    )(ids, off_row, tril)


DISPATCH_TM = 2048


DMA_UNROLL = 16


def _dispatch_kernel(off_ref, tp_ref, nu_ref, p1_ref, p2_ref, h_ref, xs_ref, zero_s, sem, zsem):
    n_tiles = xs_ref.shape[0] // ROW_TILE

    @pl.when(pl.program_id(0) == 0)
    def _zero_fill():
        zero_s[...] = jnp.zeros(zero_s.shape, zero_s.dtype)

        def tile_copy(row0):
            return pltpu.make_async_copy(
                zero_s, xs_ref.at[pl.ds(pl.multiple_of(row0, ROW_TILE), ROW_TILE), :], zsem)

        for e in range(N_EXPERTS):
            @pl.when(tp_ref[e] > 0)
            def _start(e=e):
                tile_copy(off_ref[e] + (tp_ref[e] - 1) * ROW_TILE).start()

        def tail_start(g, carry):
            tile_copy(g * ROW_TILE).start()
            return carry

        lax.fori_loop(nu_ref[0], n_tiles, tail_start, 0)
        for e in range(N_EXPERTS):
            @pl.when(tp_ref[e] > 0)
            def _wait():
                tile_copy(0).wait()

        def tail_wait(g, carry):
            tile_copy(0).wait()
            return carry

        lax.fori_loop(nu_ref[0], n_tiles, tail_wait, 0)

    def row_copy(i, pos):
        return pltpu.make_async_copy(h_ref.at[pl.ds(i, 1), :], xs_ref.at[pl.ds(pos, 1), :], sem)

    def issue(j, carry):
        for u in range(DMA_UNROLL):
            i = j * DMA_UNROLL + u
            row_copy(i, p1_ref[i]).start(priority=0)
            row_copy(i, p2_ref[i]).start(priority=1)
        return carry

    lax.fori_loop(0, DISPATCH_TM // DMA_UNROLL, issue, 0)
    whole = pltpu.make_async_copy(h_ref, xs_ref.at[pl.ds(0, DISPATCH_TM), :], sem)
    whole.wait()
    whole.wait()


def _moe_dispatch(h2, off, tiles_per, n_used, pos1, pos2, n_rows):
    N = h2.shape[0]
    smem1d = pl.BlockSpec((DISPATCH_TM,), lambda i: (i,), memory_space=pltpu.SMEM)
    smem = pl.BlockSpec(memory_space=pltpu.SMEM)
    return pl.pallas_call(
        _dispatch_kernel,
        grid=(N // DISPATCH_TM,),
        in_specs=[smem, smem, smem, smem1d, smem1d,
                  pl.BlockSpec((DISPATCH_TM, D_MODEL // 2), lambda i: (i, 0))],
        out_specs=pl.BlockSpec(memory_space=pl.ANY),
        out_shape=jax.ShapeDtypeStruct((n_rows, D_MODEL // 2), jnp.uint32),
        scratch_shapes=[pltpu.VMEM((ROW_TILE, D_MODEL // 2), jnp.uint32), pltpu.SemaphoreType.DMA(()),
                        pltpu.SemaphoreType.DMA(())],
        compiler_params=_params("arbitrary"),
        name="moe_dispatch",
    )(off, tiles_per, n_used, pos1, pos2, h2)


def _expert_kernel(te_ref, nu_ref, x_ref, wg_ref, wu_ref, wd_ref, y_ref):
    g = pl.program_id(0)

    @pl.when(g < nu_ref[0])
    def _():
        x = jnp.concatenate(_unpack_bf16_pairs(x_ref[...]), axis=1)
        a = _dot(x, wg_ref[0, 0].astype(BF16))
        u = _dot(x, wu_ref[0, 0].astype(BF16))
        h = (a * jax.nn.sigmoid(a) * u).astype(BF16)
        y_ref[...] = _dot(h, wd_ref[0, 0].astype(BF16))

    @pl.when(g >= nu_ref[0])
    def _():
        y_ref[...] = jnp.zeros(y_ref.shape, F32)


def _moe_experts(xs, tile_expert, n_used, layer, w_gate, w_up, w_down):
    R = xs.shape[0]
    G = R // ROW_TILE

    def row_map(g, te, nu):
        return (jnp.minimum(g, nu[0] - 1), 0)

    def w_map(g, te, nu):
        return (layer, te[jnp.minimum(g, nu[0] - 1)], 0, 0)

    return pl.pallas_call(
        _expert_kernel,
        grid_spec=pltpu.PrefetchScalarGridSpec(
            num_scalar_prefetch=2,
            grid=(G,),
            in_specs=[pl.BlockSpec((ROW_TILE, D_MODEL // 2), row_map),
                      pl.BlockSpec((1, 1, D_MODEL, D_EXPERT), w_map),
                      pl.BlockSpec((1, 1, D_MODEL, D_EXPERT), w_map),
                      pl.BlockSpec((1, 1, D_EXPERT, D_MODEL), w_map)],
            out_specs=pl.BlockSpec((ROW_TILE, D_MODEL), lambda g, te, nu: (g, 0))),
        out_shape=jax.ShapeDtypeStruct((R, D_MODEL), F32),
        compiler_params=_params("arbitrary"),
        name="moe_experts",
    )(tile_expert, n_used, xs, w_gate, w_up, w_down)


COMBINE_TM = 1024


def _combine_kernel(p1_ref, p2_ref, x1_ref, wts_ref, ys_ref, o_ref, b1_s, b2_s, sem):
    def row_copy(buf, i, pos):
        return pltpu.make_async_copy(ys_ref.at[pl.ds(pos, 1), :], buf.at[pl.ds(i, 1), :], sem)

    def issue(j, carry):
        for u in range(DMA_UNROLL):
            i = j * DMA_UNROLL + u
            row_copy(b1_s, i, p1_ref[i]).start(priority=0)
            row_copy(b2_s, i, p2_ref[i]).start(priority=1)
        return carry

    lax.fori_loop(0, COMBINE_TM // DMA_UNROLL, issue, 0)
    pltpu.make_async_copy(ys_ref.at[pl.ds(0, COMBINE_TM), :], b1_s, sem).wait()
    pltpu.make_async_copy(ys_ref.at[pl.ds(0, COMBINE_TM), :], b2_s, sem).wait()
    w = wts_ref[...]
    o_ref[...] = x1_ref[...] + w[:, 0:1] * b1_s[...] + w[:, 1:2] * b2_s[...]


def _moe_combine(x1, wts, ys, pos1, pos2):
    N = x1.shape[0]
    smem1d = pl.BlockSpec((COMBINE_TM,), lambda i: (i,), memory_space=pltpu.SMEM)
    return pl.pallas_call(
        _combine_kernel,
        grid=(N // COMBINE_TM,),
        in_specs=[smem1d, smem1d,
                  pl.BlockSpec((COMBINE_TM, D_MODEL), lambda i: (i, 0)),
                  pl.BlockSpec((COMBINE_TM, LANES), lambda i: (i, 0)),
                  pl.BlockSpec(memory_space=pl.ANY)],
        out_specs=pl.BlockSpec((COMBINE_TM, D_MODEL), lambda i: (i, 0)),
        out_shape=jax.ShapeDtypeStruct((N, D_MODEL), F32),
        scratch_shapes=[pltpu.VMEM((COMBINE_TM, D_MODEL), F32), pltpu.VMEM((COMBINE_TM, D_MODEL), F32),
                        pltpu.SemaphoreType.DMA(())],
        compiler_params=_params("arbitrary"),
        name="moe_combine",
    )(pos1, pos2, x1, wts, ys)


def _moe(x1, h2, ids, wts, tile_counts, layer, w_gate, w_up, w_down):
    N = x1.shape[0]
    n_rows = 2 * N + N_EXPERTS * ROW_TILE
    counts = jnp.sum(tile_counts[::8, :N_EXPERTS], axis=0).astype(jnp.int32)
    tiles_per = (counts + ROW_TILE - 1) // ROW_TILE
    tile_end = jnp.cumsum(tiles_per)
    off = ((tile_end - tiles_per) * ROW_TILE).astype(jnp.int32)
    n_used = tile_end[-1:].astype(jnp.int32)
    tile_ids = jnp.arange(n_rows // ROW_TILE, dtype=jnp.int32)
    tile_expert = jnp.minimum(jnp.sum(tile_end[None, :] <= tile_ids[:, None], axis=1), N_EXPERTS - 1).astype(jnp.int32)
    pos = _moe_positions(ids, _pad_lanes(off.astype(F32), LANES))
    pos1, pos2 = pos[:, 0], pos[:, 1]
    xs = _moe_dispatch(h2, off, tiles_per.astype(jnp.int32), n_used, pos1, pos2, n_rows)
    ys = _moe_experts(xs, tile_expert, n_used, layer, w_gate, w_up, w_down)
    return _moe_combine(x1, wts, ys, pos1, pos2)


def kernel(x, positions, rel_bias, norm1_g, w_in, nsa_cmp_pe, nsa_cmp_w1, nsa_cmp_w2, nsa_q_norm, nsa_k_norm, nsa_out_norm, mlstm_conv, mlstm_i_bias, mlstm_f_bias, mlstm_out_norm, mla_q_a_norm, mla_w_uq, mla_kv_a_norm, mla_w_ukv, mla_qk_norm, mla_out_norm, w_out, norm2_g, moe_w_group, moe_b_group, moe_w_expert, moe_b_expert, moe_w_gate, moe_w_up, moe_w_down):
    B, T, D = x.shape
    N = B * T
    depth = w_in.shape[0]
    colmap = _proj_colmap()
    tiles, biasc = _bias_tables(rel_bias, T)
    rope_cs, rope_sn = _rope_tables(positions.reshape(N, 1))
    x2 = x.reshape(N, D)
    for l in range(depth):
        w_proj = _take_cols(w_in[l], colmap).astype(BF16)
        proj2 = _inproj(x2, norm1_g[l].reshape(1, D), w_proj)
        proj = proj2.reshape(B, T, D_PROJ)
        ya = _nsa(proj, tiles, biasc, nsa_cmp_pe[l], nsa_cmp_w1[l], nsa_cmp_w2[l], nsa_q_norm[l], nsa_k_norm[l])
        yb = _mlstm4(proj, mlstm_conv[l], mlstm_i_bias[l], mlstm_f_bias[l], mlstm_out_norm[l])
        q, k, v = _mla_prep(proj2, rope_cs, rope_sn, mla_q_a_norm[l], mla_w_uq[l], mla_kv_a_norm[l], mla_w_ukv[l], mla_qk_norm[l])
        yc = _mla_attn(q.reshape(B, T, -1), k.reshape(B, T, -1), v.reshape(B, T, -1))
        x1, h2, ids, wts, tile_counts = _outproj(x2, ya.reshape(N, -1), yb.reshape(N, -1), yc.reshape(N, -1), w_out[l],
                                    nsa_out_norm[l], mla_out_norm[l], norm2_g[l], moe_w_group[l], moe_b_group[l],
                                    moe_w_expert[l], moe_b_expert[l])
        x2 = _moe(x1, h2, ids, wts, tile_counts, l, moe_w_gate, moe_w_up, moe_w_down)
    return x2.reshape(B, T, D)
```

```python
import functools
import math

import numpy as np
import jax
import jax.numpy as jnp
from jax import lax
from jax.experimental import pallas as pl
from jax.experimental.pallas import tpu as pltpu

F32 = jnp.float32
BF16 = jnp.bfloat16

D_MODEL = 1024
HEAD_DIM = 64
LANES = 128
A_HEADS, A_KV_HEADS, A_GROUP = 6, 2, 3
CMP_BLOCK, CMP_STRIDE, SEL_BLOCK, N_SELECT, WINDOW = 32, 16, 64, 16, 512
FORCE = 1e4
B_HEADS, CONV_K, MLSTM_CHUNK = 4, 4, 64
C_HEADS, C_NOPE, C_ROPE, C_V = 6, 64, 32, 64
Q_RANK, KV_RANK = 192, 128
ROPE_THETA = 10000.0
REL_BUCKETS, REL_MAX_DIST = 32, 128
N_GROUPS, EXPERTS_PER_GROUP, N_EXPERTS, D_EXPERT = 4, 8, 32, 256
EPS = 1e-6
NEG = -1e30

LOG2E = 1.4426950408889634

AT = 256
RB = 128
ROW_TILE = 512
VMEM_LIMIT = 48 * 1024 * 1024

SEG_AQ, SEG_AKV, SEG_BQK, SEG_BV, SEG_BO, SEG_CDQ = 0, 768, 1536, 2048, 2304, 2560
SEG_GATE, SEG_CDKV, SEG_CKR, SEG_BIF, D_PROJ = 2816, 3072, 3200, 3328, 3840


def _params(*sem):
    return pltpu.CompilerParams(dimension_semantics=sem, vmem_limit_bytes=VMEM_LIMIT)


def _dot(a, b):
    return jnp.dot(a, b, preferred_element_type=F32)


def _dot_nt(a, b):
    return lax.dot_general(a, b, (((1,), (1,)), ((), ())), preferred_element_type=F32)


def _dot_tn(a, b):
    return lax.dot_general(a, b, (((0,), (0,)), ((), ())), preferred_element_type=F32)


def _split3(x):
    hi = x.astype(BF16)
    r1 = x - hi.astype(F32)
    mid = r1.astype(BF16)
    lo = (r1 - mid.astype(F32)).astype(BF16)
    return hi, mid, lo


def _pack_bf16_pairs(a, b):
    ua = lax.bitcast_convert_type(a.astype(BF16).astype(F32), jnp.uint32)
    ub = lax.bitcast_convert_type(b.astype(BF16).astype(F32), jnp.uint32)
    return lax.shift_right_logical(ua, jnp.uint32(16)) | (ub & jnp.uint32(0xFFFF0000))


def _unpack_bf16_pairs(w):
    a = lax.bitcast_convert_type(lax.shift_left(w, jnp.uint32(16)), F32)
    b = lax.bitcast_convert_type(w & jnp.uint32(0xFFFF0000), F32)
    return a.astype(BF16), b.astype(BF16)


def _bucket_np(dist):
    n = np.maximum(dist, 0)
    exact = REL_BUCKETS // 2
    nf = np.maximum(n, 1).astype(np.float64)
    large = exact + (np.log(nf / exact) / math.log(REL_MAX_DIST / exact) * (REL_BUCKETS - exact)).astype(np.int64)
    large = np.minimum(large, REL_BUCKETS - 1)
    return np.where(n < exact, n, large).astype(np.int32)


def _bucket_tables(T):
    r = np.arange(AT)[:, None]
    c = np.arange(AT)[None, :]
    diag = np.where(r >= c, _bucket_np(r - c), -1)
    prev = _bucket_np(AT + r - c)
    far = np.where(c > r, _bucket_np(2 * AT + r - c), -1)
    tiles = np.stack([diag, prev, far]).astype(np.int32)
    tq = np.arange(T)[:, None]
    n = np.arange(LANES)[None, :]
    dist_c = tq - (n * CMP_STRIDE + CMP_BLOCK - 1)
    cmp_tbl = np.where(dist_c >= 0, _bucket_np(dist_c), -1).astype(np.int32)
    return tiles, cmp_tbl


def _proj_colmap():
    cm = np.full((D_PROJ,), -1, np.int64)
    for hd in range(A_HEADS):
        cm[SEG_AQ + hd * LANES: SEG_AQ + hd * LANES + HEAD_DIM] = hd * HEAD_DIM + np.arange(HEAD_DIM)
    cm[SEG_AKV: SEG_AKV + 12 * HEAD_DIM] = 384 + np.arange(12 * HEAD_DIM)
    for h in range(A_KV_HEADS):
        cm[SEG_GATE + h * LANES + HEAD_DIM: SEG_GATE + h * LANES + HEAD_DIM + 9] = 1152 + h * 9 + np.arange(9)
    cm[SEG_BQK: SEG_BQK + 512] = 1170 + np.arange(512)
    cm[SEG_BV: SEG_BV + 256] = 1682 + np.arange(256)
    cm[SEG_BIF: SEG_BIF + 2 * B_HEADS * HEAD_DIM] = 1938 + np.arange(2 * B_HEADS * HEAD_DIM) // HEAD_DIM
    cm[SEG_BO: SEG_BO + 256] = 1946 + np.arange(256)
    cm[SEG_CDQ: SEG_CDQ + Q_RANK] = 2202 + np.arange(Q_RANK)
    cm[SEG_CDKV: SEG_CDKV + KV_RANK] = 2394 + np.arange(KV_RANK)
    cm[SEG_CKR + C_NOPE: SEG_CKR + C_NOPE + C_ROPE] = 2522 + np.arange(C_ROPE)
    return cm


def _take_cols(w, colmap):
    wz = jnp.concatenate([w, jnp.zeros((w.shape[0], 1), w.dtype)], axis=1)
    return wz[:, np.where(colmap < 0, w.shape[1], colmap)]


def _take_rows(w, rowmap):
    return _take_cols(w.T, rowmap).T


def _pad_lanes(v, n):
    v = v.reshape(1, -1)
    return jnp.pad(v, ((0, 0), (0, n - v.shape[1])))


def _bias_kernel(rb_ref, bt_ref, bc_ref, tiles_ref, cmp_ref):
    hd = pl.program_id(0)
    bt = bt_ref[...]
    bc = bc_ref[...]
    t = jnp.full(bt.shape, NEG, F32)
    c = jnp.full(bc.shape, NEG, F32)
    far = rb_ref[REL_BUCKETS - 1, hd]
    for b in range(REL_BUCKETS):
        val = rb_ref[b, hd]
        t = jnp.where(bt == b, (val - far) * LOG2E, t)
        c = jnp.where(bc == b, val * LOG2E, c)
    tiles_ref[0, 0:3] = t
    tiles_ref[0, 3] = jnp.full((AT, AT), NEG, F32)
    cmp_ref[0] = c


def _bias_tables(rel_bias, T):
    bt, bc = _bucket_tables(T)
    return pl.pallas_call(
        _bias_kernel,
        grid=(A_HEADS,),
        in_specs=[pl.BlockSpec(memory_space=pltpu.SMEM),
                  pl.BlockSpec((3, AT, AT), lambda h: (0, 0, 0)),
                  pl.BlockSpec((T, LANES), lambda h: (0, 0))],
        out_specs=[pl.BlockSpec((1, 4, AT, AT), lambda h: (h, 0, 0, 0)),
                   pl.BlockSpec((1, T, LANES), lambda h: (h, 0, 0))],
        out_shape=[jax.ShapeDtypeStruct((A_HEADS, 4, AT, AT), F32),
                   jax.ShapeDtypeStruct((A_HEADS, T, LANES), F32)],
        compiler_params=_params("arbitrary"),
        name="bias_tables",
    )(rel_bias, jnp.asarray(bt), jnp.asarray(bc))


def _inproj_kernel(x_ref, g_ref, w_ref, o_ref):
    x = x_ref[...]
    ms = jnp.mean(x * x, axis=-1, keepdims=True)
    h = (x * lax.rsqrt(ms + EPS) * g_ref[...]).astype(BF16)
    for j in range(0, D_PROJ, 384):
        o_ref[:, j:j + 384] = _dot(h, w_ref[:, j:j + 384])


def _inproj(x2, g, w):
    N = x2.shape[0]
    tm = 512
    return pl.pallas_call(
        _inproj_kernel,
        grid=(N // tm,),
        in_specs=[pl.BlockSpec((tm, D_MODEL), lambda i: (i, 0)),
                  pl.BlockSpec((1, D_MODEL), lambda i: (0, 0)),
                  pl.BlockSpec((D_MODEL, D_PROJ), lambda i: (0, 0))],
        out_specs=pl.BlockSpec((tm, D_PROJ), lambda i: (i, 0)),
        out_shape=jax.ShapeDtypeStruct((N, D_PROJ), F32),
        compiler_params=_params("parallel"),
        name="inproj",
    )(x2, g, w)


def _rms_lanes(x, g, width):
    ms = jnp.sum(x * x, axis=-1, keepdims=True) * (1.0 / width)
    return x * lax.rsqrt(ms + EPS) * g


def _qk_halves(q_r, k_blk):
    half_rows = q_r.shape[0] // 2
    return [_dot_nt(q_r[h0:h0 + half_rows, :], k_blk) for h0 in range(0, q_r.shape[0], half_rows)]


def _flash_chunk(q_r, k_blk, v_blk, bias_fn, m_r, acc_r):
    _softmax_pv(_qk_halves(q_r, k_blk), v_blk, bias_fn, m_r, acc_r)


def _softmax_pv(s_halves, v_blk, bias_fn, m_r, acc_r):
    half_rows = m_r.shape[0] // 2
    assert half_rows % RB == 0
    for hi_, h0 in enumerate(range(0, m_r.shape[0], half_rows)):
        s_all = s_halves[hi_]
        ps, alphas = [], []
        for r in range(half_rows // RB):
            rows = slice(h0 + r * RB, h0 + (r + 1) * RB)
            s = s_all[r * RB:(r + 1) * RB]
            if bias_fn is not None:
                s = s + bias_fn((h0 + r * RB) // RB)
            lo, hi = s[:, :LANES], s[:, LANES:]
            mo = m_r[rows, :]
            mn = jnp.maximum(mo, jnp.max(jnp.maximum(lo, hi), axis=-1, keepdims=True))
            ps.append(jnp.exp2(jnp.concatenate([lo - mn, hi - mn], axis=1)).astype(BF16))
            alphas.append(jnp.exp2(mo - mn))
            m_r[rows, :] = mn
        pv = _dot(jnp.concatenate(ps, axis=0), v_blk[hi_] if isinstance(v_blk, (list, tuple)) else v_blk)
        for r in range(half_rows // RB):
            rows = slice(h0 + r * RB, h0 + (r + 1) * RB)
            acc_r[rows, :] = alphas[r] * acc_r[rows, :] + pv[r * RB:(r + 1) * RB]


def _nsa_kernel(q_ref, k0_ref, v0_ref, k1_ref, v1_ref, k2_ref, v2_ref, gate_ref, tiles_ref, biasc_ref,
                ovt_ref, place_ref, w1_ref, w2_ref, pe_ref, qg_ref, kg_ref, o_ref,
                kc_s, vc_s, k1_s, v1_s, k2_s, v2_s, qx_s, qw_s, s_s, m_s, acc_s, mw_s, accw_s):
    qi = pl.program_id(2)
    kg = kg_ref[...]
    n_cmp_pad = kc_s.shape[0]
    per_g = AT // RB

    def prep(kv_head):
        T = k1_s.shape[0]
        lane_t = lax.broadcasted_iota(jnp.int32, (T, LANES), 1)
        row_t = lax.broadcasted_iota(jnp.int32, (T, LANES), 0)

        def own_half(x):
            if kv_head == 1:
                x = pltpu.roll(x, HEAD_DIM, axis=1)
            return jnp.where(lane_t < HEAD_DIM, x, 0.0)

        block_onehot = lax.shift_right_logical(row_t, 6) == lane_t - HEAD_DIM
        ones_lane = (lane_t >= HEAD_DIM) & (lane_t < HEAD_DIM + 3 * A_GROUP)
        k1_s[...] = jnp.where(block_onehot, 1.0, _rms_lanes(own_half(k1_ref[0]), kg, HEAD_DIM)).astype(BF16)
        v1_s[...] = jnp.where(ones_lane, 1.0, own_half(v1_ref[0])).astype(BF16)
        k2_s[...] = _rms_lanes(own_half(k2_ref[0]), kg, HEAD_DIM).astype(BF16)
        v2_s[...] = jnp.where(ones_lane, 1.0, own_half(v2_ref[0])).astype(BF16)
        half = CMP_BLOCK // 2
        lanes_own = slice(kv_head * HEAD_DIM, (kv_head + 1) * HEAD_DIM)
        for s, (src, dst) in enumerate(((k0_ref, kc_s), (v0_ref, vc_s))):
            a = jnp.zeros((n_cmp_pad, HEAD_DIM), F32)
            bm = jnp.zeros((n_cmp_pad, HEAD_DIM), F32)
            for j in range(half):
                xj = src[0, pl.ds(j, n_cmp_pad, stride=CMP_STRIDE), :][:, lanes_own]
                a = a + _dot((xj + pe_ref[s, j:j + 1, :]).astype(BF16), w1_ref[s, j])
                bm = bm + _dot((xj + pe_ref[s, half + j:half + j + 1, :]).astype(BF16), w1_ref[s, half + j])
            pre = a + pltpu.roll(bm, n_cmp_pad - 1, axis=0)
            hid = pre * jax.nn.sigmoid(pre)
            comp = _dot(hid.astype(BF16), w2_ref[s])
            if s == 0:
                comp = _rms_lanes(comp, kg, HEAD_DIM)
            dst[...] = comp.astype(BF16)

    for kv_head in range(A_KV_HEADS):
        @pl.when((qi == 0) & (pl.program_id(0) == kv_head))
        def _prep(kv_head=kv_head):
            prep(kv_head)

    qscale = HEAD_DIM ** -0.5 * LOG2E
    q = q_ref[0]
    qg = qg_ref[...]
    qs = jnp.concatenate(
        [_rms_lanes(q[:, g * LANES:(g + 1) * LANES], qg, HEAD_DIM) * qscale for g in range(A_GROUP)],
        axis=0).astype(BF16)
    qw_s[...] = qs

    def near_bias(delta, kind):
        kd = jnp.where(qi >= delta, kind, 3)
        return lambda r: tiles_ref[r // per_g, kd, pl.ds((r % per_g) * RB, RB), :]

    def key_rows(kc):
        return pl.ds(pl.multiple_of(kc * AT, AT), AT)

    mw_s[...] = jnp.full(mw_s.shape, NEG, F32)
    accw_s[...] = jnp.zeros(accw_s.shape, F32)
    for delta, kind in ((0, 0), (1, 1), (2, 2)):
        rows_k = key_rows(jnp.maximum(qi - delta, 0))
        _flash_chunk(qw_s, k2_s[rows_k, :], v2_s[rows_k, :], near_bias(delta, kind), mw_s, accw_s)

    biasc = biasc_ref[...].reshape(A_GROUP * AT, LANES)
    valid = biasc > 0.5 * NEG
    lcm = _dot_nt(qs, kc_s[...]) + biasc
    mc = jnp.max(lcm, axis=-1, keepdims=True)
    pc = jnp.where(valid, jnp.exp2(lcm - mc), 0.0)
    lc = jnp.sum(pc, axis=-1, keepdims=True)
    pc = pc * jnp.where(lc > 0.0, 1.0 / lc, 0.0)
    o_c = _dot(pc.astype(BF16), vc_s[...])

    ovt = ovt_ref[...]
    p_hi, p_mid, p_lo = _split3(pc[0:AT] + pc[AT:2 * AT] + pc[2 * AT:3 * AT])
    ps3 = _dot_nt(ovt, jnp.concatenate([p_hi, p_mid, p_lo], axis=0))
    ps = ps3[:, :AT] + ps3[:, AT:2 * AT] + ps3[:, 2 * AT:]
    n_sel = ps.shape[0]
    sid = lax.broadcasted_iota(jnp.int32, (n_sel, AT), 0)
    qcol = lax.broadcasted_iota(jnp.int32, (n_sel, AT), 1)
    cur = qi * (AT // SEL_BLOCK) + lax.shift_right_logical(qcol, 6)
    forced = (sid == 0) | (sid == cur) | (sid == cur - 1)
    psf = jnp.where(forced, ps + FORCE, ps)
    psf = jnp.where(sid <= cur, psf, -1.0)
    rank = jnp.zeros((n_sel, AT), F32)
    for sp in range(n_sel):
        other = psf[sp:sp + 1, :]
        beats = (other > psf) | ((other == psf) & (sid > sp))
        rank = rank + jnp.where(beats, 1.0, 0.0)
    keep = (rank < float(min(N_SELECT, n_sel))) & (psf >= 0.0)
    selneg = _dot_tn(jnp.where(keep, 0.0, NEG).astype(BF16), place_ref[...]).astype(BF16)
    lane_q = lax.broadcasted_iota(jnp.int32, (AT, LANES), 1)
    for g in range(A_GROUP):
        qx_s[g * AT:(g + 1) * AT, :] = jnp.where(lane_q < HEAD_DIM, qs[g * AT:(g + 1) * AT], selneg)

    m_s[...] = jnp.full(m_s.shape, NEG, F32)
    acc_s[...] = jnp.zeros(acc_s.shape, F32)
    for delta, kind in ((0, 0), (1, 1)):
        rows_k = key_rows(jnp.maximum(qi - delta, 0))
        _flash_chunk(qx_s, k1_s[rows_k, :], v1_s[rows_k, :], near_bias(delta, kind), m_s, acc_s)

    for h, s in enumerate(_qk_halves(qx_s, k1_s[key_rows(0), :])):
        s_s[h] = s

    def far_body(kc, carry):
        s_next = _qk_halves(qx_s, k1_s[key_rows(kc + 1), :])
        _softmax_pv([s_s[0], s_s[1]], v1_s[key_rows(kc), :], None, m_s, acc_s)
        for h, s in enumerate(s_next):
            s_s[h] = s
        return carry

    lax.fori_loop(0, jnp.maximum(qi - 1, 0), far_body, 0)

    gt = jax.nn.sigmoid(gate_ref[0])
    gate_lanes = (lane_q >= HEAD_DIM) & (lane_q < HEAD_DIM + 3 * A_GROUP)
    for g in range(A_GROUP):
        rows = slice(g * AT, (g + 1) * AT)
        a_s = acc_s[rows, :]
        a_w = accw_s[rows, :]
        c_s = gt / jnp.where(gate_lanes, a_s, 1.0)
        c_w = gt / jnp.where(gate_lanes, a_w, 1.0)
        j = HEAD_DIM + 3 * g
        o = gt[:, j:j + 1] * o_c[rows] + c_s[:, j + 1:j + 2] * a_s + c_w[:, j + 2:j + 3] * a_w
        o_ref[0, :, g * LANES:(g + 1) * LANES] = jnp.where(lane_q < HEAD_DIM, o, 0.0)


def _nsa(proj, tiles, biasc, cmp_pe, cmp_w1, cmp_w2, q_norm, k_norm):
    B, T, _ = proj.shape
    nq = T // AT
    n_cmp_pad = T // CMP_STRIDE
    n_sel = T // SEL_BLOCK
    assert n_sel <= LANES - HEAD_DIM - 1 and n_cmp_pad == LANES
    starts = np.arange(n_cmp_pad) * CMP_STRIDE
    sid = np.arange(n_sel)
    overlap = ((starts[:, None] < (sid[None, :] + 1) * SEL_BLOCK)
               & (starts[:, None] + CMP_BLOCK > sid[None, :] * SEL_BLOCK)
               & (starts[:, None] + CMP_BLOCK <= T))
    place = np.zeros((n_sel, LANES), np.float32)
    place[sid, HEAD_DIM + sid] = 1.0
    w2p = jnp.pad(cmp_w2, ((0, 0), (0, 0), (0, LANES - HEAD_DIM)))

    def slab(br, kv):
        base = SEG_AKV // LANES + (br * 2 + kv)
        return pl.BlockSpec((1, T, LANES), lambda h, b, i: (b, 0, base))

    def full(shape):
        return pl.BlockSpec(shape, lambda h, b, i: (0,) * len(shape))

    rows = A_GROUP * AT
    return pl.pallas_call(
        _nsa_kernel,
        grid=(A_KV_HEADS, B, nq),
        in_specs=[pl.BlockSpec((1, AT, A_GROUP * LANES), lambda h, b, i: (b, i, h)),
                  slab(0, 0), slab(0, 1), slab(1, 0), slab(1, 1), slab(2, 0), slab(2, 1),
                  pl.BlockSpec((1, AT, LANES), lambda h, b, i: (b, i, SEG_GATE // LANES + h)),
                  pl.BlockSpec((A_GROUP, 4, AT, AT), lambda h, b, i: (h, 0, 0, 0)),
                  pl.BlockSpec((A_GROUP, AT, LANES), lambda h, b, i: (h, i, 0)),
                  full((n_sel, n_cmp_pad)), full((n_sel, LANES)),
                  full((2, CMP_BLOCK, HEAD_DIM, HEAD_DIM)), full((2, HEAD_DIM, LANES)),
                  full((2, CMP_BLOCK, HEAD_DIM)), full((1, LANES)), full((1, LANES))],
        out_specs=pl.BlockSpec((1, AT, A_GROUP * LANES), lambda h, b, i: (b, i, h)),
        out_shape=jax.ShapeDtypeStruct((B, T, A_HEADS * LANES), F32),
        scratch_shapes=[pltpu.VMEM((n_cmp_pad, LANES), BF16), pltpu.VMEM((n_cmp_pad, LANES), BF16),
                        pltpu.VMEM((T, LANES), BF16), pltpu.VMEM((T, LANES), BF16),
                        pltpu.VMEM((T, LANES), BF16), pltpu.VMEM((T, LANES), BF16),
                        pltpu.VMEM((rows, LANES), BF16), pltpu.VMEM((rows, LANES), BF16),
                        pltpu.VMEM((2, rows // 2, AT), F32),
                        pltpu.VMEM((rows, LANES), F32), pltpu.VMEM((rows, LANES), F32),
                        pltpu.VMEM((rows, LANES), F32), pltpu.VMEM((rows, LANES), F32)],
        compiler_params=_params("parallel", "parallel", "arbitrary"),
        name="nsa",
    )(proj, proj, proj, proj, proj, proj, proj, proj, tiles, biasc,
      jnp.asarray(overlap.T, BF16), jnp.asarray(place, BF16), cmp_w1.astype(BF16), w2p.astype(BF16), cmp_pe,
      _pad_lanes(q_norm, LANES), _pad_lanes(k_norm, LANES))


MLSTM_NB = 4
MLSTM_TB = 512


def _mlstm4_kernel(qk_ref, v_ref, o_ref, gi_ref, gf_ref, cw_ref, bi_ref, bf_ref, gn_ref, tril_ref, y_ref,
                   xpad_s, c_s, n_s, m_s):
    nb, tb = qk_ref.shape[0], qk_ref.shape[1]
    L = MLSTM_CHUNK
    W = B_HEADS * HEAD_DIM

    @pl.when(pl.program_id(1) == 0)
    def _init():
        xpad_s[:, tb:tb + 8, :] = jnp.zeros((nb, 8, xpad_s.shape[2]), F32)
        c_s[...] = jnp.zeros(c_s.shape, F32)
        n_s[...] = jnp.zeros(n_s.shape, F32)
        m_s[...] = jnp.zeros(m_s.shape, F32)

    for bb in range(nb):
        xpad_s[bb, 0:8, :] = xpad_s[bb, tb:tb + 8, :]
        xpad_s[bb, 8:8 + tb, :] = qk_ref[bb]
    lane_head = lax.shift_right_logical(lax.broadcasted_iota(jnp.int32, (1, W), 1), 6)
    seg = [lane_head == a for a in range(B_HEADS)]
    row_l = lax.broadcasted_iota(jnp.int32, (L, W), 0)
    lane_l = lax.broadcasted_iota(jnp.int32, (L, W), 1) & (HEAD_DIM - 1)
    tri = lane_l <= row_l
    diag = lane_l == row_l
    blockdiag = (lax.shift_right_logical(lax.broadcasted_iota(jnp.int32, (W, W), 0), 6)
                 == lax.shift_right_logical(lax.broadcasted_iota(jnp.int32, (W, W), 1), 6))
    kscale = HEAD_DIM ** -0.5

    def seg_reduce(x, op, fill):
        out = None
        for a in range(B_HEADS):
            r = op(jnp.where(seg[a], x, fill)[:, (a // 2) * LANES:(a // 2 + 1) * LANES], axis=-1, keepdims=True)
            out = r if out is None else jnp.where(seg[a], r, out)
        return out

    def chunk_one(bb, c):
        r0 = pl.multiple_of(c * L, L)
        win = xpad_s[bb, pl.ds(r0, L + 8), :]
        conv = cw_ref[CONV_K - 1:CONV_K, :] * win[8:8 + L]
        for j in range(CONV_K - 1):
            conv = conv + cw_ref[j:j + 1, :] * pltpu.roll(win, CONV_K - 1 - j, axis=0)[8:8 + L]
        qk = conv * jax.nn.sigmoid(conv)
        q4 = qk[:, :W]
        k4 = qk[:, W:] * kscale
        v4 = v_ref[bb, pl.ds(r0, L), :]
        i4 = gi_ref[bb, pl.ds(r0, L), :] + bi_ref[...]
        logf = jax.nn.log_sigmoid(gf_ref[bb, pl.ds(r0, L), :] + bf_ref[...])
        lh, lm, ll = _split3(logf)
        cum = _dot(tril_ref[...], jnp.concatenate([lh, lm, ll], axis=1))
        b4 = cum[:, :W] + cum[:, W:2 * W] + cum[:, 2 * W:]
        c_old, n_old, m_prev = c_s[bb], n_s[bb], m_s[bb][0:1, :]
        r4 = jnp.sum(jnp.where(diag, i4 - b4, 0.0), axis=0, keepdims=True)
        dmat = jnp.where(tri, b4 + r4, -jnp.inf)
        m_inter = b4 + m_prev
        m_j = jnp.maximum(m_inter, seg_reduce(dmat, jnp.max, -jnp.inf))
        isc = jnp.exp(m_inter - m_j)
        k_heads = jnp.concatenate([jnp.where(seg[a], k4, 0.0) for a in range(B_HEADS)], axis=0)
        v_heads = jnp.concatenate([jnp.where(seg[a], v4, 0.0) for a in range(B_HEADS)], axis=0)
        s = _dot_nt(q4, k_heads) * jnp.exp(dmat - m_j)
        den = seg_reduce(s, jnp.sum, 0.0) + isc * seg_reduce(q4 * n_old, jnp.sum, 0.0)
        num = _dot(s, v_heads) + isc * _dot_nt(q4, c_old)
        h = num / jnp.maximum(jnp.abs(den), jnp.exp(-m_j))
        b_last = b4[L - 1:L, :]
        w_log = b_last - b4 + i4
        m_new = jnp.maximum(b_last + m_prev, jnp.max(w_log, axis=0, keepdims=True))
        w4 = jnp.exp(w_log - m_new)
        dec = jnp.exp(b_last + m_prev - m_new)
        c_s[bb] = dec * c_old + jnp.where(blockdiag, _dot_tn(v4 * w4, k4), 0.0)
        n_s[bb] = dec * n_old + jnp.sum(w4 * k4, axis=0, keepdims=True)
        m_s[bb] = jnp.broadcast_to(m_new, m_s.shape[1:])
        ms = seg_reduce(h * h, jnp.sum, 0.0) * (1.0 / HEAD_DIM)
        hn = h * lax.rsqrt(ms + EPS) * gn_ref[...]
        y_ref[bb, pl.ds(r0, L), :] = jax.nn.sigmoid(o_ref[bb, pl.ds(r0, L), :]) * hn

    def chunk(c, carry):
        for bb in range(nb):
            chunk_one(bb, c)
        return carry

    lax.fori_loop(0, tb // L, chunk, 0)


def _mlstm4(proj, conv_w, i_bias, f_bias, out_norm):
    B, T, _ = proj.shape
    W = B_HEADS * HEAD_DIM
    nb, tb = MLSTM_NB, MLSTM_TB
    tril = jnp.asarray(np.tril(np.ones((MLSTM_CHUNK, MLSTM_CHUNK))), BF16)

    def full(shape):
        return pl.BlockSpec(shape, lambda b, t: (0,) * len(shape))

    def seg(width, off):
        return pl.BlockSpec((nb, tb, width), lambda b, t: (b, t, off // width))

    return pl.pallas_call(
        _mlstm4_kernel,
        grid=(B // nb, T // tb),
        in_specs=[seg(2 * W, SEG_BQK), seg(W, SEG_BV), seg(W, SEG_BO), seg(W, SEG_BIF), seg(W, SEG_BIF + W),
                  full((CONV_K, 2 * W)), full((1, W)), full((1, W)), full((1, W)),
                  full((MLSTM_CHUNK, MLSTM_CHUNK))],
        out_specs=pl.BlockSpec((nb, tb, W), lambda b, t: (b, t, 0)),
        out_shape=jax.ShapeDtypeStruct((B, T, W), F32),
        scratch_shapes=[pltpu.VMEM((nb, tb + 8, 2 * W), F32),
                        pltpu.VMEM((nb, W, W), F32),
                        pltpu.VMEM((nb, 1, W), F32),
                        pltpu.VMEM((nb, 8, W), F32)],
        compiler_params=_params("parallel", "arbitrary"),
        name="mlstm",
    )(proj, proj, proj, proj, proj, conv_w, jnp.repeat(i_bias, HEAD_DIM).reshape(1, W),
      jnp.repeat(f_bias, HEAD_DIM).reshape(1, W), out_norm.reshape(1, W), tril)


def _rope_kernel(pos_ref, invf_ref, sgn_ref, cs_ref, sn_ref):
    ang = pos_ref[...].astype(F32) * invf_ref[...]
    cs_ref[...] = jnp.cos(ang)
    sn_ref[...] = jnp.sin(ang) * sgn_ref[...]


def _rope_tables(pos):
    N = pos.shape[0]
    tm = 1024
    half = C_ROPE // 2
    inv = 1.0 / (ROPE_THETA ** (np.arange(half, dtype=np.float32) * 2.0 / C_ROPE))
    invf = np.zeros((1, LANES), np.float32)
    sgn = np.zeros((1, LANES), np.float32)
    invf[0, C_NOPE:C_NOPE + C_ROPE] = np.concatenate([inv, inv])
    sgn[0, C_NOPE:C_NOPE + half] = -1.0
    sgn[0, C_NOPE + half:C_NOPE + C_ROPE] = 1.0
    row = pl.BlockSpec((1, LANES), lambda i: (0, 0))
    tile = pl.BlockSpec((tm, LANES), lambda i: (i, 0))
    return pl.pallas_call(
        _rope_kernel,
        grid=(N // tm,),
        in_specs=[pl.BlockSpec((tm, 1), lambda i: (i, 0)), row, row],
        out_specs=[tile, tile],
        out_shape=[jax.ShapeDtypeStruct((N, LANES), F32), jax.ShapeDtypeStruct((N, LANES), F32)],
        compiler_params=_params("parallel"),
        name="rope_tables",
    )(pos, jnp.asarray(invf), jnp.asarray(sgn))


def _mla_prep_kernel(dq_ref, dkv_ref, kr_ref, cs_ref, sn_ref, wq_ref, wk_ref, wv_ref, qa_ref, kva_ref,
                     gq_ref, gk_ref, gkr_ref, q_out, k_out, v_out):
    lane = lax.broadcasted_iota(jnp.int32, (1, LANES), 1)
    nope = lane < C_NOPE
    ropem = (lane >= C_NOPE) & (lane < C_NOPE + C_ROPE)
    first = lane < C_NOPE + C_ROPE // 2
    cs = cs_ref[...]
    sn = sn_ref[...]

    def rope(x):
        partner = jnp.where(first, pltpu.roll(x, LANES - C_ROPE // 2, axis=1), pltpu.roll(x, C_ROPE // 2, axis=1))
        return x * cs + partner * sn

    scale = (C_NOPE + C_ROPE) ** -0.5 * LOG2E
    qn = _rms_lanes(dq_ref[...], qa_ref[...], Q_RANK).astype(BF16)
    q = _dot(qn, wq_ref[...])
    kvn = _rms_lanes(dkv_ref[...], kva_ref[...], KV_RANK).astype(BF16)
    kk = _dot(kvn, wk_ref[...])
    vv = _dot(kvn, wv_ref[...])
    lane_v = lax.broadcasted_iota(jnp.int32, vv.shape, 1)
    v_out[...] = jnp.where((lane_v & (LANES - 1)) == C_V, 1.0, vv).astype(BF16)
    kr = kr_ref[...]
    krn = kr * lax.rsqrt(jnp.sum(kr * kr, axis=-1, keepdims=True) * (1.0 / C_ROPE) + EPS) * gkr_ref[...]
    krr = rope(krn)
    for h in range(C_HEADS):
        sl = slice(h * LANES, (h + 1) * LANES)
        qh = q[:, sl]
        sq = qh * qh
        msn = jnp.sum(jnp.where(nope, sq, 0.0), axis=-1, keepdims=True) * (1.0 / C_NOPE)
        msr = jnp.sum(jnp.where(ropem, sq, 0.0), axis=-1, keepdims=True) * (1.0 / C_ROPE)
        qhn = qh * jnp.where(nope, lax.rsqrt(msn + EPS), lax.rsqrt(msr + EPS)) * gq_ref[...]
        q_out[:, sl] = (rope(qhn) * scale).astype(BF16)
        kh = kk[:, sl]
        msk = jnp.sum(kh * kh, axis=-1, keepdims=True) * (1.0 / C_NOPE)
        k_out[:, sl] = (kh * lax.rsqrt(msk + EPS) * gk_ref[...] + krr).astype(BF16)


def _mla_prep(proj2, rope_cs, rope_sn, q_a_norm, w_uq, kv_a_norm, w_ukv, qk_norm):
    N = proj2.shape[0]
    tm = 512
    qd = C_NOPE + C_ROPE
    qmap = np.full((C_HEADS * LANES,), -1, np.int64)
    kmap = np.full((C_HEADS * LANES,), -1, np.int64)
    vmap = np.full((C_HEADS * LANES,), -1, np.int64)
    for h in range(C_HEADS):
        qmap[h * LANES:h * LANES + qd] = h * qd + np.arange(qd)
        kmap[h * LANES:h * LANES + C_NOPE] = h * (C_NOPE + C_V) + np.arange(C_NOPE)
        vmap[h * LANES:h * LANES + C_V] = h * (C_NOPE + C_V) + C_NOPE + np.arange(C_V)
    wq = jnp.pad(_take_cols(w_uq, qmap), ((0, 256 - Q_RANK), (0, 0))).astype(BF16)
    wk = _take_cols(w_ukv, kmap).astype(BF16)
    wv = _take_cols(w_ukv, vmap).astype(BF16)
    gkr = jnp.pad(qk_norm[1, C_NOPE:].reshape(1, -1), ((0, 0), (C_NOPE, LANES - C_NOPE - C_ROPE)))

    def full(shape):
        return pl.BlockSpec(shape, lambda i: (0,) * len(shape))

    return pl.pallas_call(
        _mla_prep_kernel,
        grid=(N // tm,),
        in_specs=[pl.BlockSpec((tm, 256), lambda i: (i, SEG_CDQ // 256)),
                  pl.BlockSpec((tm, LANES), lambda i: (i, SEG_CDKV // LANES)),
                  pl.BlockSpec((tm, LANES), lambda i: (i, SEG_CKR // LANES)),
                  pl.BlockSpec((tm, LANES), lambda i: (i, 0)), pl.BlockSpec((tm, LANES), lambda i: (i, 0)),
                  full((256, C_HEADS * LANES)), full((KV_RANK, C_HEADS * LANES)), full((KV_RANK, C_HEADS * LANES)),
                  full((1, 256)), full((1, LANES)), full((1, LANES)), full((1, LANES)), full((1, LANES))],
        out_specs=[pl.BlockSpec((tm, C_HEADS * LANES), lambda i: (i, 0)),
                   pl.BlockSpec((tm, C_HEADS * LANES), lambda i: (i, 0)),
                   pl.BlockSpec((tm, C_HEADS * LANES), lambda i: (i, 0))],
        out_shape=[jax.ShapeDtypeStruct((N, C_HEADS * LANES), BF16),
                   jax.ShapeDtypeStruct((N, C_HEADS * LANES), BF16),
                   jax.ShapeDtypeStruct((N, C_HEADS * LANES), BF16)],
        compiler_params=_params("parallel"),
        name="mla_prep",
    )(proj2, proj2, proj2, rope_cs, rope_sn, wq, wk, wv,
      _pad_lanes(q_a_norm, 256), kv_a_norm.reshape(1, -1), _pad_lanes(qk_norm[0], LANES),
      _pad_lanes(qk_norm[1, :C_NOPE], LANES), gkr)


MLA_Q = 2 * AT


def _mla_attn_kernel(q_ref, k_ref, v_ref, causal_ref, o_ref, s_s, m_s, acc_s):
    qi = pl.program_id(2)
    m_s[...] = jnp.full(m_s.shape, NEG, F32)
    acc_s[...] = jnp.zeros(acc_s.shape, F32)
    heads = [slice(hh * LANES, (hh + 1) * LANES) for hh in range(2)]
    per_head = MLA_Q // RB
    tiles_per_q = MLA_Q // AT

    def key_rows(kc):
        return pl.ds(pl.multiple_of(kc * AT, AT), AT)

    def scores(kc):
        return [_dot_nt(q_ref[0, :, sl], k_ref[0, key_rows(kc), sl]) for sl in heads]

    def values(kc):
        return [v_ref[0, key_rows(kc), sl] for sl in heads]

    for d in range(tiles_per_q):
        kc = tiles_per_q * qi + d
        _softmax_pv(scores(kc), values(kc),
                    lambda r, d=d: causal_ref[d, pl.ds((r % per_head) * RB, RB), :], m_s, acc_s)

    for h, s in enumerate(scores(0)):
        s_s[h] = s

    def body(kc, carry):
        s_next = scores(kc + 1)
        _softmax_pv([s_s[0], s_s[1]], values(kc), None, m_s, acc_s)
        for h, s in enumerate(s_next):
            s_s[h] = s
        return carry

    lax.fori_loop(0, tiles_per_q * qi, body, 0)
    lane = lax.broadcasted_iota(jnp.int32, (MLA_Q, LANES), 1)
    outs = []
    for hh in range(2):
        a = acc_s[hh * MLA_Q:(hh + 1) * MLA_Q, :]
        outs.append(jnp.where(lane < C_V, a / a[:, C_V:C_V + 1], 0.0))
    o_ref[0] = outs[0] + pltpu.roll(outs[1], C_V, axis=1)


def _mla_attn(q, k, v):
    B, T, _ = q.shape
    qpos = np.arange(MLA_Q)[None, :, None]
    kpos = (np.arange(MLA_Q // AT)[:, None, None] * AT + np.arange(AT)[None, None, :])
    causal = np.where(qpos >= kpos, 0.0, NEG).astype(np.float32)
    pair = pl.BlockSpec((1, T, 2 * LANES), lambda b, p, i: (b, 0, p))
    return pl.pallas_call(
        _mla_attn_kernel,
        grid=(B, C_HEADS // 2, T // MLA_Q),
        in_specs=[pl.BlockSpec((1, MLA_Q, 2 * LANES), lambda b, p, i: (b, i, p)), pair, pair,
                  pl.BlockSpec((MLA_Q // AT, MLA_Q, AT), lambda b, p, i: (0, 0, 0))],
        out_specs=pl.BlockSpec((1, MLA_Q, LANES), lambda b, p, i: (b, i, p)),
        out_shape=jax.ShapeDtypeStruct((B, T, C_HEADS * C_V), F32),
        scratch_shapes=[pltpu.VMEM((2, MLA_Q, AT), F32), pltpu.VMEM((2 * MLA_Q, LANES), F32),
                        pltpu.VMEM((2 * MLA_Q, LANES), F32)],
        compiler_params=_params("parallel", "parallel", "arbitrary"),
        name="mla_attn",
    )(q, k, v, jnp.asarray(causal))


def _outproj_kernel(x_ref, ya_ref, yb_ref, yc_ref, wa_ref, wb_ref, wc_ref, ga_ref, gc_ref, g2_ref,
                    wr_ref, br_ref, x1_ref, h2_ref, ids_ref, wts_ref, cnt_ref):
    ya = _rms_lanes(ya_ref[...], ga_ref[...], A_HEADS * HEAD_DIM).astype(BF16)
    yc = _rms_lanes(yc_ref[...], gc_ref[...], C_HEADS * C_V).astype(BF16)
    x1 = (x_ref[...] + _dot(ya, wa_ref[...]) + _dot(yb_ref[...].astype(BF16), wb_ref[...])
          + _dot(yc, wc_ref[...]))
    x1_ref[...] = x1
    h2 = _rms_lanes(x1, g2_ref[...], D_MODEL)
    h2_ref[...] = _pack_bf16_pairs(h2[:, :D_MODEL // 2], h2[:, D_MODEL // 2:])
    h2_hi = h2.astype(BF16)
    h2_mid = (h2 - h2_hi.astype(F32)).astype(BF16)
    logits = (_dot(h2_hi, wr_ref[0]) + _dot(h2_mid, wr_ref[0]) + _dot(h2_hi, wr_ref[1])) + br_ref[...]
    tm = logits.shape[0]
    lane = lax.broadcasted_iota(jnp.int32, (tm, LANES), 1)
    is_g = (lane >= N_EXPERTS) & (lane < N_EXPERTS + N_GROUPS)
    gl = jnp.where(is_g, logits, NEG)
    ge = jnp.where(is_g, jnp.exp(gl - jnp.max(gl, axis=-1, keepdims=True)), 0.0)
    gp = jnp.where(is_g, ge / jnp.sum(ge, axis=-1, keepdims=True), -1.0)
    g_top = jnp.max(gp, axis=-1, keepdims=True)
    g_idx = jnp.min(jnp.where(gp == g_top, lane, 2 * LANES), axis=-1, keepdims=True) - N_EXPERTS
    in_grp = (lane < N_EXPERTS) & (lax.shift_right_logical(lane, 3) == g_idx)
    el = jnp.where(in_grp, logits, NEG)
    ee = jnp.where(in_grp, jnp.exp(el - jnp.max(el, axis=-1, keepdims=True)), 0.0)
    ep = jnp.where(in_grp, ee / jnp.sum(ee, axis=-1, keepdims=True), -1.0)
    v1 = jnp.max(ep, axis=-1, keepdims=True)
    i1 = jnp.min(jnp.where(ep == v1, lane, 2 * LANES), axis=-1, keepdims=True)
    ep2 = jnp.where(lane == i1, -1.0, ep)
    v2 = jnp.max(ep2, axis=-1, keepdims=True)
    i2 = jnp.min(jnp.where(ep2 == v2, lane, 2 * LANES), axis=-1, keepdims=True)
    tot = v1 + v2
    ids_ref[...] = jnp.where(lane == 0, i1, jnp.where(lane == 1, i2, 0))
    wts_ref[...] = jnp.where(lane == 0, v1 / tot * g_top, jnp.where(lane == 1, v2 / tot * g_top, 0.0))
    per_expert = jnp.sum(jnp.where((lane == i1) | (lane == i2), 1.0, 0.0), axis=0, keepdims=True)
    cnt_ref[...] = jnp.broadcast_to(per_expert, cnt_ref.shape)


def _outproj(x2, ya, yb, yc, w_out, nsa_out_norm, mla_out_norm, norm2_g, w_group, b_group, w_expert, b_expert):
    N = x2.shape[0]
    tm = 1024
    amap = np.full((A_HEADS * LANES,), -1, np.int64)
    for hd in range(A_HEADS):
        amap[hd * LANES:hd * LANES + HEAD_DIM] = hd * HEAD_DIM + np.arange(HEAD_DIM)
    wa = _take_rows(w_out[:384], amap).astype(BF16)
    wb = w_out[384:640].astype(BF16)
    wc = w_out[640:].astype(BF16)
    ga = _take_cols(nsa_out_norm.reshape(1, -1), amap)
    wr = jnp.pad(jnp.concatenate([w_expert, w_group], axis=1), ((0, 0), (0, LANES - N_EXPERTS - N_GROUPS)))
    wr_hi = wr.astype(BF16)
    wr = jnp.stack([wr_hi, (wr - wr_hi.astype(F32)).astype(BF16)])
    br = _pad_lanes(jnp.concatenate([b_expert, b_group]), LANES)

    def full(shape):
        return pl.BlockSpec(shape, lambda i: (0,) * len(shape))

    def rows(w):
        return pl.BlockSpec((tm, w), lambda i: (i, 0))

    return pl.pallas_call(
        _outproj_kernel,
        grid=(N // tm,),
        in_specs=[rows(D_MODEL), rows(A_HEADS * LANES), rows(256), rows(384),
                  full((A_HEADS * LANES, D_MODEL)), full((256, D_MODEL)), full((384, D_MODEL)),
                  full((1, A_HEADS * LANES)), full((1, 384)), full((1, D_MODEL)),
                  full((2, D_MODEL, LANES)), full((1, LANES))],
        out_specs=[rows(D_MODEL), rows(D_MODEL // 2), rows(LANES), rows(LANES),
                   pl.BlockSpec((8, LANES), lambda i: (i, 0))],
        out_shape=[jax.ShapeDtypeStruct((N, D_MODEL), F32), jax.ShapeDtypeStruct((N, D_MODEL // 2), jnp.uint32),
                   jax.ShapeDtypeStruct((N, LANES), jnp.int32), jax.ShapeDtypeStruct((N, LANES), F32),
                   jax.ShapeDtypeStruct((N // tm * 8, LANES), F32)],
        compiler_params=_params("parallel"),
        name="outproj_router",
    )(x2, ya, yb, yc, wa, wb, wc, ga, mla_out_norm.reshape(1, -1), norm2_g.reshape(1, -1), wr, br)


def _rank_kernel(ids_ref, off_ref, tril_ref, pos_ref, carry_s):
    i = pl.program_id(0)

    @pl.when(i == 0)
    def _():
        carry_s[...] = jnp.broadcast_to(off_ref[...], carry_s.shape)

    ids = ids_ref[...]
    tm = ids.shape[0]
    lane = lax.broadcasted_iota(jnp.int32, (tm, LANES), 1)
    i1 = ids[:, 0:1]
    i2 = ids[:, 1:2]
    oh = jnp.where((lane == i1) | (lane == i2), 1.0, 0.0)
    incl = _dot(tril_ref[...], oh.astype(BF16)) + carry_s[0:1, :]
    excl = incl - oh
    p1 = jnp.sum(jnp.where(lane == i1, excl, 0.0), axis=-1, keepdims=True)
    p2 = jnp.sum(jnp.where(lane == i2, excl, 0.0), axis=-1, keepdims=True)
    pos_ref[...] = jnp.where(lane == 0, p1, jnp.where(lane == 1, p2, 0.0)).astype(jnp.int32)
    carry_s[...] = jnp.broadcast_to(incl[tm - 1:tm, :], carry_s.shape)


def _moe_positions(ids, off_row):
    N = ids.shape[0]
    tm = 512
    tril = jnp.asarray(np.tril(np.ones((tm, tm))), BF16)
    return pl.pallas_call(
        _rank_kernel,
        grid=(N // tm,),
        in_specs=[pl.BlockSpec((tm, LANES), lambda i: (i, 0)), pl.BlockSpec((1, LANES), lambda i: (0, 0)),
                  pl.BlockSpec((tm, tm), lambda i: (0, 0))],
        out_specs=pl.BlockSpec((tm, LANES), lambda i: (i, 0)),
        out_shape=jax.ShapeDtypeStruct((N, LANES), jnp.int32),
        scratch_shapes=[pltpu.VMEM((8, LANES), F32)],
        compiler_params=_params("arbitrary"),
        name="moe_positions",
    )(ids, off_row, tril)


DISPATCH_TM = 2048


DMA_UNROLL = 16


def _dispatch_kernel(off_ref, tp_ref, nu_ref, p1_ref, p2_ref, h_ref, xs_ref, zero_s, sem, zsem):
    n_tiles = xs_ref.shape[0] // ROW_TILE

    @pl.when(pl.program_id(0) == 0)
    def _zero_fill():
        zero_s[...] = jnp.zeros(zero_s.shape, zero_s.dtype)

        def tile_copy(row0):
            return pltpu.make_async_copy(
                zero_s, xs_ref.at[pl.ds(pl.multiple_of(row0, ROW_TILE), ROW_TILE), :], zsem)

        for e in range(N_EXPERTS):
            @pl.when(tp_ref[e] > 0)
            def _start(e=e):
                tile_copy(off_ref[e] + (tp_ref[e] - 1) * ROW_TILE).start()

        def tail_start(g, carry):
            tile_copy(g * ROW_TILE).start()
            return carry

        lax.fori_loop(nu_ref[0], n_tiles, tail_start, 0)
        for e in range(N_EXPERTS):
            @pl.when(tp_ref[e] > 0)
            def _wait():
                tile_copy(0).wait()

        def tail_wait(g, carry):
            tile_copy(0).wait()
            return carry

        lax.fori_loop(nu_ref[0], n_tiles, tail_wait, 0)

    def row_copy(i, pos):
        return pltpu.make_async_copy(h_ref.at[pl.ds(i, 1), :], xs_ref.at[pl.ds(pos, 1), :], sem)

    def issue(j, carry):
        for u in range(DMA_UNROLL):
            i = j * DMA_UNROLL + u
            row_copy(i, p1_ref[i]).start(priority=0)
            row_copy(i, p2_ref[i]).start(priority=1)
        return carry

    lax.fori_loop(0, DISPATCH_TM // DMA_UNROLL, issue, 0)
    whole = pltpu.make_async_copy(h_ref, xs_ref.at[pl.ds(0, DISPATCH_TM), :], sem)
    whole.wait()
    whole.wait()


def _moe_dispatch(h2, off, tiles_per, n_used, pos1, pos2, n_rows):
    N = h2.shape[0]
    smem1d = pl.BlockSpec((DISPATCH_TM,), lambda i: (i,), memory_space=pltpu.SMEM)
    smem = pl.BlockSpec(memory_space=pltpu.SMEM)
    return pl.pallas_call(
        _dispatch_kernel,
        grid=(N // DISPATCH_TM,),
        in_specs=[smem, smem, smem, smem1d, smem1d,
                  pl.BlockSpec((DISPATCH_TM, D_MODEL // 2), lambda i: (i, 0))],
        out_specs=pl.BlockSpec(memory_space=pl.ANY),
        out_shape=jax.ShapeDtypeStruct((n_rows, D_MODEL // 2), jnp.uint32),
        scratch_shapes=[pltpu.VMEM((ROW_TILE, D_MODEL // 2), jnp.uint32), pltpu.SemaphoreType.DMA(()),
                        pltpu.SemaphoreType.DMA(())],
        compiler_params=_params("arbitrary"),
        name="moe_dispatch",
    )(off, tiles_per, n_used, pos1, pos2, h2)


def _expert_kernel(te_ref, nu_ref, x_ref, wg_ref, wu_ref, wd_ref, y_ref):
    g = pl.program_id(0)

    @pl.when(g < nu_ref[0])
    def _():
        x = jnp.concatenate(_unpack_bf16_pairs(x_ref[...]), axis=1)
        a = _dot(x, wg_ref[0, 0].astype(BF16))
        u = _dot(x, wu_ref[0, 0].astype(BF16))
        h = (a * jax.nn.sigmoid(a) * u).astype(BF16)
        y_ref[...] = _dot(h, wd_ref[0, 0].astype(BF16))

    @pl.when(g >= nu_ref[0])
    def _():
        y_ref[...] = jnp.zeros(y_ref.shape, F32)


def _moe_experts(xs, tile_expert, n_used, layer, w_gate, w_up, w_down):
    R = xs.shape[0]
    G = R // ROW_TILE

    def row_map(g, te, nu):
        return (jnp.minimum(g, nu[0] - 1), 0)

    def w_map(g, te, nu):
        return (layer, te[jnp.minimum(g, nu[0] - 1)], 0, 0)

    return pl.pallas_call(
        _expert_kernel,
        grid_spec=pltpu.PrefetchScalarGridSpec(
            num_scalar_prefetch=2,
            grid=(G,),
            in_specs=[pl.BlockSpec((ROW_TILE, D_MODEL // 2), row_map),
                      pl.BlockSpec((1, 1, D_MODEL, D_EXPERT), w_map),
                      pl.BlockSpec((1, 1, D_MODEL, D_EXPERT), w_map),
                      pl.BlockSpec((1, 1, D_EXPERT, D_MODEL), w_map)],
            out_specs=pl.BlockSpec((ROW_TILE, D_MODEL), lambda g, te, nu: (g, 0))),
        out_shape=jax.ShapeDtypeStruct((R, D_MODEL), F32),
        compiler_params=_params("arbitrary"),
        name="moe_experts",
    )(tile_expert, n_used, xs, w_gate, w_up, w_down)


COMBINE_TM = 1024


def _combine_kernel(p1_ref, p2_ref, x1_ref, wts_ref, ys_ref, o_ref, b1_s, b2_s, sem):
    def row_copy(buf, i, pos):
        return pltpu.make_async_copy(ys_ref.at[pl.ds(pos, 1), :], buf.at[pl.ds(i, 1), :], sem)

    def issue(j, carry):
        for u in range(DMA_UNROLL):
            i = j * DMA_UNROLL + u
            row_copy(b1_s, i, p1_ref[i]).start(priority=0)
            row_copy(b2_s, i, p2_ref[i]).start(priority=1)
        return carry

    lax.fori_loop(0, COMBINE_TM // DMA_UNROLL, issue, 0)
    pltpu.make_async_copy(ys_ref.at[pl.ds(0, COMBINE_TM), :], b1_s, sem).wait()
    pltpu.make_async_copy(ys_ref.at[pl.ds(0, COMBINE_TM), :], b2_s, sem).wait()
    w = wts_ref[...]
    o_ref[...] = x1_ref[...] + w[:, 0:1] * b1_s[...] + w[:, 1:2] * b2_s[...]


def _moe_combine(x1, wts, ys, pos1, pos2):
    N = x1.shape[0]
    smem1d = pl.BlockSpec((COMBINE_TM,), lambda i: (i,), memory_space=pltpu.SMEM)
    return pl.pallas_call(
        _combine_kernel,
        grid=(N // COMBINE_TM,),
        in_specs=[smem1d, smem1d,
                  pl.BlockSpec((COMBINE_TM, D_MODEL), lambda i: (i, 0)),
                  pl.BlockSpec((COMBINE_TM, LANES), lambda i: (i, 0)),
                  pl.BlockSpec(memory_space=pl.ANY)],
        out_specs=pl.BlockSpec((COMBINE_TM, D_MODEL), lambda i: (i, 0)),
        out_shape=jax.ShapeDtypeStruct((N, D_MODEL), F32),
        scratch_shapes=[pltpu.VMEM((COMBINE_TM, D_MODEL), F32), pltpu.VMEM((COMBINE_TM, D_MODEL), F32),
                        pltpu.SemaphoreType.DMA(())],
        compiler_params=_params("arbitrary"),
        name="moe_combine",
    )(pos1, pos2, x1, wts, ys)


def _moe(x1, h2, ids, wts, tile_counts, layer, w_gate, w_up, w_down):
    N = x1.shape[0]
    n_rows = 2 * N + N_EXPERTS * ROW_TILE
    counts = jnp.sum(tile_counts[::8, :N_EXPERTS], axis=0).astype(jnp.int32)
    tiles_per = (counts + ROW_TILE - 1) // ROW_TILE
    tile_end = jnp.cumsum(tiles_per)
    off = ((tile_end - tiles_per) * ROW_TILE).astype(jnp.int32)
    n_used = tile_end[-1:].astype(jnp.int32)
    tile_ids = jnp.arange(n_rows // ROW_TILE, dtype=jnp.int32)
    tile_expert = jnp.minimum(jnp.sum(tile_end[None, :] <= tile_ids[:, None], axis=1), N_EXPERTS - 1).astype(jnp.int32)
    pos = _moe_positions(ids, _pad_lanes(off.astype(F32), LANES))
    pos1, pos2 = pos[:, 0], pos[:, 1]
    xs = _moe_dispatch(h2, off, tiles_per.astype(jnp.int32), n_used, pos1, pos2, n_rows)
    ys = _moe_experts(xs, tile_expert, n_used, layer, w_gate, w_up, w_down)
    return _moe_combine(x1, wts, ys, pos1, pos2)


def kernel(x, positions, rel_bias, norm1_g, w_in, nsa_cmp_pe, nsa_cmp_w1, nsa_cmp_w2, nsa_q_norm, nsa_k_norm, nsa_out_norm, mlstm_conv, mlstm_i_bias, mlstm_f_bias, mlstm_out_norm, mla_q_a_norm, mla_w_uq, mla_kv_a_norm, mla_w_ukv, mla_qk_norm, mla_out_norm, w_out, norm2_g, moe_w_group, moe_b_group, moe_w_expert, moe_b_expert, moe_w_gate, moe_w_up, moe_w_down):
    B, T, D = x.shape
    N = B * T
    depth = w_in.shape[0]
    colmap = _proj_colmap()
    tiles, biasc = _bias_tables(rel_bias, T)
    rope_cs, rope_sn = _rope_tables(positions.reshape(N, 1))
    x2 = x.reshape(N, D)
    for l in range(depth):
        w_proj = _take_cols(w_in[l], colmap).astype(BF16)
        proj2 = _inproj(x2, norm1_g[l].reshape(1, D), w_proj)
        proj = proj2.reshape(B, T, D_PROJ)
        ya = _nsa(proj, tiles, biasc, nsa_cmp_pe[l], nsa_cmp_w1[l], nsa_cmp_w2[l], nsa_q_norm[l], nsa_k_norm[l])
        yb = _mlstm4(proj, mlstm_conv[l], mlstm_i_bias[l], mlstm_f_bias[l], mlstm_out_norm[l])
        q, k, v = _mla_prep(proj2, rope_cs, rope_sn, mla_q_a_norm[l], mla_w_uq[l], mla_kv_a_norm[l], mla_w_ukv[l], mla_qk_norm[l])
        yc = _mla_attn(q.reshape(B, T, -1), k.reshape(B, T, -1), v.reshape(B, T, -1))
        x1, h2, ids, wts, tile_counts = _outproj(x2, ya.reshape(N, -1), yb.reshape(N, -1), yc.reshape(N, -1), w_out[l],
                                    nsa_out_norm[l], mla_out_norm[l], norm2_g[l], moe_w_group[l], moe_b_group[l],
                                    moe_w_expert[l], moe_b_expert[l])
        x2 = _moe(x1, h2, ids, wts, tile_counts, l, moe_w_gate, moe_w_up, moe_w_down)
    return x2.reshape(B, T, D)
```

```python
import functools
import math

import numpy as np
import jax
import jax.numpy as jnp
from jax import lax
from jax.experimental import pallas as pl
from jax.experimental.pallas import tpu as pltpu

F32 = jnp.float32
BF16 = jnp.bfloat16

D_MODEL = 1024
HEAD_DIM = 64
LANES = 128
A_HEADS, A_KV_HEADS, A_GROUP = 6, 2, 3
CMP_BLOCK, CMP_STRIDE, SEL_BLOCK, N_SELECT, WINDOW = 32, 16, 64, 16, 512
FORCE = 1e4
B_HEADS, CONV_K, MLSTM_CHUNK = 4, 4, 64
C_HEADS, C_NOPE, C_ROPE, C_V = 6, 64, 32, 64
Q_RANK, KV_RANK = 192, 128
ROPE_THETA = 10000.0
REL_BUCKETS, REL_MAX_DIST = 32, 128
N_GROUPS, EXPERTS_PER_GROUP, N_EXPERTS, D_EXPERT = 4, 8, 32, 256
EPS = 1e-6
NEG = -1e30

LOG2E = 1.4426950408889634

AT = 256
RB = 128
ROW_TILE = 512
VMEM_LIMIT = 48 * 1024 * 1024

SEG_AQ, SEG_AKV, SEG_BQK, SEG_BV, SEG_BO, SEG_CDQ = 0, 768, 1536, 2048, 2304, 2560
SEG_GATE, SEG_CDKV, SEG_CKR, SEG_BIF, D_PROJ = 2816, 3072, 3200, 3328, 3840


def _params(*sem):
    return pltpu.CompilerParams(dimension_semantics=sem, vmem_limit_bytes=VMEM_LIMIT)


def _dot(a, b):
    return jnp.dot(a, b, preferred_element_type=F32)


def _dot_nt(a, b):
    return lax.dot_general(a, b, (((1,), (1,)), ((), ())), preferred_element_type=F32)


def _dot_tn(a, b):
    return lax.dot_general(a, b, (((0,), (0,)), ((), ())), preferred_element_type=F32)


def _split3(x):
    hi = x.astype(BF16)
    r1 = x - hi.astype(F32)
    mid = r1.astype(BF16)
    lo = (r1 - mid.astype(F32)).astype(BF16)
    return hi, mid, lo


def _pack_bf16_pairs(a, b):
    ua = lax.bitcast_convert_type(a.astype(BF16).astype(F32), jnp.uint32)
    ub = lax.bitcast_convert_type(b.astype(BF16).astype(F32), jnp.uint32)
    return lax.shift_right_logical(ua, jnp.uint32(16)) | (ub & jnp.uint32(0xFFFF0000))


def _unpack_bf16_pairs(w):
    a = lax.bitcast_convert_type(lax.shift_left(w, jnp.uint32(16)), F32)
    b = lax.bitcast_convert_type(w & jnp.uint32(0xFFFF0000), F32)
    return a.astype(BF16), b.astype(BF16)


def _bucket_np(dist):
    n = np.maximum(dist, 0)
    exact = REL_BUCKETS // 2
    nf = np.maximum(n, 1).astype(np.float64)
    large = exact + (np.log(nf / exact) / math.log(REL_MAX_DIST / exact) * (REL_BUCKETS - exact)).astype(np.int64)
    large = np.minimum(large, REL_BUCKETS - 1)
    return np.where(n < exact, n, large).astype(np.int32)


def _bucket_tables(T):
    r = np.arange(AT)[:, None]
    c = np.arange(AT)[None, :]
    diag = np.where(r >= c, _bucket_np(r - c), -1)
    prev = _bucket_np(AT + r - c)
    far = np.where(c > r, _bucket_np(2 * AT + r - c), -1)
    tiles = np.stack([diag, prev, far]).astype(np.int32)
    tq = np.arange(T)[:, None]
    n = np.arange(LANES)[None, :]
    dist_c = tq - (n * CMP_STRIDE + CMP_BLOCK - 1)
    cmp_tbl = np.where(dist_c >= 0, _bucket_np(dist_c), -1).astype(np.int32)
    return tiles, cmp_tbl


def _proj_colmap():
    cm = np.full((D_PROJ,), -1, np.int64)
    for hd in range(A_HEADS):
        cm[SEG_AQ + hd * LANES: SEG_AQ + hd * LANES + HEAD_DIM] = hd * HEAD_DIM + np.arange(HEAD_DIM)
    cm[SEG_AKV: SEG_AKV + 12 * HEAD_DIM] = 384 + np.arange(12 * HEAD_DIM)
    for h in range(A_KV_HEADS):
        cm[SEG_GATE + h * LANES + HEAD_DIM: SEG_GATE + h * LANES + HEAD_DIM + 9] = 1152 + h * 9 + np.arange(9)
    cm[SEG_BQK: SEG_BQK + 512] = 1170 + np.arange(512)
    cm[SEG_BV: SEG_BV + 256] = 1682 + np.arange(256)
    cm[SEG_BIF: SEG_BIF + 2 * B_HEADS * HEAD_DIM] = 1938 + np.arange(2 * B_HEADS * HEAD_DIM) // HEAD_DIM
    cm[SEG_BO: SEG_BO + 256] = 1946 + np.arange(256)
    cm[SEG_CDQ: SEG_CDQ + Q_RANK] = 2202 + np.arange(Q_RANK)
    cm[SEG_CDKV: SEG_CDKV + KV_RANK] = 2394 + np.arange(KV_RANK)
    cm[SEG_CKR + C_NOPE: SEG_CKR + C_NOPE + C_ROPE] = 2522 + np.arange(C_ROPE)
    return cm


def _take_cols(w, colmap):
    wz = jnp.concatenate([w, jnp.zeros((w.shape[0], 1), w.dtype)], axis=1)
    return wz[:, np.where(colmap < 0, w.shape[1], colmap)]


def _take_rows(w, rowmap):
    return _take_cols(w.T, rowmap).T


def _pad_lanes(v, n):
    v = v.reshape(1, -1)
    return jnp.pad(v, ((0, 0), (0, n - v.shape[1])))


def _bias_kernel(rb_ref, bt_ref, bc_ref, tiles_ref, cmp_ref):
    hd = pl.program_id(0)
    bt = bt_ref[...]
    bc = bc_ref[...]
    t = jnp.full(bt.shape, NEG, F32)
    c = jnp.full(bc.shape, NEG, F32)
    far = rb_ref[REL_BUCKETS - 1, hd]
    for b in range(REL_BUCKETS):
        val = rb_ref[b, hd]
        t = jnp.where(bt == b, (val - far) * LOG2E, t)
        c = jnp.where(bc == b, val * LOG2E, c)
    tiles_ref[0, 0:3] = t
    tiles_ref[0, 3] = jnp.full((AT, AT), NEG, F32)
    cmp_ref[0] = c


def _bias_tables(rel_bias, T):
    bt, bc = _bucket_tables(T)
    return pl.pallas_call(
        _bias_kernel,
        grid=(A_HEADS,),
        in_specs=[pl.BlockSpec(memory_space=pltpu.SMEM),
                  pl.BlockSpec((3, AT, AT), lambda h: (0, 0, 0)),
                  pl.BlockSpec((T, LANES), lambda h: (0, 0))],
        out_specs=[pl.BlockSpec((1, 4, AT, AT), lambda h: (h, 0, 0, 0)),
                   pl.BlockSpec((1, T, LANES), lambda h: (h, 0, 0))],
        out_shape=[jax.ShapeDtypeStruct((A_HEADS, 4, AT, AT), F32),
                   jax.ShapeDtypeStruct((A_HEADS, T, LANES), F32)],
        compiler_params=_params("arbitrary"),
        name="bias_tables",
    )(rel_bias, jnp.asarray(bt), jnp.asarray(bc))


def _inproj_kernel(x_ref, g_ref, w_ref, o_ref):
    x = x_ref[...]
    ms = jnp.mean(x * x, axis=-1, keepdims=True)
    h = (x * lax.rsqrt(ms + EPS) * g_ref[...]).astype(BF16)
    for j in range(0, D_PROJ, 384):
        o_ref[:, j:j + 384] = _dot(h, w_ref[:, j:j + 384])


def _inproj(x2, g, w):
    N = x2.shape[0]
    tm = 512
    return pl.pallas_call(
        _inproj_kernel,
        grid=(N // tm,),
        in_specs=[pl.BlockSpec((tm, D_MODEL), lambda i: (i, 0)),
                  pl.BlockSpec((1, D_MODEL), lambda i: (0, 0)),
                  pl.BlockSpec((D_MODEL, D_PROJ), lambda i: (0, 0))],
        out_specs=pl.BlockSpec((tm, D_PROJ), lambda i: (i, 0)),
        out_shape=jax.ShapeDtypeStruct((N, D_PROJ), F32),
        compiler_params=_params("parallel"),
        name="inproj",
    )(x2, g, w)


def _rms_lanes(x, g, width):
    ms = jnp.sum(x * x, axis=-1, keepdims=True) * (1.0 / width)
    return x * lax.rsqrt(ms + EPS) * g


def _qk_halves(q_r, k_blk):
    half_rows = q_r.shape[0] // 2
    return [_dot_nt(q_r[h0:h0 + half_rows, :], k_blk) for h0 in range(0, q_r.shape[0], half_rows)]


def _flash_chunk(q_r, k_blk, v_blk, bias_fn, m_r, acc_r):
    _softmax_pv(_qk_halves(q_r, k_blk), v_blk, bias_fn, m_r, acc_r)


def _softmax_pv(s_halves, v_blk, bias_fn, m_r, acc_r):
    half_rows = m_r.shape[0] // 2
    assert half_rows % RB == 0
    for hi_, h0 in enumerate(range(0, m_r.shape[0], half_rows)):
        s_all = s_halves[hi_]
        ps, alphas = [], []
        for r in range(half_rows // RB):
            rows = slice(h0 + r * RB, h0 + (r + 1) * RB)
            s = s_all[r * RB:(r + 1) * RB]
            if bias_fn is not None:
                s = s + bias_fn((h0 + r * RB) // RB)
            lo, hi = s[:, :LANES], s[:, LANES:]
            mo = m_r[rows, :]
            mn = jnp.maximum(mo, jnp.max(jnp.maximum(lo, hi), axis=-1, keepdims=True))
            ps.append(jnp.exp2(jnp.concatenate([lo - mn, hi - mn], axis=1)).astype(BF16))
            alphas.append(jnp.exp2(mo - mn))
            m_r[rows, :] = mn
        pv = _dot(jnp.concatenate(ps, axis=0), v_blk[hi_] if isinstance(v_blk, (list, tuple)) else v_blk)
        for r in range(half_rows // RB):
            rows = slice(h0 + r * RB, h0 + (r + 1) * RB)
            acc_r[rows, :] = alphas[r] * acc_r[rows, :] + pv[r * RB:(r + 1) * RB]


def _nsa_kernel(q_ref, k0_ref, v0_ref, k1_ref, v1_ref, k2_ref, v2_ref, gate_ref, tiles_ref, biasc_ref,
                ovt_ref, place_ref, w1_ref, w2_ref, pe_ref, qg_ref, kg_ref, o_ref,
                kc_s, vc_s, k1_s, v1_s, k2_s, v2_s, qx_s, qw_s, s_s, m_s, acc_s, mw_s, accw_s):
    qi = pl.program_id(2)
    kg = kg_ref[...]
    n_cmp_pad = kc_s.shape[0]
    per_g = AT // RB

    def prep(kv_head):
        T = k1_s.shape[0]
        lane_t = lax.broadcasted_iota(jnp.int32, (T, LANES), 1)
        row_t = lax.broadcasted_iota(jnp.int32, (T, LANES), 0)

        def own_half(x):
            if kv_head == 1:
                x = pltpu.roll(x, HEAD_DIM, axis=1)
            return jnp.where(lane_t < HEAD_DIM, x, 0.0)

        block_onehot = lax.shift_right_logical(row_t, 6) == lane_t - HEAD_DIM
        ones_lane = (lane_t >= HEAD_DIM) & (lane_t < HEAD_DIM + 3 * A_GROUP)
        k1_s[...] = jnp.where(block_onehot, 1.0, _rms_lanes(own_half(k1_ref[0]), kg, HEAD_DIM)).astype(BF16)
        v1_s[...] = jnp.where(ones_lane, 1.0, own_half(v1_ref[0])).astype(BF16)
        k2_s[...] = _rms_lanes(own_half(k2_ref[0]), kg, HEAD_DIM).astype(BF16)
        v2_s[...] = jnp.where(ones_lane, 1.0, own_half(v2_ref[0])).astype(BF16)
        half = CMP_BLOCK // 2
        lanes_own = slice(kv_head * HEAD_DIM, (kv_head + 1) * HEAD_DIM)
        for s, (src, dst) in enumerate(((k0_ref, kc_s), (v0_ref, vc_s))):
            a = jnp.zeros((n_cmp_pad, HEAD_DIM), F32)
            bm = jnp.zeros((n_cmp_pad, HEAD_DIM), F32)
            for j in range(half):
                xj = src[0, pl.ds(j, n_cmp_pad, stride=CMP_STRIDE), :][:, lanes_own]
                a = a + _dot((xj + pe_ref[s, j:j + 1, :]).astype(BF16), w1_ref[s, j])
                bm = bm + _dot((xj + pe_ref[s, half + j:half + j + 1, :]).astype(BF16), w1_ref[s, half + j])
            pre = a + pltpu.roll(bm, n_cmp_pad - 1, axis=0)
            hid = pre * jax.nn.sigmoid(pre)
            comp = _dot(hid.astype(BF16), w2_ref[s])
            if s == 0:
                comp = _rms_lanes(comp, kg, HEAD_DIM)
            dst[...] = comp.astype(BF16)

    for kv_head in range(A_KV_HEADS):
        @pl.when((qi == 0) & (pl.program_id(0) == kv_head))
        def _prep(kv_head=kv_head):
            prep(kv_head)

    qscale = HEAD_DIM ** -0.5 * LOG2E
    q = q_ref[0]
    qg = qg_ref[...]
    qs = jnp.concatenate(
        [_rms_lanes(q[:, g * LANES:(g + 1) * LANES], qg, HEAD_DIM) * qscale for g in range(A_GROUP)],
        axis=0).astype(BF16)
    qw_s[...] = qs

    def near_bias(delta, kind):
        kd = jnp.where(qi >= delta, kind, 3)
        return lambda r: tiles_ref[r // per_g, kd, pl.ds((r % per_g) * RB, RB), :]

    def key_rows(kc):
        return pl.ds(pl.multiple_of(kc * AT, AT), AT)

    mw_s[...] = jnp.full(mw_s.shape, NEG, F32)
    accw_s[...] = jnp.zeros(accw_s.shape, F32)
    for delta, kind in ((0, 0), (1, 1), (2, 2)):
        rows_k = key_rows(jnp.maximum(qi - delta, 0))
        _flash_chunk(qw_s, k2_s[rows_k, :], v2_s[rows_k, :], near_bias(delta, kind), mw_s, accw_s)

    biasc = biasc_ref[...].reshape(A_GROUP * AT, LANES)
    valid = biasc > 0.5 * NEG
    lcm = _dot_nt(qs, kc_s[...]) + biasc
    mc = jnp.max(lcm, axis=-1, keepdims=True)
    pc = jnp.where(valid, jnp.exp2(lcm - mc), 0.0)
    lc = jnp.sum(pc, axis=-1, keepdims=True)
    pc = pc * jnp.where(lc > 0.0, 1.0 / lc, 0.0)
    o_c = _dot(pc.astype(BF16), vc_s[...])

    ovt = ovt_ref[...]
    p_hi, p_mid, p_lo = _split3(pc[0:AT] + pc[AT:2 * AT] + pc[2 * AT:3 * AT])
    ps3 = _dot_nt(ovt, jnp.concatenate([p_hi, p_mid, p_lo], axis=0))
    ps = ps3[:, :AT] + ps3[:, AT:2 * AT] + ps3[:, 2 * AT:]
    n_sel = ps.shape[0]
    sid = lax.broadcasted_iota(jnp.int32, (n_sel, AT), 0)
    qcol = lax.broadcasted_iota(jnp.int32, (n_sel, AT), 1)
    cur = qi * (AT // SEL_BLOCK) + lax.shift_right_logical(qcol, 6)
    forced = (sid == 0) | (sid == cur) | (sid == cur - 1)
    psf = jnp.where(forced, ps + FORCE, ps)
    psf = jnp.where(sid <= cur, psf, -1.0)
    rank = jnp.zeros((n_sel, AT), F32)
    for sp in range(n_sel):
        other = psf[sp:sp + 1, :]
        beats = (other > psf) | ((other == psf) & (sid > sp))
        rank = rank + jnp.where(beats, 1.0, 0.0)
    keep = (rank < float(min(N_SELECT, n_sel))) & (psf >= 0.0)
    selneg = _dot_tn(jnp.where(keep, 0.0, NEG).astype(BF16), place_ref[...]).astype(BF16)
    lane_q = lax.broadcasted_iota(jnp.int32, (AT, LANES), 1)
    for g in range(A_GROUP):
        qx_s[g * AT:(g + 1) * AT, :] = jnp.where(lane_q < HEAD_DIM, qs[g * AT:(g + 1) * AT], selneg)

    m_s[...] = jnp.full(m_s.shape, NEG, F32)
    acc_s[...] = jnp.zeros(acc_s.shape, F32)
    for delta, kind in ((0, 0), (1, 1)):
        rows_k = key_rows(jnp.maximum(qi - delta, 0))
        _flash_chunk(qx_s, k1_s[rows_k, :], v1_s[rows_k, :], near_bias(delta, kind), m_s, acc_s)

    for h, s in enumerate(_qk_halves(qx_s, k1_s[key_rows(0), :])):
        s_s[h] = s

    def far_body(kc, carry):
        s_next = _qk_halves(qx_s, k1_s[key_rows(kc + 1), :])
        _softmax_pv([s_s[0], s_s[1]], v1_s[key_rows(kc), :], None, m_s, acc_s)
        for h, s in enumerate(s_next):
            s_s[h] = s
        return carry

    lax.fori_loop(0, jnp.maximum(qi - 1, 0), far_body, 0)

    gt = jax.nn.sigmoid(gate_ref[0])
    gate_lanes = (lane_q >= HEAD_DIM) & (lane_q < HEAD_DIM + 3 * A_GROUP)
    for g in range(A_GROUP):
        rows = slice(g * AT, (g + 1) * AT)
        a_s = acc_s[rows, :]
        a_w = accw_s[rows, :]
        c_s = gt / jnp.where(gate_lanes, a_s, 1.0)
        c_w = gt / jnp.where(gate_lanes, a_w, 1.0)
        j = HEAD_DIM + 3 * g
        o = gt[:, j:j + 1] * o_c[rows] + c_s[:, j + 1:j + 2] * a_s + c_w[:, j + 2:j + 3] * a_w
        o_ref[0, :, g * LANES:(g + 1) * LANES] = jnp.where(lane_q < HEAD_DIM, o, 0.0)


def _nsa(proj, tiles, biasc, cmp_pe, cmp_w1, cmp_w2, q_norm, k_norm):
    B, T, _ = proj.shape
    nq = T // AT
    n_cmp_pad = T // CMP_STRIDE
    n_sel = T // SEL_BLOCK
    assert n_sel <= LANES - HEAD_DIM - 1 and n_cmp_pad == LANES
    starts = np.arange(n_cmp_pad) * CMP_STRIDE
    sid = np.arange(n_sel)
    overlap = ((starts[:, None] < (sid[None, :] + 1) * SEL_BLOCK)
               & (starts[:, None] + CMP_BLOCK > sid[None, :] * SEL_BLOCK)
               & (starts[:, None] + CMP_BLOCK <= T))
    place = np.zeros((n_sel, LANES), np.float32)
    place[sid, HEAD_DIM + sid] = 1.0
    w2p = jnp.pad(cmp_w2, ((0, 0), (0, 0), (0, LANES - HEAD_DIM)))

    def slab(br, kv):
        base = SEG_AKV // LANES + (br * 2 + kv)
        return pl.BlockSpec((1, T, LANES), lambda h, b, i: (b, 0, base))

    def full(shape):
        return pl.BlockSpec(shape, lambda h, b, i: (0,) * len(shape))

    rows = A_GROUP * AT
    return pl.pallas_call(
        _nsa_kernel,
        grid=(A_KV_HEADS, B, nq),
        in_specs=[pl.BlockSpec((1, AT, A_GROUP * LANES), lambda h, b, i: (b, i, h)),
                  slab(0, 0), slab(0, 1), slab(1, 0), slab(1, 1), slab(2, 0), slab(2, 1),
                  pl.BlockSpec((1, AT, LANES), lambda h, b, i: (b, i, SEG_GATE // LANES + h)),
                  pl.BlockSpec((A_GROUP, 4, AT, AT), lambda h, b, i: (h, 0, 0, 0)),
                  pl.BlockSpec((A_GROUP, AT, LANES), lambda h, b, i: (h, i, 0)),
                  full((n_sel, n_cmp_pad)), full((n_sel, LANES)),
                  full((2, CMP_BLOCK, HEAD_DIM, HEAD_DIM)), full((2, HEAD_DIM, LANES)),
                  full((2, CMP_BLOCK, HEAD_DIM)), full((1, LANES)), full((1, LANES))],
        out_specs=pl.BlockSpec((1, AT, A_GROUP * LANES), lambda h, b, i: (b, i, h)),
        out_shape=jax.ShapeDtypeStruct((B, T, A_HEADS * LANES), F32),
        scratch_shapes=[pltpu.VMEM((n_cmp_pad, LANES), BF16), pltpu.VMEM((n_cmp_pad, LANES), BF16),
                        pltpu.VMEM((T, LANES), BF16), pltpu.VMEM((T, LANES), BF16),
                        pltpu.VMEM((T, LANES), BF16), pltpu.VMEM((T, LANES), BF16),
                        pltpu.VMEM((rows, LANES), BF16), pltpu.VMEM((rows, LANES), BF16),
                        pltpu.VMEM((2, rows // 2, AT), F32),
                        pltpu.VMEM((rows, LANES), F32), pltpu.VMEM((rows, LANES), F32),
                        pltpu.VMEM((rows, LANES), F32), pltpu.VMEM((rows, LANES), F32)],
        compiler_params=_params("parallel", "parallel", "arbitrary"),
        name="nsa",
    )(proj, proj, proj, proj, proj, proj, proj, proj, tiles, biasc,
      jnp.asarray(overlap.T, BF16), jnp.asarray(place, BF16), cmp_w1.astype(BF16), w2p.astype(BF16), cmp_pe,
      _pad_lanes(q_norm, LANES), _pad_lanes(k_norm, LANES))


MLSTM_NB = 4
MLSTM_TB = 512


def _mlstm4_kernel(qk_ref, v_ref, o_ref, gi_ref, gf_ref, cw_ref, bi_ref, bf_ref, gn_ref, tril_ref, y_ref,
                   xpad_s, c_s, n_s, m_s):
    nb, tb = qk_ref.shape[0], qk_ref.shape[1]
    L = MLSTM_CHUNK
    W = B_HEADS * HEAD_DIM

    @pl.when(pl.program_id(1) == 0)
    def _init():
        xpad_s[:, tb:tb + 8, :] = jnp.zeros((nb, 8, xpad_s.shape[2]), F32)
        c_s[...] = jnp.zeros(c_s.shape, F32)
        n_s[...] = jnp.zeros(n_s.shape, F32)
        m_s[...] = jnp.zeros(m_s.shape, F32)

    for bb in range(nb):
        xpad_s[bb, 0:8, :] = xpad_s[bb, tb:tb + 8, :]
        xpad_s[bb, 8:8 + tb, :] = qk_ref[bb]
    lane_head = lax.shift_right_logical(lax.broadcasted_iota(jnp.int32, (1, W), 1), 6)
    seg = [lane_head == a for a in range(B_HEADS)]
    row_l = lax.broadcasted_iota(jnp.int32, (L, W), 0)
    lane_l = lax.broadcasted_iota(jnp.int32, (L, W), 1) & (HEAD_DIM - 1)
    tri = lane_l <= row_l
    diag = lane_l == row_l
    blockdiag = (lax.shift_right_logical(lax.broadcasted_iota(jnp.int32, (W, W), 0), 6)
                 == lax.shift_right_logical(lax.broadcasted_iota(jnp.int32, (W, W), 1), 6))
    kscale = HEAD_DIM ** -0.5

    def seg_reduce(x, op, fill):
        out = None
        for a in range(B_HEADS):
            r = op(jnp.where(seg[a], x, fill)[:, (a // 2) * LANES:(a // 2 + 1) * LANES], axis=-1, keepdims=True)
            out = r if out is None else jnp.where(seg[a], r, out)
        return out

    def chunk_one(bb, c):
        r0 = pl.multiple_of(c * L, L)
        win = xpad_s[bb, pl.ds(r0, L + 8), :]
        conv = cw_ref[CONV_K - 1:CONV_K, :] * win[8:8 + L]
        for j in range(CONV_K - 1):
            conv = conv + cw_ref[j:j + 1, :] * pltpu.roll(win, CONV_K - 1 - j, axis=0)[8:8 + L]
        qk = conv * jax.nn.sigmoid(conv)
        q4 = qk[:, :W]
        k4 = qk[:, W:] * kscale
        v4 = v_ref[bb, pl.ds(r0, L), :]
        i4 = gi_ref[bb, pl.ds(r0, L), :] + bi_ref[...]
        logf = jax.nn.log_sigmoid(gf_ref[bb, pl.ds(r0, L), :] + bf_ref[...])
        lh, lm, ll = _split3(logf)
        cum = _dot(tril_ref[...], jnp.concatenate([lh, lm, ll], axis=1))
        b4 = cum[:, :W] + cum[:, W:2 * W] + cum[:, 2 * W:]
        c_old, n_old, m_prev = c_s[bb], n_s[bb], m_s[bb][0:1, :]
        r4 = jnp.sum(jnp.where(diag, i4 - b4, 0.0), axis=0, keepdims=True)
        dmat = jnp.where(tri, b4 + r4, -jnp.inf)
        m_inter = b4 + m_prev
        m_j = jnp.maximum(m_inter, seg_reduce(dmat, jnp.max, -jnp.inf))
        isc = jnp.exp(m_inter - m_j)
        k_heads = jnp.concatenate([jnp.where(seg[a], k4, 0.0) for a in range(B_HEADS)], axis=0)
        v_heads = jnp.concatenate([jnp.where(seg[a], v4, 0.0) for a in range(B_HEADS)], axis=0)
        s = _dot_nt(q4, k_heads) * jnp.exp(dmat - m_j)
        den = seg_reduce(s, jnp.sum, 0.0) + isc * seg_reduce(q4 * n_old, jnp.sum, 0.0)
        num = _dot(s, v_heads) + isc * _dot_nt(q4, c_old)
        h = num / jnp.maximum(jnp.abs(den), jnp.exp(-m_j))
        b_last = b4[L - 1:L, :]
        w_log = b_last - b4 + i4
        m_new = jnp.maximum(b_last + m_prev, jnp.max(w_log, axis=0, keepdims=True))
        w4 = jnp.exp(w_log - m_new)
        dec = jnp.exp(b_last + m_prev - m_new)
        c_s[bb] = dec * c_old + jnp.where(blockdiag, _dot_tn(v4 * w4, k4), 0.0)
        n_s[bb] = dec * n_old + jnp.sum(w4 * k4, axis=0, keepdims=True)
        m_s[bb] = jnp.broadcast_to(m_new, m_s.shape[1:])
        ms = seg_reduce(h * h, jnp.sum, 0.0) * (1.0 / HEAD_DIM)
        hn = h * lax.rsqrt(ms + EPS) * gn_ref[...]
        y_ref[bb, pl.ds(r0, L), :] = jax.nn.sigmoid(o_ref[bb, pl.ds(r0, L), :]) * hn

    def chunk(c, carry):
        for bb in range(nb):
            chunk_one(bb, c)
        return carry

    lax.fori_loop(0, tb // L, chunk, 0)


def _mlstm4(proj, conv_w, i_bias, f_bias, out_norm):
    B, T, _ = proj.shape
    W = B_HEADS * HEAD_DIM
    nb, tb = MLSTM_NB, MLSTM_TB
    tril = jnp.asarray(np.tril(np.ones((MLSTM_CHUNK, MLSTM_CHUNK))), BF16)

    def full(shape):
        return pl.BlockSpec(shape, lambda b, t: (0,) * len(shape))

    def seg(width, off):
        return pl.BlockSpec((nb, tb, width), lambda b, t: (b, t, off // width))

    return pl.pallas_call(
        _mlstm4_kernel,
        grid=(B // nb, T // tb),
        in_specs=[seg(2 * W, SEG_BQK), seg(W, SEG_BV), seg(W, SEG_BO), seg(W, SEG_BIF), seg(W, SEG_BIF + W),
                  full((CONV_K, 2 * W)), full((1, W)), full((1, W)), full((1, W)),
                  full((MLSTM_CHUNK, MLSTM_CHUNK))],
        out_specs=pl.BlockSpec((nb, tb, W), lambda b, t: (b, t, 0)),
        out_shape=jax.ShapeDtypeStruct((B, T, W), F32),
        scratch_shapes=[pltpu.VMEM((nb, tb + 8, 2 * W), F32),
                        pltpu.VMEM((nb, W, W), F32),
                        pltpu.VMEM((nb, 1, W), F32),
                        pltpu.VMEM((nb, 8, W), F32)],
        compiler_params=_params("parallel", "arbitrary"),
        name="mlstm",
    )(proj, proj, proj, proj, proj, conv_w, jnp.repeat(i_bias, HEAD_DIM).reshape(1, W),
      jnp.repeat(f_bias, HEAD_DIM).reshape(1, W), out_norm.reshape(1, W), tril)


def _rope_kernel(pos_ref, invf_ref, sgn_ref, cs_ref, sn_ref):
    ang = pos_ref[...].astype(F32) * invf_ref[...]
    cs_ref[...] = jnp.cos(ang)
    sn_ref[...] = jnp.sin(ang) * sgn_ref[...]


def _rope_tables(pos):
    N = pos.shape[0]
    tm = 1024
    half = C_ROPE // 2
    inv = 1.0 / (ROPE_THETA ** (np.arange(half, dtype=np.float32) * 2.0 / C_ROPE))
    invf = np.zeros((1, LANES), np.float32)
    sgn = np.zeros((1, LANES), np.float32)
    invf[0, C_NOPE:C_NOPE + C_ROPE] = np.concatenate([inv, inv])
    sgn[0, C_NOPE:C_NOPE + half] = -1.0
    sgn[0, C_NOPE + half:C_NOPE + C_ROPE] = 1.0
    row = pl.BlockSpec((1, LANES), lambda i: (0, 0))
    tile = pl.BlockSpec((tm, LANES), lambda i: (i, 0))
    return pl.pallas_call(
        _rope_kernel,
        grid=(N // tm,),
        in_specs=[pl.BlockSpec((tm, 1), lambda i: (i, 0)), row, row],
        out_specs=[tile, tile],
        out_shape=[jax.ShapeDtypeStruct((N, LANES), F32), jax.ShapeDtypeStruct((N, LANES), F32)],
        compiler_params=_params("parallel"),
        name="rope_tables",
    )(pos, jnp.asarray(invf), jnp.asarray(sgn))


def _mla_prep_kernel(dq_ref, dkv_ref, kr_ref, cs_ref, sn_ref, wq_ref, wk_ref, wv_ref, qa_ref, kva_ref,
                     gq_ref, gk_ref, gkr_ref, q_out, k_out, v_out):
    lane = lax.broadcasted_iota(jnp.int32, (1, LANES), 1)
    nope = lane < C_NOPE
    ropem = (lane >= C_NOPE) & (lane < C_NOPE + C_ROPE)
    first = lane < C_NOPE + C_ROPE // 2
    cs = cs_ref[...]
    sn = sn_ref[...]

    def rope(x):
        partner = jnp.where(first, pltpu.roll(x, LANES - C_ROPE // 2, axis=1), pltpu.roll(x, C_ROPE // 2, axis=1))
        return x * cs + partner * sn

    scale = (C_NOPE + C_ROPE) ** -0.5 * LOG2E
    qn = _rms_lanes(dq_ref[...], qa_ref[...], Q_RANK).astype(BF16)
    q = _dot(qn, wq_ref[...])
    kvn = _rms_lanes(dkv_ref[...], kva_ref[...], KV_RANK).astype(BF16)
    kk = _dot(kvn, wk_ref[...])
    vv = _dot(kvn, wv_ref[...])
    lane_v = lax.broadcasted_iota(jnp.int32, vv.shape, 1)
    v_out[...] = jnp.where((lane_v & (LANES - 1)) == C_V, 1.0, vv).astype(BF16)
    kr = kr_ref[...]
    krn = kr * lax.rsqrt(jnp.sum(kr * kr, axis=-1, keepdims=True) * (1.0 / C_ROPE) + EPS) * gkr_ref[...]
    krr = rope(krn)
    for h in range(C_HEADS):
        sl = slice(h * LANES, (h + 1) * LANES)
        qh = q[:, sl]
        sq = qh * qh
        msn = jnp.sum(jnp.where(nope, sq, 0.0), axis=-1, keepdims=True) * (1.0 / C_NOPE)
        msr = jnp.sum(jnp.where(ropem, sq, 0.0), axis=-1, keepdims=True) * (1.0 / C_ROPE)
        qhn = qh * jnp.where(nope, lax.rsqrt(msn + EPS), lax.rsqrt(msr + EPS)) * gq_ref[...]
        q_out[:, sl] = (rope(qhn) * scale).astype(BF16)
        kh = kk[:, sl]
        msk = jnp.sum(kh * kh, axis=-1, keepdims=True) * (1.0 / C_NOPE)
        k_out[:, sl] = (kh * lax.rsqrt(msk + EPS) * gk_ref[...] + krr).astype(BF16)


def _mla_prep(proj2, rope_cs, rope_sn, q_a_norm, w_uq, kv_a_norm, w_ukv, qk_norm):
    N = proj2.shape[0]
    tm = 512
    qd = C_NOPE + C_ROPE
    qmap = np.full((C_HEADS * LANES,), -1, np.int64)
    kmap = np.full((C_HEADS * LANES,), -1, np.int64)
    vmap = np.full((C_HEADS * LANES,), -1, np.int64)
    for h in range(C_HEADS):
        qmap[h * LANES:h * LANES + qd] = h * qd + np.arange(qd)
        kmap[h * LANES:h * LANES + C_NOPE] = h * (C_NOPE + C_V) + np.arange(C_NOPE)
        vmap[h * LANES:h * LANES + C_V] = h * (C_NOPE + C_V) + C_NOPE + np.arange(C_V)
    wq = jnp.pad(_take_cols(w_uq, qmap), ((0, 256 - Q_RANK), (0, 0))).astype(BF16)
    wk = _take_cols(w_ukv, kmap).astype(BF16)
    wv = _take_cols(w_ukv, vmap).astype(BF16)
    gkr = jnp.pad(qk_norm[1, C_NOPE:].reshape(1, -1), ((0, 0), (C_NOPE, LANES - C_NOPE - C_ROPE)))

    def full(shape):
        return pl.BlockSpec(shape, lambda i: (0,) * len(shape))

    return pl.pallas_call(
        _mla_prep_kernel,
        grid=(N // tm,),
        in_specs=[pl.BlockSpec((tm, 256), lambda i: (i, SEG_CDQ // 256)),
                  pl.BlockSpec((tm, LANES), lambda i: (i, SEG_CDKV // LANES)),
                  pl.BlockSpec((tm, LANES), lambda i: (i, SEG_CKR // LANES)),
                  pl.BlockSpec((tm, LANES), lambda i: (i, 0)), pl.BlockSpec((tm, LANES), lambda i: (i, 0)),
                  full((256, C_HEADS * LANES)), full((KV_RANK, C_HEADS * LANES)), full((KV_RANK, C_HEADS * LANES)),
                  full((1, 256)), full((1, LANES)), full((1, LANES)), full((1, LANES)), full((1, LANES))],
        out_specs=[pl.BlockSpec((tm, C_HEADS * LANES), lambda i: (i, 0)),
                   pl.BlockSpec((tm, C_HEADS * LANES), lambda i: (i, 0)),
                   pl.BlockSpec((tm, C_HEADS * LANES), lambda i: (i, 0))],
        out_shape=[jax.ShapeDtypeStruct((N, C_HEADS * LANES), BF16),
                   jax.ShapeDtypeStruct((N, C_HEADS * LANES), BF16),
                   jax.ShapeDtypeStruct((N, C_HEADS * LANES), BF16)],
        compiler_params=_params("parallel"),
        name="mla_prep",
    )(proj2, proj2, proj2, rope_cs, rope_sn, wq, wk, wv,
      _pad_lanes(q_a_norm, 256), kv_a_norm.reshape(1, -1), _pad_lanes(qk_norm[0], LANES),
      _pad_lanes(qk_norm[1, :C_NOPE], LANES), gkr)


MLA_Q = 2 * AT


def _mla_attn_kernel(q_ref, k_ref, v_ref, causal_ref, o_ref, s_s, m_s, acc_s):
    qi = pl.program_id(2)
    m_s[...] = jnp.full(m_s.shape, NEG, F32)
    acc_s[...] = jnp.zeros(acc_s.shape, F32)
    heads = [slice(hh * LANES, (hh + 1) * LANES) for hh in range(2)]
    per_head = MLA_Q // RB
    tiles_per_q = MLA_Q // AT

    def key_rows(kc):
        return pl.ds(pl.multiple_of(kc * AT, AT), AT)

    def scores(kc):
        return [_dot_nt(q_ref[0, :, sl], k_ref[0, key_rows(kc), sl]) for sl in heads]

    def values(kc):
        return [v_ref[0, key_rows(kc), sl] for sl in heads]

    for d in range(tiles_per_q):
        kc = tiles_per_q * qi + d
        _softmax_pv(scores(kc), values(kc),
                    lambda r, d=d: causal_ref[d, pl.ds((r % per_head) * RB, RB), :], m_s, acc_s)

    for h, s in enumerate(scores(0)):
        s_s[h] = s

    def body(kc, carry):
        s_next = scores(kc + 1)
        _softmax_pv([s_s[0], s_s[1]], values(kc), None, m_s, acc_s)
        for h, s in enumerate(s_next):
            s_s[h] = s
        return carry

    lax.fori_loop(0, tiles_per_q * qi, body, 0)
    lane = lax.broadcasted_iota(jnp.int32, (MLA_Q, LANES), 1)
    outs = []
    for hh in range(2):
        a = acc_s[hh * MLA_Q:(hh + 1) * MLA_Q, :]
        outs.append(jnp.where(lane < C_V, a / a[:, C_V:C_V + 1], 0.0))
    o_ref[0] = outs[0] + pltpu.roll(outs[1], C_V, axis=1)


def _mla_attn(q, k, v):
    B, T, _ = q.shape
    qpos = np.arange(MLA_Q)[None, :, None]
    kpos = (np.arange(MLA_Q // AT)[:, None, None] * AT + np.arange(AT)[None, None, :])
    causal = np.where(qpos >= kpos, 0.0, NEG).astype(np.float32)
    pair = pl.BlockSpec((1, T, 2 * LANES), lambda b, p, i: (b, 0, p))
    return pl.pallas_call(
        _mla_attn_kernel,
        grid=(B, C_HEADS // 2, T // MLA_Q),
        in_specs=[pl.BlockSpec((1, MLA_Q, 2 * LANES), lambda b, p, i: (b, i, p)), pair, pair,
                  pl.BlockSpec((MLA_Q // AT, MLA_Q, AT), lambda b, p, i: (0, 0, 0))],
        out_specs=pl.BlockSpec((1, MLA_Q, LANES), lambda b, p, i: (b, i, p)),
        out_shape=jax.ShapeDtypeStruct((B, T, C_HEADS * C_V), F32),
        scratch_shapes=[pltpu.VMEM((2, MLA_Q, AT), F32), pltpu.VMEM((2 * MLA_Q, LANES), F32),
                        pltpu.VMEM((2 * MLA_Q, LANES), F32)],
        compiler_params=_params("parallel", "parallel", "arbitrary"),
        name="mla_attn",
    )(q, k, v, jnp.asarray(causal))


def _outproj_kernel(x_ref, ya_ref, yb_ref, yc_ref, wa_ref, wb_ref, wc_ref, ga_ref, gc_ref, g2_ref,
                    wr_ref, br_ref, x1_ref, h2_ref, ids_ref, wts_ref, cnt_ref):
    ya = _rms_lanes(ya_ref[...], ga_ref[...], A_HEADS * HEAD_DIM).astype(BF16)
    yc = _rms_lanes(yc_ref[...], gc_ref[...], C_HEADS * C_V).astype(BF16)
    x1 = (x_ref[...] + _dot(ya, wa_ref[...]) + _dot(yb_ref[...].astype(BF16), wb_ref[...])
          + _dot(yc, wc_ref[...]))
    x1_ref[...] = x1
    h2 = _rms_lanes(x1, g2_ref[...], D_MODEL)
    h2_ref[...] = _pack_bf16_pairs(h2[:, :D_MODEL // 2], h2[:, D_MODEL // 2:])
    h2_hi = h2.astype(BF16)
    h2_mid = (h2 - h2_hi.astype(F32)).astype(BF16)
    logits = (_dot(h2_hi, wr_ref[0]) + _dot(h2_mid, wr_ref[0]) + _dot(h2_hi, wr_ref[1])) + br_ref[...]
    tm = logits.shape[0]
    lane = lax.broadcasted_iota(jnp.int32, (tm, LANES), 1)
    is_g = (lane >= N_EXPERTS) & (lane < N_EXPERTS + N_GROUPS)
    gl = jnp.where(is_g, logits, NEG)
    ge = jnp.where(is_g, jnp.exp(gl - jnp.max(gl, axis=-1, keepdims=True)), 0.0)
    gp = jnp.where(is_g, ge / jnp.sum(ge, axis=-1, keepdims=True), -1.0)
    g_top = jnp.max(gp, axis=-1, keepdims=True)
    g_idx = jnp.min(jnp.where(gp == g_top, lane, 2 * LANES), axis=-1, keepdims=True) - N_EXPERTS
    in_grp = (lane < N_EXPERTS) & (lax.shift_right_logical(lane, 3) == g_idx)
    el = jnp.where(in_grp, logits, NEG)
    ee = jnp.where(in_grp, jnp.exp(el - jnp.max(el, axis=-1, keepdims=True)), 0.0)
    ep = jnp.where(in_grp, ee / jnp.sum(ee, axis=-1, keepdims=True), -1.0)
    v1 = jnp.max(ep, axis=-1, keepdims=True)
    i1 = jnp.min(jnp.where(ep == v1, lane, 2 * LANES), axis=-1, keepdims=True)
    ep2 = jnp.where(lane == i1, -1.0, ep)
    v2 = jnp.max(ep2, axis=-1, keepdims=True)
    i2 = jnp.min(jnp.where(ep2 == v2, lane, 2 * LANES), axis=-1, keepdims=True)
    tot = v1 + v2
    ids_ref[...] = jnp.where(lane == 0, i1, jnp.where(lane == 1, i2, 0))
    wts_ref[...] = jnp.where(lane == 0, v1 / tot * g_top, jnp.where(lane == 1, v2 / tot * g_top, 0.0))
    per_expert = jnp.sum(jnp.where((lane == i1) | (lane == i2), 1.0, 0.0), axis=0, keepdims=True)
    cnt_ref[...] = jnp.broadcast_to(per_expert, cnt_ref.shape)


def _outproj(x2, ya, yb, yc, w_out, nsa_out_norm, mla_out_norm, norm2_g, w_group, b_group, w_expert, b_expert):
    N = x2.shape[0]
    tm = 1024
    amap = np.full((A_HEADS * LANES,), -1, np.int64)
    for hd in range(A_HEADS):
        amap[hd * LANES:hd * LANES + HEAD_DIM] = hd * HEAD_DIM + np.arange(HEAD_DIM)
    wa = _take_rows(w_out[:384], amap).astype(BF16)
    wb = w_out[384:640].astype(BF16)
    wc = w_out[640:].astype(BF16)
    ga = _take_cols(nsa_out_norm.reshape(1, -1), amap)
    wr = jnp.pad(jnp.concatenate([w_expert, w_group], axis=1), ((0, 0), (0, LANES - N_EXPERTS - N_GROUPS)))
    wr_hi = wr.astype(BF16)
    wr = jnp.stack([wr_hi, (wr - wr_hi.astype(F32)).astype(BF16)])
    br = _pad_lanes(jnp.concatenate([b_expert, b_group]), LANES)

    def full(shape):
        return pl.BlockSpec(shape, lambda i: (0,) * len(shape))

    def rows(w):
        return pl.BlockSpec((tm, w), lambda i: (i, 0))

    return pl.pallas_call(
        _outproj_kernel,
        grid=(N // tm,),
        in_specs=[rows(D_MODEL), rows(A_HEADS * LANES), rows(256), rows(384),
                  full((A_HEADS * LANES, D_MODEL)), full((256, D_MODEL)), full((384, D_MODEL)),
                  full((1, A_HEADS * LANES)), full((1, 384)), full((1, D_MODEL)),
                  full((2, D_MODEL, LANES)), full((1, LANES))],
        out_specs=[rows(D_MODEL), rows(D_MODEL // 2), rows(LANES), rows(LANES),
                   pl.BlockSpec((8, LANES), lambda i: (i, 0))],
        out_shape=[jax.ShapeDtypeStruct((N, D_MODEL), F32), jax.ShapeDtypeStruct((N, D_MODEL // 2), jnp.uint32),
                   jax.ShapeDtypeStruct((N, LANES), jnp.int32), jax.ShapeDtypeStruct((N, LANES), F32),
                   jax.ShapeDtypeStruct((N // tm * 8, LANES), F32)],
        compiler_params=_params("parallel"),
        name="outproj_router",
    )(x2, ya, yb, yc, wa, wb, wc, ga, mla_out_norm.reshape(1, -1), norm2_g.reshape(1, -1), wr, br)


def _rank_kernel(ids_ref, off_ref, tril_ref, pos_ref, carry_s):
    i = pl.program_id(0)

    @pl.when(i == 0)
    def _():
        carry_s[...] = jnp.broadcast_to(off_ref[...], carry_s.shape)

    ids = ids_ref[...]
    tm = ids.shape[0]
    lane = lax.broadcasted_iota(jnp.int32, (tm, LANES), 1)
    i1 = ids[:, 0:1]
    i2 = ids[:, 1:2]
    oh = jnp.where((lane == i1) | (lane == i2), 1.0, 0.0)
    incl = _dot(tril_ref[...], oh.astype(BF16)) + carry_s[0:1, :]
    excl = incl - oh
    p1 = jnp.sum(jnp.where(lane == i1, excl, 0.0), axis=-1, keepdims=True)
    p2 = jnp.sum(jnp.where(lane == i2, excl, 0.0), axis=-1, keepdims=True)
    pos_ref[...] = jnp.where(lane == 0, p1, jnp.where(lane == 1, p2, 0.0)).astype(jnp.int32)
    carry_s[...] = jnp.broadcast_to(incl[tm - 1:tm, :], carry_s.shape)


def _moe_positions(ids, off_row):
    N = ids.shape[0]
    tm = 512
    tril = jnp.asarray(np.tril(np.ones((tm, tm))), BF16)
    return pl.pallas_call(
        _rank_kernel,
        grid=(N // tm,),
        in_specs=[pl.BlockSpec((tm, LANES), lambda i: (i, 0)), pl.BlockSpec((1, LANES), lambda i: (0, 0)),
                  pl.BlockSpec((tm, tm), lambda i: (0, 0))],
        out_specs=pl.BlockSpec((tm, LANES), lambda i: (i, 0)),
        out_shape=jax.ShapeDtypeStruct((N, LANES), jnp.int32),
        scratch_shapes=[pltpu.VMEM((8, LANES), F32)],
        compiler_params=_params("arbitrary"),
        name="moe_positions",
    )(ids, off_row, tril)


DISPATCH_TM = 2048


DMA_UNROLL = 16


def _dispatch_kernel(off_ref, tp_ref, nu_ref, p1_ref, p2_ref, h_ref, xs_ref, zero_s, sem, zsem):
    n_tiles = xs_ref.shape[0] // ROW_TILE

    @pl.when(pl.program_id(0) == 0)
    def _zero_fill():
        zero_s[...] = jnp.zeros(zero_s.shape, zero_s.dtype)

        def tile_copy(row0):
            return pltpu.make_async_copy(
                zero_s, xs_ref.at[pl.ds(pl.multiple_of(row0, ROW_TILE), ROW_TILE), :], zsem)

        for e in range(N_EXPERTS):
            @pl.when(tp_ref[e] > 0)
            def _start(e=e):
                tile_copy(off_ref[e] + (tp_ref[e] - 1) * ROW_TILE).start()

        def tail_start(g, carry):
            tile_copy(g * ROW_TILE).start()
            return carry

        lax.fori_loop(nu_ref[0], n_tiles, tail_start, 0)
        for e in range(N_EXPERTS):
            @pl.when(tp_ref[e] > 0)
            def _wait():
                tile_copy(0).wait()

        def tail_wait(g, carry):
            tile_copy(0).wait()
            return carry

        lax.fori_loop(nu_ref[0], n_tiles, tail_wait, 0)

    def row_copy(i, pos):
        return pltpu.make_async_copy(h_ref.at[pl.ds(i, 1), :], xs_ref.at[pl.ds(pos, 1), :], sem)

    def issue(j, carry):
        for u in range(DMA_UNROLL):
            i = j * DMA_UNROLL + u
            row_copy(i, p1_ref[i]).start(priority=0)
            row_copy(i, p2_ref[i]).start(priority=1)
        return carry

    lax.fori_loop(0, DISPATCH_TM // DMA_UNROLL, issue, 0)
    whole = pltpu.make_async_copy(h_ref, xs_ref.at[pl.ds(0, DISPATCH_TM), :], sem)
    whole.wait()
    whole.wait()


def _moe_dispatch(h2, off, tiles_per, n_used, pos1, pos2, n_rows):
    N = h2.shape[0]
    smem1d = pl.BlockSpec((DISPATCH_TM,), lambda i: (i,), memory_space=pltpu.SMEM)
    smem = pl.BlockSpec(memory_space=pltpu.SMEM)
    return pl.pallas_call(
        _dispatch_kernel,
        grid=(N // DISPATCH_TM,),
        in_specs=[smem, smem, smem, smem1d, smem1d,
                  pl.BlockSpec((DISPATCH_TM, D_MODEL // 2), lambda i: (i, 0))],
        out_specs=pl.BlockSpec(memory_space=pl.ANY),
        out_shape=jax.ShapeDtypeStruct((n_rows, D_MODEL // 2), jnp.uint32),
        scratch_shapes=[pltpu.VMEM((ROW_TILE, D_MODEL // 2), jnp.uint32), pltpu.SemaphoreType.DMA(()),
                        pltpu.SemaphoreType.DMA(())],
        compiler_params=_params("arbitrary"),
        name="moe_dispatch",
    )(off, tiles_per, n_used, pos1, pos2, h2)


def _expert_kernel(te_ref, nu_ref, x_ref, wg_ref, wu_ref, wd_ref, y_ref):
    g = pl.program_id(0)

    @pl.when(g < nu_ref[0])
    def _():
        x = jnp.concatenate(_unpack_bf16_pairs(x_ref[...]), axis=1)
        a = _dot(x, wg_ref[0, 0].astype(BF16))
        u = _dot(x, wu_ref[0, 0].astype(BF16))
        h = (a * jax.nn.sigmoid(a) * u).astype(BF16)
        y_ref[...] = _dot(h, wd_ref[0, 0].astype(BF16))

    @pl.when(g >= nu_ref[0])
    def _():
        y_ref[...] = jnp.zeros(y_ref.shape, F32)


def _moe_experts(xs, tile_expert, n_used, layer, w_gate, w_up, w_down):
    R = xs.shape[0]
    G = R // ROW_TILE

    def row_map(g, te, nu):
        return (jnp.minimum(g, nu[0] - 1), 0)

    def w_map(g, te, nu):
        return (layer, te[jnp.minimum(g, nu[0] - 1)], 0, 0)

    return pl.pallas_call(
        _expert_kernel,
        grid_spec=pltpu.PrefetchScalarGridSpec(
            num_scalar_prefetch=2,
            grid=(G,),
            in_specs=[pl.BlockSpec((ROW_TILE, D_MODEL // 2), row_map),
                      pl.BlockSpec((1, 1, D_MODEL, D_EXPERT), w_map),
                      pl.BlockSpec((1, 1, D_MODEL, D_EXPERT), w_map),
                      pl.BlockSpec((1, 1, D_EXPERT, D_MODEL), w_map)],
            out_specs=pl.BlockSpec((ROW_TILE, D_MODEL), lambda g, te, nu: (g, 0))),
        out_shape=jax.ShapeDtypeStruct((R, D_MODEL), F32),
        compiler_params=_params("arbitrary"),
        name="moe_experts",
    )(tile_expert, n_used, xs, w_gate, w_up, w_down)


COMBINE_TM = 1024


def _combine_kernel(p1_ref, p2_ref, n1_ref, n2_ref, x1_ref, wts_ref, ys_ref, o_ref, b1_s, b2_s, sems):
    i = pl.program_id(0)
    n_steps = pl.num_programs(0)

    def gather(q1_ref, q2_ref, slot):
        def issue(j, carry):
            for u in range(DMA_UNROLL):
                r = j * DMA_UNROLL + u
                pltpu.make_async_copy(ys_ref.at[pl.ds(q1_ref[r], 1), :], b1_s.at[slot, pl.ds(r, 1), :],
                                      sems.at[slot]).start(priority=0)
                pltpu.make_async_copy(ys_ref.at[pl.ds(q2_ref[r], 1), :], b2_s.at[slot, pl.ds(r, 1), :],
                                      sems.at[slot]).start(priority=1)
            return carry

        lax.fori_loop(0, COMBINE_TM // DMA_UNROLL, issue, 0)

    @pl.when(i == 0)
    def _first():
        gather(p1_ref, p2_ref, 0)

    for parity in range(2):
        @pl.when((i + 1 < n_steps) & ((i + 1) % 2 == parity))
        def _ahead(parity=parity):
            gather(n1_ref, n2_ref, parity)

    slot = i % 2
    pltpu.make_async_copy(ys_ref.at[pl.ds(0, COMBINE_TM), :], b1_s.at[slot], sems.at[slot]).wait()
    pltpu.make_async_copy(ys_ref.at[pl.ds(0, COMBINE_TM), :], b2_s.at[slot], sems.at[slot]).wait()
    w = wts_ref[...]
    o_ref[...] = x1_ref[...] + w[:, 0:1] * b1_s[slot] + w[:, 1:2] * b2_s[slot]


def _moe_combine(x1, wts, ys, pos1, pos2):
    N = x1.shape[0]
    n_steps = N // COMBINE_TM
    smem1d = pl.BlockSpec((COMBINE_TM,), lambda i: (i,), memory_space=pltpu.SMEM)
    ahead1d = pl.BlockSpec((COMBINE_TM,), lambda i: (jnp.minimum(i + 1, n_steps - 1),), memory_space=pltpu.SMEM)
    return pl.pallas_call(
        _combine_kernel,
        grid=(n_steps,),
        in_specs=[smem1d, smem1d, ahead1d, ahead1d,
                  pl.BlockSpec((COMBINE_TM, D_MODEL), lambda i: (i, 0)),
                  pl.BlockSpec((COMBINE_TM, LANES), lambda i: (i, 0)),
                  pl.BlockSpec(memory_space=pl.ANY)],
        out_specs=pl.BlockSpec((COMBINE_TM, D_MODEL), lambda i: (i, 0)),
        out_shape=jax.ShapeDtypeStruct((N, D_MODEL), F32),
        scratch_shapes=[pltpu.VMEM((2, COMBINE_TM, D_MODEL), F32), pltpu.VMEM((2, COMBINE_TM, D_MODEL), F32),
                        pltpu.SemaphoreType.DMA((2,))],
        compiler_params=_params("arbitrary"),
        name="moe_combine",
    )(pos1, pos2, pos1, pos2, x1, wts, ys)


def _moe(x1, h2, ids, wts, tile_counts, layer, w_gate, w_up, w_down):
    N = x1.shape[0]
    n_rows = 2 * N + N_EXPERTS * ROW_TILE
    counts = jnp.sum(tile_counts[::8, :N_EXPERTS], axis=0).astype(jnp.int32)
    tiles_per = (counts + ROW_TILE - 1) // ROW_TILE
    tile_end = jnp.cumsum(tiles_per)
    off = ((tile_end - tiles_per) * ROW_TILE).astype(jnp.int32)
    n_used = tile_end[-1:].astype(jnp.int32)
    tile_ids = jnp.arange(n_rows // ROW_TILE, dtype=jnp.int32)
    tile_expert = jnp.minimum(jnp.sum(tile_end[None, :] <= tile_ids[:, None], axis=1), N_EXPERTS - 1).astype(jnp.int32)
    pos = _moe_positions(ids, _pad_lanes(off.astype(F32), LANES))
    pos1, pos2 = pos[:, 0], pos[:, 1]
    xs = _moe_dispatch(h2, off, tiles_per.astype(jnp.int32), n_used, pos1, pos2, n_rows)
    ys = _moe_experts(xs, tile_expert, n_used, layer, w_gate, w_up, w_down)
    return _moe_combine(x1, wts, ys, pos1, pos2)


def kernel(x, positions, rel_bias, norm1_g, w_in, nsa_cmp_pe, nsa_cmp_w1, nsa_cmp_w2, nsa_q_norm, nsa_k_norm, nsa_out_norm, mlstm_conv, mlstm_i_bias, mlstm_f_bias, mlstm_out_norm, mla_q_a_norm, mla_w_uq, mla_kv_a_norm, mla_w_ukv, mla_qk_norm, mla_out_norm, w_out, norm2_g, moe_w_group, moe_b_group, moe_w_expert, moe_b_expert, moe_w_gate, moe_w_up, moe_w_down):
    B, T, D = x.shape
    N = B * T
    depth = w_in.shape[0]
    colmap = _proj_colmap()
    tiles, biasc = _bias_tables(rel_bias, T)
    rope_cs, rope_sn = _rope_tables(positions.reshape(N, 1))
    x2 = x.reshape(N, D)
    for l in range(depth):
        w_proj = _take_cols(w_in[l], colmap).astype(BF16)
        proj2 = _inproj(x2, norm1_g[l].reshape(1, D), w_proj)
        proj = proj2.reshape(B, T, D_PROJ)
        ya = _nsa(proj, tiles, biasc, nsa_cmp_pe[l], nsa_cmp_w1[l], nsa_cmp_w2[l], nsa_q_norm[l], nsa_k_norm[l])
        yb = _mlstm4(proj, mlstm_conv[l], mlstm_i_bias[l], mlstm_f_bias[l], mlstm_out_norm[l])
        q, k, v = _mla_prep(proj2, rope_cs, rope_sn, mla_q_a_norm[l], mla_w_uq[l], mla_kv_a_norm[l], mla_w_ukv[l], mla_qk_norm[l])
        yc = _mla_attn(q.reshape(B, T, -1), k.reshape(B, T, -1), v.reshape(B, T, -1))
        x1, h2, ids, wts, tile_counts = _outproj(x2, ya.reshape(N, -1), yb.reshape(N, -1), yc.reshape(N, -1), w_out[l],
                                    nsa_out_norm[l], mla_out_norm[l], norm2_g[l], moe_w_group[l], moe_b_group[l],
                                    moe_w_expert[l], moe_b_expert[l])
        x2 = _moe(x1, h2, ids, wts, tile_counts, l, moe_w_gate, moe_w_up, moe_w_down)
    return x2.reshape(B, T, D)
```
